```python
import jax, jax.numpy as jnp
from jax import lax
import numpy as np

D_MODEL = 1024
BATCH = 2
SEQ = 8192
DEPTH = 1

GRID_W = 64
GDN_HEADS = 4
GDN_DK = 128
GDN_DV = 128
CONV_W = 5
CHUNK = 64
ATT_HEADS = 8
ATT_KV_HEADS = 2
ATT_HEAD_DIM = 64
Q_BLOCK = 128
ROPE_THETA = 10000.0
D_MIX = GDN_HEADS * GDN_DV + ATT_HEADS * ATT_HEAD_DIM
N_GROUPS = 4
EXPERTS_PER_GROUP = 8
N_EXPERTS = N_GROUPS * EXPERTS_PER_GROUP
TOP_K = 2
EXPERT_FF = D_MODEL // 2
ROUTE_BLOCK = 128
EPS = 1e-6

GDN_QK = GDN_HEADS * GDN_DK
GDN_V = GDN_HEADS * GDN_DV
GDN_CONV_C = 2 * GDN_QK + GDN_V
ATT_Q = ATT_HEADS * ATT_HEAD_DIM
ATT_KV = ATT_KV_HEADS * ATT_HEAD_DIM
IN_SPLITS = (GDN_QK, GDN_QK, GDN_V, GDN_V, GDN_HEADS, GDN_HEADS, GDN_HEADS, GDN_HEADS, ATT_Q, ATT_KV, ATT_KV)
D_IN = 2 * GDN_QK + 2 * GDN_V + 4 * GDN_HEADS + ATT_Q + 2 * ATT_KV

kernel_name = 'hybrid_gdn_axial_gqa_hmoe_encoder'


def rms_norm(x, w):
    x32 = x.astype(jnp.float32)
    y = x32 * lax.rsqrt(jnp.mean(x32 * x32, axis=-1, keepdims=True) + EPS)
    return (y * w.astype(jnp.float32)).astype(x.dtype)


def l2_norm(x):
    return x * lax.rsqrt(jnp.sum(x * x, axis=-1, keepdims=True) + EPS)


def gated_delta_chunked(q, k, v, g, beta):
    B, S, H, DK = q.shape
    DV = v.shape[-1]
    N = S // CHUNK
    q = q * (DK ** -0.5)
    to_chunks = lambda t: t.reshape(B, N, CHUNK, H, t.shape[-1]).transpose(0, 3, 1, 2, 4)
    q, k, v = to_chunks(q), to_chunks(k), to_chunks(v)
    g = g.reshape(B, N, CHUNK, H).transpose(0, 3, 1, 2)
    beta = beta.reshape(B, N, CHUNK, H).transpose(0, 3, 1, 2)
    g_cum = jnp.cumsum(g, axis=-1)
    tril = jnp.tril(jnp.ones((CHUNK, CHUNK), dtype=bool))
    strict = jnp.tril(jnp.ones((CHUNK, CHUNK), dtype=bool), k=-1)
    diff = g_cum[..., :, None] - g_cum[..., None, :]
    decay = jnp.where(tril, jnp.exp(jnp.where(tril, diff, 0.0)), 0.0)
    k_beta = k * beta[..., None]
    v_beta = v * beta[..., None]
    lower = jnp.where(strict, jnp.einsum('bhncd,bhnsd->bhncs', k_beta, k) * decay, 0.0)
    eye = jnp.eye(CHUNK, dtype=jnp.float32)
    rhs = jnp.concatenate([v_beta, k_beta * jnp.exp(g_cum)[..., None]], axis=-1)
    sol = lax.linalg.triangular_solve(eye + lower, rhs, left_side=True, lower=True, unit_diagonal=True)
    u = sol[..., :DV]
    w = sol[..., DV:]
    attn_intra = jnp.where(tril, jnp.einsum('bhncd,bhnsd->bhncs', q, k) * decay, 0.0)
    q_decay = q * jnp.exp(g_cum)[..., None]
    k_tail = k * jnp.exp(g_cum[..., -1:] - g_cum)[..., None]
    chunk_decay = jnp.exp(g_cum[..., -1])

    def step(state, xs):
        u_i, w_i, qd_i, kt_i, a_i, d_i = xs
        v_new = u_i - jnp.einsum('bhcd,bhde->bhce', w_i, state)
        o_i = jnp.einsum('bhcd,bhde->bhce', qd_i, state) + jnp.einsum('bhcs,bhse->bhce', a_i, v_new)
        state = state * d_i[..., None, None] + jnp.einsum('bhcd,bhce->bhde', kt_i, v_new)
        return state, o_i

    xs = tuple(jnp.moveaxis(t, 2, 0) for t in (u, w, q_decay, k_tail, attn_intra, chunk_decay))
    state0 = jnp.zeros((B, H, DK, DV), jnp.float32)
    _, o = lax.scan(step, state0, xs)
    return o.transpose(1, 0, 3, 2, 4).reshape(B, S, H, DV)


def gdn_group(q, k, v, z, b_f, b_b, a_f, a_b, conv_w, a_log, dt_bias, norm_w):
    B, S, _ = q.shape
    dtype = q.dtype
    qkv = jnp.concatenate([q, k, v], axis=-1)
    qkv = lax.conv_general_dilated(
        qkv, conv_w.reshape(CONV_W, 1, GDN_CONV_C).astype(dtype), window_strides=(1,),
        padding=((CONV_W // 2, CONV_W // 2),), dimension_numbers=('NWC', 'WIO', 'NWC'),
        feature_group_count=GDN_CONV_C)
    qkv = jax.nn.silu(qkv).astype(jnp.float32)
    q = l2_norm(qkv[..., :GDN_QK].reshape(B, S, GDN_HEADS, GDN_DK))
    k = l2_norm(qkv[..., GDN_QK:2 * GDN_QK].reshape(B, S, GDN_HEADS, GDN_DK))
    v = qkv[..., 2 * GDN_QK:].reshape(B, S, GDN_HEADS, GDN_DV)
    a_log = a_log.astype(jnp.float32)
    dt_bias = dt_bias.astype(jnp.float32)
    beta_f = jax.nn.sigmoid(b_f.astype(jnp.float32))
    beta_b = jax.nn.sigmoid(b_b.astype(jnp.float32))
    g_f = -jnp.exp(a_log[0]) * jax.nn.softplus(a_f.astype(jnp.float32) + dt_bias[0])
    g_b = -jnp.exp(a_log[1]) * jax.nn.softplus(a_b.astype(jnp.float32) + dt_bias[1])
    flip = lambda t: jnp.flip(t, axis=1)
    o_fwd = gated_delta_chunked(q, k, v, g_f, beta_f)
    o_bwd = flip(gated_delta_chunked(flip(q), flip(k), flip(v), flip(g_b), flip(beta_b)))
    o = o_fwd + o_bwd
    o = o * lax.rsqrt(jnp.mean(o * o, axis=-1, keepdims=True) + EPS) * norm_w.astype(jnp.float32)
    o = o * jax.nn.silu(z.astype(jnp.float32).reshape(B, S, GDN_HEADS, GDN_DV))
    return o.reshape(B, S, GDN_HEADS * GDN_DV).astype(dtype)


def axial_rope_tables(seq_len):
    rows = seq_len // GRID_W
    row = jnp.repeat(jnp.arange(rows), GRID_W).astype(jnp.float32)
    col = jnp.tile(jnp.arange(GRID_W), rows).astype(jnp.float32)
    axis_dims = ATT_HEAD_DIM // 2
    inv_freq = ROPE_THETA ** (-jnp.arange(0, axis_dims, 2, dtype=jnp.float32) / axis_dims)
    ang = jnp.concatenate([row[:, None] * inv_freq, col[:, None] * inv_freq], axis=-1)
    return jnp.cos(ang), jnp.sin(ang)


def apply_rope(x, cos, sin):
    x32 = x.astype(jnp.float32).reshape(*x.shape[:-1], x.shape[-1] // 2, 2)
    x0, x1 = x32[..., 0], x32[..., 1]
    c, s = cos[None, :, None, :], sin[None, :, None, :]
    out = jnp.stack([x0 * c - x1 * s, x0 * s + x1 * c], axis=-1)
    return out.reshape(x.shape).astype(x.dtype)


def bidir_gqa_blocks(q, k, v):
    B, S, H, D = q.shape
    kvh = k.shape[2]
    grp = H // kvh
    nb = S // Q_BLOCK
    qb = jnp.moveaxis(q.reshape(B, nb, Q_BLOCK, kvh, grp, D), 1, 0)
    scale = D ** -0.5

    def one_block(q_blk):
        s = jnp.einsum('bqkgd,bskd->bkgqs', q_blk, k).astype(jnp.float32) * scale
        p = jax.nn.softmax(s, axis=-1).astype(v.dtype)
        return jnp.einsum('bkgqs,bskd->bqkgd', p, v)

    o = lax.map(one_block, qb)
    return jnp.moveaxis(o, 0, 1).reshape(B, S, H * D)


def hierarchical_moe(h, w_router_group, w_router_expert, w_gate, w_up, w_down):
    B, S, D = h.shape
    T = B * S
    xf = h.reshape(T, D)
    gp = jax.nn.softmax((xf @ w_router_group).astype(jnp.float32), axis=-1)
    gp_top, g_sel = lax.top_k(gp, 1)
    fine = (xf @ w_router_expert).astype(jnp.float32).reshape(T, N_GROUPS, EXPERTS_PER_GROUP)
    fine_sel = jnp.take_along_axis(fine, g_sel[:, :, None], axis=1)[:, 0]
    pf_top, e_loc = lax.top_k(jax.nn.softmax(fine_sel, axis=-1), TOP_K)
    gate = gp_top * pf_top / jnp.sum(pf_top, axis=-1, keepdims=True)
    expert = g_sel * EXPERTS_PER_GROUP + e_loc

    n_assign = T * TOP_K
    n_blocks = -(-(n_assign + N_EXPERTS * (ROUTE_BLOCK - 1)) // ROUTE_BLOCK)
    e_flat = expert.reshape(-1)
    tok_flat = jnp.repeat(jnp.arange(T), TOP_K)
    gate_flat = gate.reshape(-1)
    order = jnp.argsort(e_flat, stable=True)
    e_sorted, tok_sorted, gate_sorted = e_flat[order], tok_flat[order], gate_flat[order]
    counts = jnp.bincount(e_flat, length=N_EXPERTS)
    padded = (counts + ROUTE_BLOCK - 1) // ROUTE_BLOCK * ROUTE_BLOCK
    count_start = jnp.cumsum(counts) - counts
    pad_end = jnp.cumsum(padded)
    pad_start = pad_end - padded
    dest = pad_start[e_sorted] + (jnp.arange(n_assign) - count_start[e_sorted])
    x_buf = jnp.zeros((n_blocks * ROUTE_BLOCK, D), h.dtype).at[dest].set(xf[tok_sorted])
    block_start = jnp.arange(n_blocks) * ROUTE_BLOCK
    block_expert = jnp.minimum(jnp.searchsorted(pad_end, block_start, side='right'), N_EXPERTS - 1)

    def one_block(args):
        xb, e = args
        hid = jax.nn.silu(xb @ w_gate[e]) * (xb @ w_up[e])
        return hid @ w_down[e]

    y_buf = lax.map(one_block, (x_buf.reshape(n_blocks, ROUTE_BLOCK, D), block_expert)).reshape(-1, D)
    y = y_buf[dest] * gate_sorted[:, None].astype(h.dtype)
    out = jnp.zeros((T, D), h.dtype).at[tok_sorted].add(y)
    return out.reshape(B, S, D)


def setup_inputs(seed: int = 0) -> dict:
    key = jax.random.key(seed)
    ks = jax.random.split(key, 20)
    f32 = jnp.float32
    nrm = lambda k, shape, fan_in: jax.random.normal(k, shape, f32) * (fan_in ** -0.5)
    gain = lambda k, shape: 1.0 + 0.02 * jax.random.normal(k, shape, f32)
    dt = jnp.exp(jax.random.uniform(ks[5], (DEPTH, 2, GDN_HEADS), f32, np.log(1e-3), np.log(1e-1)))
    return {
        'x': jax.random.normal(ks[0], (BATCH, SEQ, D_MODEL), f32),
        'norm_mix_w': gain(ks[1], (DEPTH, D_MODEL)),
        'w_in': nrm(ks[2], (DEPTH, D_MODEL, D_IN), D_MODEL),
        'conv_w': nrm(ks[3], (DEPTH, CONV_W, GDN_CONV_C), CONV_W),
        'a_log': jnp.log(jax.random.uniform(ks[4], (DEPTH, 2, GDN_HEADS), f32, 1.0, 16.0)),
        'dt_bias': dt + jnp.log(-jnp.expm1(-dt)),
        'gdn_norm_w': gain(ks[6], (DEPTH, GDN_DV)),
        'q_norm_w': gain(ks[7], (DEPTH, ATT_HEAD_DIM)),
        'k_norm_w': gain(ks[8], (DEPTH, ATT_HEAD_DIM)),
        'w_out': nrm(ks[9], (DEPTH, D_MIX, D_MODEL), D_MIX),
        'norm_ffn_w': gain(ks[10], (DEPTH, D_MODEL)),
        'w_router_group': nrm(ks[11], (DEPTH, D_MODEL, N_GROUPS), D_MODEL),
        'w_router_expert': nrm(ks[12], (DEPTH, D_MODEL, N_EXPERTS), D_MODEL),
        'w_gate': nrm(ks[13], (DEPTH, N_EXPERTS, D_MODEL, EXPERT_FF), D_MODEL),
        'w_up': nrm(ks[14], (DEPTH, N_EXPERTS, D_MODEL, EXPERT_FF), D_MODEL),
        'w_down': nrm(ks[15], (DEPTH, N_EXPERTS, EXPERT_FF, D_MODEL), EXPERT_FF),
        'final_norm_w': gain(ks[16], (D_MODEL,)),
    }


def reference(x, norm_mix_w, w_in, conv_w, a_log, dt_bias, gdn_norm_w, q_norm_w, k_norm_w, w_out,
              norm_ffn_w, w_router_group, w_router_expert, w_gate, w_up, w_down, final_norm_w):
    B, S, _ = x.shape
    cos, sin = axial_rope_tables(S)
    split_idx = np.cumsum(IN_SPLITS)[:-1].tolist()
    for l in range(DEPTH):
        h = rms_norm(x, norm_mix_w[l])
        proj = h @ w_in[l]
        (q_a, k_a, v_a, z_a, b_f, b_b, a_f, a_b, q_b, k_b, v_b) = jnp.split(proj, split_idx, axis=-1)
        o_a = gdn_group(q_a, k_a, v_a, z_a, b_f, b_b, a_f, a_b, conv_w[l], a_log[l], dt_bias[l], gdn_norm_w[l])
        q_b = apply_rope(rms_norm(q_b.reshape(B, S, ATT_HEADS, ATT_HEAD_DIM), q_norm_w[l]), cos, sin)
        k_b = apply_rope(rms_norm(k_b.reshape(B, S, ATT_KV_HEADS, ATT_HEAD_DIM), k_norm_w[l]), cos, sin)
        v_b = v_b.reshape(B, S, ATT_KV_HEADS, ATT_HEAD_DIM)
        o_b = bidir_gqa_blocks(q_b, k_b, v_b)
        x = x + jnp.concatenate([o_a, o_b], axis=-1) @ w_out[l]
        x = x + hierarchical_moe(rms_norm(x, norm_ffn_w[l]), w_router_group[l], w_router_expert[l],
                                 w_gate[l], w_up[l], w_down[l])
    return rms_norm(x, final_norm_w)
```

```python
import functools

import jax
import jax.numpy as jnp
import numpy as np
from jax import lax
from jax.experimental import pallas as pl
from jax.experimental.pallas import tpu as pltpu

F32 = jnp.float32
BF16 = jnp.bfloat16
EPS = 1e-6

GRID_W = 64
GDN_HEADS = 4
GDN_D = 128
CONV_W = 5
CHUNK = 64
ATT_HEADS = 8
ATT_KV_HEADS = 2
ATT_GROUP = ATT_HEADS // ATT_KV_HEADS
ATT_D = 64
ROPE_THETA = 10000.0
N_GROUPS = 4
EXPERTS_PER_GROUP = 8
N_EXPERTS = N_GROUPS * EXPERTS_PER_GROUP
TOP_K = 2

GDN_W = GDN_HEADS * GDN_D
ATT_Q = ATT_HEADS * ATT_D
ATT_KV = ATT_KV_HEADS * ATT_D
LANES = 128
SUBLANES = 8

COL_QKV = 0
COL_Z = 3 * GDN_W
COL_QB = COL_Z + GDN_W
COL_KB = COL_QB + ATT_Q
COL_VB = COL_KB + ATT_KV
COL_GATE = COL_VB + ATT_KV
D_PROJ = COL_GATE + LANES

VMEM_LIMIT = 56 * 1024 * 1024


def _cparams(sem):
    return pltpu.CompilerParams(dimension_semantics=sem, vmem_limit_bytes=VMEM_LIMIT)


def _silu(x):
    return x * jax.nn.sigmoid(x)


def _inproj_body(x_ref, nw_ref, w_ref, o_ref):
    x = x_ref[...]
    h = x * lax.rsqrt(jnp.mean(x * x, axis=-1, keepdims=True) + EPS) * nw_ref[...]
    o_ref[...] = jnp.dot(h.astype(BF16), w_ref[...], preferred_element_type=F32)


def _inproj(x2, norm_w, w_all, tm):
    T, D = x2.shape
    return pl.pallas_call(
        _inproj_body,
        grid=(T // tm,),
        in_specs=[
            pl.BlockSpec((tm, D), lambda i: (i, 0)),
            pl.BlockSpec((1, D), lambda i: (0, 0)),
            pl.BlockSpec((D, D_PROJ), lambda i: (0, 0)),
        ],
        out_specs=pl.BlockSpec((tm, D_PROJ), lambda i: (i, 0)),
        out_shape=jax.ShapeDtypeStruct((T, D_PROJ), F32),
        compiler_params=_cparams(("parallel",)),
        name="inproj",
    )(x2, norm_w.reshape(1, D), w_all)


def _gdn_prep_body(cur_ref, prev_ref, next_ref, cw_ref, gin_ref, gp_ref,
                   q_ref, k_ref, v_ref, g_ref, ext_ref, *, tr):
    i = pl.program_id(1)
    nr = pl.num_programs(1)
    halo = SUBLANES
    pad = CONV_W // 2
    ext_ref[0:halo, :] = jnp.where(i > 0, prev_ref[...], 0.0)
    ext_ref[halo:halo + tr, :] = cur_ref[...]
    ext_ref[halo + tr:2 * halo + tr, :] = jnp.where(i < nr - 1, next_ref[...], 0.0)
    acc = cw_ref[0:1, :] * ext_ref[pl.ds(halo - pad, tr), :]
    for j in range(1, CONV_W):
        acc = acc + cw_ref[j:j + 1, :] * ext_ref[pl.ds(halo - pad + j, tr), :]
    y = _silu(acc)
    for h in range(GDN_HEADS):
        for base, ref, scale in ((0, q_ref, GDN_D ** -0.5), (GDN_W, k_ref, 1.0)):
            t = y[:, base + h * GDN_D: base + (h + 1) * GDN_D]
            t = t * (lax.rsqrt(jnp.sum(t * t, axis=-1, keepdims=True) + EPS) * scale)
            ref[:, h * GDN_D:(h + 1) * GDN_D] = t
    v_ref[...] = y[:, 2 * GDN_W:]
    gin = gin_ref[...]
    lane = lax.broadcasted_iota(jnp.int32, gin.shape, 1)
    a = gin + gp_ref[1:2, :]
    softplus = jnp.maximum(a, 0.0) + jnp.log1p(jnp.exp(-jnp.abs(a)))
    g_ref[...] = jnp.where(lane < 2 * GDN_HEADS, jax.nn.sigmoid(gin), -gp_ref[0:1, :] * softplus)


def _gdn_prep(proj, conv_w8, gate_params, B, S, tr):
    T = B * S
    nr = S // tr
    C = 3 * GDN_W
    rb = tr // SUBLANES
    nrow8 = T // SUBLANES
    kern = functools.partial(_gdn_prep_body, tr=tr)
    out_sd = jax.ShapeDtypeStruct((T, GDN_W), F32)
    return pl.pallas_call(
        kern,
        grid=(B, nr),
        in_specs=[
            pl.BlockSpec((tr, C), lambda b, i: (b * nr + i, 0)),
            pl.BlockSpec((SUBLANES, C), lambda b, i: (jnp.maximum((b * nr + i) * rb - 1, 0), 0)),
            pl.BlockSpec((SUBLANES, C), lambda b, i: (jnp.minimum((b * nr + i + 1) * rb, nrow8 - 1), 0)),
            pl.BlockSpec((SUBLANES, C), lambda b, i: (0, 0)),
            pl.BlockSpec((tr, LANES), lambda b, i: (b * nr + i, COL_GATE // LANES)),
            pl.BlockSpec((SUBLANES, LANES), lambda b, i: (0, 0)),
        ],
        out_specs=[
            pl.BlockSpec((tr, GDN_W), lambda b, i: (b * nr + i, 0)),
            pl.BlockSpec((tr, GDN_W), lambda b, i: (b * nr + i, 0)),
            pl.BlockSpec((tr, GDN_W), lambda b, i: (b * nr + i, 0)),
            pl.BlockSpec((tr, LANES), lambda b, i: (b * nr + i, 0)),
        ],
        out_shape=[out_sd, out_sd, out_sd, jax.ShapeDtypeStruct((T, LANES), F32)],
        scratch_shapes=[pltpu.VMEM((tr + 2 * SUBLANES, C), F32)],
        compiler_params=_cparams(("parallel", "parallel")),
        name="gdn_prep",
    )(proj, proj, proj, conv_w8, proj, gate_params)


def _bdot(a, b):
    return jnp.dot(a.astype(BF16), b.astype(BF16), preferred_element_type=F32)


def _bdot_nt(a, b):
    return lax.dot_general(a.astype(BF16), b.astype(BF16), (((1,), (1,)), ((), ())),
                           preferred_element_type=F32)


def _bdot_tn(a, b):
    return lax.dot_general(a.astype(BF16), b.astype(BF16), (((0,), (0,)), ((), ())),
                           preferred_element_type=F32)


def _gdn_scan_body(qf_ref, kf_ref, vf_ref, gf_ref, gtf_ref,
                   qb_ref, kb_ref, vb_ref, gb_ref, gtb_ref,
                   of_ref, ob_ref, state_ref):
    c = pl.program_id(1)

    @pl.when(c == 0)
    def _():
        state_ref[...] = jnp.zeros_like(state_ref)

    C = CHUNK
    row = lax.broadcasted_iota(jnp.int32, (C, C), 0)
    col = lax.broadcasted_iota(jnp.int32, (C, C), 1)
    eye = (row == col).astype(F32)
    dirs = (
        (qf_ref, kf_ref, vf_ref, gf_ref, gtf_ref, of_ref, row >= col, row > col),
        (qb_ref, kb_ref, vb_ref, gb_ref, gtb_ref, ob_ref, row <= col, row < col),
    )
    for d, (q_ref, k_ref, v_ref, g_ref, gt_ref, o_ref, incl, strict) in enumerate(dirs):
        gates = g_ref[...]
        gates_t = gt_ref[0]
        cum = incl.astype(F32)
        gc_cols = jnp.dot(cum, gates, precision=lax.Precision.HIGHEST, preferred_element_type=F32)
        gc_rows = lax.dot_general(gates_t, cum, (((1,), (1,)), ((), ())),
                                  precision=lax.Precision.HIGHEST, preferred_element_type=F32)
        g_tot = jnp.sum(gates, axis=0, keepdims=True)
        for h in range(GDN_HEADS):
            lane_b = d * GDN_HEADS + h
            lane_g = 2 * GDN_HEADS + lane_b
            sl = slice(h * GDN_D, (h + 1) * GDN_D)
            q = q_ref[:, sl]
            k = k_ref[:, sl]
            v = v_ref[:, sl]
            beta = gates[:, lane_b:lane_b + 1]
            gc_col = gc_cols[:, lane_g:lane_g + 1]
            gc_row = gc_rows[lane_g:lane_g + 1, :]
            g_last = g_tot[:, lane_g:lane_g + 1]
            decay = jnp.where(incl, jnp.exp(jnp.where(incl, gc_col - gc_row, 0.0)), 0.0)
            e_col = jnp.exp(gc_col)
            k_beta = k * beta
            v_beta = v * beta
            k_bf = k.astype(BF16)
            kk = _bdot_nt(jnp.concatenate([k_beta, q], axis=0), k_bf)
            neg_l = jnp.where(strict, -kk[:C] * decay, 0.0)
            attn = jnp.where(incl, kk[C:] * decay, 0.0)
            inv = eye + neg_l
            pw = neg_l
            for _ in range(5):
                pw = _bdot(pw, pw)
                inv = inv + _bdot(inv, pw)
            sol = _bdot(inv, jnp.concatenate([v_beta, k_beta * e_col], axis=1))
            u = sol[:, :GDN_D]
            w = sol[:, GDN_D:]
            st = state_ref[lane_b]
            ws = _bdot(jnp.concatenate([w, q * e_col], axis=0), st)
            v_new = u - ws[:C]
            o_ref[:, sl] = ws[C:] + _bdot(attn, v_new)
            k_tail = k * jnp.exp(g_last - gc_col)
            state_ref[lane_b] = st * jnp.exp(g_last) + _bdot_tn(k_tail, v_new)


def _gdn_scan(q, k, v, gates, gates_t, B, S):
    T = B * S
    N = S // CHUNK
    fwd = lambda b, c: (b * N + c, 0)
    bwd = lambda b, c: (b * N + N - 1 - c, 0)
    fwd3 = lambda b, c: (b * N + c, 0, 0)
    bwd3 = lambda b, c: (b * N + N - 1 - c, 0, 0)
    wide = lambda im: pl.BlockSpec((CHUNK, GDN_W), im)
    out_sd = jax.ShapeDtypeStruct((T, GDN_W), F32)
    return pl.pallas_call(
        _gdn_scan_body,
        grid=(B, N),
        in_specs=[
            wide(fwd), wide(fwd), wide(fwd), pl.BlockSpec((CHUNK, LANES), fwd),
            pl.BlockSpec((1, 4 * GDN_HEADS, CHUNK), fwd3),
            wide(bwd), wide(bwd), wide(bwd), pl.BlockSpec((CHUNK, LANES), bwd),
            pl.BlockSpec((1, 4 * GDN_HEADS, CHUNK), bwd3),
        ],
        out_specs=[wide(fwd), wide(bwd)],
        out_shape=[out_sd, out_sd],
        scratch_shapes=[pltpu.VMEM((2 * GDN_HEADS, GDN_D, GDN_D), F32)],
        compiler_params=_cparams(("parallel", "arbitrary")),
        name="gdn_scan",
    )(q, k, v, gates, gates_t, q, k, v, gates, gates_t)


def _rope_norm(x, w, cos, sin):
    y = x * lax.rsqrt(jnp.mean(x * x, axis=-1, keepdims=True) + EPS) * w
    half = ATT_D // 2
    swapped = jnp.concatenate([y[:, half:], y[:, :half]], axis=-1)
    return y * cos + swapped * sin


def _attn_prep_body(q_ref, k_ref, v_ref, qw_ref, kw_ref, cos_ref, sin_ref,
                    qo_ref, ko_ref, vo_ref):
    cos = cos_ref[...]
    sin = sin_ref[...]
    q = q_ref[...]
    k = k_ref[...]
    v = v_ref[...]
    for h in range(ATT_HEADS):
        r = _rope_norm(q[:, h * ATT_D:(h + 1) * ATT_D], qw_ref[...], cos, sin)
        qo_ref[h] = (r * ATT_D ** -0.5).astype(BF16)
    for h in range(ATT_KV_HEADS):
        r = _rope_norm(k[:, h * ATT_D:(h + 1) * ATT_D], kw_ref[...], cos, sin)
        ko_ref[h] = r.astype(BF16)
        vo_ref[h] = v[:, h * ATT_D:(h + 1) * ATT_D].astype(BF16)


def _attn_prep(proj, qw, kw, cosf, sinf, B, S, tr):
    nr = S // tr
    row = lambda b, i: b * nr + i
    return pl.pallas_call(
        _attn_prep_body,
        grid=(B, nr),
        in_specs=[
            pl.BlockSpec((tr, ATT_Q), lambda b, i: (row(b, i), COL_QB // ATT_Q)),
            pl.BlockSpec((tr, ATT_KV), lambda b, i: (row(b, i), COL_KB // ATT_KV)),
            pl.BlockSpec((tr, ATT_KV), lambda b, i: (row(b, i), COL_VB // ATT_KV)),
            pl.BlockSpec((1, ATT_D), lambda b, i: (0, 0)),
            pl.BlockSpec((1, ATT_D), lambda b, i: (0, 0)),
            pl.BlockSpec((tr, ATT_D), lambda b, i: (i, 0)),
            pl.BlockSpec((tr, ATT_D), lambda b, i: (i, 0)),
        ],
        out_specs=[
            pl.BlockSpec((None, ATT_HEADS, tr, ATT_D), lambda b, i: (b, 0, i, 0)),
            pl.BlockSpec((None, ATT_KV_HEADS, tr, ATT_D), lambda b, i: (b, 0, i, 0)),
            pl.BlockSpec((None, ATT_KV_HEADS, tr, ATT_D), lambda b, i: (b, 0, i, 0)),
        ],
        out_shape=[
            jax.ShapeDtypeStruct((B, ATT_HEADS, S, ATT_D), BF16),
            jax.ShapeDtypeStruct((B, ATT_KV_HEADS, S, ATT_D), BF16),
            jax.ShapeDtypeStruct((B, ATT_KV_HEADS, S, ATT_D), BF16),
        ],
        compiler_params=_cparams(("parallel", "parallel")),
        name="attn_prep",
    )(proj, proj, proj, qw, kw, cosf, sinf)


def _attn_body(q_ref, k_ref, v_ref, o_ref, m_ref, l_ref, acc_ref, *, tq):
    j = pl.program_id(3)

    @pl.when(j == 0)
    def _():
        m_ref[...] = jnp.full_like(m_ref, -jnp.inf)
        l_ref[...] = jnp.zeros_like(l_ref)
        acc_ref[...] = jnp.zeros_like(acc_ref)

    q = q_ref[...].reshape(ATT_GROUP * tq, ATT_D)
    s = lax.dot_general(q, k_ref[...], (((1,), (1,)), ((), ())), preferred_element_type=F32)
    m_prev = m_ref[...]
    m_new = jnp.maximum(m_prev, jnp.max(s, axis=-1, keepdims=True))
    alpha = jnp.exp(m_prev - m_new)
    p = jnp.exp(s - m_new)
    l_ref[...] = alpha * l_ref[...] + jnp.sum(p, axis=-1, keepdims=True)
    acc_ref[...] = alpha * acc_ref[...] + jnp.dot(p.astype(BF16), v_ref[...], preferred_element_type=F32)
    m_ref[...] = m_new

    @pl.when(j == pl.num_programs(3) - 1)
    def _():
        o = acc_ref[...] / l_ref[...]
        for h in range(ATT_GROUP):
            o_ref[:, h * ATT_D:(h + 1) * ATT_D] = o[h * tq:(h + 1) * tq].astype(o_ref.dtype)


def _attention(q, k, v, B, S, tq, tk):
    kern = functools.partial(_attn_body, tq=tq)
    gw = ATT_GROUP * ATT_D
    return pl.pallas_call(
        kern,
        grid=(B, ATT_KV_HEADS, S // tq, S // tk),
        in_specs=[
            pl.BlockSpec((None, ATT_GROUP, tq, ATT_D), lambda b, g, i, j: (b, g, i, 0)),
            pl.BlockSpec((None, None, tk, ATT_D), lambda b, g, i, j: (b, g, j, 0)),
            pl.BlockSpec((None, None, tk, ATT_D), lambda b, g, i, j: (b, g, j, 0)),
        ],
        out_specs=pl.BlockSpec((None, tq, gw), lambda b, g, i, j: (b, i, g)),
        out_shape=jax.ShapeDtypeStruct((B, S, ATT_Q), BF16),
        scratch_shapes=[
            pltpu.VMEM((ATT_GROUP * tq, 1), F32),
            pltpu.VMEM((ATT_GROUP * tq, 1), F32),
            pltpu.VMEM((ATT_GROUP * tq, ATT_D), F32),
        ],
        compiler_params=_cparams(("parallel", "parallel", "parallel", "arbitrary")),
        name="attention",
    )(q, k, v)


def _outproj_body(of_ref, ob_ref, z_ref, att_ref, x_ref, gnw_ref, wa_ref, wb_ref,
                  fnw_ref, wr_ref, xm_ref, h_ref, r_ref):
    o = of_ref[...] + ob_ref[...]
    z = z_ref[...]
    parts = []
    for h in range(GDN_HEADS):
        sl = slice(h * GDN_D, (h + 1) * GDN_D)
        t = o[:, sl]
        t = t * lax.rsqrt(jnp.mean(t * t, axis=-1, keepdims=True) + EPS) * gnw_ref[...]
        parts.append((t * _silu(z[:, sl])).astype(BF16))
    mix_a = jnp.concatenate(parts, axis=-1)
    xm = x_ref[...] + jnp.dot(mix_a, wa_ref[...], preferred_element_type=F32)
    xm = xm + jnp.dot(att_ref[...], wb_ref[...], preferred_element_type=F32)
    xm_ref[...] = xm
    hn = xm * lax.rsqrt(jnp.mean(xm * xm, axis=-1, keepdims=True) + EPS) * fnw_ref[...]
    h_ref[...] = hn
    logits = jnp.dot(hn, wr_ref[...], precision=lax.Precision.HIGHEST, preferred_element_type=F32)

    lane = lax.broadcasted_iota(jnp.int32, logits.shape, 1)
    big = jnp.int32(LANES)
    neg = -jnp.inf

    def masked_top(vals, mask):
        m = jnp.max(jnp.where(mask, vals, neg), axis=-1, keepdims=True)
        idx = jnp.min(jnp.where(mask & (vals == m), lane, big), axis=-1, keepdims=True)
        return m, idx

    gmask = lane < N_GROUPS
    gmax, gsel = masked_top(logits, gmask)
    gp_top = 1.0 / jnp.sum(jnp.where(gmask, jnp.exp(logits - gmax), 0.0), axis=-1, keepdims=True)
    lo = N_GROUPS + gsel * EXPERTS_PER_GROUP
    emask = (lane >= lo) & (lane < lo + EXPERTS_PER_GROUP)
    m1, i1 = masked_top(logits, emask)
    ex = jnp.where(emask, jnp.exp(logits - m1), 0.0)
    pf = ex / jnp.sum(ex, axis=-1, keepdims=True)
    p1, _ = masked_top(pf, emask)
    p2, i2 = masked_top(pf, emask & (lane != i1))
    denom = p1 + p2
    g1 = gp_top * p1 / denom
    g2 = gp_top * p2 / denom
    e1 = (i1 - N_GROUPS).astype(F32)
    e2 = (i2 - N_GROUPS).astype(F32)
    r_ref[...] = jnp.where(lane == 0, e1, jnp.where(lane == 1, e2, jnp.where(lane == 2, g1, jnp.where(lane == 3, g2, 0.0))))


def _outproj(o_f, o_b, proj, att, x2, gnw, w_a, w_b, fnw, w_r, tm):
    T, D = x2.shape
    row = lambda i: (i, 0)
    const = lambda i: (0, 0)
    return pl.pallas_call(
        _outproj_body,
        grid=(T // tm,),
        in_specs=[
            pl.BlockSpec((tm, GDN_W), row),
            pl.BlockSpec((tm, GDN_W), row),
            pl.BlockSpec((tm, GDN_W), lambda i: (i, COL_Z // GDN_W)),
            pl.BlockSpec((tm, ATT_Q), row),
            pl.BlockSpec((tm, D), row),
            pl.BlockSpec((1, GDN_D), const),
            pl.BlockSpec((GDN_W, D), const),
            pl.BlockSpec((ATT_Q, D), const),
            pl.BlockSpec((1, D), const),
            pl.BlockSpec((D, LANES), const),
        ],
        out_specs=[pl.BlockSpec((tm, D), row), pl.BlockSpec((tm, D), row), pl.BlockSpec((tm, LANES), row)],
        out_shape=[jax.ShapeDtypeStruct((T, D), F32), jax.ShapeDtypeStruct((T, D), F32),
                   jax.ShapeDtypeStruct((T, LANES), F32)],
        compiler_params=_cparams(("parallel",)),
        name="outproj_router",
    )(o_f, o_b, proj, att, x2, gnw, w_a, w_b, fnw, w_r)


def _row_copy(src_hbm, dst_buf, sem, tok, slot, r):
    return pltpu.make_async_copy(src_hbm.at[pl.ds(tok, 1)], dst_buf.at[slot, pl.ds(r, 1)], sem.at[slot])


def _experts_body(be_ref, src_ref, nused_ref, h_hbm, wg_ref, wu_ref, wd_ref, y_ref, xbuf, sem, *, bm):
    i = pl.program_id(0)
    n_used = nused_ref[0]
    slot = i % 2

    def issue(blk, slot_):
        def body(r, carry):
            _row_copy(h_hbm, xbuf, sem, src_ref[blk * bm + r], slot_, r).start()
            return carry
        lax.fori_loop(0, bm, body, 0)

    @pl.when((i == 0) & (n_used > 0))
    def _():
        issue(0, 0)

    @pl.when(i + 1 < n_used)
    def _():
        issue(i + 1, 1 - slot)

    @pl.when(i < n_used)
    def _():
        def wbody(r, carry):
            _row_copy(h_hbm, xbuf, sem, 0, slot, r).wait()
            return carry
        lax.fori_loop(0, bm, wbody, 0)
        x = xbuf[slot].astype(BF16)
        gate = jnp.dot(x, wg_ref[...].astype(BF16), preferred_element_type=F32)
        up = jnp.dot(x, wu_ref[...].astype(BF16), preferred_element_type=F32)
        hid = (_silu(gate) * up).astype(BF16)
        y_ref[...] = jnp.dot(hid, wd_ref[...].astype(BF16), preferred_element_type=F32)

    @pl.when(i >= n_used)
    def _():
        y_ref[...] = jnp.zeros_like(y_ref)


def _experts(block_expert, src_tok, n_used, h2, w_gate, w_up, w_down, bm):
    T, D = h2.shape
    P = src_tok.shape[0]
    FF = w_gate.shape[-1]
    kern = functools.partial(_experts_body, bm=bm)
    return pl.pallas_call(
        kern,
        grid_spec=pltpu.PrefetchScalarGridSpec(
            num_scalar_prefetch=3,
            grid=(P // bm,),
            in_specs=[
                pl.BlockSpec(memory_space=pl.ANY),
                pl.BlockSpec((None, D, FF), lambda i, be, st, nu: (be[i], 0, 0)),
                pl.BlockSpec((None, D, FF), lambda i, be, st, nu: (be[i], 0, 0)),
                pl.BlockSpec((None, FF, D), lambda i, be, st, nu: (be[i], 0, 0)),
            ],
            out_specs=pl.BlockSpec((bm, D), lambda i, be, st, nu: (i, 0)),
            scratch_shapes=[pltpu.VMEM((2, bm, D), F32), pltpu.SemaphoreType.DMA((2,))],
        ),
        out_shape=jax.ShapeDtypeStruct((P, D), F32),
        compiler_params=_cparams(("arbitrary",)),
        name="moe_experts",
    )(block_expert, src_tok, n_used, h2, w_gate, w_up, w_down)


def _combine_body(dest_ref, y_hbm, xm_ref, r_ref, fw_ref, o_ref, ybuf, sem, *, tc):
    i = pl.program_id(0)
    n = pl.num_programs(0)
    slot = i % 2
    rows = TOP_K * tc

    def issue(blk, slot_):
        def body(r, carry):
            _row_copy(y_hbm, ybuf, sem, dest_ref[blk * rows + r], slot_, r).start()
            return carry
        lax.fori_loop(0, rows, body, 0)

    @pl.when(i == 0)
    def _():
        issue(0, 0)

    @pl.when(i + 1 < n)
    def _():
        issue(i + 1, 1 - slot)

    def wbody(r, carry):
        _row_copy(y_hbm, ybuf, sem, 0, slot, r).wait()
        return carry
    lax.fori_loop(0, rows, wbody, 0)
    route = r_ref[...]
    y = ybuf[slot]
    xo = xm_ref[...] + route[:, 2:3] * y[:tc] + route[:, 3:4] * y[tc:]
    o_ref[...] = xo * lax.rsqrt(jnp.mean(xo * xo, axis=-1, keepdims=True) + EPS) * fw_ref[...]


def _combine(dest_blocked, y_buf, x_mid, route, final_w, tc):
    T, D = x_mid.shape
    kern = functools.partial(_combine_body, tc=tc)
    return pl.pallas_call(
        kern,
        grid_spec=pltpu.PrefetchScalarGridSpec(
            num_scalar_prefetch=1,
            grid=(T // tc,),
            in_specs=[
                pl.BlockSpec(memory_space=pl.ANY),
                pl.BlockSpec((tc, D), lambda i, d: (i, 0)),
                pl.BlockSpec((tc, LANES), lambda i, d: (i, 0)),
                pl.BlockSpec((1, D), lambda i, d: (0, 0)),
            ],
            out_specs=pl.BlockSpec((tc, D), lambda i, d: (i, 0)),
            scratch_shapes=[pltpu.VMEM((2, TOP_K * tc, D), F32), pltpu.SemaphoreType.DMA((2,))],
        ),
        out_shape=jax.ShapeDtypeStruct((T, D), F32),
        compiler_params=_cparams(("arbitrary",)),
        name="moe_combine",
    )(dest_blocked, y_buf, x_mid, route, final_w.reshape(1, D))


def _tile(n, want):
    t = min(n, want)
    assert n % t == 0
    return t


def _layer(x2, B, S, norm_mix_w, w_in, conv_w, a_log, dt_bias, gdn_norm_w, q_norm_w, k_norm_w, w_out,
           norm_ffn_w, w_router_group, w_router_expert, w_gate, w_up, w_down, final_w):
    T, D = x2.shape
    perm = np.concatenate([np.arange(0, ATT_D, 2), np.arange(1, ATT_D, 2)])
    o_z = 3 * GDN_W
    o_gate = o_z + GDN_W
    o_qb = o_gate + 4 * GDN_HEADS
    o_kb = o_qb + ATT_Q
    o_vb = o_kb + ATT_KV
    qb_cols = o_qb + (np.arange(ATT_HEADS)[:, None] * ATT_D + perm[None, :]).reshape(-1)
    kb_cols = o_kb + (np.arange(ATT_KV_HEADS)[:, None] * ATT_D + perm[None, :]).reshape(-1)
    w_all = jnp.concatenate([
        w_in[:, :o_gate], w_in[:, qb_cols], w_in[:, kb_cols], w_in[:, o_vb:o_vb + ATT_KV],
        w_in[:, o_gate:o_qb], jnp.zeros((D, LANES - 4 * GDN_HEADS), w_in.dtype)], axis=1).astype(BF16)

    proj = _inproj(x2, norm_mix_w, w_all, _tile(T, 512))

    conv_w8 = jnp.concatenate([conv_w, jnp.zeros((SUBLANES - CONV_W, conv_w.shape[1]), F32)], axis=0)
    gp = jnp.zeros((SUBLANES, LANES), F32)
    gp = gp.at[0, 2 * GDN_HEADS:4 * GDN_HEADS].set(jnp.exp(a_log.astype(F32)).reshape(-1))
    gp = gp.at[1, 2 * GDN_HEADS:4 * GDN_HEADS].set(dt_bias.astype(F32).reshape(-1))
    q_a, k_a, v_a, gates = _gdn_prep(proj, conv_w8, gp, B, S, _tile(S, 512))
    n_chunks = T // CHUNK
    gates_t = gates[:, :4 * GDN_HEADS].reshape(n_chunks, CHUNK, 4 * GDN_HEADS).transpose(0, 2, 1)
    o_f, o_b = _gdn_scan(q_a, k_a, v_a, gates, gates_t, B, S)

    rows = S // GRID_W
    rowp = jnp.repeat(jnp.arange(rows), GRID_W).astype(F32)
    colp = jnp.tile(jnp.arange(GRID_W), rows).astype(F32)
    axis_dims = ATT_D // 2
    inv_freq = ROPE_THETA ** (-jnp.arange(0, axis_dims, 2, dtype=F32) / axis_dims)
    ang = jnp.concatenate([rowp[:, None] * inv_freq, colp[:, None] * inv_freq], axis=-1)
    cosf = jnp.concatenate([jnp.cos(ang), jnp.cos(ang)], axis=-1)
    sinf = jnp.concatenate([-jnp.sin(ang), jnp.sin(ang)], axis=-1)
    qh, kh, vh = _attn_prep(proj, q_norm_w[perm].reshape(1, ATT_D), k_norm_w[perm].reshape(1, ATT_D),
                            cosf, sinf, B, S, _tile(S, 512))
    att = _attention(qh, kh, vh, B, S, _tile(S, 256), _tile(S, 1024)).reshape(T, ATT_Q)

    w_r = jnp.concatenate([w_router_group, w_router_expert,
                           jnp.zeros((D, LANES - N_GROUPS - N_EXPERTS), F32)], axis=1)
    w_out_bf = w_out.astype(BF16)
    x_mid, h2, route = _outproj(o_f, o_b, proj, att, x2, gdn_norm_w.reshape(1, GDN_D),
                                w_out_bf[:GDN_W], w_out_bf[GDN_W:], norm_ffn_w.reshape(1, D), w_r,
                                _tile(T, 512))

    bm = 256
    n_assign = T * TOP_K
    n_blocks = -(-(n_assign + N_EXPERTS * (bm - 1)) // bm)
    e_flat = route[:, :TOP_K].astype(jnp.int32).reshape(-1)
    onehot = (e_flat[:, None] == jnp.arange(N_EXPERTS)[None, :]).astype(jnp.int32)
    csum = jnp.cumsum(onehot, axis=0)
    rank = jnp.sum(csum * onehot, axis=1) - 1
    counts = csum[-1]
    padded = (counts + bm - 1) // bm * bm
    pad_end = jnp.cumsum(padded)
    pad_start = pad_end - padded
    dest = pad_start[e_flat] + rank
    src_tok = jnp.zeros((n_blocks * bm,), jnp.int32).at[dest].set(jnp.arange(n_assign, dtype=jnp.int32) // TOP_K)
    block_expert = jnp.minimum(jnp.searchsorted(pad_end, jnp.arange(n_blocks) * bm, side='right'),
                               N_EXPERTS - 1).astype(jnp.int32)
    n_used = (pad_end[-1:] // bm).astype(jnp.int32)

    y_buf = _experts(block_expert, src_tok, n_used, h2, w_gate, w_up, w_down, bm)

    tc = _tile(T, 256)
    dest_blocked = dest.reshape(T // tc, tc, TOP_K).transpose(0, 2, 1).reshape(-1)
    return _combine(dest_blocked, y_buf, x_mid, route, final_w, tc)


def kernel(x, norm_mix_w, w_in, conv_w, a_log, dt_bias, gdn_norm_w, q_norm_w, k_norm_w, w_out, norm_ffn_w,
           w_router_group, w_router_expert, w_gate, w_up, w_down, final_norm_w):
    B, S, D = x.shape
    depth = w_in.shape[0]
    assert depth == 1, "the final norm is fused into the last (only) layer's combine step"
    out = _layer(x.reshape(B * S, D), B, S, norm_mix_w[0], w_in[0], conv_w[0], a_log[0], dt_bias[0],
                 gdn_norm_w[0], q_norm_w[0], k_norm_w[0], w_out[0], norm_ffn_w[0], w_router_group[0],
                 w_router_expert[0], w_gate[0], w_up[0], w_down[0], final_norm_w)
    return out.reshape(B, S, D)
```

```python
import functools
import math
from typing import NamedTuple

import jax
import jax.numpy as jnp
import numpy as np
from jax import lax
from jax.experimental import pallas as pl
from jax.experimental.pallas import tpu as pltpu

F32 = jnp.float32
BF16 = jnp.bfloat16
EPS = 1e-6

GRID_W = 64
GDN_HEADS = 4
GDN_D = 128
CONV_W = 5
CHUNK = 64
ATT_HEADS = 8
ATT_KV_HEADS = 2
ATT_GROUP = ATT_HEADS // ATT_KV_HEADS
ATT_D = 64
ROPE_THETA = 10000.0
N_GROUPS = 4
EXPERTS_PER_GROUP = 8
N_EXPERTS = N_GROUPS * EXPERTS_PER_GROUP
TOP_K = 2

GDN_W = GDN_HEADS * GDN_D
GDN_CHAINS = 2 * GDN_HEADS
ATT_Q = ATT_HEADS * ATT_D
ATT_KV = ATT_KV_HEADS * ATT_D
LANES = 128
SUBLANES = 8
ATT_DP = LANES

COL_Z = 3 * GDN_W
COL_QB = COL_Z + GDN_W
COL_KB = COL_QB + ATT_Q
COL_VB = COL_KB + ATT_KV
COL_GATE = COL_VB + ATT_KV
D_PROJ = COL_GATE + LANES

VMEM_LIMIT = 56 * 1024 * 1024
LOG2E = math.log2(math.e)
SOFTMAX_SAFE_SPAN = 60.0


class Tiles(NamedTuple):
    proj_rows: int
    prep_rows: int
    gdn_chunks: int
    scan_chunks: int
    att_q: int
    att_k: int
    moe_rows: int
    comb_rows: int


def _tile(n, want):
    t = min(n, want)
    assert n % t == 0, (n, want)
    return t


def _tiles(B, S):
    T = B * S
    n_chunks = S // CHUNK
    return Tiles(proj_rows=_tile(T, 512), prep_rows=_tile(S, 512), gdn_chunks=_tile(n_chunks, 2),
                 scan_chunks=_tile(n_chunks, 4), att_q=_tile(S, 256), att_k=_tile(S, 1024),
                 moe_rows=256, comb_rows=_tile(T, 256))


def _cparams(sem):
    return pltpu.CompilerParams(dimension_semantics=sem, vmem_limit_bytes=VMEM_LIMIT)


def _silu(x):
    return x * jax.nn.sigmoid(x)


def _inproj_body(x_ref, nw_ref, w_ref, o_ref):
    x = x_ref[...]
    h = x * lax.rsqrt(jnp.mean(x * x, axis=-1, keepdims=True) + EPS) * nw_ref[...]
    o_ref[...] = jnp.dot(h.astype(BF16), w_ref[...], preferred_element_type=F32)


def _inproj(x2, norm_w, w_all, tm):
    T, D = x2.shape
    return pl.pallas_call(
        _inproj_body,
        grid=(T // tm,),
        in_specs=[
            pl.BlockSpec((tm, D), lambda i: (i, 0)),
            pl.BlockSpec((1, D), lambda i: (0, 0)),
            pl.BlockSpec((D, D_PROJ), lambda i: (0, 0)),
        ],
        out_specs=pl.BlockSpec((tm, D_PROJ), lambda i: (i, 0)),
        out_shape=jax.ShapeDtypeStruct((T, D_PROJ), F32),
        compiler_params=_cparams(("parallel",)),
        name="inproj",
    )(x2, norm_w.reshape(1, D), w_all)


def _gdn_prep_body(cur_ref, prev_ref, next_ref, cw_ref, gin_ref, gp_ref,
                   q_ref, k_ref, v_ref, g_ref, ext_ref, *, tr):
    i = pl.program_id(1)
    nr = pl.num_programs(1)
    halo = SUBLANES
    pad = CONV_W // 2
    ext_ref[0:halo, :] = jnp.where(i > 0, prev_ref[...], 0.0)
    ext_ref[halo:halo + tr, :] = cur_ref[...]
    ext_ref[halo + tr:2 * halo + tr, :] = jnp.where(i < nr - 1, next_ref[...], 0.0)
    acc = cw_ref[0:1, :] * ext_ref[pl.ds(halo - pad, tr), :]
    for j in range(1, CONV_W):
        acc = acc + cw_ref[j:j + 1, :] * ext_ref[pl.ds(halo - pad + j, tr), :]
    y = _silu(acc)
    for h in range(GDN_HEADS):
        for base, ref, scale in ((0, q_ref, GDN_D ** -0.5), (GDN_W, k_ref, 1.0)):
            t = y[:, base + h * GDN_D: base + (h + 1) * GDN_D]
            t = t * (lax.rsqrt(jnp.sum(t * t, axis=-1, keepdims=True) + EPS) * scale)
            ref[:, h * GDN_D:(h + 1) * GDN_D] = t
    v_ref[...] = y[:, 2 * GDN_W:]
    gin = gin_ref[...]
    lane = lax.broadcasted_iota(jnp.int32, gin.shape, 1)
    a = gin + gp_ref[1:2, :]
    softplus = jnp.maximum(a, 0.0) + jnp.log1p(jnp.exp(-jnp.abs(a)))
    g_ref[...] = jnp.where(lane < GDN_CHAINS, jax.nn.sigmoid(gin), -gp_ref[0:1, :] * softplus)


def _gdn_prep(proj, conv_w8, gate_params, B, S, tr):
    T = B * S
    nr = S // tr
    C = 3 * GDN_W
    rb = tr // SUBLANES
    nrow8 = T // SUBLANES
    kern = functools.partial(_gdn_prep_body, tr=tr)
    out_sd = jax.ShapeDtypeStruct((T, GDN_W), F32)
    return pl.pallas_call(
        kern,
        grid=(B, nr),
        in_specs=[
            pl.BlockSpec((tr, C), lambda b, i: (b * nr + i, 0)),
            pl.BlockSpec((SUBLANES, C), lambda b, i: (jnp.maximum((b * nr + i) * rb - 1, 0), 0)),
            pl.BlockSpec((SUBLANES, C), lambda b, i: (jnp.minimum((b * nr + i + 1) * rb, nrow8 - 1), 0)),
            pl.BlockSpec((SUBLANES, C), lambda b, i: (0, 0)),
            pl.BlockSpec((tr, LANES), lambda b, i: (b * nr + i, COL_GATE // LANES)),
            pl.BlockSpec((SUBLANES, LANES), lambda b, i: (0, 0)),
        ],
        out_specs=[
            pl.BlockSpec((tr, GDN_W), lambda b, i: (b * nr + i, 0)),
            pl.BlockSpec((tr, GDN_W), lambda b, i: (b * nr + i, 0)),
            pl.BlockSpec((tr, GDN_W), lambda b, i: (b * nr + i, 0)),
            pl.BlockSpec((tr, LANES), lambda b, i: (b * nr + i, 0)),
        ],
        out_shape=[out_sd, out_sd, out_sd, jax.ShapeDtypeStruct((T, LANES), F32)],
        scratch_shapes=[pltpu.VMEM((tr + 2 * SUBLANES, C), F32)],
        compiler_params=_cparams(("parallel", "parallel")),
        name="gdn_prep",
    )(proj, proj, proj, conv_w8, proj, gate_params)


def _bdot(a, b):
    return jnp.dot(a.astype(BF16), b.astype(BF16), preferred_element_type=F32)


def _bdot_nt(a, b):
    return lax.dot_general(a.astype(BF16), b.astype(BF16), (((1,), (1,)), ((), ())),
                           preferred_element_type=F32)


def _gdn_chunk_body(q_ref, k_ref, v_ref, g_ref, gt_ref, u_ref, wq_ref, ak_ref, dec_ref, *, cb):
    C = CHUNK
    row = lax.broadcasted_iota(jnp.int32, (C, C), 0)
    col = lax.broadcasted_iota(jnp.int32, (C, C), 1)
    eye = (row == col).astype(F32)
    masks = ((row >= col, row > col), (row <= col, row < col))
    hi = lax.Precision.HIGHEST
    for c in range(cb):
        rs = slice(c * C, (c + 1) * C)
        gates = g_ref[rs, :]
        gates_t = gt_ref[c]
        g_tot = jnp.sum(gates, axis=0, keepdims=True)
        tot_rows = jnp.sum(gates_t, axis=1, keepdims=True)
        dec_ref[c] = jnp.broadcast_to(jnp.exp(tot_rows[GDN_CHAINS:2 * GDN_CHAINS]), (GDN_CHAINS, LANES))
        chains = []
        for d in range(2):
            incl, strict = masks[d]
            cum = incl.astype(F32)
            gc_cols = jnp.dot(cum, gates, precision=hi, preferred_element_type=F32)
            gc_rows = lax.dot_general(gates_t, cum, (((1,), (1,)), ((), ())), precision=hi,
                                      preferred_element_type=F32)
            for h in range(GDN_HEADS):
                lane_b = d * GDN_HEADS + h
                lane_g = GDN_CHAINS + lane_b
                sl = slice(h * GDN_D, (h + 1) * GDN_D)
                q = q_ref[rs, sl]
                k = k_ref[rs, sl]
                v = v_ref[rs, sl]
                beta = gates[:, lane_b:lane_b + 1]
                gc_col = gc_cols[:, lane_g:lane_g + 1]
                gc_row = gc_rows[lane_g:lane_g + 1, :]
                g_last = g_tot[:, lane_g:lane_g + 1]
                decay = jnp.where(incl, jnp.exp(jnp.where(incl, gc_col - gc_row, 0.0)), 0.0)
                e_col = jnp.exp(gc_col)
                k_beta = k * beta
                kk = _bdot_nt(jnp.concatenate([k_beta, q], axis=0), k)
                neg_l = jnp.where(strict, -kk[:C] * decay, 0.0)
                attn = jnp.where(incl, kk[C:] * decay, 0.0)
                rhs = jnp.concatenate([v * beta, k_beta * e_col], axis=1).astype(BF16)
                wq_ref[d, c, C:2 * C, sl] = (q * e_col).astype(BF16)
                ak_ref[d, c, h, 0:C, :] = attn.astype(BF16)
                ak_ref[d, c, h, C:C + GDN_D, :] = (k * jnp.exp(g_last - gc_col)).T.astype(BF16)
                chains.append((d, sl, neg_l, rhs))
        pws = [nl for (_, _, nl, _) in chains]
        invs = [eye + nl for nl in pws]
        for _ in range(int(math.log2(C)) - 1):
            pws = [_bdot(p, p) for p in pws]
            invs = [t + _bdot(t, p) for t, p in zip(invs, pws)]
        for (d, sl, _, rhs), inv in zip(chains, invs):
            sol = jnp.dot(inv.astype(BF16), rhs, preferred_element_type=F32)
            u_ref[d, rs, sl] = sol[:, :GDN_D]
            wq_ref[d, c, 0:C, sl] = sol[:, GDN_D:].astype(BF16)


def _gdn_chunk(q, k, v, gates, gates_t, cb):
    T = q.shape[0]
    nc = T // CHUNK
    rows = cb * CHUNK
    kern = functools.partial(_gdn_chunk_body, cb=cb)
    wide = pl.BlockSpec((rows, GDN_W), lambda i: (i, 0))
    return pl.pallas_call(
        kern,
        grid=(nc // cb,),
        in_specs=[wide, wide, wide, pl.BlockSpec((rows, LANES), lambda i: (i, 0)),
                  pl.BlockSpec((cb, 2 * GDN_CHAINS, CHUNK), lambda i: (i, 0, 0))],
        out_specs=[
            pl.BlockSpec((2, rows, GDN_W), lambda i: (0, i, 0)),
            pl.BlockSpec((2, cb, 2 * CHUNK, GDN_W), lambda i: (0, i, 0, 0)),
            pl.BlockSpec((2, cb, GDN_HEADS, CHUNK + GDN_D, CHUNK), lambda i: (0, i, 0, 0, 0)),
            pl.BlockSpec((cb, GDN_CHAINS, LANES), lambda i: (i, 0, 0)),
        ],
        out_shape=[
            jax.ShapeDtypeStruct((2, T, GDN_W), F32),
            jax.ShapeDtypeStruct((2, nc, 2 * CHUNK, GDN_W), BF16),
            jax.ShapeDtypeStruct((2, nc, GDN_HEADS, CHUNK + GDN_D, CHUNK), BF16),
            jax.ShapeDtypeStruct((nc, GDN_CHAINS, LANES), F32),
        ],
        compiler_params=_cparams(("parallel",)),
        name="gdn_chunk",
    )(q, k, v, gates, gates_t)


def _gdn_scan_body(uf_ref, wqf_ref, akf_ref, decf_ref, ub_ref, wqb_ref, akb_ref, decb_ref,
                   of_ref, ob_ref, state_ref, *, cs):
    @pl.when(pl.program_id(1) == 0)
    def _():
        state_ref[...] = jnp.zeros_like(state_ref)

    C = CHUNK
    refs = ((uf_ref, wqf_ref, akf_ref, decf_ref, of_ref), (ub_ref, wqb_ref, akb_ref, decb_ref, ob_ref))
    chains = [(d, h) for d in range(2) for h in range(GDN_HEADS)]
    st = [state_ref[j] for j in range(GDN_CHAINS)]
    for step in range(cs):
        cidx = (step, cs - 1 - step)
        ws = [jnp.dot(refs[d][1][cidx[d], :, h * GDN_D:(h + 1) * GDN_D], st[j].astype(BF16),
                      preferred_element_type=F32) for j, (d, h) in enumerate(chains)]
        vn = [(refs[d][0][cidx[d] * C:(cidx[d] + 1) * C, h * GDN_D:(h + 1) * GDN_D] - ws[j][:C]).astype(BF16)
              for j, (d, h) in enumerate(chains)]
        rr = [jnp.dot(refs[d][2][cidx[d], h], vn[j], preferred_element_type=F32)
              for j, (d, h) in enumerate(chains)]
        for j, (d, h) in enumerate(chains):
            c = cidx[d]
            refs[d][4][c * C:(c + 1) * C, h * GDN_D:(h + 1) * GDN_D] = ws[j][C:] + rr[j][:C]
            st[j] = st[j] * refs[d][3][c, j:j + 1, :] + rr[j][C:]
    for j in range(GDN_CHAINS):
        state_ref[j] = st[j]


def _gdn_scan(u, wq, ak, dec, B, S, cs):
    T = B * S
    nb = S // (CHUNK * cs)
    rows = cs * CHUNK
    kern = functools.partial(_gdn_scan_body, cs=cs)
    fwd = lambda b, i: b * nb + i
    bwd = lambda b, i: b * nb + nb - 1 - i

    def specs(d, pos):
        return [
            pl.BlockSpec((None, rows, GDN_W), lambda b, i: (d, pos(b, i), 0)),
            pl.BlockSpec((None, cs, 2 * CHUNK, GDN_W), lambda b, i: (d, pos(b, i), 0, 0)),
            pl.BlockSpec((None, cs, GDN_HEADS, CHUNK + GDN_D, CHUNK), lambda b, i: (d, pos(b, i), 0, 0, 0)),
            pl.BlockSpec((cs, GDN_CHAINS, LANES), lambda b, i: (pos(b, i), 0, 0)),
        ]

    out_sd = jax.ShapeDtypeStruct((T, GDN_W), F32)
    return pl.pallas_call(
        kern,
        grid=(B, nb),
        in_specs=specs(0, fwd) + specs(1, bwd),
        out_specs=[pl.BlockSpec((rows, GDN_W), lambda b, i: (fwd(b, i), 0)),
                   pl.BlockSpec((rows, GDN_W), lambda b, i: (bwd(b, i), 0))],
        out_shape=[out_sd, out_sd],
        scratch_shapes=[pltpu.VMEM((GDN_CHAINS, GDN_D, GDN_D), F32)],
        compiler_params=_cparams(("parallel", "arbitrary")),
        name="gdn_scan",
    )(u, wq, ak, dec, u, wq, ak, dec)


def _rope_norm(x, w, cos, sin):
    y = x * lax.rsqrt(jnp.mean(x * x, axis=-1, keepdims=True) + EPS) * w
    half = ATT_D // 2
    swapped = jnp.concatenate([y[:, half:], y[:, :half]], axis=-1)
    return y * cos + swapped * sin


def _attn_prep_body(q_ref, k_ref, v_ref, qw_ref, kw_ref, cos_ref, sin_ref, kmax_ref,
                    qo_ref, ko_ref, vo_ref):
    cos = cos_ref[...]
    sin = sin_ref[...]
    q = q_ref[...]
    k = k_ref[...]
    v = v_ref[...]
    rows = q.shape[0]
    lane = lax.broadcasted_iota(jnp.int32, (rows, ATT_DP - ATT_D), 1)
    one_col = jnp.where(lane == 0, 1.0, 0.0)
    scale = LOG2E * ATT_D ** -0.5
    for h in range(ATT_HEADS):
        r = _rope_norm(q[:, h * ATT_D:(h + 1) * ATT_D], qw_ref[...], cos, sin) * scale
        bound = jnp.sqrt(jnp.sum(r * r, axis=-1, keepdims=True)) * kmax_ref[...]
        qo_ref[h] = jnp.concatenate([r, -bound * one_col], axis=-1).astype(BF16)
    for h in range(ATT_KV_HEADS):
        r = _rope_norm(k[:, h * ATT_D:(h + 1) * ATT_D], kw_ref[...], cos, sin)
        ko_ref[h] = jnp.concatenate([r, one_col], axis=-1).astype(BF16)
        vo_ref[h] = jnp.concatenate([v[:, h * ATT_D:(h + 1) * ATT_D], one_col], axis=-1).astype(BF16)


def _attn_prep(proj, qw, kw, cosf, sinf, kmax, B, S, tr):
    nr = S // tr
    row = lambda b, i: b * nr + i
    return pl.pallas_call(
        _attn_prep_body,
        grid=(B, nr),
        in_specs=[
            pl.BlockSpec((tr, ATT_Q), lambda b, i: (row(b, i), COL_QB // ATT_Q)),
            pl.BlockSpec((tr, ATT_KV), lambda b, i: (row(b, i), COL_KB // ATT_KV)),
            pl.BlockSpec((tr, ATT_KV), lambda b, i: (row(b, i), COL_VB // ATT_KV)),
            pl.BlockSpec((1, ATT_D), lambda b, i: (0, 0)),
            pl.BlockSpec((1, ATT_D), lambda b, i: (0, 0)),
            pl.BlockSpec((tr, ATT_D), lambda b, i: (i, 0)),
            pl.BlockSpec((tr, ATT_D), lambda b, i: (i, 0)),
            pl.BlockSpec((1, 1), lambda b, i: (0, 0)),
        ],
        out_specs=[
            pl.BlockSpec((None, ATT_HEADS, tr, ATT_DP), lambda b, i: (b, 0, i, 0)),
            pl.BlockSpec((None, ATT_KV_HEADS, tr, ATT_DP), lambda b, i: (b, 0, i, 0)),
            pl.BlockSpec((None, ATT_KV_HEADS, tr, ATT_DP), lambda b, i: (b, 0, i, 0)),
        ],
        out_shape=[
            jax.ShapeDtypeStruct((B, ATT_HEADS, S, ATT_DP), BF16),
            jax.ShapeDtypeStruct((B, ATT_KV_HEADS, S, ATT_DP), BF16),
            jax.ShapeDtypeStruct((B, ATT_KV_HEADS, S, ATT_DP), BF16),
        ],
        compiler_params=_cparams(("parallel", "parallel")),
        name="attn_prep",
    )(proj, proj, proj, qw, kw, cosf, sinf, kmax)


def _attn_body(safe_ref, q_ref, k_ref, v_ref, o_ref, m_ref, acc_ref, *, tq):
    j = pl.program_id(3)
    safe = safe_ref[0] != 0

    @pl.when(j == 0)
    def _():
        m_ref[...] = jnp.full_like(m_ref, -jnp.inf)
        acc_ref[...] = jnp.zeros_like(acc_ref)

    q = q_ref[...].reshape(ATT_GROUP * tq, ATT_DP)

    @pl.when(safe)
    def _():
        s = lax.dot_general(q, k_ref[...], (((1,), (1,)), ((), ())), preferred_element_type=F32)
        acc_ref[...] += jnp.dot(jnp.exp2(s).astype(BF16), v_ref[...], preferred_element_type=F32)

    @pl.when(jnp.logical_not(safe))
    def _():
        s = lax.dot_general(q, k_ref[...], (((1,), (1,)), ((), ())), preferred_element_type=F32)
        m_prev = m_ref[...]
        m_new = jnp.maximum(m_prev, jnp.max(s, axis=-1, keepdims=True))
        p = jnp.exp2(s - m_new).astype(BF16)
        acc_ref[...] = jnp.exp2(m_prev - m_new) * acc_ref[...] + jnp.dot(p, v_ref[...], preferred_element_type=F32)
        m_ref[...] = m_new

    @pl.when(j == pl.num_programs(3) - 1)
    def _():
        acc = acc_ref[...]
        o = acc[:, :ATT_D] / acc[:, ATT_D:ATT_D + 1]
        for h in range(ATT_GROUP):
            o_ref[:, h * ATT_D:(h + 1) * ATT_D] = o[h * tq:(h + 1) * tq].astype(o_ref.dtype)


def _attention(safe, q, k, v, B, S, tq, tk):
    kern = functools.partial(_attn_body, tq=tq)
    gw = ATT_GROUP * ATT_D
    return pl.pallas_call(
        kern,
        grid_spec=pltpu.PrefetchScalarGridSpec(
            num_scalar_prefetch=1,
            grid=(B, ATT_KV_HEADS, S // tq, S // tk),
            in_specs=[
                pl.BlockSpec((None, ATT_GROUP, tq, ATT_DP), lambda b, g, i, j, s: (b, g, i, 0)),
                pl.BlockSpec((None, None, tk, ATT_DP), lambda b, g, i, j, s: (b, g, j, 0)),
                pl.BlockSpec((None, None, tk, ATT_DP), lambda b, g, i, j, s: (b, g, j, 0)),
            ],
            out_specs=pl.BlockSpec((None, tq, gw), lambda b, g, i, j, s: (b, i, g)),
            scratch_shapes=[
                pltpu.VMEM((ATT_GROUP * tq, 1), F32),
                pltpu.VMEM((ATT_GROUP * tq, ATT_DP), F32),
            ],
        ),
        out_shape=jax.ShapeDtypeStruct((B, S, ATT_Q), BF16),
        compiler_params=_cparams(("parallel", "parallel", "parallel", "arbitrary")),
        name="attention",
    )(safe, q, k, v)


def _outproj_body(of_ref, ob_ref, z_ref, att_ref, x_ref, gnw_ref, wa_ref, wb_ref,
                  fnw_ref, wr_ref, xm_ref, h_ref, r_ref):
    o = of_ref[...] + ob_ref[...]
    z = z_ref[...]
    parts = []
    for h in range(GDN_HEADS):
        sl = slice(h * GDN_D, (h + 1) * GDN_D)
        t = o[:, sl]
        t = t * lax.rsqrt(jnp.mean(t * t, axis=-1, keepdims=True) + EPS) * gnw_ref[...]
        parts.append((t * _silu(z[:, sl])).astype(BF16))
    mix_a = jnp.concatenate(parts, axis=-1)
    xm = x_ref[...] + jnp.dot(mix_a, wa_ref[...], preferred_element_type=F32)
    xm = xm + jnp.dot(att_ref[...], wb_ref[...], preferred_element_type=F32)
    xm_ref[...] = xm
    hn = xm * lax.rsqrt(jnp.mean(xm * xm, axis=-1, keepdims=True) + EPS) * fnw_ref[...]
    h_ref[...] = hn
    logits = jnp.dot(hn, wr_ref[...], precision=lax.Precision.HIGHEST, preferred_element_type=F32)

    lane = lax.broadcasted_iota(jnp.int32, logits.shape, 1)
    big = jnp.int32(LANES)
    neg = -jnp.inf

    def masked_top(vals, mask):
        m = jnp.max(jnp.where(mask, vals, neg), axis=-1, keepdims=True)
        idx = jnp.min(jnp.where(mask & (vals == m), lane, big), axis=-1, keepdims=True)
        return m, idx

    gmask = lane < N_GROUPS
    gmax, gsel = masked_top(logits, gmask)
    gp_top = 1.0 / jnp.sum(jnp.where(gmask, jnp.exp(logits - gmax), 0.0), axis=-1, keepdims=True)
    lo = N_GROUPS + gsel * EXPERTS_PER_GROUP
    emask = (lane >= lo) & (lane < lo + EXPERTS_PER_GROUP)
    m1, i1 = masked_top(logits, emask)
    ex = jnp.where(emask, jnp.exp(logits - m1), 0.0)
    pf = ex / jnp.sum(ex, axis=-1, keepdims=True)
    p1, _ = masked_top(pf, emask)
    p2, i2 = masked_top(pf, emask & (lane != i1))
    denom = p1 + p2
    g1 = gp_top * p1 / denom
    g2 = gp_top * p2 / denom
    e1 = (i1 - N_GROUPS).astype(F32)
    e2 = (i2 - N_GROUPS).astype(F32)
    r_ref[...] = jnp.where(lane == 0, e1, jnp.where(lane == 1, e2, jnp.where(lane == 2, g1, jnp.where(lane == 3, g2, 0.0))))


def _outproj(o_f, o_b, proj, att, x2, gnw, w_a, w_b, fnw, w_r, tm):
    T, D = x2.shape
    row = lambda i: (i, 0)
    const = lambda i: (0, 0)
    return pl.pallas_call(
        _outproj_body,
        grid=(T // tm,),
        in_specs=[
            pl.BlockSpec((tm, GDN_W), row),
            pl.BlockSpec((tm, GDN_W), row),
            pl.BlockSpec((tm, GDN_W), lambda i: (i, COL_Z // GDN_W)),
            pl.BlockSpec((tm, ATT_Q), row),
            pl.BlockSpec((tm, D), row),
            pl.BlockSpec((1, GDN_D), const),
            pl.BlockSpec((GDN_W, D), const),
            pl.BlockSpec((ATT_Q, D), const),
            pl.BlockSpec((1, D), const),
            pl.BlockSpec((D, LANES), const),
        ],
        out_specs=[pl.BlockSpec((tm, D), row), pl.BlockSpec((tm, D), row), pl.BlockSpec((tm, LANES), row)],
        out_shape=[jax.ShapeDtypeStruct((T, D), F32), jax.ShapeDtypeStruct((T, D), F32),
                   jax.ShapeDtypeStruct((T, LANES), F32)],
        compiler_params=_cparams(("parallel",)),
        name="outproj_router",
    )(o_f, o_b, proj, att, x2, gnw, w_a, w_b, fnw, w_r)


def _row_copy(src_hbm, dst_buf, sem, tok, slot, r):
    return pltpu.make_async_copy(src_hbm.at[pl.ds(tok, 1)], dst_buf.at[slot, pl.ds(r, 1)], sem.at[slot])


def _experts_body(be_ref, src_ref, nused_ref, h_hbm, wg_ref, wu_ref, wd_ref, y_ref, xbuf, sem, *, bm):
    i = pl.program_id(0)
    n_used = nused_ref[0]
    slot = i % 2

    def issue(blk, slot_):
        def body(r, carry):
            _row_copy(h_hbm, xbuf, sem, src_ref[blk * bm + r], slot_, r).start()
            return carry
        lax.fori_loop(0, bm, body, 0)

    @pl.when((i == 0) & (n_used > 0))
    def _():
        issue(0, 0)

    @pl.when(i + 1 < n_used)
    def _():
        issue(i + 1, 1 - slot)

    @pl.when(i < n_used)
    def _():
        def wbody(r, carry):
            _row_copy(h_hbm, xbuf, sem, 0, slot, r).wait()
            return carry
        lax.fori_loop(0, bm, wbody, 0)
        x = xbuf[slot].astype(BF16)
        gate = jnp.dot(x, wg_ref[...].astype(BF16), preferred_element_type=F32)
        up = jnp.dot(x, wu_ref[...].astype(BF16), preferred_element_type=F32)
        hid = (_silu(gate) * up).astype(BF16)
        y_ref[...] = jnp.dot(hid, wd_ref[...].astype(BF16), preferred_element_type=F32)

    @pl.when(i >= n_used)
    def _():
        y_ref[...] = jnp.zeros_like(y_ref)


def _experts(block_expert, src_tok, n_used, h2, w_gate, w_up, w_down, bm):
    T, D = h2.shape
    P = src_tok.shape[0]
    FF = w_gate.shape[-1]
    kern = functools.partial(_experts_body, bm=bm)
    return pl.pallas_call(
        kern,
        grid_spec=pltpu.PrefetchScalarGridSpec(
            num_scalar_prefetch=3,
            grid=(P // bm,),
            in_specs=[
                pl.BlockSpec(memory_space=pl.ANY),
                pl.BlockSpec((None, D, FF), lambda i, be, st, nu: (be[i], 0, 0)),
                pl.BlockSpec((None, D, FF), lambda i, be, st, nu: (be[i], 0, 0)),
                pl.BlockSpec((None, FF, D), lambda i, be, st, nu: (be[i], 0, 0)),
            ],
            out_specs=pl.BlockSpec((bm, D), lambda i, be, st, nu: (i, 0)),
            scratch_shapes=[pltpu.VMEM((2, bm, D), F32), pltpu.SemaphoreType.DMA((2,))],
        ),
        out_shape=jax.ShapeDtypeStruct((P, D), F32),
        compiler_params=_cparams(("arbitrary",)),
        name="moe_experts",
    )(block_expert, src_tok, n_used, h2, w_gate, w_up, w_down)


def _combine_body(dest_ref, y_hbm, xm_ref, r_ref, fw_ref, o_ref, ybuf, sem, *, tc):
    i = pl.program_id(0)
    n = pl.num_programs(0)
    slot = i % 2
    rows = TOP_K * tc

    def issue(blk, slot_):
        def body(r, carry):
            _row_copy(y_hbm, ybuf, sem, dest_ref[blk * rows + r], slot_, r).start()
            return carry
        lax.fori_loop(0, rows, body, 0)

    @pl.when(i == 0)
    def _():
        issue(0, 0)

    @pl.when(i + 1 < n)
    def _():
        issue(i + 1, 1 - slot)

    def wbody(r, carry):
        _row_copy(y_hbm, ybuf, sem, 0, slot, r).wait()
        return carry
    lax.fori_loop(0, rows, wbody, 0)
    route = r_ref[...]
    y = ybuf[slot]
    xo = xm_ref[...] + route[:, 2:3] * y[:tc] + route[:, 3:4] * y[tc:]
    o_ref[...] = xo * lax.rsqrt(jnp.mean(xo * xo, axis=-1, keepdims=True) + EPS) * fw_ref[...]


def _combine(dest_blocked, y_buf, x_mid, route, final_w, tc):
    T, D = x_mid.shape
    kern = functools.partial(_combine_body, tc=tc)
    return pl.pallas_call(
        kern,
        grid_spec=pltpu.PrefetchScalarGridSpec(
            num_scalar_prefetch=1,
            grid=(T // tc,),
            in_specs=[
                pl.BlockSpec(memory_space=pl.ANY),
                pl.BlockSpec((tc, D), lambda i, d: (i, 0)),
                pl.BlockSpec((tc, LANES), lambda i, d: (i, 0)),
                pl.BlockSpec((1, D), lambda i, d: (0, 0)),
            ],
            out_specs=pl.BlockSpec((tc, D), lambda i, d: (i, 0)),
            scratch_shapes=[pltpu.VMEM((2, TOP_K * tc, D), F32), pltpu.SemaphoreType.DMA((2,))],
        ),
        out_shape=jax.ShapeDtypeStruct((T, D), F32),
        compiler_params=_cparams(("arbitrary",)),
        name="moe_combine",
    )(dest_blocked, y_buf, x_mid, route, final_w.reshape(1, D))


def _layer(x2, B, S, norm_mix_w, w_in, conv_w, a_log, dt_bias, gdn_norm_w, q_norm_w, k_norm_w, w_out,
           norm_ffn_w, w_router_group, w_router_expert, w_gate, w_up, w_down, final_w):
    T, D = x2.shape
    tl = _tiles(B, S)
    perm = np.concatenate([np.arange(0, ATT_D, 2), np.arange(1, ATT_D, 2)])
    o_z = 3 * GDN_W
    o_gate = o_z + GDN_W
    o_qb = o_gate + 2 * GDN_CHAINS
    o_kb = o_qb + ATT_Q
    o_vb = o_kb + ATT_KV
    qb_cols = o_qb + (np.arange(ATT_HEADS)[:, None] * ATT_D + perm[None, :]).reshape(-1)
    kb_cols = o_kb + (np.arange(ATT_KV_HEADS)[:, None] * ATT_D + perm[None, :]).reshape(-1)
    w_all = jnp.concatenate([
        w_in[:, :o_gate], w_in[:, qb_cols], w_in[:, kb_cols], w_in[:, o_vb:o_vb + ATT_KV],
        w_in[:, o_gate:o_qb], jnp.zeros((D, LANES - 2 * GDN_CHAINS), w_in.dtype)], axis=1).astype(BF16)

    proj = _inproj(x2, norm_mix_w, w_all, tl.proj_rows)

    conv_w8 = jnp.concatenate([conv_w, jnp.zeros((SUBLANES - CONV_W, conv_w.shape[1]), F32)], axis=0)
    gp = jnp.zeros((SUBLANES, LANES), F32)
    gp = gp.at[0, GDN_CHAINS:2 * GDN_CHAINS].set(jnp.exp(a_log.astype(F32)).reshape(-1))
    gp = gp.at[1, GDN_CHAINS:2 * GDN_CHAINS].set(dt_bias.astype(F32).reshape(-1))
    q_a, k_a, v_a, gates = _gdn_prep(proj, conv_w8, gp, B, S, tl.prep_rows)
    n_chunks = T // CHUNK
    gates_t = gates[:, :2 * GDN_CHAINS].reshape(n_chunks, CHUNK, 2 * GDN_CHAINS).transpose(0, 2, 1)
    u, wq, ak, dec = _gdn_chunk(q_a, k_a, v_a, gates, gates_t, tl.gdn_chunks)
    o_f, o_b = _gdn_scan(u, wq, ak, dec, B, S, tl.scan_chunks)

    rows = S // GRID_W
    rowp = jnp.repeat(jnp.arange(rows), GRID_W).astype(F32)
    colp = jnp.tile(jnp.arange(GRID_W), rows).astype(F32)
    axis_dims = ATT_D // 2
    inv_freq = ROPE_THETA ** (-jnp.arange(0, axis_dims, 2, dtype=F32) / axis_dims)
    ang = jnp.concatenate([rowp[:, None] * inv_freq, colp[:, None] * inv_freq], axis=-1)
    cosf = jnp.concatenate([jnp.cos(ang), jnp.cos(ang)], axis=-1)
    sinf = jnp.concatenate([-jnp.sin(ang), jnp.sin(ang)], axis=-1)
    q_gain = jnp.max(jnp.abs(q_norm_w)).astype(F32)
    k_gain = jnp.max(jnp.abs(k_norm_w)).astype(F32)
    kmax = (math.sqrt(ATT_D) * k_gain).reshape(1, 1)
    score_bound = ATT_D ** 0.5 * q_gain * k_gain
    safe = (2.0 * score_bound <= SOFTMAX_SAFE_SPAN).astype(jnp.int32).reshape(1)
    qh, kh, vh = _attn_prep(proj, q_norm_w[perm].reshape(1, ATT_D), k_norm_w[perm].reshape(1, ATT_D),
                            cosf, sinf, kmax, B, S, tl.prep_rows)
    att = _attention(safe, qh, kh, vh, B, S, tl.att_q, tl.att_k).reshape(T, ATT_Q)

    w_r = jnp.concatenate([w_router_group, w_router_expert,
                           jnp.zeros((D, LANES - N_GROUPS - N_EXPERTS), F32)], axis=1)
    w_out_bf = w_out.astype(BF16)
    x_mid, h2, route = _outproj(o_f, o_b, proj, att, x2, gdn_norm_w.reshape(1, GDN_D),
                                w_out_bf[:GDN_W], w_out_bf[GDN_W:], norm_ffn_w.reshape(1, D), w_r,
                                tl.proj_rows)

    bm = tl.moe_rows
    n_assign = T * TOP_K
    n_blocks = -(-(n_assign + N_EXPERTS * (bm - 1)) // bm)
    e_flat = route[:, :TOP_K].astype(jnp.int32).reshape(-1)
    onehot = (e_flat[:, None] == jnp.arange(N_EXPERTS)[None, :]).astype(jnp.int32)
    csum = jnp.cumsum(onehot, axis=0)
    rank = jnp.sum(csum * onehot, axis=1) - 1
    counts = csum[-1]
    padded = (counts + bm - 1) // bm * bm
    pad_end = jnp.cumsum(padded)
    pad_start = pad_end - padded
    dest = pad_start[e_flat] + rank
    src_tok = jnp.zeros((n_blocks * bm,), jnp.int32).at[dest].set(jnp.arange(n_assign, dtype=jnp.int32) // TOP_K)
    block_start = jnp.arange(n_blocks, dtype=jnp.int32) * bm
    block_expert = jnp.minimum(jnp.sum((pad_end[None, :] <= block_start[:, None]).astype(jnp.int32), axis=1),
                               N_EXPERTS - 1)
    n_used = (pad_end[-1:] // bm).astype(jnp.int32)

    y_buf = _experts(block_expert, src_tok, n_used, h2, w_gate, w_up, w_down, bm)

    tc = tl.comb_rows
    dest_blocked = dest.reshape(T // tc, tc, TOP_K).transpose(0, 2, 1).reshape(-1)
    return _combine(dest_blocked, y_buf, x_mid, route, final_w, tc)


def kernel(x, norm_mix_w, w_in, conv_w, a_log, dt_bias, gdn_norm_w, q_norm_w, k_norm_w, w_out, norm_ffn_w,
           w_router_group, w_router_expert, w_gate, w_up, w_down, final_norm_w):
    B, S, D = x.shape
    depth = w_in.shape[0]
    assert depth == 1, "the final norm is fused into the last (only) layer's combine step"
    out = _layer(x.reshape(B * S, D), B, S, norm_mix_w[0], w_in[0], conv_w[0], a_log[0], dt_bias[0],
                 gdn_norm_w[0], q_norm_w[0], k_norm_w[0], w_out[0], norm_ffn_w[0], w_router_group[0],
                 w_router_expert[0], w_gate[0], w_up[0], w_down[0], final_norm_w)
    return out.reshape(B, S, D)
```

```python
import functools
import math
from typing import NamedTuple

import jax
import jax.numpy as jnp
import numpy as np
from jax import lax
from jax.experimental import pallas as pl
from jax.experimental.pallas import tpu as pltpu

F32 = jnp.float32
BF16 = jnp.bfloat16
EPS = 1e-6

GRID_W = 64
GDN_HEADS = 4
GDN_D = 128
CONV_W = 5
CHUNK = 64
ATT_HEADS = 8
ATT_KV_HEADS = 2
ATT_GROUP = ATT_HEADS // ATT_KV_HEADS
ATT_D = 64
ROPE_THETA = 10000.0
N_GROUPS = 4
EXPERTS_PER_GROUP = 8
N_EXPERTS = N_GROUPS * EXPERTS_PER_GROUP
TOP_K = 2

GDN_W = GDN_HEADS * GDN_D
GDN_CHAINS = 2 * GDN_HEADS
ATT_Q = ATT_HEADS * ATT_D
ATT_KV = ATT_KV_HEADS * ATT_D
LANES = 128
SUBLANES = 8
ATT_DP = LANES

COL_Z = 3 * GDN_W
COL_QB = COL_Z + GDN_W
COL_KB = COL_QB + ATT_Q
COL_VB = COL_KB + ATT_KV
COL_GATE = COL_VB + ATT_KV
D_PROJ = COL_GATE + LANES

VMEM_LIMIT = 56 * 1024 * 1024
LOG2E = math.log2(math.e)
SOFTMAX_SAFE_SPAN = 60.0


class Tiles(NamedTuple):
    proj_rows: int
    prep_rows: int
    gdn_chunks: int
    scan_chunks: int
    att_q: int
    att_k: int
    moe_rows: int
    comb_rows: int


def _tile(n, want):
    t = min(n, want)
    assert n % t == 0, (n, want)
    return t


def _tiles(B, S):
    T = B * S
    n_chunks = S // CHUNK
    return Tiles(proj_rows=_tile(T, 512), prep_rows=_tile(S, 512), gdn_chunks=_tile(n_chunks, 2),
                 scan_chunks=_tile(n_chunks, 4), att_q=_tile(S, 256), att_k=_tile(S, 1024),
                 moe_rows=256, comb_rows=_tile(T, 256))


def _cparams(sem):
    return pltpu.CompilerParams(dimension_semantics=sem, vmem_limit_bytes=VMEM_LIMIT)


def _silu(x):
    return x * jax.nn.sigmoid(x)


ROW_TILE = SUBLANES


def _store_token_tiles(ref, x, base=0):
    rows, d = x.shape
    assert d == ROW_TILE * LANES
    for c in range(ROW_TILE):
        ref[pl.ds(base + c, rows, stride=ROW_TILE), :] = x[:, c * LANES:(c + 1) * LANES]


def _load_token_tiles(ref, base, rows):
    return jnp.concatenate([ref[pl.ds(base + c, rows, stride=ROW_TILE), :] for c in range(ROW_TILE)], axis=-1)


def _inproj_body(x_ref, nw_ref, w_ref, o_ref):
    x = x_ref[...]
    h = x * lax.rsqrt(jnp.mean(x * x, axis=-1, keepdims=True) + EPS) * nw_ref[...]
    o_ref[...] = jnp.dot(h.astype(BF16), w_ref[...], preferred_element_type=F32)


def _inproj(x2, norm_w, w_all, tm):
    T, D = x2.shape
    return pl.pallas_call(
        _inproj_body,
        grid=(T // tm,),
        in_specs=[
            pl.BlockSpec((tm, D), lambda i: (i, 0)),
            pl.BlockSpec((1, D), lambda i: (0, 0)),
            pl.BlockSpec((D, D_PROJ), lambda i: (0, 0)),
        ],
        out_specs=pl.BlockSpec((tm, D_PROJ), lambda i: (i, 0)),
        out_shape=jax.ShapeDtypeStruct((T, D_PROJ), F32),
        compiler_params=_cparams(("parallel",)),
        name="inproj",
    )(x2, norm_w.reshape(1, D), w_all)


def _gdn_prep_body(cur_ref, prev_ref, next_ref, cw_ref, gin_ref, gp_ref,
                   q_ref, k_ref, v_ref, g_ref, ext_ref, *, tr):
    i = pl.program_id(1)
    nr = pl.num_programs(1)
    halo = SUBLANES
    pad = CONV_W // 2
    ext_ref[0:halo, :] = jnp.where(i > 0, prev_ref[...], 0.0)
    ext_ref[halo:halo + tr, :] = cur_ref[...]
    ext_ref[halo + tr:2 * halo + tr, :] = jnp.where(i < nr - 1, next_ref[...], 0.0)
    acc = cw_ref[0:1, :] * ext_ref[pl.ds(halo - pad, tr), :]
    for j in range(1, CONV_W):
        acc = acc + cw_ref[j:j + 1, :] * ext_ref[pl.ds(halo - pad + j, tr), :]
    y = _silu(acc)
    for h in range(GDN_HEADS):
        for base, ref, scale in ((0, q_ref, GDN_D ** -0.5), (GDN_W, k_ref, 1.0)):
            t = y[:, base + h * GDN_D: base + (h + 1) * GDN_D]
            t = t * (lax.rsqrt(jnp.sum(t * t, axis=-1, keepdims=True) + EPS) * scale)
            ref[:, h * GDN_D:(h + 1) * GDN_D] = t
    v_ref[...] = y[:, 2 * GDN_W:]
    gin = gin_ref[...]
    lane = lax.broadcasted_iota(jnp.int32, gin.shape, 1)
    a = gin + gp_ref[1:2, :]
    softplus = jnp.maximum(a, 0.0) + jnp.log1p(jnp.exp(-jnp.abs(a)))
    g_ref[...] = jnp.where(lane < GDN_CHAINS, jax.nn.sigmoid(gin), -gp_ref[0:1, :] * softplus)


def _gdn_prep(proj, conv_w8, gate_params, B, S, tr):
    T = B * S
    nr = S // tr
    C = 3 * GDN_W
    rb = tr // SUBLANES
    nrow8 = T // SUBLANES
    kern = functools.partial(_gdn_prep_body, tr=tr)
    out_sd = jax.ShapeDtypeStruct((T, GDN_W), F32)
    return pl.pallas_call(
        kern,
        grid=(B, nr),
        in_specs=[
            pl.BlockSpec((tr, C), lambda b, i: (b * nr + i, 0)),
            pl.BlockSpec((SUBLANES, C), lambda b, i: (jnp.maximum((b * nr + i) * rb - 1, 0), 0)),
            pl.BlockSpec((SUBLANES, C), lambda b, i: (jnp.minimum((b * nr + i + 1) * rb, nrow8 - 1), 0)),
            pl.BlockSpec((SUBLANES, C), lambda b, i: (0, 0)),
            pl.BlockSpec((tr, LANES), lambda b, i: (b * nr + i, COL_GATE // LANES)),
            pl.BlockSpec((SUBLANES, LANES), lambda b, i: (0, 0)),
        ],
        out_specs=[
            pl.BlockSpec((tr, GDN_W), lambda b, i: (b * nr + i, 0)),
            pl.BlockSpec((tr, GDN_W), lambda b, i: (b * nr + i, 0)),
            pl.BlockSpec((tr, GDN_W), lambda b, i: (b * nr + i, 0)),
            pl.BlockSpec((tr, LANES), lambda b, i: (b * nr + i, 0)),
        ],
        out_shape=[out_sd, out_sd, out_sd, jax.ShapeDtypeStruct((T, LANES), F32)],
        scratch_shapes=[pltpu.VMEM((tr + 2 * SUBLANES, C), F32)],
        compiler_params=_cparams(("parallel", "parallel")),
        name="gdn_prep",
    )(proj, proj, proj, conv_w8, proj, gate_params)


def _bdot(a, b):
    return jnp.dot(a.astype(BF16), b.astype(BF16), preferred_element_type=F32)


def _bdot_nt(a, b):
    return lax.dot_general(a.astype(BF16), b.astype(BF16), (((1,), (1,)), ((), ())),
                           preferred_element_type=F32)


def _gdn_chunk_body(q_ref, k_ref, v_ref, g_ref, gt_ref, u_ref, wq_ref, ak_ref, dec_ref, *, cb):
    C = CHUNK
    row = lax.broadcasted_iota(jnp.int32, (C, C), 0)
    col = lax.broadcasted_iota(jnp.int32, (C, C), 1)
    eye = (row == col).astype(F32)
    masks = ((row >= col, row > col), (row <= col, row < col))
    hi = lax.Precision.HIGHEST
    for c in range(cb):
        rs = slice(c * C, (c + 1) * C)
        gates = g_ref[rs, :]
        gates_t = gt_ref[c]
        g_tot = jnp.sum(gates, axis=0, keepdims=True)
        tot_rows = jnp.sum(gates_t, axis=1, keepdims=True)
        dec_ref[c] = jnp.broadcast_to(jnp.exp(tot_rows[GDN_CHAINS:2 * GDN_CHAINS]), (GDN_CHAINS, LANES))
        chains = []
        for d in range(2):
            incl, strict = masks[d]
            cum = incl.astype(F32)
            gc_cols = jnp.dot(cum, gates, precision=hi, preferred_element_type=F32)
            gc_rows = lax.dot_general(gates_t, cum, (((1,), (1,)), ((), ())), precision=hi,
                                      preferred_element_type=F32)
            for h in range(GDN_HEADS):
                lane_b = d * GDN_HEADS + h
                lane_g = GDN_CHAINS + lane_b
                sl = slice(h * GDN_D, (h + 1) * GDN_D)
                q = q_ref[rs, sl]
                k = k_ref[rs, sl]
                v = v_ref[rs, sl]
                beta = gates[:, lane_b:lane_b + 1]
                gc_col = gc_cols[:, lane_g:lane_g + 1]
                gc_row = gc_rows[lane_g:lane_g + 1, :]
                g_last = g_tot[:, lane_g:lane_g + 1]
                decay = jnp.where(incl, jnp.exp(jnp.where(incl, gc_col - gc_row, 0.0)), 0.0)
                e_col = jnp.exp(gc_col)
                k_beta = k * beta
                kk = _bdot_nt(jnp.concatenate([k_beta, q], axis=0), k)
                neg_l = jnp.where(strict, -kk[:C] * decay, 0.0)
                attn = jnp.where(incl, kk[C:] * decay, 0.0)
                rhs = jnp.concatenate([v * beta, k_beta * e_col], axis=1).astype(BF16)
                wq_ref[d, c, C:2 * C, sl] = (q * e_col).astype(BF16)
                ak_ref[d, c, h, 0:C, :] = attn.astype(BF16)
                ak_ref[d, c, h, C:C + GDN_D, :] = (k * jnp.exp(g_last - gc_col)).T.astype(BF16)
                chains.append((d, sl, neg_l, rhs))
        pws = [nl for (_, _, nl, _) in chains]
        invs = [eye + nl for nl in pws]
        for _ in range(int(math.log2(C)) - 1):
            pws = [_bdot(p, p) for p in pws]
            invs = [t + _bdot(t, p) for t, p in zip(invs, pws)]
        for (d, sl, _, rhs), inv in zip(chains, invs):
            sol = jnp.dot(inv.astype(BF16), rhs, preferred_element_type=F32)
            u_ref[d, rs, sl] = sol[:, :GDN_D]
            wq_ref[d, c, 0:C, sl] = sol[:, GDN_D:].astype(BF16)


def _gdn_chunk(q, k, v, gates, gates_t, cb):
    T = q.shape[0]
    nc = T // CHUNK
    rows = cb * CHUNK
    kern = functools.partial(_gdn_chunk_body, cb=cb)
    wide = pl.BlockSpec((rows, GDN_W), lambda i: (i, 0))
    return pl.pallas_call(
        kern,
        grid=(nc // cb,),
        in_specs=[wide, wide, wide, pl.BlockSpec((rows, LANES), lambda i: (i, 0)),
                  pl.BlockSpec((cb, 2 * GDN_CHAINS, CHUNK), lambda i: (i, 0, 0))],
        out_specs=[
            pl.BlockSpec((2, rows, GDN_W), lambda i: (0, i, 0)),
            pl.BlockSpec((2, cb, 2 * CHUNK, GDN_W), lambda i: (0, i, 0, 0)),
            pl.BlockSpec((2, cb, GDN_HEADS, CHUNK + GDN_D, CHUNK), lambda i: (0, i, 0, 0, 0)),
            pl.BlockSpec((cb, GDN_CHAINS, LANES), lambda i: (i, 0, 0)),
        ],
        out_shape=[
            jax.ShapeDtypeStruct((2, T, GDN_W), F32),
            jax.ShapeDtypeStruct((2, nc, 2 * CHUNK, GDN_W), BF16),
            jax.ShapeDtypeStruct((2, nc, GDN_HEADS, CHUNK + GDN_D, CHUNK), BF16),
            jax.ShapeDtypeStruct((nc, GDN_CHAINS, LANES), F32),
        ],
        compiler_params=_cparams(("parallel",)),
        name="gdn_chunk",
    )(q, k, v, gates, gates_t)


def _gdn_scan_body(uf_ref, wqf_ref, akf_ref, decf_ref, ub_ref, wqb_ref, akb_ref, decb_ref,
                   of_ref, ob_ref, state_ref, *, cs):
    @pl.when(pl.program_id(1) == 0)
    def _():
        state_ref[...] = jnp.zeros_like(state_ref)

    C = CHUNK
    refs = ((uf_ref, wqf_ref, akf_ref, decf_ref, of_ref), (ub_ref, wqb_ref, akb_ref, decb_ref, ob_ref))
    chains = [(d, h) for d in range(2) for h in range(GDN_HEADS)]
    st = [state_ref[j] for j in range(GDN_CHAINS)]
    for step in range(cs):
        cidx = (step, cs - 1 - step)
        ws = [jnp.dot(refs[d][1][cidx[d], :, h * GDN_D:(h + 1) * GDN_D], st[j].astype(BF16),
                      preferred_element_type=F32) for j, (d, h) in enumerate(chains)]
        vn = [(refs[d][0][cidx[d] * C:(cidx[d] + 1) * C, h * GDN_D:(h + 1) * GDN_D] - ws[j][:C]).astype(BF16)
              for j, (d, h) in enumerate(chains)]
        rr = [jnp.dot(refs[d][2][cidx[d], h], vn[j], preferred_element_type=F32)
              for j, (d, h) in enumerate(chains)]
        for j, (d, h) in enumerate(chains):
            c = cidx[d]
            refs[d][4][c * C:(c + 1) * C, h * GDN_D:(h + 1) * GDN_D] = ws[j][C:] + rr[j][:C]
            st[j] = st[j] * refs[d][3][c, j:j + 1, :] + rr[j][C:]
    for j in range(GDN_CHAINS):
        state_ref[j] = st[j]


def _gdn_scan(u, wq, ak, dec, B, S, cs):
    T = B * S
    nb = S // (CHUNK * cs)
    rows = cs * CHUNK
    kern = functools.partial(_gdn_scan_body, cs=cs)
    fwd = lambda b, i: b * nb + i
    bwd = lambda b, i: b * nb + nb - 1 - i

    def specs(d, pos):
        return [
            pl.BlockSpec((None, rows, GDN_W), lambda b, i: (d, pos(b, i), 0)),
            pl.BlockSpec((None, cs, 2 * CHUNK, GDN_W), lambda b, i: (d, pos(b, i), 0, 0)),
            pl.BlockSpec((None, cs, GDN_HEADS, CHUNK + GDN_D, CHUNK), lambda b, i: (d, pos(b, i), 0, 0, 0)),
            pl.BlockSpec((cs, GDN_CHAINS, LANES), lambda b, i: (pos(b, i), 0, 0)),
        ]

    out_sd = jax.ShapeDtypeStruct((T, GDN_W), F32)
    return pl.pallas_call(
        kern,
        grid=(B, nb),
        in_specs=specs(0, fwd) + specs(1, bwd),
        out_specs=[pl.BlockSpec((rows, GDN_W), lambda b, i: (fwd(b, i), 0)),
                   pl.BlockSpec((rows, GDN_W), lambda b, i: (bwd(b, i), 0))],
        out_shape=[out_sd, out_sd],
        scratch_shapes=[pltpu.VMEM((GDN_CHAINS, GDN_D, GDN_D), F32)],
        compiler_params=_cparams(("parallel", "arbitrary")),
        name="gdn_scan",
    )(u, wq, ak, dec, u, wq, ak, dec)


def _rope_norm(x, w, cos, sin):
    y = x * lax.rsqrt(jnp.mean(x * x, axis=-1, keepdims=True) + EPS) * w
    half = ATT_D // 2
    swapped = jnp.concatenate([y[:, half:], y[:, :half]], axis=-1)
    return y * cos + swapped * sin


def _attn_prep_body(q_ref, k_ref, v_ref, qw_ref, kw_ref, cos_ref, sin_ref, kmax_ref,
                    qo_ref, ko_ref, vo_ref):
    cos = cos_ref[...]
    sin = sin_ref[...]
    q = q_ref[...]
    k = k_ref[...]
    v = v_ref[...]
    rows = q.shape[0]
    lane = lax.broadcasted_iota(jnp.int32, (rows, ATT_DP - ATT_D), 1)
    one_col = jnp.where(lane == 0, 1.0, 0.0)
    scale = LOG2E * ATT_D ** -0.5
    for h in range(ATT_HEADS):
        r = _rope_norm(q[:, h * ATT_D:(h + 1) * ATT_D], qw_ref[...], cos, sin) * scale
        bound = jnp.sqrt(jnp.sum(r * r, axis=-1, keepdims=True)) * kmax_ref[...]
        qo_ref[h] = jnp.concatenate([r, -bound * one_col], axis=-1).astype(BF16)
    for h in range(ATT_KV_HEADS):
        r = _rope_norm(k[:, h * ATT_D:(h + 1) * ATT_D], kw_ref[...], cos, sin)
        ko_ref[h] = jnp.concatenate([r, one_col], axis=-1).astype(BF16)
        vo_ref[h] = jnp.concatenate([v[:, h * ATT_D:(h + 1) * ATT_D], one_col], axis=-1).astype(BF16)


def _attn_prep(proj, qw, kw, cosf, sinf, kmax, B, S, tr):
    nr = S // tr
    row = lambda b, i: b * nr + i
    return pl.pallas_call(
        _attn_prep_body,
        grid=(B, nr),
        in_specs=[
            pl.BlockSpec((tr, ATT_Q), lambda b, i: (row(b, i), COL_QB // ATT_Q)),
            pl.BlockSpec((tr, ATT_KV), lambda b, i: (row(b, i), COL_KB // ATT_KV)),
            pl.BlockSpec((tr, ATT_KV), lambda b, i: (row(b, i), COL_VB // ATT_KV)),
            pl.BlockSpec((1, ATT_D), lambda b, i: (0, 0)),
            pl.BlockSpec((1, ATT_D), lambda b, i: (0, 0)),
            pl.BlockSpec((tr, ATT_D), lambda b, i: (i, 0)),
            pl.BlockSpec((tr, ATT_D), lambda b, i: (i, 0)),
            pl.BlockSpec((1, 1), lambda b, i: (0, 0)),
        ],
        out_specs=[
            pl.BlockSpec((None, ATT_HEADS, tr, ATT_DP), lambda b, i: (b, 0, i, 0)),
            pl.BlockSpec((None, ATT_KV_HEADS, tr, ATT_DP), lambda b, i: (b, 0, i, 0)),
            pl.BlockSpec((None, ATT_KV_HEADS, tr, ATT_DP), lambda b, i: (b, 0, i, 0)),
        ],
        out_shape=[
            jax.ShapeDtypeStruct((B, ATT_HEADS, S, ATT_DP), BF16),
            jax.ShapeDtypeStruct((B, ATT_KV_HEADS, S, ATT_DP), BF16),
            jax.ShapeDtypeStruct((B, ATT_KV_HEADS, S, ATT_DP), BF16),
        ],
        compiler_params=_cparams(("parallel", "parallel")),
        name="attn_prep",
    )(proj, proj, proj, qw, kw, cosf, sinf, kmax)


def _attn_body(safe_ref, q_ref, k_ref, v_ref, o_ref, m_ref, acc_ref, *, tq):
    j = pl.program_id(3)
    safe = safe_ref[0] != 0

    @pl.when(j == 0)
    def _():
        m_ref[...] = jnp.full_like(m_ref, -jnp.inf)
        acc_ref[...] = jnp.zeros_like(acc_ref)

    q = q_ref[...].reshape(ATT_GROUP * tq, ATT_DP)

    @pl.when(safe)
    def _():
        s = lax.dot_general(q, k_ref[...], (((1,), (1,)), ((), ())), preferred_element_type=F32)
        acc_ref[...] += jnp.dot(jnp.exp2(s).astype(BF16), v_ref[...], preferred_element_type=F32)

    @pl.when(jnp.logical_not(safe))
    def _():
        s = lax.dot_general(q, k_ref[...], (((1,), (1,)), ((), ())), preferred_element_type=F32)
        m_prev = m_ref[...]
        m_new = jnp.maximum(m_prev, jnp.max(s, axis=-1, keepdims=True))
        p = jnp.exp2(s - m_new).astype(BF16)
        acc_ref[...] = jnp.exp2(m_prev - m_new) * acc_ref[...] + jnp.dot(p, v_ref[...], preferred_element_type=F32)
        m_ref[...] = m_new

    @pl.when(j == pl.num_programs(3) - 1)
    def _():
        acc = acc_ref[...]
        o = acc[:, :ATT_D] / acc[:, ATT_D:ATT_D + 1]
        for h in range(ATT_GROUP):
            o_ref[:, h * ATT_D:(h + 1) * ATT_D] = o[h * tq:(h + 1) * tq].astype(o_ref.dtype)


def _attention(safe, q, k, v, B, S, tq, tk):
    kern = functools.partial(_attn_body, tq=tq)
    gw = ATT_GROUP * ATT_D
    return pl.pallas_call(
        kern,
        grid_spec=pltpu.PrefetchScalarGridSpec(
            num_scalar_prefetch=1,
            grid=(B, ATT_KV_HEADS, S // tq, S // tk),
            in_specs=[
                pl.BlockSpec((None, ATT_GROUP, tq, ATT_DP), lambda b, g, i, j, s: (b, g, i, 0)),
                pl.BlockSpec((None, None, tk, ATT_DP), lambda b, g, i, j, s: (b, g, j, 0)),
                pl.BlockSpec((None, None, tk, ATT_DP), lambda b, g, i, j, s: (b, g, j, 0)),
            ],
            out_specs=pl.BlockSpec((None, tq, gw), lambda b, g, i, j, s: (b, i, g)),
            scratch_shapes=[
                pltpu.VMEM((ATT_GROUP * tq, 1), F32),
                pltpu.VMEM((ATT_GROUP * tq, ATT_DP), F32),
            ],
        ),
        out_shape=jax.ShapeDtypeStruct((B, S, ATT_Q), BF16),
        compiler_params=_cparams(("parallel", "parallel", "parallel", "arbitrary")),
        name="attention",
    )(safe, q, k, v)


def _outproj_body(of_ref, ob_ref, z_ref, att_ref, x_ref, gnw_ref, wa_ref, wb_ref,
                  fnw_ref, wr_ref, xm_ref, h_ref, r_ref):
    o = of_ref[...] + ob_ref[...]
    z = z_ref[...]
    parts = []
    for h in range(GDN_HEADS):
        sl = slice(h * GDN_D, (h + 1) * GDN_D)
        t = o[:, sl]
        t = t * lax.rsqrt(jnp.mean(t * t, axis=-1, keepdims=True) + EPS) * gnw_ref[...]
        parts.append((t * _silu(z[:, sl])).astype(BF16))
    mix_a = jnp.concatenate(parts, axis=-1)
    xm = x_ref[...] + jnp.dot(mix_a, wa_ref[...], preferred_element_type=F32)
    xm = xm + jnp.dot(att_ref[...], wb_ref[...], preferred_element_type=F32)
    xm_ref[...] = xm
    hn = xm * lax.rsqrt(jnp.mean(xm * xm, axis=-1, keepdims=True) + EPS) * fnw_ref[...]
    _store_token_tiles(h_ref, hn)
    logits = jnp.dot(hn, wr_ref[...], precision=lax.Precision.HIGHEST, preferred_element_type=F32)

    lane = lax.broadcasted_iota(jnp.int32, logits.shape, 1)
    big = jnp.int32(LANES)
    neg = -jnp.inf

    def masked_top(vals, mask):
        m = jnp.max(jnp.where(mask, vals, neg), axis=-1, keepdims=True)
        idx = jnp.min(jnp.where(mask & (vals == m), lane, big), axis=-1, keepdims=True)
        return m, idx

    gmask = lane < N_GROUPS
    gmax, gsel = masked_top(logits, gmask)
    gp_top = 1.0 / jnp.sum(jnp.where(gmask, jnp.exp(logits - gmax), 0.0), axis=-1, keepdims=True)
    lo = N_GROUPS + gsel * EXPERTS_PER_GROUP
    emask = (lane >= lo) & (lane < lo + EXPERTS_PER_GROUP)
    m1, i1 = masked_top(logits, emask)
    ex = jnp.where(emask, jnp.exp(logits - m1), 0.0)
    pf = ex / jnp.sum(ex, axis=-1, keepdims=True)
    p1, _ = masked_top(pf, emask)
    p2, i2 = masked_top(pf, emask & (lane != i1))
    denom = p1 + p2
    g1 = gp_top * p1 / denom
    g2 = gp_top * p2 / denom
    e1 = (i1 - N_GROUPS).astype(F32)
    e2 = (i2 - N_GROUPS).astype(F32)
    r_ref[...] = jnp.where(lane == 0, e1, jnp.where(lane == 1, e2, jnp.where(lane == 2, g1, jnp.where(lane == 3, g2, 0.0))))


def _outproj(o_f, o_b, proj, att, x2, gnw, w_a, w_b, fnw, w_r, tm):
    T, D = x2.shape
    row = lambda i: (i, 0)
    const = lambda i: (0, 0)
    return pl.pallas_call(
        _outproj_body,
        grid=(T // tm,),
        in_specs=[
            pl.BlockSpec((tm, GDN_W), row),
            pl.BlockSpec((tm, GDN_W), row),
            pl.BlockSpec((tm, GDN_W), lambda i: (i, COL_Z // GDN_W)),
            pl.BlockSpec((tm, ATT_Q), row),
            pl.BlockSpec((tm, D), row),
            pl.BlockSpec((1, GDN_D), const),
            pl.BlockSpec((GDN_W, D), const),
            pl.BlockSpec((ATT_Q, D), const),
            pl.BlockSpec((1, D), const),
            pl.BlockSpec((D, LANES), const),
        ],
        out_specs=[pl.BlockSpec((tm, D), row), pl.BlockSpec((tm * ROW_TILE, LANES), row),
                   pl.BlockSpec((tm, LANES), row)],
        out_shape=[jax.ShapeDtypeStruct((T, D), F32), jax.ShapeDtypeStruct((T * ROW_TILE, LANES), F32),
                   jax.ShapeDtypeStruct((T, LANES), F32)],
        compiler_params=_cparams(("parallel",)),
        name="outproj_router",
    )(o_f, o_b, proj, att, x2, gnw, w_a, w_b, fnw, w_r)


def _gather_start(src_hbm, dst_buf, sem, idx_ref, base, slot, n):
    for r in range(n):
        src = pl.multiple_of(idx_ref[base + r] * ROW_TILE, ROW_TILE)
        dst = pl.multiple_of((slot * n + r) * ROW_TILE, ROW_TILE)
        pltpu.make_async_copy(src_hbm.at[pl.ds(src, ROW_TILE)], dst_buf.at[pl.ds(dst, ROW_TILE)],
                              sem.at[slot]).start()


def _gather_wait(src_hbm, dst_buf, sem, slot, n):
    dst = pl.multiple_of(slot * n * ROW_TILE, ROW_TILE)
    pltpu.make_async_copy(src_hbm.at[pl.ds(0, n * ROW_TILE)], dst_buf.at[pl.ds(dst, n * ROW_TILE)],
                          sem.at[slot]).wait()


def _experts_body(be_ref, src_ref, nused_ref, h_hbm, wg_ref, wu_ref, wd_ref, y_ref, xbuf, sem, *, bm):
    i = pl.program_id(0)
    n_used = nused_ref[0]
    slot = i % 2

    def compute(prefetch_next):
        _gather_wait(h_hbm, xbuf, sem, slot, bm)
        x = _load_token_tiles(xbuf, slot * (bm * ROW_TILE), bm).astype(BF16)
        if prefetch_next:
            _gather_start(h_hbm, xbuf, sem, src_ref, (i + 1) * bm, 1 - slot, bm)
        gate = jnp.dot(x, wg_ref[...].astype(BF16), preferred_element_type=F32)
        up = jnp.dot(x, wu_ref[...].astype(BF16), preferred_element_type=F32)
        hid = (_silu(gate) * up).astype(BF16)
        _store_token_tiles(y_ref, jnp.dot(hid, wd_ref[...].astype(BF16), preferred_element_type=F32))

    @pl.when((i == 0) & (n_used > 0))
    def _():
        _gather_start(h_hbm, xbuf, sem, src_ref, 0, 0, bm)

    @pl.when(i + 1 < n_used)
    def _():
        compute(True)

    @pl.when(i + 1 == n_used)
    def _():
        compute(False)

    @pl.when(i >= n_used)
    def _():
        y_ref[...] = jnp.zeros_like(y_ref)


def _experts(block_expert, src_tok, n_used, h2, w_gate, w_up, w_down, bm):
    D = ROW_TILE * LANES
    P = src_tok.shape[0]
    FF = w_gate.shape[-1]
    kern = functools.partial(_experts_body, bm=bm)
    return pl.pallas_call(
        kern,
        grid_spec=pltpu.PrefetchScalarGridSpec(
            num_scalar_prefetch=3,
            grid=(P // bm,),
            in_specs=[
                pl.BlockSpec(memory_space=pl.ANY),
                pl.BlockSpec((None, D, FF), lambda i, be, st, nu: (be[i], 0, 0)),
                pl.BlockSpec((None, D, FF), lambda i, be, st, nu: (be[i], 0, 0)),
                pl.BlockSpec((None, FF, D), lambda i, be, st, nu: (be[i], 0, 0)),
            ],
            out_specs=pl.BlockSpec((bm * ROW_TILE, LANES), lambda i, be, st, nu: (i, 0)),
            scratch_shapes=[pltpu.VMEM((2 * bm * ROW_TILE, LANES), F32), pltpu.SemaphoreType.DMA((2,))],
        ),
        out_shape=jax.ShapeDtypeStruct((P * ROW_TILE, LANES), F32),
        compiler_params=_cparams(("arbitrary",)),
        name="moe_experts",
    )(block_expert, src_tok, n_used, h2, w_gate, w_up, w_down)


def _combine_body(dest_ref, y_hbm, xm_ref, r_ref, fw_ref, o_ref, ybuf, sem, *, tc):
    i = pl.program_id(0)
    n = pl.num_programs(0)
    slot = i % 2
    rows = TOP_K * tc

    def compute(prefetch_next):
        _gather_wait(y_hbm, ybuf, sem, slot, rows)
        route = r_ref[...]
        y0 = _load_token_tiles(ybuf, slot * (rows * ROW_TILE), tc)
        y1 = _load_token_tiles(ybuf, (slot * rows + tc) * ROW_TILE, tc)
        if prefetch_next:
            _gather_start(y_hbm, ybuf, sem, dest_ref, (i + 1) * rows, 1 - slot, rows)
        xo = xm_ref[...] + route[:, 2:3] * y0 + route[:, 3:4] * y1
        o_ref[...] = xo * lax.rsqrt(jnp.mean(xo * xo, axis=-1, keepdims=True) + EPS) * fw_ref[...]

    @pl.when(i == 0)
    def _():
        _gather_start(y_hbm, ybuf, sem, dest_ref, 0, 0, rows)

    @pl.when(i + 1 < n)
    def _():
        compute(True)

    @pl.when(i + 1 == n)
    def _():
        compute(False)


def _combine(dest_blocked, y_buf, x_mid, route, final_w, tc):
    T, D = x_mid.shape
    kern = functools.partial(_combine_body, tc=tc)
    return pl.pallas_call(
        kern,
        grid_spec=pltpu.PrefetchScalarGridSpec(
            num_scalar_prefetch=1,
            grid=(T // tc,),
            in_specs=[
                pl.BlockSpec(memory_space=pl.ANY),
                pl.BlockSpec((tc, D), lambda i, d: (i, 0)),
                pl.BlockSpec((tc, LANES), lambda i, d: (i, 0)),
                pl.BlockSpec((1, D), lambda i, d: (0, 0)),
            ],
            out_specs=pl.BlockSpec((tc, D), lambda i, d: (i, 0)),
            scratch_shapes=[pltpu.VMEM((2 * TOP_K * tc * ROW_TILE, LANES), F32), pltpu.SemaphoreType.DMA((2,))],
        ),
        out_shape=jax.ShapeDtypeStruct((T, D), F32),
        compiler_params=_cparams(("arbitrary",)),
        name="moe_combine",
    )(dest_blocked, y_buf, x_mid, route, final_w.reshape(1, D))


def _layer(x2, B, S, norm_mix_w, w_in, conv_w, a_log, dt_bias, gdn_norm_w, q_norm_w, k_norm_w, w_out,
           norm_ffn_w, w_router_group, w_router_expert, w_gate, w_up, w_down, final_w):
    T, D = x2.shape
    tl = _tiles(B, S)
    perm = np.concatenate([np.arange(0, ATT_D, 2), np.arange(1, ATT_D, 2)])
    o_z = 3 * GDN_W
    o_gate = o_z + GDN_W
    o_qb = o_gate + 2 * GDN_CHAINS
    o_kb = o_qb + ATT_Q
    o_vb = o_kb + ATT_KV
    qb_cols = o_qb + (np.arange(ATT_HEADS)[:, None] * ATT_D + perm[None, :]).reshape(-1)
    kb_cols = o_kb + (np.arange(ATT_KV_HEADS)[:, None] * ATT_D + perm[None, :]).reshape(-1)
    w_all = jnp.concatenate([
        w_in[:, :o_gate], w_in[:, qb_cols], w_in[:, kb_cols], w_in[:, o_vb:o_vb + ATT_KV],
        w_in[:, o_gate:o_qb], jnp.zeros((D, LANES - 2 * GDN_CHAINS), w_in.dtype)], axis=1).astype(BF16)

    proj = _inproj(x2, norm_mix_w, w_all, tl.proj_rows)

    conv_w8 = jnp.concatenate([conv_w, jnp.zeros((SUBLANES - CONV_W, conv_w.shape[1]), F32)], axis=0)
    gp = jnp.zeros((SUBLANES, LANES), F32)
    gp = gp.at[0, GDN_CHAINS:2 * GDN_CHAINS].set(jnp.exp(a_log.astype(F32)).reshape(-1))
    gp = gp.at[1, GDN_CHAINS:2 * GDN_CHAINS].set(dt_bias.astype(F32).reshape(-1))
    q_a, k_a, v_a, gates = _gdn_prep(proj, conv_w8, gp, B, S, tl.prep_rows)
    n_chunks = T // CHUNK
    gates_t = gates[:, :2 * GDN_CHAINS].reshape(n_chunks, CHUNK, 2 * GDN_CHAINS).transpose(0, 2, 1)
    u, wq, ak, dec = _gdn_chunk(q_a, k_a, v_a, gates, gates_t, tl.gdn_chunks)
    o_f, o_b = _gdn_scan(u, wq, ak, dec, B, S, tl.scan_chunks)

    rows = S // GRID_W
    rowp = jnp.repeat(jnp.arange(rows), GRID_W).astype(F32)
    colp = jnp.tile(jnp.arange(GRID_W), rows).astype(F32)
    axis_dims = ATT_D // 2
    inv_freq = ROPE_THETA ** (-jnp.arange(0, axis_dims, 2, dtype=F32) / axis_dims)
    ang = jnp.concatenate([rowp[:, None] * inv_freq, colp[:, None] * inv_freq], axis=-1)
    cosf = jnp.concatenate([jnp.cos(ang), jnp.cos(ang)], axis=-1)
    sinf = jnp.concatenate([-jnp.sin(ang), jnp.sin(ang)], axis=-1)
    q_gain = jnp.max(jnp.abs(q_norm_w)).astype(F32)
    k_gain = jnp.max(jnp.abs(k_norm_w)).astype(F32)
    kmax = (math.sqrt(ATT_D) * k_gain).reshape(1, 1)
    score_bound = ATT_D ** 0.5 * q_gain * k_gain
    safe = (2.0 * score_bound <= SOFTMAX_SAFE_SPAN).astype(jnp.int32).reshape(1)
    qh, kh, vh = _attn_prep(proj, q_norm_w[perm].reshape(1, ATT_D), k_norm_w[perm].reshape(1, ATT_D),
                            cosf, sinf, kmax, B, S, tl.prep_rows)
    att = _attention(safe, qh, kh, vh, B, S, tl.att_q, tl.att_k).reshape(T, ATT_Q)

    w_r = jnp.concatenate([w_router_group, w_router_expert,
                           jnp.zeros((D, LANES - N_GROUPS - N_EXPERTS), F32)], axis=1)
    w_out_bf = w_out.astype(BF16)
    x_mid, h2, route = _outproj(o_f, o_b, proj, att, x2, gdn_norm_w.reshape(1, GDN_D),
                                w_out_bf[:GDN_W], w_out_bf[GDN_W:], norm_ffn_w.reshape(1, D), w_r,
                                tl.proj_rows)

    bm = tl.moe_rows
    n_assign = T * TOP_K
    n_blocks = -(-(n_assign + N_EXPERTS * (bm - 1)) // bm)
    e_flat = route[:, :TOP_K].astype(jnp.int32).reshape(-1)
    onehot = (e_flat[:, None] == jnp.arange(N_EXPERTS)[None, :]).astype(jnp.int32)
    csum = jnp.cumsum(onehot, axis=0)
    rank = jnp.sum(csum * onehot, axis=1) - 1
    counts = csum[-1]
    padded = (counts + bm - 1) // bm * bm
    pad_end = jnp.cumsum(padded)
    pad_start = pad_end - padded
    dest = pad_start[e_flat] + rank
    src_tok = jnp.zeros((n_blocks * bm,), jnp.int32).at[dest].set(jnp.arange(n_assign, dtype=jnp.int32) // TOP_K)
    block_start = jnp.arange(n_blocks, dtype=jnp.int32) * bm
    block_expert = jnp.minimum(jnp.sum((pad_end[None, :] <= block_start[:, None]).astype(jnp.int32), axis=1),
                               N_EXPERTS - 1)
    n_used = (pad_end[-1:] // bm).astype(jnp.int32)

    y_buf = _experts(block_expert, src_tok, n_used, h2, w_gate, w_up, w_down, bm)

    tc = tl.comb_rows
    dest_blocked = dest.reshape(T // tc, tc, TOP_K).transpose(0, 2, 1).reshape(-1)
    return _combine(dest_blocked, y_buf, x_mid, route, final_w, tc)


def kernel(x, norm_mix_w, w_in, conv_w, a_log, dt_bias, gdn_norm_w, q_norm_w, k_norm_w, w_out, norm_ffn_w,
           w_router_group, w_router_expert, w_gate, w_up, w_down, final_norm_w):
    B, S, D = x.shape
    depth = w_in.shape[0]
    assert depth == 1, "the final norm is fused into the last (only) layer's combine step"
    out = _layer(x.reshape(B * S, D), B, S, norm_mix_w[0], w_in[0], conv_w[0], a_log[0], dt_bias[0],
                 gdn_norm_w[0], q_norm_w[0], k_norm_w[0], w_out[0], norm_ffn_w[0], w_router_group[0],
                 w_router_expert[0], w_gate[0], w_up[0], w_down[0], final_norm_w)
    return out.reshape(B, S, D)
```

```python
import functools
import math
from typing import NamedTuple

import jax
import jax.numpy as jnp
import numpy as np
from jax import lax
from jax.experimental import pallas as pl
from jax.experimental.pallas import tpu as pltpu

F32 = jnp.float32
BF16 = jnp.bfloat16
EPS = 1e-6

GRID_W = 64
GDN_HEADS = 4
GDN_D = 128
CONV_W = 5
CHUNK = 64
ATT_HEADS = 8
ATT_KV_HEADS = 2
ATT_GROUP = ATT_HEADS // ATT_KV_HEADS
ATT_D = 64
ROPE_THETA = 10000.0
N_GROUPS = 4
EXPERTS_PER_GROUP = 8
N_EXPERTS = N_GROUPS * EXPERTS_PER_GROUP
TOP_K = 2

GDN_W = GDN_HEADS * GDN_D
GDN_CHAINS = 2 * GDN_HEADS
ATT_Q = ATT_HEADS * ATT_D
ATT_KV = ATT_KV_HEADS * ATT_D
LANES = 128
SUBLANES = 8
ATT_DP = LANES

COL_Z = 3 * GDN_W
COL_QB = COL_Z + GDN_W
COL_KB = COL_QB + ATT_Q
COL_VB = COL_KB + ATT_KV
COL_GATE = COL_VB + ATT_KV
D_PROJ = COL_GATE + LANES

VMEM_LIMIT = 56 * 1024 * 1024
LOG2E = math.log2(math.e)
SOFTMAX_SAFE_SPAN = 60.0


class Tiles(NamedTuple):
    proj_rows: int
    prep_rows: int
    gdn_chunks: int
    scan_chunks: int
    att_q: int
    att_k: int
    moe_rows: int
    comb_rows: int


def _tile(n, want):
    t = min(n, want)
    assert n % t == 0, (n, want)
    return t


def _tiles(B, S):
    T = B * S
    n_chunks = S // CHUNK
    return Tiles(proj_rows=_tile(T, 512), prep_rows=_tile(S, 512), gdn_chunks=_tile(n_chunks, 4),
                 scan_chunks=_tile(n_chunks, 4), att_q=_tile(S, 256), att_k=_tile(S, 1024),
                 moe_rows=256, comb_rows=_tile(T, 256))


def _cparams(sem):
    return pltpu.CompilerParams(dimension_semantics=sem, vmem_limit_bytes=VMEM_LIMIT)


def _silu(x):
    return x * jax.nn.sigmoid(x)


ROW_TILE = SUBLANES
DMA_QUEUES = 2


def _store_token_tiles(ref, x, base=0):
    rows, d = x.shape
    assert d == ROW_TILE * LANES
    for c in range(ROW_TILE):
        ref[pl.ds(base + c, rows, stride=ROW_TILE), :] = x[:, c * LANES:(c + 1) * LANES]


def _load_token_tiles(ref, base, rows):
    return jnp.concatenate([ref[pl.ds(base + c, rows, stride=ROW_TILE), :] for c in range(ROW_TILE)], axis=-1)


def _inproj_body(x_ref, nw_ref, w_ref, o_ref):
    x = x_ref[...]
    h = x * lax.rsqrt(jnp.mean(x * x, axis=-1, keepdims=True) + EPS) * nw_ref[...]
    o_ref[...] = jnp.dot(h.astype(BF16), w_ref[...], preferred_element_type=F32)


def _inproj(x2, norm_w, w_all, tm):
    T, D = x2.shape
    return pl.pallas_call(
        _inproj_body,
        grid=(T // tm,),
        in_specs=[
            pl.BlockSpec((tm, D), lambda i: (i, 0)),
            pl.BlockSpec((1, D), lambda i: (0, 0)),
            pl.BlockSpec((D, D_PROJ), lambda i: (0, 0)),
        ],
        out_specs=pl.BlockSpec((tm, D_PROJ), lambda i: (i, 0)),
        out_shape=jax.ShapeDtypeStruct((T, D_PROJ), F32),
        compiler_params=_cparams(("parallel",)),
        name="inproj",
    )(x2, norm_w.reshape(1, D), w_all)


def _gdn_prep_body(cur_ref, prev_ref, next_ref, cw_ref, gin_ref, gp_ref,
                   q_ref, k_ref, v_ref, g_ref, ext_ref, *, tr):
    i = pl.program_id(1)
    nr = pl.num_programs(1)
    halo = SUBLANES
    pad = CONV_W // 2
    ext_ref[0:halo, :] = jnp.where(i > 0, prev_ref[...], 0.0)
    ext_ref[halo:halo + tr, :] = cur_ref[...]
    ext_ref[halo + tr:2 * halo + tr, :] = jnp.where(i < nr - 1, next_ref[...], 0.0)
    acc = cw_ref[0:1, :] * ext_ref[pl.ds(halo - pad, tr), :]
    for j in range(1, CONV_W):
        acc = acc + cw_ref[j:j + 1, :] * ext_ref[pl.ds(halo - pad + j, tr), :]
    y = _silu(acc)
    for h in range(GDN_HEADS):
        for base, ref, scale in ((0, q_ref, GDN_D ** -0.5), (GDN_W, k_ref, 1.0)):
            t = y[:, base + h * GDN_D: base + (h + 1) * GDN_D]
            t = t * (lax.rsqrt(jnp.sum(t * t, axis=-1, keepdims=True) + EPS) * scale)
            ref[:, h * GDN_D:(h + 1) * GDN_D] = t
    v_ref[...] = y[:, 2 * GDN_W:]
    gin = gin_ref[...]
    lane = lax.broadcasted_iota(jnp.int32, gin.shape, 1)
    a = gin + gp_ref[1:2, :]
    softplus = jnp.maximum(a, 0.0) + jnp.log1p(jnp.exp(-jnp.abs(a)))
    g_ref[...] = jnp.where(lane < GDN_CHAINS, jax.nn.sigmoid(gin), -gp_ref[0:1, :] * softplus)


def _gdn_prep(proj, conv_w8, gate_params, B, S, tr):
    T = B * S
    nr = S // tr
    C = 3 * GDN_W
    rb = tr // SUBLANES
    nrow8 = T // SUBLANES
    kern = functools.partial(_gdn_prep_body, tr=tr)
    out_sd = jax.ShapeDtypeStruct((T, GDN_W), F32)
    return pl.pallas_call(
        kern,
        grid=(B, nr),
        in_specs=[
            pl.BlockSpec((tr, C), lambda b, i: (b * nr + i, 0)),
            pl.BlockSpec((SUBLANES, C), lambda b, i: (jnp.maximum((b * nr + i) * rb - 1, 0), 0)),
            pl.BlockSpec((SUBLANES, C), lambda b, i: (jnp.minimum((b * nr + i + 1) * rb, nrow8 - 1), 0)),
            pl.BlockSpec((SUBLANES, C), lambda b, i: (0, 0)),
            pl.BlockSpec((tr, LANES), lambda b, i: (b * nr + i, COL_GATE // LANES)),
            pl.BlockSpec((SUBLANES, LANES), lambda b, i: (0, 0)),
        ],
        out_specs=[
            pl.BlockSpec((tr, GDN_W), lambda b, i: (b * nr + i, 0)),
            pl.BlockSpec((tr, GDN_W), lambda b, i: (b * nr + i, 0)),
            pl.BlockSpec((tr, GDN_W), lambda b, i: (b * nr + i, 0)),
            pl.BlockSpec((tr, LANES), lambda b, i: (b * nr + i, 0)),
        ],
        out_shape=[out_sd, out_sd, out_sd, jax.ShapeDtypeStruct((T, LANES), F32)],
        scratch_shapes=[pltpu.VMEM((tr + 2 * SUBLANES, C), F32)],
        compiler_params=_cparams(("parallel", "parallel")),
        name="gdn_prep",
    )(proj, proj, proj, conv_w8, proj, gate_params)


def _bdot(a, b):
    return jnp.dot(a.astype(BF16), b.astype(BF16), preferred_element_type=F32)


def _bdot_nt(a, b):
    return lax.dot_general(a.astype(BF16), b.astype(BF16), (((1,), (1,)), ((), ())),
                           preferred_element_type=F32)


def _gdn_chunk_body(q_ref, k_ref, v_ref, g_ref, gt_ref, u_ref, wq_ref, ak_ref, dec_ref, *, cb):
    C = CHUNK
    row = lax.broadcasted_iota(jnp.int32, (C, C), 0)
    col = lax.broadcasted_iota(jnp.int32, (C, C), 1)
    eye = (row == col).astype(F32)
    masks = ((row >= col, row > col), (row <= col, row < col))
    hi = lax.Precision.HIGHEST
    for c in range(cb):
        rs = slice(c * C, (c + 1) * C)
        gates = g_ref[rs, :]
        gates_t = gt_ref[c]
        g_tot = jnp.sum(gates, axis=0, keepdims=True)
        tot_rows = jnp.sum(gates_t, axis=1, keepdims=True)
        dec_ref[c] = jnp.broadcast_to(jnp.exp(tot_rows[GDN_CHAINS:2 * GDN_CHAINS]), (GDN_CHAINS, LANES))
        chains = []
        for d in range(2):
            incl, strict = masks[d]
            cum = incl.astype(F32)
            gc_cols = jnp.dot(cum, gates, precision=hi, preferred_element_type=F32)
            gc_rows = lax.dot_general(gates_t, cum, (((1,), (1,)), ((), ())), precision=hi,
                                      preferred_element_type=F32)
            for h in range(GDN_HEADS):
                lane_b = d * GDN_HEADS + h
                lane_g = GDN_CHAINS + lane_b
                sl = slice(h * GDN_D, (h + 1) * GDN_D)
                q = q_ref[rs, sl]
                k = k_ref[rs, sl]
                v = v_ref[rs, sl]
                beta = gates[:, lane_b:lane_b + 1]
                gc_col = gc_cols[:, lane_g:lane_g + 1]
                gc_row = gc_rows[lane_g:lane_g + 1, :]
                g_last = g_tot[:, lane_g:lane_g + 1]
                decay = jnp.where(incl, jnp.exp(jnp.where(incl, gc_col - gc_row, 0.0)), 0.0)
                e_col = jnp.exp(gc_col)
                k_beta = k * beta
                kk = _bdot_nt(jnp.concatenate([k_beta, q], axis=0), k)
                neg_l = jnp.where(strict, -kk[:C] * decay, 0.0)
                attn = jnp.where(incl, kk[C:] * decay, 0.0)
                rhs = jnp.concatenate([v * beta, k_beta * e_col], axis=1).astype(BF16)
                wq_ref[d, c, C:2 * C, sl] = (q * e_col).astype(BF16)
                ak_ref[d, c, h, 0:C, :] = attn.astype(BF16)
                ak_ref[d, c, h, C:C + GDN_D, :] = (k * jnp.exp(g_last - gc_col)).T.astype(BF16)
                chains.append((d, sl, neg_l, rhs))
        pws = [nl for (_, _, nl, _) in chains]
        invs = [eye + nl for nl in pws]
        for _ in range(int(math.log2(C)) - 1):
            pws = [_bdot(p, p) for p in pws]
            invs = [t + _bdot(t, p) for t, p in zip(invs, pws)]
        for (d, sl, _, rhs), inv in zip(chains, invs):
            sol = jnp.dot(inv.astype(BF16), rhs, preferred_element_type=F32)
            u_ref[d, rs, sl] = sol[:, :GDN_D]
            wq_ref[d, c, 0:C, sl] = sol[:, GDN_D:].astype(BF16)


def _gdn_chunk(q, k, v, gates, gates_t, cb):
    T = q.shape[0]
    nc = T // CHUNK
    rows = cb * CHUNK
    kern = functools.partial(_gdn_chunk_body, cb=cb)
    wide = pl.BlockSpec((rows, GDN_W), lambda i: (i, 0))
    return pl.pallas_call(
        kern,
        grid=(nc // cb,),
        in_specs=[wide, wide, wide, pl.BlockSpec((rows, LANES), lambda i: (i, 0)),
                  pl.BlockSpec((cb, 2 * GDN_CHAINS, CHUNK), lambda i: (i, 0, 0))],
        out_specs=[
            pl.BlockSpec((2, rows, GDN_W), lambda i: (0, i, 0)),
            pl.BlockSpec((2, cb, 2 * CHUNK, GDN_W), lambda i: (0, i, 0, 0)),
            pl.BlockSpec((2, cb, GDN_HEADS, CHUNK + GDN_D, CHUNK), lambda i: (0, i, 0, 0, 0)),
            pl.BlockSpec((cb, GDN_CHAINS, LANES), lambda i: (i, 0, 0)),
        ],
        out_shape=[
            jax.ShapeDtypeStruct((2, T, GDN_W), F32),
            jax.ShapeDtypeStruct((2, nc, 2 * CHUNK, GDN_W), BF16),
            jax.ShapeDtypeStruct((2, nc, GDN_HEADS, CHUNK + GDN_D, CHUNK), BF16),
            jax.ShapeDtypeStruct((nc, GDN_CHAINS, LANES), F32),
        ],
        compiler_params=_cparams(("parallel",)),
        name="gdn_chunk",
    )(q, k, v, gates, gates_t)


def _gdn_scan_body(uf_ref, wqf_ref, akf_ref, decf_ref, ub_ref, wqb_ref, akb_ref, decb_ref,
                   of_ref, ob_ref, state_ref, *, cs):
    @pl.when(pl.program_id(1) == 0)
    def _():
        state_ref[...] = jnp.zeros_like(state_ref)

    C = CHUNK
    refs = ((uf_ref, wqf_ref, akf_ref, decf_ref, of_ref), (ub_ref, wqb_ref, akb_ref, decb_ref, ob_ref))
    chains = [(d, h) for d in range(2) for h in range(GDN_HEADS)]
    st = [state_ref[j] for j in range(GDN_CHAINS)]
    for step in range(cs):
        cidx = (step, cs - 1 - step)
        ws = [jnp.dot(refs[d][1][cidx[d], :, h * GDN_D:(h + 1) * GDN_D], st[j].astype(BF16),
                      preferred_element_type=F32) for j, (d, h) in enumerate(chains)]
        vn = [(refs[d][0][cidx[d] * C:(cidx[d] + 1) * C, h * GDN_D:(h + 1) * GDN_D] - ws[j][:C]).astype(BF16)
              for j, (d, h) in enumerate(chains)]
        rr = [jnp.dot(refs[d][2][cidx[d], h], vn[j], preferred_element_type=F32)
              for j, (d, h) in enumerate(chains)]
        for j, (d, h) in enumerate(chains):
            c = cidx[d]
            refs[d][4][c * C:(c + 1) * C, h * GDN_D:(h + 1) * GDN_D] = ws[j][C:] + rr[j][:C]
            st[j] = st[j] * refs[d][3][c, j:j + 1, :] + rr[j][C:]
    for j in range(GDN_CHAINS):
        state_ref[j] = st[j]


def _gdn_scan(u, wq, ak, dec, B, S, cs):
    T = B * S
    nb = S // (CHUNK * cs)
    rows = cs * CHUNK
    kern = functools.partial(_gdn_scan_body, cs=cs)
    fwd = lambda b, i: b * nb + i
    bwd = lambda b, i: b * nb + nb - 1 - i

    def specs(d, pos):
        return [
            pl.BlockSpec((None, rows, GDN_W), lambda b, i: (d, pos(b, i), 0)),
            pl.BlockSpec((None, cs, 2 * CHUNK, GDN_W), lambda b, i: (d, pos(b, i), 0, 0)),
            pl.BlockSpec((None, cs, GDN_HEADS, CHUNK + GDN_D, CHUNK), lambda b, i: (d, pos(b, i), 0, 0, 0)),
            pl.BlockSpec((cs, GDN_CHAINS, LANES), lambda b, i: (pos(b, i), 0, 0)),
        ]

    out_sd = jax.ShapeDtypeStruct((T, GDN_W), F32)
    return pl.pallas_call(
        kern,
        grid=(B, nb),
        in_specs=specs(0, fwd) + specs(1, bwd),
        out_specs=[pl.BlockSpec((rows, GDN_W), lambda b, i: (fwd(b, i), 0)),
                   pl.BlockSpec((rows, GDN_W), lambda b, i: (bwd(b, i), 0))],
        out_shape=[out_sd, out_sd],
        scratch_shapes=[pltpu.VMEM((GDN_CHAINS, GDN_D, GDN_D), F32)],
        compiler_params=_cparams(("parallel", "arbitrary")),
        name="gdn_scan",
    )(u, wq, ak, dec, u, wq, ak, dec)


def _rope_norm(x, w, cos, sin):
    y = x * lax.rsqrt(jnp.mean(x * x, axis=-1, keepdims=True) + EPS) * w
    half = ATT_D // 2
    swapped = jnp.concatenate([y[:, half:], y[:, :half]], axis=-1)
    return y * cos + swapped * sin


def _attn_prep_body(q_ref, k_ref, v_ref, qw_ref, kw_ref, cos_ref, sin_ref, kmax_ref,
                    qo_ref, ko_ref, vo_ref):
    cos = cos_ref[...]
    sin = sin_ref[...]
    q = q_ref[...]
    k = k_ref[...]
    v = v_ref[...]
    rows = q.shape[0]
    lane = lax.broadcasted_iota(jnp.int32, (rows, ATT_DP - ATT_D), 1)
    one_col = jnp.where(lane == 0, 1.0, 0.0)
    scale = LOG2E * ATT_D ** -0.5
    for h in range(ATT_HEADS):
        r = _rope_norm(q[:, h * ATT_D:(h + 1) * ATT_D], qw_ref[...], cos, sin) * scale
        bound = jnp.sqrt(jnp.sum(r * r, axis=-1, keepdims=True)) * kmax_ref[...]
        qo_ref[h] = jnp.concatenate([r, -bound * one_col], axis=-1).astype(BF16)
    for h in range(ATT_KV_HEADS):
        r = _rope_norm(k[:, h * ATT_D:(h + 1) * ATT_D], kw_ref[...], cos, sin)
        ko_ref[h] = jnp.concatenate([r, one_col], axis=-1).astype(BF16)
        vo_ref[h] = jnp.concatenate([v[:, h * ATT_D:(h + 1) * ATT_D], one_col], axis=-1).astype(BF16)


def _attn_prep(proj, qw, kw, cosf, sinf, kmax, B, S, tr):
    nr = S // tr
    row = lambda b, i: b * nr + i
    return pl.pallas_call(
        _attn_prep_body,
        grid=(B, nr),
        in_specs=[
            pl.BlockSpec((tr, ATT_Q), lambda b, i: (row(b, i), COL_QB // ATT_Q)),
            pl.BlockSpec((tr, ATT_KV), lambda b, i: (row(b, i), COL_KB // ATT_KV)),
            pl.BlockSpec((tr, ATT_KV), lambda b, i: (row(b, i), COL_VB // ATT_KV)),
            pl.BlockSpec((1, ATT_D), lambda b, i: (0, 0)),
            pl.BlockSpec((1, ATT_D), lambda b, i: (0, 0)),
            pl.BlockSpec((tr, ATT_D), lambda b, i: (i, 0)),
            pl.BlockSpec((tr, ATT_D), lambda b, i: (i, 0)),
            pl.BlockSpec((1, 1), lambda b, i: (0, 0)),
        ],
        out_specs=[
            pl.BlockSpec((None, ATT_HEADS, tr, ATT_DP), lambda b, i: (b, 0, i, 0)),
            pl.BlockSpec((None, ATT_KV_HEADS, tr, ATT_DP), lambda b, i: (b, 0, i, 0)),
            pl.BlockSpec((None, ATT_KV_HEADS, tr, ATT_DP), lambda b, i: (b, 0, i, 0)),
        ],
        out_shape=[
            jax.ShapeDtypeStruct((B, ATT_HEADS, S, ATT_DP), BF16),
            jax.ShapeDtypeStruct((B, ATT_KV_HEADS, S, ATT_DP), BF16),
            jax.ShapeDtypeStruct((B, ATT_KV_HEADS, S, ATT_DP), BF16),
        ],
        compiler_params=_cparams(("parallel", "parallel")),
        name="attn_prep",
    )(proj, proj, proj, qw, kw, cosf, sinf, kmax)


def _attn_body(safe_ref, q_ref, k_ref, v_ref, o_ref, m_ref, acc_ref, *, tq, tk):
    safe = safe_ref[0] != 0
    nk = k_ref.shape[0] // tk
    q = q_ref[...].reshape(ATT_GROUP * tq, ATT_DP)
    acc_ref[...] = jnp.zeros_like(acc_ref)

    def scores(j):
        keys = pl.ds(pl.multiple_of(j * tk, tk), tk)
        s = lax.dot_general(q, k_ref[keys, :], (((1,), (1,)), ((), ())), preferred_element_type=F32)
        return s, v_ref[keys, :]

    @pl.when(safe)
    def _():
        def step(j, carry):
            s, v = scores(j)
            acc_ref[...] += jnp.dot(jnp.exp2(s).astype(BF16), v, preferred_element_type=F32)
            return carry
        lax.fori_loop(0, nk, step, 0)

    @pl.when(jnp.logical_not(safe))
    def _():
        m_ref[...] = jnp.full_like(m_ref, -jnp.inf)

        def step(j, carry):
            s, v = scores(j)
            m_prev = m_ref[...]
            m_new = jnp.maximum(m_prev, jnp.max(s, axis=-1, keepdims=True))
            p = jnp.exp2(s - m_new).astype(BF16)
            acc_ref[...] = jnp.exp2(m_prev - m_new) * acc_ref[...] + jnp.dot(p, v, preferred_element_type=F32)
            m_ref[...] = m_new
            return carry
        lax.fori_loop(0, nk, step, 0)

    acc = acc_ref[...]
    o = acc[:, :ATT_D] / acc[:, ATT_D:ATT_D + 1]
    for h in range(ATT_GROUP):
        o_ref[:, h * ATT_D:(h + 1) * ATT_D] = o[h * tq:(h + 1) * tq].astype(o_ref.dtype)


def _attention(safe, q, k, v, B, S, tq, tk):
    kern = functools.partial(_attn_body, tq=tq, tk=tk)
    gw = ATT_GROUP * ATT_D
    return pl.pallas_call(
        kern,
        grid_spec=pltpu.PrefetchScalarGridSpec(
            num_scalar_prefetch=1,
            grid=(B, ATT_KV_HEADS, S // tq),
            in_specs=[
                pl.BlockSpec((None, ATT_GROUP, tq, ATT_DP), lambda b, g, i, s: (b, g, i, 0)),
                pl.BlockSpec((None, None, S, ATT_DP), lambda b, g, i, s: (b, g, 0, 0)),
                pl.BlockSpec((None, None, S, ATT_DP), lambda b, g, i, s: (b, g, 0, 0)),
            ],
            out_specs=pl.BlockSpec((None, tq, gw), lambda b, g, i, s: (b, i, g)),
            scratch_shapes=[
                pltpu.VMEM((ATT_GROUP * tq, 1), F32),
                pltpu.VMEM((ATT_GROUP * tq, ATT_DP), F32),
            ],
        ),
        out_shape=jax.ShapeDtypeStruct((B, S, ATT_Q), BF16),
        compiler_params=_cparams(("parallel", "parallel", "parallel")),
        name="attention",
    )(safe, q, k, v)


def _outproj_body(of_ref, ob_ref, z_ref, att_ref, x_ref, gnw_ref, wa_ref, wb_ref,
                  fnw_ref, wr_ref, xm_ref, h_ref, r_ref):
    o = of_ref[...] + ob_ref[...]
    z = z_ref[...]
    parts = []
    for h in range(GDN_HEADS):
        sl = slice(h * GDN_D, (h + 1) * GDN_D)
        t = o[:, sl]
        t = t * lax.rsqrt(jnp.mean(t * t, axis=-1, keepdims=True) + EPS) * gnw_ref[...]
        parts.append((t * _silu(z[:, sl])).astype(BF16))
    mix_a = jnp.concatenate(parts, axis=-1)
    xm = x_ref[...] + jnp.dot(mix_a, wa_ref[...], preferred_element_type=F32)
    xm = xm + jnp.dot(att_ref[...], wb_ref[...], preferred_element_type=F32)
    xm_ref[...] = xm
    hn = xm * lax.rsqrt(jnp.mean(xm * xm, axis=-1, keepdims=True) + EPS) * fnw_ref[...]
    _store_token_tiles(h_ref, hn)
    logits = jnp.dot(hn, wr_ref[...], precision=lax.Precision.HIGHEST, preferred_element_type=F32)

    lane = lax.broadcasted_iota(jnp.int32, logits.shape, 1)
    big = jnp.int32(LANES)
    neg = -jnp.inf

    def masked_top(vals, mask):
        m = jnp.max(jnp.where(mask, vals, neg), axis=-1, keepdims=True)
        idx = jnp.min(jnp.where(mask & (vals == m), lane, big), axis=-1, keepdims=True)
        return m, idx

    gmask = lane < N_GROUPS
    gmax, gsel = masked_top(logits, gmask)
    gp_top = 1.0 / jnp.sum(jnp.where(gmask, jnp.exp(logits - gmax), 0.0), axis=-1, keepdims=True)
    lo = N_GROUPS + gsel * EXPERTS_PER_GROUP
    emask = (lane >= lo) & (lane < lo + EXPERTS_PER_GROUP)
    m1, i1 = masked_top(logits, emask)
    ex = jnp.where(emask, jnp.exp(logits - m1), 0.0)
    pf = ex / jnp.sum(ex, axis=-1, keepdims=True)
    p1, _ = masked_top(pf, emask)
    p2, i2 = masked_top(pf, emask & (lane != i1))
    denom = p1 + p2
    g1 = gp_top * p1 / denom
    g2 = gp_top * p2 / denom
    e1 = (i1 - N_GROUPS).astype(F32)
    e2 = (i2 - N_GROUPS).astype(F32)
    r_ref[...] = jnp.where(lane == 0, e1, jnp.where(lane == 1, e2, jnp.where(lane == 2, g1, jnp.where(lane == 3, g2, 0.0))))


def _outproj(o_f, o_b, proj, att, x2, gnw, w_a, w_b, fnw, w_r, tm):
    T, D = x2.shape
    row = lambda i: (i, 0)
    const = lambda i: (0, 0)
    return pl.pallas_call(
        _outproj_body,
        grid=(T // tm,),
        in_specs=[
            pl.BlockSpec((tm, GDN_W), row),
            pl.BlockSpec((tm, GDN_W), row),
            pl.BlockSpec((tm, GDN_W), lambda i: (i, COL_Z // GDN_W)),
            pl.BlockSpec((tm, ATT_Q), row),
            pl.BlockSpec((tm, D), row),
            pl.BlockSpec((1, GDN_D), const),
            pl.BlockSpec((GDN_W, D), const),
            pl.BlockSpec((ATT_Q, D), const),
            pl.BlockSpec((1, D), const),
            pl.BlockSpec((D, LANES), const),
        ],
        out_specs=[pl.BlockSpec((tm, D), row), pl.BlockSpec((tm * ROW_TILE, LANES), row),
                   pl.BlockSpec((tm, LANES), row)],
        out_shape=[jax.ShapeDtypeStruct((T, D), F32), jax.ShapeDtypeStruct((T * ROW_TILE, LANES), F32),
                   jax.ShapeDtypeStruct((T, LANES), F32)],
        compiler_params=_cparams(("parallel",)),
        name="outproj_router",
    )(o_f, o_b, proj, att, x2, gnw, w_a, w_b, fnw, w_r)


def _gather_start(src_hbm, dst_buf, sem, idx_ref, base, slot, n):
    for r in range(n):
        src = pl.multiple_of(idx_ref[base + r] * ROW_TILE, ROW_TILE)
        dst = pl.multiple_of((slot * n + r) * ROW_TILE, ROW_TILE)
        pltpu.make_async_copy(src_hbm.at[pl.ds(src, ROW_TILE)], dst_buf.at[pl.ds(dst, ROW_TILE)],
                              sem.at[slot]).start(priority=r % DMA_QUEUES)


def _gather_wait(src_hbm, dst_buf, sem, slot, n):
    dst = pl.multiple_of(slot * n * ROW_TILE, ROW_TILE)
    pltpu.make_async_copy(src_hbm.at[pl.ds(0, n * ROW_TILE)], dst_buf.at[pl.ds(dst, n * ROW_TILE)],
                          sem.at[slot]).wait()


def _experts_body(be_ref, src_ref, nused_ref, h_hbm, wg_ref, wu_ref, wd_ref, y_ref, xbuf, sem, *, bm):
    i = pl.program_id(0)
    n_used = nused_ref[0]
    slot = i % 2

    def compute(prefetch_next):
        _gather_wait(h_hbm, xbuf, sem, slot, bm)
        x = _load_token_tiles(xbuf, slot * (bm * ROW_TILE), bm).astype(BF16)
        if prefetch_next:
            _gather_start(h_hbm, xbuf, sem, src_ref, (i + 1) * bm, 1 - slot, bm)
        gate = jnp.dot(x, wg_ref[...].astype(BF16), preferred_element_type=F32)
        up = jnp.dot(x, wu_ref[...].astype(BF16), preferred_element_type=F32)
        hid = (_silu(gate) * up).astype(BF16)
        _store_token_tiles(y_ref, jnp.dot(hid, wd_ref[...].astype(BF16), preferred_element_type=F32))

    @pl.when((i == 0) & (n_used > 0))
    def _():
        _gather_start(h_hbm, xbuf, sem, src_ref, 0, 0, bm)

    @pl.when(i + 1 < n_used)
    def _():
        compute(True)

    @pl.when(i + 1 == n_used)
    def _():
        compute(False)

    @pl.when(i >= n_used)
    def _():
        y_ref[...] = jnp.zeros_like(y_ref)


def _experts(block_expert, src_tok, n_used, h2, w_gate, w_up, w_down, bm):
    D = ROW_TILE * LANES
    P = src_tok.shape[0]
    FF = w_gate.shape[-1]
    kern = functools.partial(_experts_body, bm=bm)
    return pl.pallas_call(
        kern,
        grid_spec=pltpu.PrefetchScalarGridSpec(
            num_scalar_prefetch=3,
            grid=(P // bm,),
            in_specs=[
                pl.BlockSpec(memory_space=pl.ANY),
                pl.BlockSpec((None, D, FF), lambda i, be, st, nu: (be[i], 0, 0)),
                pl.BlockSpec((None, D, FF), lambda i, be, st, nu: (be[i], 0, 0)),
                pl.BlockSpec((None, FF, D), lambda i, be, st, nu: (be[i], 0, 0)),
            ],
            out_specs=pl.BlockSpec((bm * ROW_TILE, LANES), lambda i, be, st, nu: (i, 0)),
            scratch_shapes=[pltpu.VMEM((2 * bm * ROW_TILE, LANES), F32), pltpu.SemaphoreType.DMA((2,))],
        ),
        out_shape=jax.ShapeDtypeStruct((P * ROW_TILE, LANES), F32),
        compiler_params=_cparams(("arbitrary",)),
        name="moe_experts",
    )(block_expert, src_tok, n_used, h2, w_gate, w_up, w_down)


def _combine_body(dest_ref, y_hbm, xm_ref, r_ref, fw_ref, o_ref, ybuf, sem, *, tc):
    i = pl.program_id(0)
    n = pl.num_programs(0)
    slot = i % 2
    rows = TOP_K * tc

    def compute(prefetch_next):
        _gather_wait(y_hbm, ybuf, sem, slot, rows)
        route = r_ref[...]
        y0 = _load_token_tiles(ybuf, slot * (rows * ROW_TILE), tc)
        y1 = _load_token_tiles(ybuf, (slot * rows + tc) * ROW_TILE, tc)
        if prefetch_next:
            _gather_start(y_hbm, ybuf, sem, dest_ref, (i + 1) * rows, 1 - slot, rows)
        xo = xm_ref[...] + route[:, 2:3] * y0 + route[:, 3:4] * y1
        o_ref[...] = xo * lax.rsqrt(jnp.mean(xo * xo, axis=-1, keepdims=True) + EPS) * fw_ref[...]

    @pl.when(i == 0)
    def _():
        _gather_start(y_hbm, ybuf, sem, dest_ref, 0, 0, rows)

    @pl.when(i + 1 < n)
    def _():
        compute(True)

    @pl.when(i + 1 == n)
    def _():
        compute(False)


def _combine(dest_blocked, y_buf, x_mid, route, final_w, tc):
    T, D = x_mid.shape
    kern = functools.partial(_combine_body, tc=tc)
    return pl.pallas_call(
        kern,
        grid_spec=pltpu.PrefetchScalarGridSpec(
            num_scalar_prefetch=1,
            grid=(T // tc,),
            in_specs=[
                pl.BlockSpec(memory_space=pl.ANY),
                pl.BlockSpec((tc, D), lambda i, d: (i, 0)),
                pl.BlockSpec((tc, LANES), lambda i, d: (i, 0)),
                pl.BlockSpec((1, D), lambda i, d: (0, 0)),
            ],
            out_specs=pl.BlockSpec((tc, D), lambda i, d: (i, 0)),
            scratch_shapes=[pltpu.VMEM((2 * TOP_K * tc * ROW_TILE, LANES), F32), pltpu.SemaphoreType.DMA((2,))],
        ),
        out_shape=jax.ShapeDtypeStruct((T, D), F32),
        compiler_params=_cparams(("arbitrary",)),
        name="moe_combine",
    )(dest_blocked, y_buf, x_mid, route, final_w.reshape(1, D))


def _layer(x2, B, S, norm_mix_w, w_in, conv_w, a_log, dt_bias, gdn_norm_w, q_norm_w, k_norm_w, w_out,
           norm_ffn_w, w_router_group, w_router_expert, w_gate, w_up, w_down, final_w):
    T, D = x2.shape
    tl = _tiles(B, S)
    perm = np.concatenate([np.arange(0, ATT_D, 2), np.arange(1, ATT_D, 2)])
    o_z = 3 * GDN_W
    o_gate = o_z + GDN_W
    o_qb = o_gate + 2 * GDN_CHAINS
    o_kb = o_qb + ATT_Q
    o_vb = o_kb + ATT_KV
    qb_cols = o_qb + (np.arange(ATT_HEADS)[:, None] * ATT_D + perm[None, :]).reshape(-1)
    kb_cols = o_kb + (np.arange(ATT_KV_HEADS)[:, None] * ATT_D + perm[None, :]).reshape(-1)
    w_all = jnp.concatenate([
        w_in[:, :o_gate], w_in[:, qb_cols], w_in[:, kb_cols], w_in[:, o_vb:o_vb + ATT_KV],
        w_in[:, o_gate:o_qb], jnp.zeros((D, LANES - 2 * GDN_CHAINS), w_in.dtype)], axis=1).astype(BF16)

    proj = _inproj(x2, norm_mix_w, w_all, tl.proj_rows)

    conv_w8 = jnp.concatenate([conv_w, jnp.zeros((SUBLANES - CONV_W, conv_w.shape[1]), F32)], axis=0)
    gp = jnp.zeros((SUBLANES, LANES), F32)
    gp = gp.at[0, GDN_CHAINS:2 * GDN_CHAINS].set(jnp.exp(a_log.astype(F32)).reshape(-1))
    gp = gp.at[1, GDN_CHAINS:2 * GDN_CHAINS].set(dt_bias.astype(F32).reshape(-1))
    q_a, k_a, v_a, gates = _gdn_prep(proj, conv_w8, gp, B, S, tl.prep_rows)
    n_chunks = T // CHUNK
    gates_t = gates[:, :2 * GDN_CHAINS].reshape(n_chunks, CHUNK, 2 * GDN_CHAINS).transpose(0, 2, 1)
    u, wq, ak, dec = _gdn_chunk(q_a, k_a, v_a, gates, gates_t, tl.gdn_chunks)
    o_f, o_b = _gdn_scan(u, wq, ak, dec, B, S, tl.scan_chunks)

    rows = S // GRID_W
    rowp = jnp.repeat(jnp.arange(rows), GRID_W).astype(F32)
    colp = jnp.tile(jnp.arange(GRID_W), rows).astype(F32)
    axis_dims = ATT_D // 2
    inv_freq = ROPE_THETA ** (-jnp.arange(0, axis_dims, 2, dtype=F32) / axis_dims)
    ang = jnp.concatenate([rowp[:, None] * inv_freq, colp[:, None] * inv_freq], axis=-1)
    cosf = jnp.concatenate([jnp.cos(ang), jnp.cos(ang)], axis=-1)
    sinf = jnp.concatenate([-jnp.sin(ang), jnp.sin(ang)], axis=-1)
    q_gain = jnp.max(jnp.abs(q_norm_w)).astype(F32)
    k_gain = jnp.max(jnp.abs(k_norm_w)).astype(F32)
    kmax = (math.sqrt(ATT_D) * k_gain).reshape(1, 1)
    score_bound = ATT_D ** 0.5 * q_gain * k_gain
    safe = (2.0 * score_bound <= SOFTMAX_SAFE_SPAN).astype(jnp.int32).reshape(1)
    qh, kh, vh = _attn_prep(proj, q_norm_w[perm].reshape(1, ATT_D), k_norm_w[perm].reshape(1, ATT_D),
                            cosf, sinf, kmax, B, S, tl.prep_rows)
    att = _attention(safe, qh, kh, vh, B, S, tl.att_q, tl.att_k).reshape(T, ATT_Q)

    w_r = jnp.concatenate([w_router_group, w_router_expert,
                           jnp.zeros((D, LANES - N_GROUPS - N_EXPERTS), F32)], axis=1)
    w_out_bf = w_out.astype(BF16)
    x_mid, h2, route = _outproj(o_f, o_b, proj, att, x2, gdn_norm_w.reshape(1, GDN_D),
                                w_out_bf[:GDN_W], w_out_bf[GDN_W:], norm_ffn_w.reshape(1, D), w_r,
                                tl.proj_rows)

    bm = tl.moe_rows
    n_assign = T * TOP_K
    n_blocks = -(-(n_assign + N_EXPERTS * (bm - 1)) // bm)
    e_flat = route[:, :TOP_K].astype(jnp.int32).reshape(-1)
    onehot = (e_flat[:, None] == jnp.arange(N_EXPERTS)[None, :]).astype(jnp.int32)
    csum = jnp.cumsum(onehot, axis=0)
    rank = jnp.sum(csum * onehot, axis=1) - 1
    counts = csum[-1]
    padded = (counts + bm - 1) // bm * bm
    pad_end = jnp.cumsum(padded)
    pad_start = pad_end - padded
    dest = pad_start[e_flat] + rank
    src_tok = jnp.zeros((n_blocks * bm,), jnp.int32).at[dest].set(jnp.arange(n_assign, dtype=jnp.int32) // TOP_K)
    block_start = jnp.arange(n_blocks, dtype=jnp.int32) * bm
    block_expert = jnp.minimum(jnp.sum((pad_end[None, :] <= block_start[:, None]).astype(jnp.int32), axis=1),
                               N_EXPERTS - 1)
    n_used = (pad_end[-1:] // bm).astype(jnp.int32)

    y_buf = _experts(block_expert, src_tok, n_used, h2, w_gate, w_up, w_down, bm)

    tc = tl.comb_rows
    dest_blocked = dest.reshape(T // tc, tc, TOP_K).transpose(0, 2, 1).reshape(-1)
    return _combine(dest_blocked, y_buf, x_mid, route, final_w, tc)


def kernel(x, norm_mix_w, w_in, conv_w, a_log, dt_bias, gdn_norm_w, q_norm_w, k_norm_w, w_out, norm_ffn_w,
           w_router_group, w_router_expert, w_gate, w_up, w_down, final_norm_w):
    B, S, D = x.shape
    depth = w_in.shape[0]
    assert depth == 1, "the final norm is fused into the last (only) layer's combine step"
    out = _layer(x.reshape(B * S, D), B, S, norm_mix_w[0], w_in[0], conv_w[0], a_log[0], dt_bias[0],
                 gdn_norm_w[0], q_norm_w[0], k_norm_w[0], w_out[0], norm_ffn_w[0], w_router_group[0],
                 w_router_expert[0], w_gate[0], w_up[0], w_down[0], final_norm_w)
    return out.reshape(B, S, D)
```

```python
import functools
import math
from typing import NamedTuple

import jax
import jax.numpy as jnp
import numpy as np
from jax import lax
from jax.experimental import pallas as pl
from jax.experimental.pallas import tpu as pltpu

F32 = jnp.float32
BF16 = jnp.bfloat16
EPS = 1e-6

GRID_W = 64
GDN_HEADS = 4
GDN_D = 128
CONV_W = 5
CHUNK = 64
ATT_HEADS = 8
ATT_KV_HEADS = 2
ATT_GROUP = ATT_HEADS // ATT_KV_HEADS
ATT_D = 64
ROPE_THETA = 10000.0
N_GROUPS = 4
EXPERTS_PER_GROUP = 8
N_EXPERTS = N_GROUPS * EXPERTS_PER_GROUP
TOP_K = 2

GDN_W = GDN_HEADS * GDN_D
GDN_CHAINS = 2 * GDN_HEADS
ATT_Q = ATT_HEADS * ATT_D
ATT_KV = ATT_KV_HEADS * ATT_D
LANES = 128
SUBLANES = 8
ATT_DP = LANES

COL_Z = 3 * GDN_W
COL_QB = COL_Z + GDN_W
COL_KB = COL_QB + ATT_Q
COL_VB = COL_KB + ATT_KV
COL_GATE = COL_VB + ATT_KV
D_PROJ = COL_GATE + LANES

VMEM_LIMIT = 56 * 1024 * 1024
LOG2E = math.log2(math.e)
SOFTMAX_SAFE_SPAN = 60.0


class Tiles(NamedTuple):
    proj_rows: int
    prep_rows: int
    gdn_chunks: int
    scan_chunks: int
    att_q: int
    att_k: int
    moe_rows: int
    comb_rows: int


def _tile(n, want):
    t = min(n, want)
    assert n % t == 0, (n, want)
    return t


def _tiles(B, S):
    T = B * S
    n_chunks = S // CHUNK
    return Tiles(proj_rows=_tile(T, 512), prep_rows=_tile(S, 512), gdn_chunks=_tile(n_chunks, 4),
                 scan_chunks=_tile(n_chunks, 4), att_q=_tile(S, 256), att_k=_tile(S, 1024),
                 moe_rows=256, comb_rows=_tile(T, 256))


def _cparams(sem):
    return pltpu.CompilerParams(dimension_semantics=sem, vmem_limit_bytes=VMEM_LIMIT)


def _silu(x):
    return x * jax.nn.sigmoid(x)


ROW_TILE = SUBLANES
DMA_QUEUES = 2


def _store_token_tiles(ref, x, base=0):
    rows, d = x.shape
    assert d == ROW_TILE * LANES
    for c in range(ROW_TILE):
        ref[pl.ds(base + c, rows, stride=ROW_TILE), :] = x[:, c * LANES:(c + 1) * LANES]


def _load_token_tiles(ref, base, rows):
    return jnp.concatenate([ref[pl.ds(base + c, rows, stride=ROW_TILE), :] for c in range(ROW_TILE)], axis=-1)


def _inproj_body(x_ref, nw_ref, w_ref, o_ref):
    x = x_ref[...]
    h = x * lax.rsqrt(jnp.mean(x * x, axis=-1, keepdims=True) + EPS) * nw_ref[...]
    o_ref[...] = jnp.dot(h.astype(BF16), w_ref[...], preferred_element_type=F32)


def _inproj(x2, norm_w, w_all, tm):
    T, D = x2.shape
    return pl.pallas_call(
        _inproj_body,
        grid=(T // tm,),
        in_specs=[
            pl.BlockSpec((tm, D), lambda i: (i, 0)),
            pl.BlockSpec((1, D), lambda i: (0, 0)),
            pl.BlockSpec((D, D_PROJ), lambda i: (0, 0)),
        ],
        out_specs=pl.BlockSpec((tm, D_PROJ), lambda i: (i, 0)),
        out_shape=jax.ShapeDtypeStruct((T, D_PROJ), F32),
        compiler_params=_cparams(("parallel",)),
        name="inproj",
    )(x2, norm_w.reshape(1, D), w_all)


def _gdn_prep_body(cur_ref, prev_ref, next_ref, cw_ref, gin_ref, gp_ref,
                   q_ref, k_ref, v_ref, g_ref, ext_ref, *, tr):
    i = pl.program_id(1)
    nr = pl.num_programs(1)
    halo = SUBLANES
    pad = CONV_W // 2
    ext_ref[0:halo, :] = jnp.where(i > 0, prev_ref[...], 0.0)
    ext_ref[halo:halo + tr, :] = cur_ref[...]
    ext_ref[halo + tr:2 * halo + tr, :] = jnp.where(i < nr - 1, next_ref[...], 0.0)
    acc = cw_ref[0:1, :] * ext_ref[pl.ds(halo - pad, tr), :]
    for j in range(1, CONV_W):
        acc = acc + cw_ref[j:j + 1, :] * ext_ref[pl.ds(halo - pad + j, tr), :]
    y = _silu(acc)
    for h in range(GDN_HEADS):
        for base, ref, scale in ((0, q_ref, GDN_D ** -0.5), (GDN_W, k_ref, 1.0)):
            t = y[:, base + h * GDN_D: base + (h + 1) * GDN_D]
            t = t * (lax.rsqrt(jnp.sum(t * t, axis=-1, keepdims=True) + EPS) * scale)
            ref[:, h * GDN_D:(h + 1) * GDN_D] = t
    v_ref[...] = y[:, 2 * GDN_W:]
    gin = gin_ref[...]
    lane = lax.broadcasted_iota(jnp.int32, gin.shape, 1)
    a = gin + gp_ref[1:2, :]
    softplus = jnp.maximum(a, 0.0) + jnp.log1p(jnp.exp(-jnp.abs(a)))
    g_ref[...] = jnp.where(lane < GDN_CHAINS, jax.nn.sigmoid(gin), -gp_ref[0:1, :] * softplus)


def _gdn_prep(proj, conv_w8, gate_params, B, S, tr):
    T = B * S
    nr = S // tr
    C = 3 * GDN_W
    rb = tr // SUBLANES
    nrow8 = T // SUBLANES
    kern = functools.partial(_gdn_prep_body, tr=tr)
    out_sd = jax.ShapeDtypeStruct((T, GDN_W), F32)
    return pl.pallas_call(
        kern,
        grid=(B, nr),
        in_specs=[
            pl.BlockSpec((tr, C), lambda b, i: (b * nr + i, 0)),
            pl.BlockSpec((SUBLANES, C), lambda b, i: (jnp.maximum((b * nr + i) * rb - 1, 0), 0)),
            pl.BlockSpec((SUBLANES, C), lambda b, i: (jnp.minimum((b * nr + i + 1) * rb, nrow8 - 1), 0)),
            pl.BlockSpec((SUBLANES, C), lambda b, i: (0, 0)),
            pl.BlockSpec((tr, LANES), lambda b, i: (b * nr + i, COL_GATE // LANES)),
            pl.BlockSpec((SUBLANES, LANES), lambda b, i: (0, 0)),
        ],
        out_specs=[
            pl.BlockSpec((tr, GDN_W), lambda b, i: (b * nr + i, 0)),
            pl.BlockSpec((tr, GDN_W), lambda b, i: (b * nr + i, 0)),
            pl.BlockSpec((tr, GDN_W), lambda b, i: (b * nr + i, 0)),
            pl.BlockSpec((tr, LANES), lambda b, i: (b * nr + i, 0)),
        ],
        out_shape=[out_sd, out_sd, out_sd, jax.ShapeDtypeStruct((T, LANES), F32)],
        scratch_shapes=[pltpu.VMEM((tr + 2 * SUBLANES, C), F32)],
        compiler_params=_cparams(("parallel", "parallel")),
        name="gdn_prep",
    )(proj, proj, proj, conv_w8, proj, gate_params)


def _bdot(a, b):
    return jnp.dot(a.astype(BF16), b.astype(BF16), preferred_element_type=F32)


def _bdot_nt(a, b):
    return lax.dot_general(a.astype(BF16), b.astype(BF16), (((1,), (1,)), ((), ())),
                           preferred_element_type=F32)


def _gdn_chunk_body(q_ref, k_ref, v_ref, g_ref, gt_ref, u_ref, wq_ref, ak_ref, dec_ref, *, cb):
    C = CHUNK
    row = lax.broadcasted_iota(jnp.int32, (C, C), 0)
    col = lax.broadcasted_iota(jnp.int32, (C, C), 1)
    eye = (row == col).astype(F32)
    masks = ((row >= col, row > col), (row <= col, row < col))
    hi = lax.Precision.HIGHEST
    for c in range(cb):
        rs = slice(c * C, (c + 1) * C)
        gates = g_ref[rs, :]
        gates_t = gt_ref[c]
        g_tot = jnp.sum(gates, axis=0, keepdims=True)
        tot_rows = jnp.sum(gates_t, axis=1, keepdims=True)
        dec_ref[c] = jnp.broadcast_to(jnp.exp(tot_rows[GDN_CHAINS:2 * GDN_CHAINS]), (GDN_CHAINS, LANES))
        chains = []
        for d in range(2):
            incl, strict = masks[d]
            cum = incl.astype(F32)
            gc_cols = jnp.dot(cum, gates, precision=hi, preferred_element_type=F32)
            gc_rows = lax.dot_general(gates_t, cum, (((1,), (1,)), ((), ())), precision=hi,
                                      preferred_element_type=F32)
            for h in range(GDN_HEADS):
                lane_b = d * GDN_HEADS + h
                lane_g = GDN_CHAINS + lane_b
                sl = slice(h * GDN_D, (h + 1) * GDN_D)
                q = q_ref[rs, sl]
                k = k_ref[rs, sl]
                v = v_ref[rs, sl]
                beta = gates[:, lane_b:lane_b + 1]
                gc_col = gc_cols[:, lane_g:lane_g + 1]
                gc_row = gc_rows[lane_g:lane_g + 1, :]
                g_last = g_tot[:, lane_g:lane_g + 1]
                decay = jnp.where(incl, jnp.exp(jnp.where(incl, gc_col - gc_row, 0.0)), 0.0)
                e_col = jnp.exp(gc_col)
                k_beta = k * beta
                kk = _bdot_nt(jnp.concatenate([k_beta, q], axis=0), k)
                neg_l = jnp.where(strict, -kk[:C] * decay, 0.0)
                attn = jnp.where(incl, kk[C:] * decay, 0.0)
                rhs = jnp.concatenate([v * beta, k_beta * e_col], axis=1).astype(BF16)
                wq_ref[d, c, C:2 * C, sl] = (q * e_col).astype(BF16)
                ak_ref[d, c, h, 0:C, :] = attn.astype(BF16)
                ak_ref[d, c, h, C:C + GDN_D, :] = (k * jnp.exp(g_last - gc_col)).T.astype(BF16)
                chains.append((d, sl, neg_l, rhs))
        pws = [nl for (_, _, nl, _) in chains]
        invs = [eye + nl for nl in pws]
        for _ in range(int(math.log2(C)) - 1):
            pws = [_bdot(p, p) for p in pws]
            invs = [t + _bdot(t, p) for t, p in zip(invs, pws)]
        for (d, sl, _, rhs), inv in zip(chains, invs):
            sol = jnp.dot(inv.astype(BF16), rhs, preferred_element_type=F32)
            u_ref[d, rs, sl] = sol[:, :GDN_D]
            wq_ref[d, c, 0:C, sl] = sol[:, GDN_D:].astype(BF16)


def _gdn_chunk(q, k, v, gates, gates_t, cb):
    T = q.shape[0]
    nc = T // CHUNK
    rows = cb * CHUNK
    kern = functools.partial(_gdn_chunk_body, cb=cb)
    wide = pl.BlockSpec((rows, GDN_W), lambda i: (i, 0))
    return pl.pallas_call(
        kern,
        grid=(nc // cb,),
        in_specs=[wide, wide, wide, pl.BlockSpec((rows, LANES), lambda i: (i, 0)),
                  pl.BlockSpec((cb, 2 * GDN_CHAINS, CHUNK), lambda i: (i, 0, 0))],
        out_specs=[
            pl.BlockSpec((2, rows, GDN_W), lambda i: (0, i, 0)),
            pl.BlockSpec((2, cb, 2 * CHUNK, GDN_W), lambda i: (0, i, 0, 0)),
            pl.BlockSpec((2, cb, GDN_HEADS, CHUNK + GDN_D, CHUNK), lambda i: (0, i, 0, 0, 0)),
            pl.BlockSpec((cb, GDN_CHAINS, LANES), lambda i: (i, 0, 0)),
        ],
        out_shape=[
            jax.ShapeDtypeStruct((2, T, GDN_W), F32),
            jax.ShapeDtypeStruct((2, nc, 2 * CHUNK, GDN_W), BF16),
            jax.ShapeDtypeStruct((2, nc, GDN_HEADS, CHUNK + GDN_D, CHUNK), BF16),
            jax.ShapeDtypeStruct((nc, GDN_CHAINS, LANES), F32),
        ],
        compiler_params=_cparams(("parallel",)),
        name="gdn_chunk",
    )(q, k, v, gates, gates_t)


def _gdn_scan_body(uf_ref, wqf_ref, akf_ref, decf_ref, ub_ref, wqb_ref, akb_ref, decb_ref,
                   of_ref, ob_ref, state_ref, *, cs):
    @pl.when(pl.program_id(1) == 0)
    def _():
        state_ref[...] = jnp.zeros_like(state_ref)

    C = CHUNK
    refs = ((uf_ref, wqf_ref, akf_ref, decf_ref, of_ref), (ub_ref, wqb_ref, akb_ref, decb_ref, ob_ref))
    chains = [(d, h) for d in range(2) for h in range(GDN_HEADS)]
    st = [state_ref[j] for j in range(GDN_CHAINS)]
    for step in range(cs):
        cidx = (step, cs - 1 - step)
        ws = [jnp.dot(refs[d][1][cidx[d], :, h * GDN_D:(h + 1) * GDN_D], st[j].astype(BF16),
                      preferred_element_type=F32) for j, (d, h) in enumerate(chains)]
        vn = [(refs[d][0][cidx[d] * C:(cidx[d] + 1) * C, h * GDN_D:(h + 1) * GDN_D] - ws[j][:C]).astype(BF16)
              for j, (d, h) in enumerate(chains)]
        rr = [jnp.dot(refs[d][2][cidx[d], h], vn[j], preferred_element_type=F32)
              for j, (d, h) in enumerate(chains)]
        for j, (d, h) in enumerate(chains):
            c = cidx[d]
            refs[d][4][c * C:(c + 1) * C, h * GDN_D:(h + 1) * GDN_D] = ws[j][C:] + rr[j][:C]
            st[j] = st[j] * refs[d][3][c, j:j + 1, :] + rr[j][C:]
    for j in range(GDN_CHAINS):
        state_ref[j] = st[j]


def _gdn_scan(u, wq, ak, dec, B, S, cs):
    T = B * S
    nb = S // (CHUNK * cs)
    rows = cs * CHUNK
    kern = functools.partial(_gdn_scan_body, cs=cs)
    fwd = lambda b, i: b * nb + i
    bwd = lambda b, i: b * nb + nb - 1 - i

    def specs(d, pos):
        return [
            pl.BlockSpec((None, rows, GDN_W), lambda b, i: (d, pos(b, i), 0)),
            pl.BlockSpec((None, cs, 2 * CHUNK, GDN_W), lambda b, i: (d, pos(b, i), 0, 0)),
            pl.BlockSpec((None, cs, GDN_HEADS, CHUNK + GDN_D, CHUNK), lambda b, i: (d, pos(b, i), 0, 0, 0)),
            pl.BlockSpec((cs, GDN_CHAINS, LANES), lambda b, i: (pos(b, i), 0, 0)),
        ]

    out_sd = jax.ShapeDtypeStruct((T, GDN_W), F32)
    return pl.pallas_call(
        kern,
        grid=(B, nb),
        in_specs=specs(0, fwd) + specs(1, bwd),
        out_specs=[pl.BlockSpec((rows, GDN_W), lambda b, i: (fwd(b, i), 0)),
                   pl.BlockSpec((rows, GDN_W), lambda b, i: (bwd(b, i), 0))],
        out_shape=[out_sd, out_sd],
        scratch_shapes=[pltpu.VMEM((GDN_CHAINS, GDN_D, GDN_D), F32)],
        compiler_params=_cparams(("parallel", "arbitrary")),
        name="gdn_scan",
    )(u, wq, ak, dec, u, wq, ak, dec)


def _rope_norm(x, w, cos, sin):
    y = x * lax.rsqrt(jnp.mean(x * x, axis=-1, keepdims=True) + EPS) * w
    half = ATT_D // 2
    swapped = jnp.concatenate([y[:, half:], y[:, :half]], axis=-1)
    return y * cos + swapped * sin


def _attn_prep_body(q_ref, k_ref, v_ref, qw_ref, kw_ref, cos_ref, sin_ref, kmax_ref,
                    qo_ref, ko_ref, vo_ref):
    cos = cos_ref[...]
    sin = sin_ref[...]
    q = q_ref[...]
    k = k_ref[...]
    v = v_ref[...]
    rows = q.shape[0]
    lane = lax.broadcasted_iota(jnp.int32, (rows, ATT_DP - ATT_D), 1)
    one_col = jnp.where(lane == 0, 1.0, 0.0)
    scale = LOG2E * ATT_D ** -0.5
    for h in range(ATT_HEADS):
        r = _rope_norm(q[:, h * ATT_D:(h + 1) * ATT_D], qw_ref[...], cos, sin) * scale
        bound = jnp.sqrt(jnp.sum(r * r, axis=-1, keepdims=True)) * kmax_ref[...]
        qo_ref[h] = jnp.concatenate([r, -bound * one_col], axis=-1).astype(BF16)
    for h in range(ATT_KV_HEADS):
        r = _rope_norm(k[:, h * ATT_D:(h + 1) * ATT_D], kw_ref[...], cos, sin)
        ko_ref[h] = jnp.concatenate([r, one_col], axis=-1).astype(BF16)
        vo_ref[h] = jnp.concatenate([v[:, h * ATT_D:(h + 1) * ATT_D], one_col], axis=-1).astype(BF16)


def _attn_prep(proj, qw, kw, cosf, sinf, kmax, B, S, tr):
    nr = S // tr
    row = lambda b, i: b * nr + i
    return pl.pallas_call(
        _attn_prep_body,
        grid=(B, nr),
        in_specs=[
            pl.BlockSpec((tr, ATT_Q), lambda b, i: (row(b, i), COL_QB // ATT_Q)),
            pl.BlockSpec((tr, ATT_KV), lambda b, i: (row(b, i), COL_KB // ATT_KV)),
            pl.BlockSpec((tr, ATT_KV), lambda b, i: (row(b, i), COL_VB // ATT_KV)),
            pl.BlockSpec((1, ATT_D), lambda b, i: (0, 0)),
            pl.BlockSpec((1, ATT_D), lambda b, i: (0, 0)),
            pl.BlockSpec((tr, ATT_D), lambda b, i: (i, 0)),
            pl.BlockSpec((tr, ATT_D), lambda b, i: (i, 0)),
            pl.BlockSpec((1, 1), lambda b, i: (0, 0)),
        ],
        out_specs=[
            pl.BlockSpec((None, ATT_HEADS, tr, ATT_DP), lambda b, i: (b, 0, i, 0)),
            pl.BlockSpec((None, ATT_KV_HEADS, tr, ATT_DP), lambda b, i: (b, 0, i, 0)),
            pl.BlockSpec((None, ATT_KV_HEADS, tr, ATT_DP), lambda b, i: (b, 0, i, 0)),
        ],
        out_shape=[
            jax.ShapeDtypeStruct((B, ATT_HEADS, S, ATT_DP), BF16),
            jax.ShapeDtypeStruct((B, ATT_KV_HEADS, S, ATT_DP), BF16),
            jax.ShapeDtypeStruct((B, ATT_KV_HEADS, S, ATT_DP), BF16),
        ],
        compiler_params=_cparams(("parallel", "parallel")),
        name="attn_prep",
    )(proj, proj, proj, qw, kw, cosf, sinf, kmax)


def _attn_body(safe_ref, q_ref, k_ref, v_ref, o_ref, m_ref, acc_ref, *, tq, tk):
    safe = safe_ref[0] != 0
    nk = k_ref.shape[0] // tk
    q = q_ref[...].reshape(ATT_GROUP * tq, ATT_DP)
    acc_ref[...] = jnp.zeros_like(acc_ref)

    def scores(j):
        keys = pl.ds(pl.multiple_of(j * tk, tk), tk)
        s = lax.dot_general(q, k_ref[keys, :], (((1,), (1,)), ((), ())), preferred_element_type=F32)
        return s, v_ref[keys, :]

    @pl.when(safe)
    def _():
        def step(j, carry):
            s, v = scores(j)
            acc_ref[...] += jnp.dot(jnp.exp2(s).astype(BF16), v, preferred_element_type=F32)
            return carry
        lax.fori_loop(0, nk, step, 0)

    @pl.when(jnp.logical_not(safe))
    def _():
        m_ref[...] = jnp.full_like(m_ref, -jnp.inf)

        def step(j, carry):
            s, v = scores(j)
            m_prev = m_ref[...]
            m_new = jnp.maximum(m_prev, jnp.max(s, axis=-1, keepdims=True))
            p = jnp.exp2(s - m_new).astype(BF16)
            acc_ref[...] = jnp.exp2(m_prev - m_new) * acc_ref[...] + jnp.dot(p, v, preferred_element_type=F32)
            m_ref[...] = m_new
            return carry
        lax.fori_loop(0, nk, step, 0)

    acc = acc_ref[...]
    o = acc[:, :ATT_D] / acc[:, ATT_D:ATT_D + 1]
    for h in range(ATT_GROUP):
        o_ref[:, h * ATT_D:(h + 1) * ATT_D] = o[h * tq:(h + 1) * tq].astype(o_ref.dtype)


def _attention(safe, q, k, v, B, S, tq, tk):
    kern = functools.partial(_attn_body, tq=tq, tk=tk)
    gw = ATT_GROUP * ATT_D
    return pl.pallas_call(
        kern,
        grid_spec=pltpu.PrefetchScalarGridSpec(
            num_scalar_prefetch=1,
            grid=(B, ATT_KV_HEADS, S // tq),
            in_specs=[
                pl.BlockSpec((None, ATT_GROUP, tq, ATT_DP), lambda b, g, i, s: (b, g, i, 0)),
                pl.BlockSpec((None, None, S, ATT_DP), lambda b, g, i, s: (b, g, 0, 0)),
                pl.BlockSpec((None, None, S, ATT_DP), lambda b, g, i, s: (b, g, 0, 0)),
            ],
            out_specs=pl.BlockSpec((None, tq, gw), lambda b, g, i, s: (b, i, g)),
            scratch_shapes=[
                pltpu.VMEM((ATT_GROUP * tq, 1), F32),
                pltpu.VMEM((ATT_GROUP * tq, ATT_DP), F32),
            ],
        ),
        out_shape=jax.ShapeDtypeStruct((B, S, ATT_Q), BF16),
        compiler_params=_cparams(("parallel", "parallel", "parallel")),
        name="attention",
    )(safe, q, k, v)


def _outproj_body(of_ref, ob_ref, z_ref, att_ref, x_ref, gnw_ref, wa_ref, wb_ref,
                  fnw_ref, wr_ref, xm_ref, h_ref, r_ref):
    o = of_ref[...] + ob_ref[...]
    z = z_ref[...]
    parts = []
    for h in range(GDN_HEADS):
        sl = slice(h * GDN_D, (h + 1) * GDN_D)
        t = o[:, sl]
        t = t * lax.rsqrt(jnp.mean(t * t, axis=-1, keepdims=True) + EPS) * gnw_ref[...]
        parts.append((t * _silu(z[:, sl])).astype(BF16))
    mix_a = jnp.concatenate(parts, axis=-1)
    xm = x_ref[...] + jnp.dot(mix_a, wa_ref[...], preferred_element_type=F32)
    xm = xm + jnp.dot(att_ref[...], wb_ref[...], preferred_element_type=F32)
    xm_ref[...] = xm
    hn = xm * lax.rsqrt(jnp.mean(xm * xm, axis=-1, keepdims=True) + EPS) * fnw_ref[...]
    _store_token_tiles(h_ref, hn)
    logits = jnp.dot(hn, wr_ref[...], precision=lax.Precision.HIGHEST, preferred_element_type=F32)

    lane = lax.broadcasted_iota(jnp.int32, logits.shape, 1)
    big = jnp.int32(LANES)
    neg = -jnp.inf

    def masked_top(vals, mask):
        m = jnp.max(jnp.where(mask, vals, neg), axis=-1, keepdims=True)
        idx = jnp.min(jnp.where(mask & (vals == m), lane, big), axis=-1, keepdims=True)
        return m, idx

    gmask = lane < N_GROUPS
    gmax, gsel = masked_top(logits, gmask)
    gp_top = 1.0 / jnp.sum(jnp.where(gmask, jnp.exp(logits - gmax), 0.0), axis=-1, keepdims=True)
    lo = N_GROUPS + gsel * EXPERTS_PER_GROUP
    emask = (lane >= lo) & (lane < lo + EXPERTS_PER_GROUP)
    m1, i1 = masked_top(logits, emask)
    ex = jnp.where(emask, jnp.exp(logits - m1), 0.0)
    pf = ex / jnp.sum(ex, axis=-1, keepdims=True)
    p1, _ = masked_top(pf, emask)
    p2, i2 = masked_top(pf, emask & (lane != i1))
    denom = p1 + p2
    g1 = gp_top * p1 / denom
    g2 = gp_top * p2 / denom
    e1 = (i1 - N_GROUPS).astype(F32)
    e2 = (i2 - N_GROUPS).astype(F32)
    r_ref[...] = jnp.where(lane == 0, e1, jnp.where(lane == 1, e2, jnp.where(lane == 2, g1, jnp.where(lane == 3, g2, 0.0))))


def _outproj(o_f, o_b, proj, att, x2, gnw, w_a, w_b, fnw, w_r, tm):
    T, D = x2.shape
    row = lambda i: (i, 0)
    const = lambda i: (0, 0)
    return pl.pallas_call(
        _outproj_body,
        grid=(T // tm,),
        in_specs=[
            pl.BlockSpec((tm, GDN_W), row),
            pl.BlockSpec((tm, GDN_W), row),
            pl.BlockSpec((tm, GDN_W), lambda i: (i, COL_Z // GDN_W)),
            pl.BlockSpec((tm, ATT_Q), row),
            pl.BlockSpec((tm, D), row),
            pl.BlockSpec((1, GDN_D), const),
            pl.BlockSpec((GDN_W, D), const),
            pl.BlockSpec((ATT_Q, D), const),
            pl.BlockSpec((1, D), const),
            pl.BlockSpec((D, LANES), const),
        ],
        out_specs=[pl.BlockSpec((tm, D), row), pl.BlockSpec((tm * ROW_TILE, LANES), row),
                   pl.BlockSpec((tm, LANES), row)],
        out_shape=[jax.ShapeDtypeStruct((T, D), F32), jax.ShapeDtypeStruct((T * ROW_TILE, LANES), F32),
                   jax.ShapeDtypeStruct((T, LANES), F32)],
        compiler_params=_cparams(("parallel",)),
        name="outproj_router",
    )(o_f, o_b, proj, att, x2, gnw, w_a, w_b, fnw, w_r)


def _gather_start(src_hbm, dst_buf, sem, idx_ref, base, slot, n):
    for r in range(n):
        src = pl.multiple_of(idx_ref[base + r] * ROW_TILE, ROW_TILE)
        dst = pl.multiple_of((slot * n + r) * ROW_TILE, ROW_TILE)
        pltpu.make_async_copy(src_hbm.at[pl.ds(src, ROW_TILE)], dst_buf.at[pl.ds(dst, ROW_TILE)],
                              sem.at[slot]).start(priority=r % DMA_QUEUES)


def _gather_wait(src_hbm, dst_buf, sem, slot, n):
    dst = pl.multiple_of(slot * n * ROW_TILE, ROW_TILE)
    pltpu.make_async_copy(src_hbm.at[pl.ds(0, n * ROW_TILE)], dst_buf.at[pl.ds(dst, n * ROW_TILE)],
                          sem.at[slot]).wait()


def _experts_body(be_ref, src_ref, nused_ref, wslot_ref, first_ref, nxt_ref,
                  h_hbm, wg_hbm, wu_hbm, wd_hbm, y_ref, xbuf, wg_buf, wu_buf, wd_buf, sem, wsem, *, bm):
    i = pl.program_id(0)
    n_used = nused_ref[0]
    slot = i % 2
    ws = wslot_ref[i]

    def weight_copies(e, s):
        return [pltpu.make_async_copy(hbm.at[e], buf.at[s], wsem.at[s])
                for hbm, buf in ((wg_hbm, wg_buf), (wu_hbm, wu_buf), (wd_hbm, wd_buf))]

    def compute(prefetch_next):
        @pl.when(first_ref[i] == 1)
        def _():
            for c in weight_copies(0, ws):
                c.wait()

            @pl.when(nxt_ref[i] >= 0)
            def _():
                for c in weight_copies(nxt_ref[i], 1 - ws):
                    c.start()

        _gather_wait(h_hbm, xbuf, sem, slot, bm)
        x = _load_token_tiles(xbuf, slot * (bm * ROW_TILE), bm).astype(BF16)
        if prefetch_next:
            _gather_start(h_hbm, xbuf, sem, src_ref, (i + 1) * bm, 1 - slot, bm)
        gate = jnp.dot(x, wg_buf[ws].astype(BF16), preferred_element_type=F32)
        up = jnp.dot(x, wu_buf[ws].astype(BF16), preferred_element_type=F32)
        hid = (_silu(gate) * up).astype(BF16)
        _store_token_tiles(y_ref, jnp.dot(hid, wd_buf[ws].astype(BF16), preferred_element_type=F32))

    @pl.when((i == 0) & (n_used > 0))
    def _():
        for c in weight_copies(be_ref[0], 0):
            c.start()
        _gather_start(h_hbm, xbuf, sem, src_ref, 0, 0, bm)

    @pl.when(i + 1 < n_used)
    def _():
        compute(True)

    @pl.when(i + 1 == n_used)
    def _():
        compute(False)

    @pl.when(i >= n_used)
    def _():
        y_ref[...] = jnp.zeros_like(y_ref)


def _experts(block_expert, src_tok, n_used, h2, w_gate, w_up, w_down, bm):
    D = ROW_TILE * LANES
    P = src_tok.shape[0]
    n_blocks = P // bm
    FF = w_gate.shape[-1]
    pos = jnp.arange(n_blocks, dtype=jnp.int32)
    used = pos < n_used[0]
    first = used & jnp.concatenate([jnp.ones((1,), bool), block_expert[1:] != block_expert[:-1]])
    wslot = ((jnp.cumsum(first.astype(jnp.int32)) - 1) % 2).astype(jnp.int32)
    first_pos = jnp.where(first, pos, n_blocks)
    next_first = jnp.concatenate([lax.cummin(first_pos, reverse=True)[1:], jnp.full((1,), n_blocks, jnp.int32)])
    nxt = jnp.where(next_first < n_blocks, block_expert[jnp.minimum(next_first, n_blocks - 1)], -1).astype(jnp.int32)
    kern = functools.partial(_experts_body, bm=bm)
    hbm = pl.BlockSpec(memory_space=pl.ANY)
    return pl.pallas_call(
        kern,
        grid_spec=pltpu.PrefetchScalarGridSpec(
            num_scalar_prefetch=6,
            grid=(n_blocks,),
            in_specs=[hbm, hbm, hbm, hbm],
            out_specs=pl.BlockSpec((bm * ROW_TILE, LANES), lambda i, *_: (i, 0)),
            scratch_shapes=[
                pltpu.VMEM((2 * bm * ROW_TILE, LANES), F32),
                pltpu.VMEM((2, D, FF), F32), pltpu.VMEM((2, D, FF), F32), pltpu.VMEM((2, FF, D), F32),
                pltpu.SemaphoreType.DMA((2,)), pltpu.SemaphoreType.DMA((2,)),
            ],
        ),
        out_shape=jax.ShapeDtypeStruct((P * ROW_TILE, LANES), F32),
        compiler_params=_cparams(("arbitrary",)),
        name="moe_experts",
    )(block_expert, src_tok, n_used, wslot, first.astype(jnp.int32), nxt, h2, w_gate, w_up, w_down)


def _combine_body(dest_ref, y_hbm, xm_ref, r_ref, fw_ref, o_ref, ybuf, sem, *, tc):
    i = pl.program_id(0)
    n = pl.num_programs(0)
    slot = i % 2
    rows = TOP_K * tc

    def compute(prefetch_next):
        _gather_wait(y_hbm, ybuf, sem, slot, rows)
        route = r_ref[...]
        y0 = _load_token_tiles(ybuf, slot * (rows * ROW_TILE), tc)
        y1 = _load_token_tiles(ybuf, (slot * rows + tc) * ROW_TILE, tc)
        if prefetch_next:
            _gather_start(y_hbm, ybuf, sem, dest_ref, (i + 1) * rows, 1 - slot, rows)
        xo = xm_ref[...] + route[:, 2:3] * y0 + route[:, 3:4] * y1
        o_ref[...] = xo * lax.rsqrt(jnp.mean(xo * xo, axis=-1, keepdims=True) + EPS) * fw_ref[...]

    @pl.when(i == 0)
    def _():
        _gather_start(y_hbm, ybuf, sem, dest_ref, 0, 0, rows)

    @pl.when(i + 1 < n)
    def _():
        compute(True)

    @pl.when(i + 1 == n)
    def _():
        compute(False)


def _combine(dest_blocked, y_buf, x_mid, route, final_w, tc):
    T, D = x_mid.shape
    kern = functools.partial(_combine_body, tc=tc)
    return pl.pallas_call(
        kern,
        grid_spec=pltpu.PrefetchScalarGridSpec(
            num_scalar_prefetch=1,
            grid=(T // tc,),
            in_specs=[
                pl.BlockSpec(memory_space=pl.ANY),
                pl.BlockSpec((tc, D), lambda i, d: (i, 0)),
                pl.BlockSpec((tc, LANES), lambda i, d: (i, 0)),
                pl.BlockSpec((1, D), lambda i, d: (0, 0)),
            ],
            out_specs=pl.BlockSpec((tc, D), lambda i, d: (i, 0)),
            scratch_shapes=[pltpu.VMEM((2 * TOP_K * tc * ROW_TILE, LANES), F32), pltpu.SemaphoreType.DMA((2,))],
        ),
        out_shape=jax.ShapeDtypeStruct((T, D), F32),
        compiler_params=_cparams(("arbitrary",)),
        name="moe_combine",
    )(dest_blocked, y_buf, x_mid, route, final_w.reshape(1, D))


def _scatter_order(n):
    step = max(1, int(n * 0.6180339887))
    while math.gcd(step, n) != 1:
        step += 1
    return (np.arange(n, dtype=np.int64) * step % n).astype(np.int32)


def _layer(x2, B, S, norm_mix_w, w_in, conv_w, a_log, dt_bias, gdn_norm_w, q_norm_w, k_norm_w, w_out,
           norm_ffn_w, w_router_group, w_router_expert, w_gate, w_up, w_down, final_w):
    T, D = x2.shape
    tl = _tiles(B, S)
    perm = np.concatenate([np.arange(0, ATT_D, 2), np.arange(1, ATT_D, 2)])
    o_z = 3 * GDN_W
    o_gate = o_z + GDN_W
    o_qb = o_gate + 2 * GDN_CHAINS
    o_kb = o_qb + ATT_Q
    o_vb = o_kb + ATT_KV
    qb_cols = o_qb + (np.arange(ATT_HEADS)[:, None] * ATT_D + perm[None, :]).reshape(-1)
    kb_cols = o_kb + (np.arange(ATT_KV_HEADS)[:, None] * ATT_D + perm[None, :]).reshape(-1)
    w_all = jnp.concatenate([
        w_in[:, :o_gate], w_in[:, qb_cols], w_in[:, kb_cols], w_in[:, o_vb:o_vb + ATT_KV],
        w_in[:, o_gate:o_qb], jnp.zeros((D, LANES - 2 * GDN_CHAINS), w_in.dtype)], axis=1).astype(BF16)

    proj = _inproj(x2, norm_mix_w, w_all, tl.proj_rows)

    conv_w8 = jnp.concatenate([conv_w, jnp.zeros((SUBLANES - CONV_W, conv_w.shape[1]), F32)], axis=0)
    gp = jnp.zeros((SUBLANES, LANES), F32)
    gp = gp.at[0, GDN_CHAINS:2 * GDN_CHAINS].set(jnp.exp(a_log.astype(F32)).reshape(-1))
    gp = gp.at[1, GDN_CHAINS:2 * GDN_CHAINS].set(dt_bias.astype(F32).reshape(-1))
    q_a, k_a, v_a, gates = _gdn_prep(proj, conv_w8, gp, B, S, tl.prep_rows)
    n_chunks = T // CHUNK
    gates_t = gates[:, :2 * GDN_CHAINS].reshape(n_chunks, CHUNK, 2 * GDN_CHAINS).transpose(0, 2, 1)
    u, wq, ak, dec = _gdn_chunk(q_a, k_a, v_a, gates, gates_t, tl.gdn_chunks)
    o_f, o_b = _gdn_scan(u, wq, ak, dec, B, S, tl.scan_chunks)

    rows = S // GRID_W
    rowp = jnp.repeat(jnp.arange(rows), GRID_W).astype(F32)
    colp = jnp.tile(jnp.arange(GRID_W), rows).astype(F32)
    axis_dims = ATT_D // 2
    inv_freq = ROPE_THETA ** (-jnp.arange(0, axis_dims, 2, dtype=F32) / axis_dims)
    ang = jnp.concatenate([rowp[:, None] * inv_freq, colp[:, None] * inv_freq], axis=-1)
    cosf = jnp.concatenate([jnp.cos(ang), jnp.cos(ang)], axis=-1)
    sinf = jnp.concatenate([-jnp.sin(ang), jnp.sin(ang)], axis=-1)
    q_gain = jnp.max(jnp.abs(q_norm_w)).astype(F32)
    k_gain = jnp.max(jnp.abs(k_norm_w)).astype(F32)
    kmax = (math.sqrt(ATT_D) * k_gain).reshape(1, 1)
    score_bound = ATT_D ** 0.5 * q_gain * k_gain
    safe = (2.0 * score_bound <= SOFTMAX_SAFE_SPAN).astype(jnp.int32).reshape(1)
    qh, kh, vh = _attn_prep(proj, q_norm_w[perm].reshape(1, ATT_D), k_norm_w[perm].reshape(1, ATT_D),
                            cosf, sinf, kmax, B, S, tl.prep_rows)
    att = _attention(safe, qh, kh, vh, B, S, tl.att_q, tl.att_k).reshape(T, ATT_Q)

    w_r = jnp.concatenate([w_router_group, w_router_expert,
                           jnp.zeros((D, LANES - N_GROUPS - N_EXPERTS), F32)], axis=1)
    w_out_bf = w_out.astype(BF16)
    x_mid, h2, route = _outproj(o_f, o_b, proj, att, x2, gdn_norm_w.reshape(1, GDN_D),
                                w_out_bf[:GDN_W], w_out_bf[GDN_W:], norm_ffn_w.reshape(1, D), w_r,
                                tl.proj_rows)

    bm = tl.moe_rows
    n_assign = T * TOP_K
    n_blocks = -(-(n_assign + N_EXPERTS * (bm - 1)) // bm)
    order = _scatter_order(T)
    inv_order = np.argsort(order)
    e_flat = route[:, :TOP_K].astype(jnp.int32)[order].reshape(-1)
    onehot = (e_flat[:, None] == jnp.arange(N_EXPERTS)[None, :]).astype(jnp.int32)
    csum = jnp.cumsum(onehot, axis=0)
    rank = jnp.sum(csum * onehot, axis=1) - 1
    counts = csum[-1]
    padded = (counts + bm - 1) // bm * bm
    pad_end = jnp.cumsum(padded)
    pad_start = pad_end - padded
    dest_ordered = pad_start[e_flat] + rank
    src_tok = jnp.zeros((n_blocks * bm,), jnp.int32).at[dest_ordered].set(
        jnp.asarray(np.repeat(order, TOP_K), jnp.int32))
    dest = dest_ordered.reshape(T, TOP_K)[inv_order]
    block_start = jnp.arange(n_blocks, dtype=jnp.int32) * bm
    block_expert = jnp.minimum(jnp.sum((pad_end[None, :] <= block_start[:, None]).astype(jnp.int32), axis=1),
                               N_EXPERTS - 1)
    n_used = (pad_end[-1:] // bm).astype(jnp.int32)

    y_buf = _experts(block_expert, src_tok, n_used, h2, w_gate, w_up, w_down, bm)

    tc = tl.comb_rows
    dest_blocked = dest.reshape(T // tc, tc, TOP_K).transpose(0, 2, 1).reshape(-1)
    return _combine(dest_blocked, y_buf, x_mid, route, final_w, tc)


def kernel(x, norm_mix_w, w_in, conv_w, a_log, dt_bias, gdn_norm_w, q_norm_w, k_norm_w, w_out, norm_ffn_w,
           w_router_group, w_router_expert, w_gate, w_up, w_down, final_norm_w):
    B, S, D = x.shape
    depth = w_in.shape[0]
    assert depth == 1, "the final norm is fused into the last (only) layer's combine step"
    out = _layer(x.reshape(B * S, D), B, S, norm_mix_w[0], w_in[0], conv_w[0], a_log[0], dt_bias[0],
                 gdn_norm_w[0], q_norm_w[0], k_norm_w[0], w_out[0], norm_ffn_w[0], w_router_group[0],
                 w_router_expert[0], w_gate[0], w_up[0], w_down[0], final_norm_w)
    return out.reshape(B, S, D)
```

```python
import functools
import math
from typing import NamedTuple

import jax
import jax.numpy as jnp
import numpy as np
from jax import lax
from jax.experimental import pallas as pl
from jax.experimental.pallas import tpu as pltpu

F32 = jnp.float32
BF16 = jnp.bfloat16
EPS = 1e-6

GRID_W = 64
GDN_HEADS = 4
GDN_D = 128
CONV_W = 5
CHUNK = 64
ATT_HEADS = 8
ATT_KV_HEADS = 2
ATT_GROUP = ATT_HEADS // ATT_KV_HEADS
ATT_D = 64
ROPE_THETA = 10000.0
N_GROUPS = 4
EXPERTS_PER_GROUP = 8
N_EXPERTS = N_GROUPS * EXPERTS_PER_GROUP
TOP_K = 2

GDN_W = GDN_HEADS * GDN_D
GDN_CHAINS = 2 * GDN_HEADS
ATT_Q = ATT_HEADS * ATT_D
ATT_KV = ATT_KV_HEADS * ATT_D
LANES = 128
SUBLANES = 8
ATT_DP = LANES

COL_Z = 3 * GDN_W
COL_QB = COL_Z + GDN_W
COL_KB = COL_QB + ATT_Q
COL_VB = COL_KB + ATT_KV
COL_GATE = COL_VB + ATT_KV
D_PROJ = COL_GATE + LANES

VMEM_LIMIT = 56 * 1024 * 1024
LOG2E = math.log2(math.e)
SOFTMAX_SAFE_SPAN = 60.0


class Tiles(NamedTuple):
    proj_rows: int
    prep_rows: int
    gdn_chunks: int
    scan_chunks: int
    att_q: int
    att_k: int
    moe_rows: int
    comb_rows: int


def _tile(n, want):
    t = min(n, want)
    assert n % t == 0, (n, want)
    return t


def _tiles(B, S):
    T = B * S
    n_chunks = S // CHUNK
    return Tiles(proj_rows=_tile(T, 512), prep_rows=_tile(S, 512), gdn_chunks=_tile(n_chunks, 4),
                 scan_chunks=_tile(n_chunks, 4), att_q=_tile(S, 256), att_k=_tile(S, 1024),
                 moe_rows=256, comb_rows=_tile(T, 256))


def _cparams(sem):
    return pltpu.CompilerParams(dimension_semantics=sem, vmem_limit_bytes=VMEM_LIMIT)


def _silu(x):
    return x * jax.nn.sigmoid(x)


ROW_TILE = SUBLANES
DMA_QUEUES = 2


def _store_token_tiles(ref, x, base=0):
    rows, d = x.shape
    assert d == ROW_TILE * LANES
    for c in range(ROW_TILE):
        ref[pl.ds(base + c, rows, stride=ROW_TILE), :] = x[:, c * LANES:(c + 1) * LANES]


def _load_token_tiles(ref, base, rows):
    return jnp.concatenate([ref[pl.ds(base + c, rows, stride=ROW_TILE), :] for c in range(ROW_TILE)], axis=-1)


def _inproj_body(x_ref, nw_ref, w_ref, o_ref):
    x = x_ref[...]
    h = x * lax.rsqrt(jnp.mean(x * x, axis=-1, keepdims=True) + EPS) * nw_ref[...]
    o_ref[...] = jnp.dot(h.astype(BF16), w_ref[...], preferred_element_type=F32)


def _inproj(x2, norm_w, w_all, tm):
    T, D = x2.shape
    return pl.pallas_call(
        _inproj_body,
        grid=(T // tm,),
        in_specs=[
            pl.BlockSpec((tm, D), lambda i: (i, 0)),
            pl.BlockSpec((1, D), lambda i: (0, 0)),
            pl.BlockSpec((D, D_PROJ), lambda i: (0, 0)),
        ],
        out_specs=pl.BlockSpec((tm, D_PROJ), lambda i: (i, 0)),
        out_shape=jax.ShapeDtypeStruct((T, D_PROJ), F32),
        compiler_params=_cparams(("parallel",)),
        name="inproj",
    )(x2, norm_w.reshape(1, D), w_all)


def _gdn_prep_body(cur_ref, prev_ref, next_ref, cw_ref, gin_ref, gp_ref,
                   q_ref, k_ref, v_ref, g_ref, ext_ref, *, tr):
    i = pl.program_id(1)
    nr = pl.num_programs(1)
    halo = SUBLANES
    pad = CONV_W // 2
    ext_ref[0:halo, :] = jnp.where(i > 0, prev_ref[...], 0.0)
    ext_ref[halo:halo + tr, :] = cur_ref[...]
    ext_ref[halo + tr:2 * halo + tr, :] = jnp.where(i < nr - 1, next_ref[...], 0.0)
    acc = cw_ref[0:1, :] * ext_ref[pl.ds(halo - pad, tr), :]
    for j in range(1, CONV_W):
        acc = acc + cw_ref[j:j + 1, :] * ext_ref[pl.ds(halo - pad + j, tr), :]
    y = _silu(acc)
    for h in range(GDN_HEADS):
        for base, ref, scale in ((0, q_ref, GDN_D ** -0.5), (GDN_W, k_ref, 1.0)):
            t = y[:, base + h * GDN_D: base + (h + 1) * GDN_D]
            t = t * (lax.rsqrt(jnp.sum(t * t, axis=-1, keepdims=True) + EPS) * scale)
            ref[:, h * GDN_D:(h + 1) * GDN_D] = t
    v_ref[...] = y[:, 2 * GDN_W:]
    gin = gin_ref[...]
    lane = lax.broadcasted_iota(jnp.int32, gin.shape, 1)
    a = gin + gp_ref[1:2, :]
    softplus = jnp.maximum(a, 0.0) + jnp.log1p(jnp.exp(-jnp.abs(a)))
    g_ref[...] = jnp.where(lane < GDN_CHAINS, jax.nn.sigmoid(gin), -gp_ref[0:1, :] * softplus)


def _gdn_prep(proj, conv_w8, gate_params, B, S, tr):
    T = B * S
    nr = S // tr
    C = 3 * GDN_W
    rb = tr // SUBLANES
    nrow8 = T // SUBLANES
    kern = functools.partial(_gdn_prep_body, tr=tr)
    out_sd = jax.ShapeDtypeStruct((T, GDN_W), F32)
    return pl.pallas_call(
        kern,
        grid=(B, nr),
        in_specs=[
            pl.BlockSpec((tr, C), lambda b, i: (b * nr + i, 0)),
            pl.BlockSpec((SUBLANES, C), lambda b, i: (jnp.maximum((b * nr + i) * rb - 1, 0), 0)),
            pl.BlockSpec((SUBLANES, C), lambda b, i: (jnp.minimum((b * nr + i + 1) * rb, nrow8 - 1), 0)),
            pl.BlockSpec((SUBLANES, C), lambda b, i: (0, 0)),
            pl.BlockSpec((tr, LANES), lambda b, i: (b * nr + i, COL_GATE // LANES)),
            pl.BlockSpec((SUBLANES, LANES), lambda b, i: (0, 0)),
        ],
        out_specs=[
            pl.BlockSpec((tr, GDN_W), lambda b, i: (b * nr + i, 0)),
            pl.BlockSpec((tr, GDN_W), lambda b, i: (b * nr + i, 0)),
            pl.BlockSpec((tr, GDN_W), lambda b, i: (b * nr + i, 0)),
            pl.BlockSpec((tr, LANES), lambda b, i: (b * nr + i, 0)),
        ],
        out_shape=[out_sd, out_sd, out_sd, jax.ShapeDtypeStruct((T, LANES), F32)],
        scratch_shapes=[pltpu.VMEM((tr + 2 * SUBLANES, C), F32)],
        compiler_params=_cparams(("parallel", "parallel")),
        name="gdn_prep",
    )(proj, proj, proj, conv_w8, proj, gate_params)


def _bdot(a, b):
    return jnp.dot(a.astype(BF16), b.astype(BF16), preferred_element_type=F32)


def _bdot_nt(a, b):
    return lax.dot_general(a.astype(BF16), b.astype(BF16), (((1,), (1,)), ((), ())),
                           preferred_element_type=F32)


def _gdn_chunk_body(q_ref, k_ref, v_ref, g_ref, gt_ref, u_ref, wq_ref, ak_ref, dec_ref, *, cb):
    C = CHUNK
    row = lax.broadcasted_iota(jnp.int32, (C, C), 0)
    col = lax.broadcasted_iota(jnp.int32, (C, C), 1)
    eye = (row == col).astype(F32)
    masks = ((row >= col, row > col), (row <= col, row < col))
    hi = lax.Precision.HIGHEST
    for c in range(cb):
        rs = slice(c * C, (c + 1) * C)
        gates = g_ref[rs, :]
        gates_t = gt_ref[c]
        g_tot = jnp.sum(gates, axis=0, keepdims=True)
        tot_rows = jnp.sum(gates_t, axis=1, keepdims=True)
        dec_ref[c] = jnp.broadcast_to(jnp.exp(tot_rows[GDN_CHAINS:2 * GDN_CHAINS]), (GDN_CHAINS, LANES))
        chains = []
        for d in range(2):
            incl, strict = masks[d]
            cum = incl.astype(F32)
            gc_cols = jnp.dot(cum, gates, precision=hi, preferred_element_type=F32)
            gc_rows = lax.dot_general(gates_t, cum, (((1,), (1,)), ((), ())), precision=hi,
                                      preferred_element_type=F32)
            for h in range(GDN_HEADS):
                lane_b = d * GDN_HEADS + h
                lane_g = GDN_CHAINS + lane_b
                sl = slice(h * GDN_D, (h + 1) * GDN_D)
                q = q_ref[rs, sl]
                k = k_ref[rs, sl]
                v = v_ref[rs, sl]
                beta = gates[:, lane_b:lane_b + 1]
                gc_col = gc_cols[:, lane_g:lane_g + 1]
                gc_row = gc_rows[lane_g:lane_g + 1, :]
                g_last = g_tot[:, lane_g:lane_g + 1]
                decay = jnp.where(incl, jnp.exp(jnp.where(incl, gc_col - gc_row, 0.0)), 0.0)
                e_col = jnp.exp(gc_col)
                k_beta = k * beta
                kk = _bdot_nt(jnp.concatenate([k_beta, q], axis=0), k)
                neg_l = jnp.where(strict, -kk[:C] * decay, 0.0)
                attn = jnp.where(incl, kk[C:] * decay, 0.0)
                rhs = jnp.concatenate([v * beta, k_beta * e_col], axis=1).astype(BF16)
                wq_ref[d, c, C:2 * C, sl] = (q * e_col).astype(BF16)
                ak_ref[d, c, h, 0:C, :] = attn.astype(BF16)
                ak_ref[d, c, h, C:C + GDN_D, :] = (k * jnp.exp(g_last - gc_col)).T.astype(BF16)
                chains.append((d, sl, neg_l, rhs))
        pws = [nl for (_, _, nl, _) in chains]
        invs = [eye + nl for nl in pws]
        for _ in range(int(math.log2(C)) - 1):
            pws = [_bdot(p, p) for p in pws]
            invs = [t + _bdot(t, p) for t, p in zip(invs, pws)]
        for (d, sl, _, rhs), inv in zip(chains, invs):
            sol = jnp.dot(inv.astype(BF16), rhs, preferred_element_type=F32)
            u_ref[d, rs, sl] = sol[:, :GDN_D]
            wq_ref[d, c, 0:C, sl] = sol[:, GDN_D:].astype(BF16)


def _gdn_chunk(q, k, v, gates, gates_t, cb):
    T = q.shape[0]
    nc = T // CHUNK
    rows = cb * CHUNK
    kern = functools.partial(_gdn_chunk_body, cb=cb)
    wide = pl.BlockSpec((rows, GDN_W), lambda i: (i, 0))
    return pl.pallas_call(
        kern,
        grid=(nc // cb,),
        in_specs=[wide, wide, wide, pl.BlockSpec((rows, LANES), lambda i: (i, 0)),
                  pl.BlockSpec((cb, 2 * GDN_CHAINS, CHUNK), lambda i: (i, 0, 0))],
        out_specs=[
            pl.BlockSpec((2, rows, GDN_W), lambda i: (0, i, 0)),
            pl.BlockSpec((2, cb, 2 * CHUNK, GDN_W), lambda i: (0, i, 0, 0)),
            pl.BlockSpec((2, cb, GDN_HEADS, CHUNK + GDN_D, CHUNK), lambda i: (0, i, 0, 0, 0)),
            pl.BlockSpec((cb, GDN_CHAINS, LANES), lambda i: (i, 0, 0)),
        ],
        out_shape=[
            jax.ShapeDtypeStruct((2, T, GDN_W), F32),
            jax.ShapeDtypeStruct((2, nc, 2 * CHUNK, GDN_W), BF16),
            jax.ShapeDtypeStruct((2, nc, GDN_HEADS, CHUNK + GDN_D, CHUNK), BF16),
            jax.ShapeDtypeStruct((nc, GDN_CHAINS, LANES), F32),
        ],
        compiler_params=_cparams(("parallel",)),
        name="gdn_chunk",
    )(q, k, v, gates, gates_t)


def _gdn_scan_body(uf_ref, wqf_ref, akf_ref, decf_ref, ub_ref, wqb_ref, akb_ref, decb_ref,
                   of_ref, ob_ref, state_ref, *, cs):
    @pl.when(pl.program_id(1) == 0)
    def _():
        state_ref[...] = jnp.zeros_like(state_ref)

    C = CHUNK
    refs = ((uf_ref, wqf_ref, akf_ref, decf_ref, of_ref), (ub_ref, wqb_ref, akb_ref, decb_ref, ob_ref))
    chains = [(d, h) for d in range(2) for h in range(GDN_HEADS)]
    st = [state_ref[j] for j in range(GDN_CHAINS)]
    for step in range(cs):
        cidx = (step, cs - 1 - step)
        ws = [jnp.dot(refs[d][1][cidx[d], :, h * GDN_D:(h + 1) * GDN_D], st[j].astype(BF16),
                      preferred_element_type=F32) for j, (d, h) in enumerate(chains)]
        vn = [(refs[d][0][cidx[d] * C:(cidx[d] + 1) * C, h * GDN_D:(h + 1) * GDN_D] - ws[j][:C]).astype(BF16)
              for j, (d, h) in enumerate(chains)]
        rr = [jnp.dot(refs[d][2][cidx[d], h], vn[j], preferred_element_type=F32)
              for j, (d, h) in enumerate(chains)]
        for j, (d, h) in enumerate(chains):
            c = cidx[d]
            refs[d][4][c * C:(c + 1) * C, h * GDN_D:(h + 1) * GDN_D] = ws[j][C:] + rr[j][:C]
            st[j] = st[j] * refs[d][3][c, j:j + 1, :] + rr[j][C:]
    for j in range(GDN_CHAINS):
        state_ref[j] = st[j]


def _gdn_scan(u, wq, ak, dec, B, S, cs):
    T = B * S
    nb = S // (CHUNK * cs)
    rows = cs * CHUNK
    kern = functools.partial(_gdn_scan_body, cs=cs)
    fwd = lambda b, i: b * nb + i
    bwd = lambda b, i: b * nb + nb - 1 - i

    def specs(d, pos):
        return [
            pl.BlockSpec((None, rows, GDN_W), lambda b, i: (d, pos(b, i), 0)),
            pl.BlockSpec((None, cs, 2 * CHUNK, GDN_W), lambda b, i: (d, pos(b, i), 0, 0)),
            pl.BlockSpec((None, cs, GDN_HEADS, CHUNK + GDN_D, CHUNK), lambda b, i: (d, pos(b, i), 0, 0, 0)),
            pl.BlockSpec((cs, GDN_CHAINS, LANES), lambda b, i: (pos(b, i), 0, 0)),
        ]

    out_sd = jax.ShapeDtypeStruct((T, GDN_W), F32)
    return pl.pallas_call(
        kern,
        grid=(B, nb),
        in_specs=specs(0, fwd) + specs(1, bwd),
        out_specs=[pl.BlockSpec((rows, GDN_W), lambda b, i: (fwd(b, i), 0)),
                   pl.BlockSpec((rows, GDN_W), lambda b, i: (bwd(b, i), 0))],
        out_shape=[out_sd, out_sd],
        scratch_shapes=[pltpu.VMEM((GDN_CHAINS, GDN_D, GDN_D), F32)],
        compiler_params=_cparams(("parallel", "arbitrary")),
        name="gdn_scan",
    )(u, wq, ak, dec, u, wq, ak, dec)


def _attn_prep_body(q_ref, k_ref, v_ref, qw_ref, kw_ref, cos_ref, sin_ref, bias_ref,
                    qo_ref, ko_ref, vo_ref):
    cos = cos_ref[...]
    sin = sin_ref[...]
    rows = cos.shape[0]
    half = ATT_D // 2
    lane = lax.broadcasted_iota(jnp.int32, (rows, LANES), 1)
    low_half = (lane % ATT_D) < half
    first_head = lane < ATT_D
    extra_lane = lane == ATT_D
    mi = lax.broadcasted_iota(jnp.int32, (LANES, LANES), 0) // ATT_D
    mj = lax.broadcasted_iota(jnp.int32, (LANES, LANES), 1) // ATT_D
    head_mean = jnp.where(mi == mj, 1.0 / ATT_D, 0.0).astype(BF16)

    def norm_rope(x, w):
        sq = x * x
        hi = sq.astype(BF16)
        lo = (sq - hi.astype(F32)).astype(BF16)
        ms = (jnp.dot(hi, head_mean, preferred_element_type=F32)
              + jnp.dot(lo, head_mean, preferred_element_type=F32))
        y = x * lax.rsqrt(ms + EPS) * w
        partner = jnp.where(low_half, pltpu.roll(y, LANES - half, 1), pltpu.roll(y, half, 1))
        return y * cos + partner * sin

    def split_heads(r, extra):
        tail = jnp.where(extra_lane, extra, 0.0)
        return (jnp.where(first_head, r, tail).astype(BF16),
                jnp.where(first_head, pltpu.roll(r, ATT_D, 1), tail).astype(BF16))

    scale = LOG2E * ATT_D ** -0.5
    for c in range(ATT_HEADS // 2):
        r = norm_rope(q_ref[:, c * LANES:(c + 1) * LANES], qw_ref[...]) * scale
        qo_ref[2 * c], qo_ref[2 * c + 1] = split_heads(r, bias_ref[...])
    for c in range(ATT_KV_HEADS // 2):
        r = norm_rope(k_ref[:, c * LANES:(c + 1) * LANES], kw_ref[...])
        ko_ref[2 * c], ko_ref[2 * c + 1] = split_heads(r, 1.0)
        vo_ref[2 * c], vo_ref[2 * c + 1] = split_heads(v_ref[:, c * LANES:(c + 1) * LANES], 1.0)


def _attn_prep(proj, qw, kw, cosf, sinf, bias, B, S, tr):
    nr = S // tr
    row = lambda b, i: b * nr + i
    return pl.pallas_call(
        _attn_prep_body,
        grid=(B, nr),
        in_specs=[
            pl.BlockSpec((tr, ATT_Q), lambda b, i: (row(b, i), COL_QB // ATT_Q)),
            pl.BlockSpec((tr, ATT_KV), lambda b, i: (row(b, i), COL_KB // ATT_KV)),
            pl.BlockSpec((tr, ATT_KV), lambda b, i: (row(b, i), COL_VB // ATT_KV)),
            pl.BlockSpec((1, LANES), lambda b, i: (0, 0)),
            pl.BlockSpec((1, LANES), lambda b, i: (0, 0)),
            pl.BlockSpec((tr, LANES), lambda b, i: (i, 0)),
            pl.BlockSpec((tr, LANES), lambda b, i: (i, 0)),
            pl.BlockSpec((1, 1), lambda b, i: (0, 0)),
        ],
        out_specs=[
            pl.BlockSpec((None, ATT_HEADS, tr, ATT_DP), lambda b, i: (b, 0, i, 0)),
            pl.BlockSpec((None, ATT_KV_HEADS, tr, ATT_DP), lambda b, i: (b, 0, i, 0)),
            pl.BlockSpec((None, ATT_KV_HEADS, tr, ATT_DP), lambda b, i: (b, 0, i, 0)),
        ],
        out_shape=[
            jax.ShapeDtypeStruct((B, ATT_HEADS, S, ATT_DP), BF16),
            jax.ShapeDtypeStruct((B, ATT_KV_HEADS, S, ATT_DP), BF16),
            jax.ShapeDtypeStruct((B, ATT_KV_HEADS, S, ATT_DP), BF16),
        ],
        compiler_params=_cparams(("parallel", "parallel")),
        name="attn_prep",
    )(proj, proj, proj, qw, kw, cosf, sinf, bias)


def _attn_body(safe_ref, q_ref, k_ref, v_ref, o_ref, m_ref, acc_ref, *, tq, tk):
    safe = safe_ref[0] != 0
    nk = k_ref.shape[0] // tk
    q = q_ref[...].reshape(ATT_GROUP * tq, ATT_DP)
    acc_ref[...] = jnp.zeros_like(acc_ref)

    def scores(j):
        keys = pl.ds(pl.multiple_of(j * tk, tk), tk)
        s = lax.dot_general(q, k_ref[keys, :], (((1,), (1,)), ((), ())), preferred_element_type=F32)
        return s, v_ref[keys, :]

    @pl.when(safe)
    def _():
        def step(j, carry):
            s, v = scores(j)
            acc_ref[...] += jnp.dot(jnp.exp2(s).astype(BF16), v, preferred_element_type=F32)
            return carry
        lax.fori_loop(0, nk, step, 0)

    @pl.when(jnp.logical_not(safe))
    def _():
        m_ref[...] = jnp.full_like(m_ref, -jnp.inf)

        def step(j, carry):
            s, v = scores(j)
            m_prev = m_ref[...]
            m_new = jnp.maximum(m_prev, jnp.max(s, axis=-1, keepdims=True))
            p = jnp.exp2(s - m_new).astype(BF16)
            acc_ref[...] = jnp.exp2(m_prev - m_new) * acc_ref[...] + jnp.dot(p, v, preferred_element_type=F32)
            m_ref[...] = m_new
            return carry
        lax.fori_loop(0, nk, step, 0)

    acc = acc_ref[...]
    o = acc[:, :ATT_D] / acc[:, ATT_D:ATT_D + 1]
    for h in range(ATT_GROUP):
        o_ref[:, h * ATT_D:(h + 1) * ATT_D] = o[h * tq:(h + 1) * tq].astype(o_ref.dtype)


def _attention(safe, q, k, v, B, S, tq, tk):
    kern = functools.partial(_attn_body, tq=tq, tk=tk)
    gw = ATT_GROUP * ATT_D
    return pl.pallas_call(
        kern,
        grid_spec=pltpu.PrefetchScalarGridSpec(
            num_scalar_prefetch=1,
            grid=(B, ATT_KV_HEADS, S // tq),
            in_specs=[
                pl.BlockSpec((None, ATT_GROUP, tq, ATT_DP), lambda b, g, i, s: (b, g, i, 0)),
                pl.BlockSpec((None, None, S, ATT_DP), lambda b, g, i, s: (b, g, 0, 0)),
                pl.BlockSpec((None, None, S, ATT_DP), lambda b, g, i, s: (b, g, 0, 0)),
            ],
            out_specs=pl.BlockSpec((None, tq, gw), lambda b, g, i, s: (b, i, g)),
            scratch_shapes=[
                pltpu.VMEM((ATT_GROUP * tq, 1), F32),
                pltpu.VMEM((ATT_GROUP * tq, ATT_DP), F32),
            ],
        ),
        out_shape=jax.ShapeDtypeStruct((B, S, ATT_Q), BF16),
        compiler_params=_cparams(("parallel", "parallel", "parallel")),
        name="attention",
    )(safe, q, k, v)


def _outproj_body(of_ref, ob_ref, z_ref, att_ref, x_ref, gnw_ref, wa_ref, wb_ref,
                  fnw_ref, wr_ref, xm_ref, h_ref, r_ref):
    o = of_ref[...] + ob_ref[...]
    z = z_ref[...]
    parts = []
    for h in range(GDN_HEADS):
        sl = slice(h * GDN_D, (h + 1) * GDN_D)
        t = o[:, sl]
        t = t * lax.rsqrt(jnp.mean(t * t, axis=-1, keepdims=True) + EPS) * gnw_ref[...]
        parts.append((t * _silu(z[:, sl])).astype(BF16))
    mix_a = jnp.concatenate(parts, axis=-1)
    xm = x_ref[...] + jnp.dot(mix_a, wa_ref[...], preferred_element_type=F32)
    xm = xm + jnp.dot(att_ref[...], wb_ref[...], preferred_element_type=F32)
    xm_ref[...] = xm
    hn = xm * lax.rsqrt(jnp.mean(xm * xm, axis=-1, keepdims=True) + EPS) * fnw_ref[...]
    _store_token_tiles(h_ref, hn)
    hn_hi = hn.astype(BF16)
    hn_lo = (hn - hn_hi.astype(F32)).astype(BF16)
    both = jnp.dot(hn_hi, wr_ref[...], preferred_element_type=F32)
    logits = (both[:, :LANES] + both[:, LANES:]
              + jnp.dot(hn_lo, wr_ref[:, :LANES], preferred_element_type=F32))

    lane = lax.broadcasted_iota(jnp.int32, logits.shape, 1)
    big = jnp.int32(LANES)
    neg = -jnp.inf

    def masked_top(vals, mask):
        m = jnp.max(jnp.where(mask, vals, neg), axis=-1, keepdims=True)
        idx = jnp.min(jnp.where(mask & (vals == m), lane, big), axis=-1, keepdims=True)
        return m, idx

    gmask = lane < N_GROUPS
    gmax, gsel = masked_top(logits, gmask)
    gp_top = 1.0 / jnp.sum(jnp.where(gmask, jnp.exp(logits - gmax), 0.0), axis=-1, keepdims=True)
    lo = N_GROUPS + gsel * EXPERTS_PER_GROUP
    emask = (lane >= lo) & (lane < lo + EXPERTS_PER_GROUP)
    m1, i1 = masked_top(logits, emask)
    ex = jnp.where(emask, jnp.exp(logits - m1), 0.0)
    pf = ex / jnp.sum(ex, axis=-1, keepdims=True)
    p1, _ = masked_top(pf, emask)
    p2, i2 = masked_top(pf, emask & (lane != i1))
    denom = p1 + p2
    g1 = gp_top * p1 / denom
    g2 = gp_top * p2 / denom
    e1 = (i1 - N_GROUPS).astype(F32)
    e2 = (i2 - N_GROUPS).astype(F32)
    r_ref[...] = jnp.where(lane == 0, e1, jnp.where(lane == 1, e2, jnp.where(lane == 2, g1, jnp.where(lane == 3, g2, 0.0))))


def _outproj(o_f, o_b, proj, att, x2, gnw, w_a, w_b, fnw, w_r, tm):
    T, D = x2.shape
    row = lambda i: (i, 0)
    const = lambda i: (0, 0)
    return pl.pallas_call(
        _outproj_body,
        grid=(T // tm,),
        in_specs=[
            pl.BlockSpec((tm, GDN_W), row),
            pl.BlockSpec((tm, GDN_W), row),
            pl.BlockSpec((tm, GDN_W), lambda i: (i, COL_Z // GDN_W)),
            pl.BlockSpec((tm, ATT_Q), row),
            pl.BlockSpec((tm, D), row),
            pl.BlockSpec((1, GDN_D), const),
            pl.BlockSpec((GDN_W, D), const),
            pl.BlockSpec((ATT_Q, D), const),
            pl.BlockSpec((1, D), const),
            pl.BlockSpec((D, 2 * LANES), const),
        ],
        out_specs=[pl.BlockSpec((tm, D), row), pl.BlockSpec((tm * ROW_TILE, LANES), row),
                   pl.BlockSpec((tm, LANES), row)],
        out_shape=[jax.ShapeDtypeStruct((T, D), F32), jax.ShapeDtypeStruct((T * ROW_TILE, LANES), F32),
                   jax.ShapeDtypeStruct((T, LANES), F32)],
        compiler_params=_cparams(("parallel",)),
        name="outproj_router",
    )(o_f, o_b, proj, att, x2, gnw, w_a, w_b, fnw, w_r)


def _gather_start(src_hbm, dst_buf, sem, idx_ref, base, slot, n):
    for r in range(n):
        src = pl.multiple_of(idx_ref[base + r] * ROW_TILE, ROW_TILE)
        dst = pl.multiple_of((slot * n + r) * ROW_TILE, ROW_TILE)
        pltpu.make_async_copy(src_hbm.at[pl.ds(src, ROW_TILE)], dst_buf.at[pl.ds(dst, ROW_TILE)],
                              sem.at[slot]).start(priority=r % DMA_QUEUES)


def _gather_wait(src_hbm, dst_buf, sem, slot, n):
    dst = pl.multiple_of(slot * n * ROW_TILE, ROW_TILE)
    pltpu.make_async_copy(src_hbm.at[pl.ds(0, n * ROW_TILE)], dst_buf.at[pl.ds(dst, n * ROW_TILE)],
                          sem.at[slot]).wait()


def _experts_body(be_ref, src_ref, nused_ref, wslot_ref, first_ref, nxt_ref,
                  h_hbm, wg_hbm, wu_hbm, wd_hbm, y_ref, xbuf, wg_buf, wu_buf, wd_buf, sem, wsem, *, bm):
    i = pl.program_id(0)
    n_used = nused_ref[0]
    slot = i % 2
    ws = wslot_ref[i]

    def weight_copies(e, s):
        return [pltpu.make_async_copy(hbm.at[e], buf.at[s], wsem.at[s])
                for hbm, buf in ((wg_hbm, wg_buf), (wu_hbm, wu_buf), (wd_hbm, wd_buf))]

    def compute(prefetch_next):
        @pl.when(first_ref[i] == 1)
        def _():
            for c in weight_copies(0, ws):
                c.wait()

            @pl.when(nxt_ref[i] >= 0)
            def _():
                for c in weight_copies(nxt_ref[i], 1 - ws):
                    c.start()

        _gather_wait(h_hbm, xbuf, sem, slot, bm)
        x = _load_token_tiles(xbuf, slot * (bm * ROW_TILE), bm).astype(BF16)
        if prefetch_next:
            _gather_start(h_hbm, xbuf, sem, src_ref, (i + 1) * bm, 1 - slot, bm)
        gate = jnp.dot(x, wg_buf[ws].astype(BF16), preferred_element_type=F32)
        up = jnp.dot(x, wu_buf[ws].astype(BF16), preferred_element_type=F32)
        hid = (_silu(gate) * up).astype(BF16)
        _store_token_tiles(y_ref, jnp.dot(hid, wd_buf[ws].astype(BF16), preferred_element_type=F32))

    @pl.when((i == 0) & (n_used > 0))
    def _():
        for c in weight_copies(be_ref[0], 0):
            c.start()
        _gather_start(h_hbm, xbuf, sem, src_ref, 0, 0, bm)

    @pl.when(i + 1 < n_used)
    def _():
        compute(True)

    @pl.when(i + 1 == n_used)
    def _():
        compute(False)

    @pl.when(i >= n_used)
    def _():
        y_ref[...] = jnp.zeros_like(y_ref)


def _experts(block_expert, src_tok, n_used, h2, w_gate, w_up, w_down, bm):
    D = ROW_TILE * LANES
    P = src_tok.shape[0]
    n_blocks = P // bm
    FF = w_gate.shape[-1]
    pos = jnp.arange(n_blocks, dtype=jnp.int32)
    used = pos < n_used[0]
    first = used & jnp.concatenate([jnp.ones((1,), bool), block_expert[1:] != block_expert[:-1]])
    wslot = ((jnp.cumsum(first.astype(jnp.int32)) - 1) % 2).astype(jnp.int32)
    first_pos = jnp.where(first, pos, n_blocks)
    next_first = jnp.concatenate([lax.cummin(first_pos, reverse=True)[1:], jnp.full((1,), n_blocks, jnp.int32)])
    nxt = jnp.where(next_first < n_blocks, block_expert[jnp.minimum(next_first, n_blocks - 1)], -1).astype(jnp.int32)
    kern = functools.partial(_experts_body, bm=bm)
    hbm = pl.BlockSpec(memory_space=pl.ANY)
    return pl.pallas_call(
        kern,
        grid_spec=pltpu.PrefetchScalarGridSpec(
            num_scalar_prefetch=6,
            grid=(n_blocks,),
            in_specs=[hbm, hbm, hbm, hbm],
            out_specs=pl.BlockSpec((bm * ROW_TILE, LANES), lambda i, *_: (i, 0)),
            scratch_shapes=[
                pltpu.VMEM((2 * bm * ROW_TILE, LANES), F32),
                pltpu.VMEM((2, D, FF), F32), pltpu.VMEM((2, D, FF), F32), pltpu.VMEM((2, FF, D), F32),
                pltpu.SemaphoreType.DMA((2,)), pltpu.SemaphoreType.DMA((2,)),
            ],
        ),
        out_shape=jax.ShapeDtypeStruct((P * ROW_TILE, LANES), F32),
        compiler_params=_cparams(("arbitrary",)),
        name="moe_experts",
    )(block_expert, src_tok, n_used, wslot, first.astype(jnp.int32), nxt, h2, w_gate, w_up, w_down)


def _combine_body(dest_ref, y_hbm, xm_ref, r_ref, fw_ref, o_ref, ybuf, sem, *, tc):
    i = pl.program_id(0)
    n = pl.num_programs(0)
    slot = i % 2
    rows = TOP_K * tc

    def compute(prefetch_next):
        _gather_wait(y_hbm, ybuf, sem, slot, rows)
        route = r_ref[...]
        y0 = _load_token_tiles(ybuf, slot * (rows * ROW_TILE), tc)
        y1 = _load_token_tiles(ybuf, (slot * rows + tc) * ROW_TILE, tc)
        if prefetch_next:
            _gather_start(y_hbm, ybuf, sem, dest_ref, (i + 1) * rows, 1 - slot, rows)
        xo = xm_ref[...] + route[:, 2:3] * y0 + route[:, 3:4] * y1
        o_ref[...] = xo * lax.rsqrt(jnp.mean(xo * xo, axis=-1, keepdims=True) + EPS) * fw_ref[...]

    @pl.when(i == 0)
    def _():
        _gather_start(y_hbm, ybuf, sem, dest_ref, 0, 0, rows)

    @pl.when(i + 1 < n)
    def _():
        compute(True)

    @pl.when(i + 1 == n)
    def _():
        compute(False)


def _combine(dest_blocked, y_buf, x_mid, route, final_w, tc):
    T, D = x_mid.shape
    kern = functools.partial(_combine_body, tc=tc)
    return pl.pallas_call(
        kern,
        grid_spec=pltpu.PrefetchScalarGridSpec(
            num_scalar_prefetch=1,
            grid=(T // tc,),
            in_specs=[
                pl.BlockSpec(memory_space=pl.ANY),
                pl.BlockSpec((tc, D), lambda i, d: (i, 0)),
                pl.BlockSpec((tc, LANES), lambda i, d: (i, 0)),
                pl.BlockSpec((1, D), lambda i, d: (0, 0)),
            ],
            out_specs=pl.BlockSpec((tc, D), lambda i, d: (i, 0)),
            scratch_shapes=[pltpu.VMEM((2 * TOP_K * tc * ROW_TILE, LANES), F32), pltpu.SemaphoreType.DMA((2,))],
        ),
        out_shape=jax.ShapeDtypeStruct((T, D), F32),
        compiler_params=_cparams(("arbitrary",)),
        name="moe_combine",
    )(dest_blocked, y_buf, x_mid, route, final_w.reshape(1, D))


def _layer(x2, B, S, norm_mix_w, w_in, conv_w, a_log, dt_bias, gdn_norm_w, q_norm_w, k_norm_w, w_out,
           norm_ffn_w, w_router_group, w_router_expert, w_gate, w_up, w_down, final_w):
    T, D = x2.shape
    tl = _tiles(B, S)
    perm = np.concatenate([np.arange(0, ATT_D, 2), np.arange(1, ATT_D, 2)])
    o_z = 3 * GDN_W
    o_gate = o_z + GDN_W
    o_qb = o_gate + 2 * GDN_CHAINS
    o_kb = o_qb + ATT_Q
    o_vb = o_kb + ATT_KV
    qb_cols = o_qb + (np.arange(ATT_HEADS)[:, None] * ATT_D + perm[None, :]).reshape(-1)
    kb_cols = o_kb + (np.arange(ATT_KV_HEADS)[:, None] * ATT_D + perm[None, :]).reshape(-1)
    w_all = jnp.concatenate([
        w_in[:, :o_gate], w_in[:, qb_cols], w_in[:, kb_cols], w_in[:, o_vb:o_vb + ATT_KV],
        w_in[:, o_gate:o_qb], jnp.zeros((D, LANES - 2 * GDN_CHAINS), w_in.dtype)], axis=1).astype(BF16)

    proj = _inproj(x2, norm_mix_w, w_all, tl.proj_rows)

    conv_w8 = jnp.concatenate([conv_w, jnp.zeros((SUBLANES - CONV_W, conv_w.shape[1]), F32)], axis=0)
    gp = jnp.zeros((SUBLANES, LANES), F32)
    gp = gp.at[0, GDN_CHAINS:2 * GDN_CHAINS].set(jnp.exp(a_log.astype(F32)).reshape(-1))
    gp = gp.at[1, GDN_CHAINS:2 * GDN_CHAINS].set(dt_bias.astype(F32).reshape(-1))
    q_a, k_a, v_a, gates = _gdn_prep(proj, conv_w8, gp, B, S, tl.prep_rows)
    n_chunks = T // CHUNK
    gates_t = gates[:, :2 * GDN_CHAINS].reshape(n_chunks, CHUNK, 2 * GDN_CHAINS).transpose(0, 2, 1)
    u, wq, ak, dec = _gdn_chunk(q_a, k_a, v_a, gates, gates_t, tl.gdn_chunks)
    o_f, o_b = _gdn_scan(u, wq, ak, dec, B, S, tl.scan_chunks)

    rows = S // GRID_W
    rowp = jnp.repeat(jnp.arange(rows), GRID_W).astype(F32)
    colp = jnp.tile(jnp.arange(GRID_W), rows).astype(F32)
    axis_dims = ATT_D // 2
    inv_freq = ROPE_THETA ** (-jnp.arange(0, axis_dims, 2, dtype=F32) / axis_dims)
    ang = jnp.concatenate([rowp[:, None] * inv_freq, colp[:, None] * inv_freq], axis=-1)
    cosf = jnp.tile(jnp.cos(ang), (1, LANES // axis_dims))
    sinf = jnp.tile(jnp.concatenate([-jnp.sin(ang), jnp.sin(ang)], axis=-1), (1, LANES // ATT_D))
    q_gain = jnp.max(jnp.abs(q_norm_w)).astype(F32)
    k_gain = jnp.max(jnp.abs(k_norm_w)).astype(F32)
    score_bound = ATT_D ** 0.5 * q_gain * k_gain
    safe = (2.0 * score_bound <= SOFTMAX_SAFE_SPAN).astype(jnp.int32).reshape(1)
    bias = (-LOG2E * score_bound).reshape(1, 1)
    pair = lambda w: jnp.tile(w[perm], LANES // ATT_D).reshape(1, LANES)
    qh, kh, vh = _attn_prep(proj, pair(q_norm_w), pair(k_norm_w), cosf, sinf, bias, B, S, tl.prep_rows)
    att = _attention(safe, qh, kh, vh, B, S, tl.att_q, tl.att_k).reshape(T, ATT_Q)

    w_r32 = jnp.concatenate([w_router_group, w_router_expert,
                             jnp.zeros((D, LANES - N_GROUPS - N_EXPERTS), F32)], axis=1).astype(F32)
    w_r_hi = w_r32.astype(BF16)
    w_r = jnp.concatenate([w_r_hi, (w_r32 - w_r_hi.astype(F32)).astype(BF16)], axis=1)
    w_out_bf = w_out.astype(BF16)
    x_mid, h2, route = _outproj(o_f, o_b, proj, att, x2, gdn_norm_w.reshape(1, GDN_D),
                                w_out_bf[:GDN_W], w_out_bf[GDN_W:], norm_ffn_w.reshape(1, D), w_r,
                                tl.proj_rows)

    bm = tl.moe_rows
    n_assign = T * TOP_K
    n_blocks = -(-(n_assign + N_EXPERTS * (bm - 1)) // bm)
    e_flat = route[:, :TOP_K].astype(jnp.int32).reshape(-1)
    onehot = (e_flat[:, None] == jnp.arange(N_EXPERTS)[None, :]).astype(jnp.int32)
    csum = jnp.cumsum(onehot, axis=0)
    rank = jnp.sum(csum * onehot, axis=1) - 1
    counts = csum[-1]
    padded = (counts + bm - 1) // bm * bm
    pad_end = jnp.cumsum(padded)
    pad_start = pad_end - padded
    dest = pad_start[e_flat] + rank
    filler = jnp.arange(n_blocks * bm, dtype=jnp.int32) % T
    src_tok = filler.at[dest].set(jnp.arange(n_assign, dtype=jnp.int32) // TOP_K)
    block_start = jnp.arange(n_blocks, dtype=jnp.int32) * bm
    block_expert = jnp.minimum(jnp.sum((pad_end[None, :] <= block_start[:, None]).astype(jnp.int32), axis=1),
                               N_EXPERTS - 1)
    n_used = (pad_end[-1:] // bm).astype(jnp.int32)

    y_buf = _experts(block_expert, src_tok, n_used, h2, w_gate, w_up, w_down, bm)

    tc = tl.comb_rows
    dest_blocked = dest.reshape(T // tc, tc, TOP_K).transpose(0, 2, 1).reshape(-1)
    return _combine(dest_blocked, y_buf, x_mid, route, final_w, tc)


def kernel(x, norm_mix_w, w_in, conv_w, a_log, dt_bias, gdn_norm_w, q_norm_w, k_norm_w, w_out, norm_ffn_w,
           w_router_group, w_router_expert, w_gate, w_up, w_down, final_norm_w):
    B, S, D = x.shape
    depth = w_in.shape[0]
    assert depth == 1, "the final norm is fused into the last (only) layer's combine step"
    out = _layer(x.reshape(B * S, D), B, S, norm_mix_w[0], w_in[0], conv_w[0], a_log[0], dt_bias[0],
                 gdn_norm_w[0], q_norm_w[0], k_norm_w[0], w_out[0], norm_ffn_w[0], w_router_group[0],
                 w_router_expert[0], w_gate[0], w_up[0], w_down[0], final_norm_w)
    return out.reshape(B, S, D)
```

```python
import functools
import math
from typing import NamedTuple

import jax
import jax.numpy as jnp
import numpy as np
from jax import lax
from jax.experimental import pallas as pl
from jax.experimental.pallas import tpu as pltpu

F32 = jnp.float32
BF16 = jnp.bfloat16
EPS = 1e-6

GRID_W = 64
GDN_HEADS = 4
GDN_D = 128
CONV_W = 5
CHUNK = 64
ATT_HEADS = 8
ATT_KV_HEADS = 2
ATT_GROUP = ATT_HEADS // ATT_KV_HEADS
ATT_D = 64
ROPE_THETA = 10000.0
N_GROUPS = 4
EXPERTS_PER_GROUP = 8
N_EXPERTS = N_GROUPS * EXPERTS_PER_GROUP
TOP_K = 2

GDN_W = GDN_HEADS * GDN_D
GDN_CHAINS = 2 * GDN_HEADS
ATT_Q = ATT_HEADS * ATT_D
ATT_KV = ATT_KV_HEADS * ATT_D
LANES = 128
SUBLANES = 8
SUBLANES_BF16 = 2 * SUBLANES
ATT_DP = LANES

COL_Z = 3 * GDN_W
COL_QB = COL_Z + GDN_W
COL_KB = COL_QB + ATT_Q
COL_VB = COL_KB + ATT_KV
COL_GATE = COL_VB + ATT_KV
D_PROJ = COL_GATE + LANES

VMEM_LIMIT = 56 * 1024 * 1024
LOG2E = math.log2(math.e)
SOFTMAX_SAFE_SPAN = 60.0


class Tiles(NamedTuple):
    proj_rows: int
    prep_rows: int
    gdn_chunks: int
    scan_chunks: int
    att_q: int
    att_k: int
    moe_rows: int
    comb_rows: int


def _tile(n, want):
    t = min(n, want)
    assert n % t == 0, (n, want)
    return t


def _tiles(B, S):
    T = B * S
    n_chunks = S // CHUNK
    return Tiles(proj_rows=_tile(T, 512), prep_rows=_tile(S, 512), gdn_chunks=_tile(n_chunks, 4),
                 scan_chunks=_tile(n_chunks, 4), att_q=_tile(S, 256), att_k=_tile(S, 1024),
                 moe_rows=256, comb_rows=_tile(T, 256))


def _cparams(sem):
    return pltpu.CompilerParams(dimension_semantics=sem, vmem_limit_bytes=VMEM_LIMIT)


def _silu(x):
    return x * jax.nn.sigmoid(x)


ROW_TILE = SUBLANES
DMA_QUEUES = 2


def _store_token_tiles(ref, x, base=0):
    rows, d = x.shape
    assert d == ROW_TILE * LANES
    for c in range(ROW_TILE):
        ref[pl.ds(base + c, rows, stride=ROW_TILE), :] = x[:, c * LANES:(c + 1) * LANES]


def _load_token_tiles(ref, base, rows):
    return jnp.concatenate([ref[pl.ds(base + c, rows, stride=ROW_TILE), :] for c in range(ROW_TILE)], axis=-1)


def _inproj_body(x_ref, nw_ref, w_ref, o_ref, g_ref):
    x = x_ref[...]
    h = x * lax.rsqrt(jnp.mean(x * x, axis=-1, keepdims=True) + EPS) * nw_ref[...]
    acc = jnp.dot(h.astype(BF16), w_ref[...], preferred_element_type=F32)
    o_ref[...] = acc[:, :COL_GATE].astype(BF16)
    g_ref[...] = acc[:, COL_GATE:]


def _inproj(x2, norm_w, w_all, tm):
    T, D = x2.shape
    return pl.pallas_call(
        _inproj_body,
        grid=(T // tm,),
        in_specs=[
            pl.BlockSpec((tm, D), lambda i: (i, 0)),
            pl.BlockSpec((1, D), lambda i: (0, 0)),
            pl.BlockSpec((D, D_PROJ), lambda i: (0, 0)),
        ],
        out_specs=[pl.BlockSpec((tm, COL_GATE), lambda i: (i, 0)), pl.BlockSpec((tm, LANES), lambda i: (i, 0))],
        out_shape=[jax.ShapeDtypeStruct((T, COL_GATE), BF16), jax.ShapeDtypeStruct((T, LANES), F32)],
        compiler_params=_cparams(("parallel",)),
        name="inproj",
    )(x2, norm_w.reshape(1, D), w_all)


def _gdn_prep_body(cur_ref, prev_ref, next_ref, cw_ref, gin_ref, gp_ref,
                   q_ref, k_ref, v_ref, g_ref, ext_ref, *, tr):
    i = pl.program_id(1)
    nr = pl.num_programs(1)
    halo = prev_ref.shape[0]
    pad = CONV_W // 2
    ext_ref[0:halo, :] = jnp.where(i > 0, prev_ref[...].astype(F32), 0.0)
    ext_ref[halo:halo + tr, :] = cur_ref[...].astype(F32)
    ext_ref[halo + tr:2 * halo + tr, :] = jnp.where(i < nr - 1, next_ref[...].astype(F32), 0.0)
    acc = cw_ref[0:1, :] * ext_ref[pl.ds(halo - pad, tr), :]
    for j in range(1, CONV_W):
        acc = acc + cw_ref[j:j + 1, :] * ext_ref[pl.ds(halo - pad + j, tr), :]
    y = _silu(acc)
    for h in range(GDN_HEADS):
        for base, ref, scale in ((0, q_ref, GDN_D ** -0.5), (GDN_W, k_ref, 1.0)):
            t = y[:, base + h * GDN_D: base + (h + 1) * GDN_D]
            t = t * (lax.rsqrt(jnp.sum(t * t, axis=-1, keepdims=True) + EPS) * scale)
            ref[:, h * GDN_D:(h + 1) * GDN_D] = t.astype(ref.dtype)
    v_ref[...] = y[:, 2 * GDN_W:].astype(v_ref.dtype)
    gin = gin_ref[...]
    lane = lax.broadcasted_iota(jnp.int32, gin.shape, 1)
    a = gin + gp_ref[1:2, :]
    softplus = jnp.maximum(a, 0.0) + jnp.log1p(jnp.exp(-jnp.abs(a)))
    g_ref[...] = jnp.where(lane < GDN_CHAINS, jax.nn.sigmoid(gin), -gp_ref[0:1, :] * softplus)


def _gdn_prep(proj, gate_logits, conv_w8, gate_params, B, S, tr):
    T = B * S
    nr = S // tr
    C = 3 * GDN_W
    halo = SUBLANES_BF16
    rb = tr // halo
    n_halo = T // halo
    kern = functools.partial(_gdn_prep_body, tr=tr)
    out_sd = jax.ShapeDtypeStruct((T, GDN_W), BF16)
    return pl.pallas_call(
        kern,
        grid=(B, nr),
        in_specs=[
            pl.BlockSpec((tr, C), lambda b, i: (b * nr + i, 0)),
            pl.BlockSpec((halo, C), lambda b, i: (jnp.maximum((b * nr + i) * rb - 1, 0), 0)),
            pl.BlockSpec((halo, C), lambda b, i: (jnp.minimum((b * nr + i + 1) * rb, n_halo - 1), 0)),
            pl.BlockSpec((SUBLANES, C), lambda b, i: (0, 0)),
            pl.BlockSpec((tr, LANES), lambda b, i: (b * nr + i, 0)),
            pl.BlockSpec((SUBLANES, LANES), lambda b, i: (0, 0)),
        ],
        out_specs=[
            pl.BlockSpec((tr, GDN_W), lambda b, i: (b * nr + i, 0)),
            pl.BlockSpec((tr, GDN_W), lambda b, i: (b * nr + i, 0)),
            pl.BlockSpec((tr, GDN_W), lambda b, i: (b * nr + i, 0)),
            pl.BlockSpec((tr, LANES), lambda b, i: (b * nr + i, 0)),
        ],
        out_shape=[out_sd, out_sd, out_sd, jax.ShapeDtypeStruct((T, LANES), F32)],
        scratch_shapes=[pltpu.VMEM((tr + 2 * halo, C), F32)],
        compiler_params=_cparams(("parallel", "parallel")),
        name="gdn_prep",
    )(proj, proj, proj, conv_w8, gate_logits, gate_params)


def _bdot(a, b):
    return jnp.dot(a.astype(BF16), b.astype(BF16), preferred_element_type=F32)


def _bdot_nt(a, b):
    return lax.dot_general(a.astype(BF16), b.astype(BF16), (((1,), (1,)), ((), ())),
                           preferred_element_type=F32)


def _gdn_chunk_body(q_ref, k_ref, v_ref, g_ref, gt_ref, u_ref, wq_ref, ak_ref, dec_ref, *, cb):
    C = CHUNK
    row = lax.broadcasted_iota(jnp.int32, (C, C), 0)
    col = lax.broadcasted_iota(jnp.int32, (C, C), 1)
    eye = (row == col).astype(F32)
    masks = ((row >= col, row > col), (row <= col, row < col))
    hi = lax.Precision.HIGHEST
    chains = []
    for c in range(cb):
        rs = slice(c * C, (c + 1) * C)
        gates = g_ref[rs, :]
        gates_t = gt_ref[c]
        g_tot = jnp.sum(gates, axis=0, keepdims=True)
        tot_rows = jnp.sum(gates_t, axis=1, keepdims=True)
        dec_ref[c] = jnp.broadcast_to(jnp.exp(tot_rows[GDN_CHAINS:2 * GDN_CHAINS]), (GDN_CHAINS, LANES))
        for d in range(2):
            incl, strict = masks[d]
            cum = incl.astype(F32)
            gc_cols = jnp.dot(cum, gates, precision=hi, preferred_element_type=F32)
            gc_rows = lax.dot_general(gates_t, cum, (((1,), (1,)), ((), ())), precision=hi,
                                      preferred_element_type=F32)
            for h in range(GDN_HEADS):
                lane_b = d * GDN_HEADS + h
                lane_g = GDN_CHAINS + lane_b
                sl = slice(h * GDN_D, (h + 1) * GDN_D)
                q = q_ref[rs, sl].astype(F32)
                k = k_ref[rs, sl].astype(F32)
                v = v_ref[rs, sl].astype(F32)
                beta = gates[:, lane_b:lane_b + 1]
                gc_col = gc_cols[:, lane_g:lane_g + 1]
                gc_row = gc_rows[lane_g:lane_g + 1, :]
                g_last = g_tot[:, lane_g:lane_g + 1]
                decay = jnp.where(incl, jnp.exp(jnp.where(incl, gc_col - gc_row, 0.0)), 0.0)
                e_col = jnp.exp(gc_col)
                k_beta = k * beta
                kk = _bdot_nt(jnp.concatenate([k_beta, q], axis=0), k)
                neg_l = jnp.where(strict, -kk[:C] * decay, 0.0)
                attn = jnp.where(incl, kk[C:] * decay, 0.0)
                rhs = jnp.concatenate([v * beta, k_beta * e_col], axis=1).astype(BF16)
                wq_ref[d, c, C:2 * C, sl] = (q * e_col).astype(BF16)
                ak_ref[d, c, h, 0:C, :] = attn.astype(BF16)
                ak_ref[d, c, h, C:C + GDN_D, :] = (k * jnp.exp(g_last - gc_col)).T.astype(BF16)
                chains.append((d, c, rs, sl, neg_l, rhs))
    zs = [jnp.concatenate([ch[4], eye], axis=1) for ch in chains]
    keep_s = lax.broadcasted_iota(jnp.int32, (C, 2 * C), 1) >= C
    for _ in range(int(math.log2(C))):
        zs = [_bdot(z[:, :C], z) + jnp.where(keep_s, z, 0.0) for z in zs]
    for (d, c, rs, sl, _, rhs), z in zip(chains, zs):
        sol = jnp.dot(z[:, C:].astype(BF16), rhs, preferred_element_type=F32)
        u_ref[d, rs, sl] = sol[:, :GDN_D]
        wq_ref[d, c, 0:C, sl] = sol[:, GDN_D:].astype(BF16)


def _gdn_chunk(q, k, v, gates, gates_t, cb):
    T = q.shape[0]
    nc = T // CHUNK
    rows = cb * CHUNK
    kern = functools.partial(_gdn_chunk_body, cb=cb)
    wide = pl.BlockSpec((rows, GDN_W), lambda i: (i, 0))
    return pl.pallas_call(
        kern,
        grid=(nc // cb,),
        in_specs=[wide, wide, wide, pl.BlockSpec((rows, LANES), lambda i: (i, 0)),
                  pl.BlockSpec((cb, 2 * GDN_CHAINS, CHUNK), lambda i: (i, 0, 0))],
        out_specs=[
            pl.BlockSpec((2, rows, GDN_W), lambda i: (0, i, 0)),
            pl.BlockSpec((2, cb, 2 * CHUNK, GDN_W), lambda i: (0, i, 0, 0)),
            pl.BlockSpec((2, cb, GDN_HEADS, CHUNK + GDN_D, CHUNK), lambda i: (0, i, 0, 0, 0)),
            pl.BlockSpec((cb, GDN_CHAINS, LANES), lambda i: (i, 0, 0)),
        ],
        out_shape=[
            jax.ShapeDtypeStruct((2, T, GDN_W), F32),
            jax.ShapeDtypeStruct((2, nc, 2 * CHUNK, GDN_W), BF16),
            jax.ShapeDtypeStruct((2, nc, GDN_HEADS, CHUNK + GDN_D, CHUNK), BF16),
            jax.ShapeDtypeStruct((nc, GDN_CHAINS, LANES), F32),
        ],
        compiler_params=_cparams(("parallel",)),
        name="gdn_chunk",
    )(q, k, v, gates, gates_t)


def _gdn_scan_body(uf_ref, wqf_ref, akf_ref, decf_ref, ub_ref, wqb_ref, akb_ref, decb_ref,
                   of_ref, ob_ref, state_ref, *, cs):
    @pl.when(pl.program_id(1) == 0)
    def _():
        state_ref[...] = jnp.zeros_like(state_ref)

    C = CHUNK
    refs = ((uf_ref, wqf_ref, akf_ref, decf_ref, of_ref), (ub_ref, wqb_ref, akb_ref, decb_ref, ob_ref))
    chains = [(d, h) for d in range(2) for h in range(GDN_HEADS)]
    st = [state_ref[j] for j in range(GDN_CHAINS)]
    for step in range(cs):
        cidx = (step, cs - 1 - step)
        ws = [jnp.dot(refs[d][1][cidx[d], :, h * GDN_D:(h + 1) * GDN_D], st[j].astype(BF16),
                      preferred_element_type=F32) for j, (d, h) in enumerate(chains)]
        vn = [(refs[d][0][cidx[d] * C:(cidx[d] + 1) * C, h * GDN_D:(h + 1) * GDN_D] - ws[j][:C]).astype(BF16)
              for j, (d, h) in enumerate(chains)]
        rr = [jnp.dot(refs[d][2][cidx[d], h], vn[j], preferred_element_type=F32)
              for j, (d, h) in enumerate(chains)]
        for j, (d, h) in enumerate(chains):
            c = cidx[d]
            refs[d][4][c * C:(c + 1) * C, h * GDN_D:(h + 1) * GDN_D] = ws[j][C:] + rr[j][:C]
            st[j] = st[j] * refs[d][3][c, j:j + 1, :] + rr[j][C:]
    for j in range(GDN_CHAINS):
        state_ref[j] = st[j]


def _gdn_scan(u, wq, ak, dec, B, S, cs):
    T = B * S
    nb = S // (CHUNK * cs)
    rows = cs * CHUNK
    kern = functools.partial(_gdn_scan_body, cs=cs)
    fwd = lambda b, i: b * nb + i
    bwd = lambda b, i: b * nb + nb - 1 - i

    def specs(d, pos):
        return [
            pl.BlockSpec((None, rows, GDN_W), lambda b, i: (d, pos(b, i), 0)),
            pl.BlockSpec((None, cs, 2 * CHUNK, GDN_W), lambda b, i: (d, pos(b, i), 0, 0)),
            pl.BlockSpec((None, cs, GDN_HEADS, CHUNK + GDN_D, CHUNK), lambda b, i: (d, pos(b, i), 0, 0, 0)),
            pl.BlockSpec((cs, GDN_CHAINS, LANES), lambda b, i: (pos(b, i), 0, 0)),
        ]

    out_sd = jax.ShapeDtypeStruct((T, GDN_W), F32)
    return pl.pallas_call(
        kern,
        grid=(B, nb),
        in_specs=specs(0, fwd) + specs(1, bwd),
        out_specs=[pl.BlockSpec((rows, GDN_W), lambda b, i: (fwd(b, i), 0)),
                   pl.BlockSpec((rows, GDN_W), lambda b, i: (bwd(b, i), 0))],
        out_shape=[out_sd, out_sd],
        scratch_shapes=[pltpu.VMEM((GDN_CHAINS, GDN_D, GDN_D), F32)],
        compiler_params=_cparams(("parallel", "arbitrary")),
        name="gdn_scan",
    )(u, wq, ak, dec, u, wq, ak, dec)


def _attn_prep_body(q_ref, k_ref, v_ref, qw_ref, kw_ref, cos_ref, sin_ref, bias_ref,
                    qo_ref, ko_ref, vo_ref):
    cos = cos_ref[...]
    sin = sin_ref[...]
    rows = cos.shape[0]
    half = ATT_D // 2
    lane = lax.broadcasted_iota(jnp.int32, (rows, LANES), 1)
    low_half = (lane % ATT_D) < half
    first_head = lane < ATT_D
    extra_lane = lane == ATT_D
    mi = lax.broadcasted_iota(jnp.int32, (LANES, LANES), 0) // ATT_D
    mj = lax.broadcasted_iota(jnp.int32, (LANES, LANES), 1) // ATT_D
    head_mean = jnp.where(mi == mj, 1.0 / ATT_D, 0.0).astype(BF16)

    def norm_rope(x, w):
        sq = x * x
        hi = sq.astype(BF16)
        lo = (sq - hi.astype(F32)).astype(BF16)
        ms = (jnp.dot(hi, head_mean, preferred_element_type=F32)
              + jnp.dot(lo, head_mean, preferred_element_type=F32))
        y = x * lax.rsqrt(ms + EPS) * w
        partner = jnp.where(low_half, pltpu.roll(y, LANES - half, 1), pltpu.roll(y, half, 1))
        return y * cos + partner * sin

    def split_heads(r, extra):
        tail = jnp.where(extra_lane, extra, 0.0)
        return (jnp.where(first_head, r, tail).astype(BF16),
                jnp.where(first_head, pltpu.roll(r, ATT_D, 1), tail).astype(BF16))

    scale = LOG2E * ATT_D ** -0.5
    for c in range(ATT_HEADS // 2):
        r = norm_rope(q_ref[:, c * LANES:(c + 1) * LANES].astype(F32), qw_ref[...]) * scale
        qo_ref[2 * c], qo_ref[2 * c + 1] = split_heads(r, bias_ref[...])
    for c in range(ATT_KV_HEADS // 2):
        r = norm_rope(k_ref[:, c * LANES:(c + 1) * LANES].astype(F32), kw_ref[...])
        ko_ref[2 * c], ko_ref[2 * c + 1] = split_heads(r, 1.0)
        vo_ref[2 * c], vo_ref[2 * c + 1] = split_heads(v_ref[:, c * LANES:(c + 1) * LANES].astype(F32), 1.0)


def _attn_prep(proj, qw, kw, cosf, sinf, bias, B, S, tr):
    nr = S // tr
    row = lambda b, i: b * nr + i
    return pl.pallas_call(
        _attn_prep_body,
        grid=(B, nr),
        in_specs=[
            pl.BlockSpec((tr, ATT_Q), lambda b, i: (row(b, i), COL_QB // ATT_Q)),
            pl.BlockSpec((tr, ATT_KV), lambda b, i: (row(b, i), COL_KB // ATT_KV)),
            pl.BlockSpec((tr, ATT_KV), lambda b, i: (row(b, i), COL_VB // ATT_KV)),
            pl.BlockSpec((1, LANES), lambda b, i: (0, 0)),
            pl.BlockSpec((1, LANES), lambda b, i: (0, 0)),
            pl.BlockSpec((tr, LANES), lambda b, i: (i, 0)),
            pl.BlockSpec((tr, LANES), lambda b, i: (i, 0)),
            pl.BlockSpec((1, 1), lambda b, i: (0, 0)),
        ],
        out_specs=[
            pl.BlockSpec((None, ATT_HEADS, tr, ATT_DP), lambda b, i: (b, 0, i, 0)),
            pl.BlockSpec((None, ATT_KV_HEADS, tr, ATT_DP), lambda b, i: (b, 0, i, 0)),
            pl.BlockSpec((None, ATT_KV_HEADS, tr, ATT_DP), lambda b, i: (b, 0, i, 0)),
        ],
        out_shape=[
            jax.ShapeDtypeStruct((B, ATT_HEADS, S, ATT_DP), BF16),
            jax.ShapeDtypeStruct((B, ATT_KV_HEADS, S, ATT_DP), BF16),
            jax.ShapeDtypeStruct((B, ATT_KV_HEADS, S, ATT_DP), BF16),
        ],
        compiler_params=_cparams(("parallel", "parallel")),
        name="attn_prep",
    )(proj, proj, proj, qw, kw, cosf, sinf, bias)


def _attn_body(safe_ref, q_ref, k_ref, v_ref, o_ref, m_ref, acc_ref, *, tq, tk):
    safe = safe_ref[0] != 0
    nk = k_ref.shape[0] // tk
    q = q_ref[...].reshape(ATT_GROUP * tq, ATT_DP)
    acc_ref[...] = jnp.zeros_like(acc_ref)

    def scores(j):
        keys = pl.ds(pl.multiple_of(j * tk, tk), tk)
        s = lax.dot_general(q, k_ref[keys, :], (((1,), (1,)), ((), ())), preferred_element_type=F32)
        return s, v_ref[keys, :]

    @pl.when(safe)
    def _():
        def step(j, carry):
            s, v = scores(j)
            acc_ref[...] += jnp.dot(jnp.exp2(s).astype(BF16), v, preferred_element_type=F32)
            return carry
        lax.fori_loop(0, nk, step, 0)

    @pl.when(jnp.logical_not(safe))
    def _():
        m_ref[...] = jnp.full_like(m_ref, -jnp.inf)

        def step(j, carry):
            s, v = scores(j)
            m_prev = m_ref[...]
            m_new = jnp.maximum(m_prev, jnp.max(s, axis=-1, keepdims=True))
            p = jnp.exp2(s - m_new).astype(BF16)
            acc_ref[...] = jnp.exp2(m_prev - m_new) * acc_ref[...] + jnp.dot(p, v, preferred_element_type=F32)
            m_ref[...] = m_new
            return carry
        lax.fori_loop(0, nk, step, 0)

    acc = acc_ref[...]
    o = acc[:, :ATT_D] / acc[:, ATT_D:ATT_D + 1]
    for h in range(ATT_GROUP):
        o_ref[:, h * ATT_D:(h + 1) * ATT_D] = o[h * tq:(h + 1) * tq].astype(o_ref.dtype)


def _attention(safe, q, k, v, B, S, tq, tk):
    kern = functools.partial(_attn_body, tq=tq, tk=tk)
    gw = ATT_GROUP * ATT_D
    return pl.pallas_call(
        kern,
        grid_spec=pltpu.PrefetchScalarGridSpec(
            num_scalar_prefetch=1,
            grid=(B, ATT_KV_HEADS, S // tq),
            in_specs=[
                pl.BlockSpec((None, ATT_GROUP, tq, ATT_DP), lambda b, g, i, s: (b, g, i, 0)),
                pl.BlockSpec((None, None, S, ATT_DP), lambda b, g, i, s: (b, g, 0, 0)),
                pl.BlockSpec((None, None, S, ATT_DP), lambda b, g, i, s: (b, g, 0, 0)),
            ],
            out_specs=pl.BlockSpec((None, tq, gw), lambda b, g, i, s: (b, i, g)),
            scratch_shapes=[
                pltpu.VMEM((ATT_GROUP * tq, 1), F32),
                pltpu.VMEM((ATT_GROUP * tq, ATT_DP), F32),
            ],
        ),
        out_shape=jax.ShapeDtypeStruct((B, S, ATT_Q), BF16),
        compiler_params=_cparams(("parallel", "parallel", "parallel")),
        name="attention",
    )(safe, q, k, v)


def _outproj_body(of_ref, ob_ref, z_ref, att_ref, x_ref, gnw_ref, wa_ref, wb_ref,
                  fnw_ref, wr_ref, xm_ref, h_ref, r_ref):
    o = of_ref[...] + ob_ref[...]
    z = z_ref[...].astype(F32)
    parts = []
    for h in range(GDN_HEADS):
        sl = slice(h * GDN_D, (h + 1) * GDN_D)
        t = o[:, sl]
        t = t * lax.rsqrt(jnp.mean(t * t, axis=-1, keepdims=True) + EPS) * gnw_ref[...]
        parts.append((t * _silu(z[:, sl])).astype(BF16))
    mix_a = jnp.concatenate(parts, axis=-1)
    xm = x_ref[...] + jnp.dot(mix_a, wa_ref[...], preferred_element_type=F32)
    xm = xm + jnp.dot(att_ref[...], wb_ref[...], preferred_element_type=F32)
    xm_ref[...] = xm
    hn = xm * lax.rsqrt(jnp.mean(xm * xm, axis=-1, keepdims=True) + EPS) * fnw_ref[...]
    _store_token_tiles(h_ref, hn)
    hn_hi = hn.astype(BF16)
    hn_lo = (hn - hn_hi.astype(F32)).astype(BF16)
    both = jnp.dot(hn_hi, wr_ref[...], preferred_element_type=F32)
    logits = (both[:, :LANES] + both[:, LANES:]
              + jnp.dot(hn_lo, wr_ref[:, :LANES], preferred_element_type=F32))

    lane = lax.broadcasted_iota(jnp.int32, logits.shape, 1)
    big = jnp.int32(LANES)
    neg = -jnp.inf

    def masked_top(vals, mask):
        m = jnp.max(jnp.where(mask, vals, neg), axis=-1, keepdims=True)
        idx = jnp.min(jnp.where(mask & (vals == m), lane, big), axis=-1, keepdims=True)
        return m, idx

    gmask = lane < N_GROUPS
    gmax, gsel = masked_top(logits, gmask)
    gp_top = 1.0 / jnp.sum(jnp.where(gmask, jnp.exp(logits - gmax), 0.0), axis=-1, keepdims=True)
    lo = N_GROUPS + gsel * EXPERTS_PER_GROUP
    emask = (lane >= lo) & (lane < lo + EXPERTS_PER_GROUP)
    m1, i1 = masked_top(logits, emask)
    ex = jnp.where(emask, jnp.exp(logits - m1), 0.0)
    pf = ex / jnp.sum(ex, axis=-1, keepdims=True)
    p1, _ = masked_top(pf, emask)
    p2, i2 = masked_top(pf, emask & (lane != i1))
    denom = p1 + p2
    g1 = gp_top * p1 / denom
    g2 = gp_top * p2 / denom
    e1 = (i1 - N_GROUPS).astype(F32)
    e2 = (i2 - N_GROUPS).astype(F32)
    r_ref[...] = jnp.where(lane == 0, e1, jnp.where(lane == 1, e2, jnp.where(lane == 2, g1, jnp.where(lane == 3, g2, 0.0))))


def _outproj(o_f, o_b, proj, att, x2, gnw, w_a, w_b, fnw, w_r, tm):
    T, D = x2.shape
    row = lambda i: (i, 0)
    const = lambda i: (0, 0)
    return pl.pallas_call(
        _outproj_body,
        grid=(T // tm,),
        in_specs=[
            pl.BlockSpec((tm, GDN_W), row),
            pl.BlockSpec((tm, GDN_W), row),
            pl.BlockSpec((tm, GDN_W), lambda i: (i, COL_Z // GDN_W)),
            pl.BlockSpec((tm, ATT_Q), row),
            pl.BlockSpec((tm, D), row),
            pl.BlockSpec((1, GDN_D), const),
            pl.BlockSpec((GDN_W, D), const),
            pl.BlockSpec((ATT_Q, D), const),
            pl.BlockSpec((1, D), const),
            pl.BlockSpec((D, 2 * LANES), const),
        ],
        out_specs=[pl.BlockSpec((tm, D), row), pl.BlockSpec((tm * ROW_TILE, LANES), row),
                   pl.BlockSpec((tm, LANES), row)],
        out_shape=[jax.ShapeDtypeStruct((T, D), F32), jax.ShapeDtypeStruct((T * ROW_TILE, LANES), F32),
                   jax.ShapeDtypeStruct((T, LANES), F32)],
        compiler_params=_cparams(("parallel",)),
        name="outproj_router",
    )(o_f, o_b, proj, att, x2, gnw, w_a, w_b, fnw, w_r)


def _gather_start(src_hbm, dst_buf, sem, idx_ref, base, slot, n):
    for r in range(n):
        src = pl.multiple_of(idx_ref[base + r] * ROW_TILE, ROW_TILE)
        dst = pl.multiple_of((slot * n + r) * ROW_TILE, ROW_TILE)
        pltpu.make_async_copy(src_hbm.at[pl.ds(src, ROW_TILE)], dst_buf.at[pl.ds(dst, ROW_TILE)],
                              sem.at[slot]).start(priority=r % DMA_QUEUES)


def _gather_wait(src_hbm, dst_buf, sem, slot, n):
    dst = pl.multiple_of(slot * n * ROW_TILE, ROW_TILE)
    pltpu.make_async_copy(src_hbm.at[pl.ds(0, n * ROW_TILE)], dst_buf.at[pl.ds(dst, n * ROW_TILE)],
                          sem.at[slot]).wait()


def _experts_body(be_ref, src_ref, nused_ref, wslot_ref, first_ref, nxt_ref,
                  h_hbm, wg_hbm, wu_hbm, wd_hbm, y_ref, xbuf, wg_buf, wu_buf, wd_buf, sem, wsem, *, bm):
    i = pl.program_id(0)
    n_used = nused_ref[0]
    slot = i % 2
    ws = wslot_ref[i]

    def weight_copies(e, s):
        return [pltpu.make_async_copy(hbm.at[e], buf.at[s], wsem.at[s])
                for hbm, buf in ((wg_hbm, wg_buf), (wu_hbm, wu_buf), (wd_hbm, wd_buf))]

    def compute(prefetch_next):
        @pl.when(first_ref[i] == 1)
        def _():
            for c in weight_copies(0, ws):
                c.wait()

            @pl.when(nxt_ref[i] >= 0)
            def _():
                for c in weight_copies(nxt_ref[i], 1 - ws):
                    c.start()

        _gather_wait(h_hbm, xbuf, sem, slot, bm)
        x = _load_token_tiles(xbuf, slot * (bm * ROW_TILE), bm).astype(BF16)
        if prefetch_next:
            _gather_start(h_hbm, xbuf, sem, src_ref, (i + 1) * bm, 1 - slot, bm)
        gate = jnp.dot(x, wg_buf[ws].astype(BF16), preferred_element_type=F32)
        up = jnp.dot(x, wu_buf[ws].astype(BF16), preferred_element_type=F32)
        hid = (_silu(gate) * up).astype(BF16)
        _store_token_tiles(y_ref, jnp.dot(hid, wd_buf[ws].astype(BF16), preferred_element_type=F32))

    @pl.when((i == 0) & (n_used > 0))
    def _():
        for c in weight_copies(be_ref[0], 0):
            c.start()
        _gather_start(h_hbm, xbuf, sem, src_ref, 0, 0, bm)

    @pl.when(i + 1 < n_used)
    def _():
        compute(True)

    @pl.when(i + 1 == n_used)
    def _():
        compute(False)

    @pl.when(i >= n_used)
    def _():
        y_ref[...] = jnp.zeros_like(y_ref)


def _experts(block_expert, src_tok, n_used, h2, w_gate, w_up, w_down, bm):
    D = ROW_TILE * LANES
    P = src_tok.shape[0]
    n_blocks = P // bm
    FF = w_gate.shape[-1]
    pos = jnp.arange(n_blocks, dtype=jnp.int32)
    used = pos < n_used[0]
    first = used & jnp.concatenate([jnp.ones((1,), bool), block_expert[1:] != block_expert[:-1]])
    wslot = ((jnp.cumsum(first.astype(jnp.int32)) - 1) % 2).astype(jnp.int32)
    first_pos = jnp.where(first, pos, n_blocks)
    next_first = jnp.concatenate([lax.cummin(first_pos, reverse=True)[1:], jnp.full((1,), n_blocks, jnp.int32)])
    nxt = jnp.where(next_first < n_blocks, block_expert[jnp.minimum(next_first, n_blocks - 1)], -1).astype(jnp.int32)
    kern = functools.partial(_experts_body, bm=bm)
    hbm = pl.BlockSpec(memory_space=pl.ANY)
    return pl.pallas_call(
        kern,
        grid_spec=pltpu.PrefetchScalarGridSpec(
            num_scalar_prefetch=6,
            grid=(n_blocks,),
            in_specs=[hbm, hbm, hbm, hbm],
            out_specs=pl.BlockSpec((bm * ROW_TILE, LANES), lambda i, *_: (i, 0)),
            scratch_shapes=[
                pltpu.VMEM((2 * bm * ROW_TILE, LANES), F32),
                pltpu.VMEM((2, D, FF), F32), pltpu.VMEM((2, D, FF), F32), pltpu.VMEM((2, FF, D), F32),
                pltpu.SemaphoreType.DMA((2,)), pltpu.SemaphoreType.DMA((2,)),
            ],
        ),
        out_shape=jax.ShapeDtypeStruct((P * ROW_TILE, LANES), F32),
        compiler_params=_cparams(("arbitrary",)),
        name="moe_experts",
    )(block_expert, src_tok, n_used, wslot, first.astype(jnp.int32), nxt, h2, w_gate, w_up, w_down)


def _combine_body(dest_ref, y_hbm, xm_ref, r_ref, fw_ref, o_ref, ybuf, sem, *, tc):
    i = pl.program_id(0)
    n = pl.num_programs(0)
    slot = i % 2
    rows = TOP_K * tc

    def compute(prefetch_next):
        _gather_wait(y_hbm, ybuf, sem, slot, rows)
        route = r_ref[...]
        y0 = _load_token_tiles(ybuf, slot * (rows * ROW_TILE), tc)
        y1 = _load_token_tiles(ybuf, (slot * rows + tc) * ROW_TILE, tc)
        if prefetch_next:
            _gather_start(y_hbm, ybuf, sem, dest_ref, (i + 1) * rows, 1 - slot, rows)
        xo = xm_ref[...] + route[:, 2:3] * y0 + route[:, 3:4] * y1
        o_ref[...] = xo * lax.rsqrt(jnp.mean(xo * xo, axis=-1, keepdims=True) + EPS) * fw_ref[...]

    @pl.when(i == 0)
    def _():
        _gather_start(y_hbm, ybuf, sem, dest_ref, 0, 0, rows)

    @pl.when(i + 1 < n)
    def _():
        compute(True)

    @pl.when(i + 1 == n)
    def _():
        compute(False)


def _combine(dest_blocked, y_buf, x_mid, route, final_w, tc):
    T, D = x_mid.shape
    kern = functools.partial(_combine_body, tc=tc)
    return pl.pallas_call(
        kern,
        grid_spec=pltpu.PrefetchScalarGridSpec(
            num_scalar_prefetch=1,
            grid=(T // tc,),
            in_specs=[
                pl.BlockSpec(memory_space=pl.ANY),
                pl.BlockSpec((tc, D), lambda i, d: (i, 0)),
                pl.BlockSpec((tc, LANES), lambda i, d: (i, 0)),
                pl.BlockSpec((1, D), lambda i, d: (0, 0)),
            ],
            out_specs=pl.BlockSpec((tc, D), lambda i, d: (i, 0)),
            scratch_shapes=[pltpu.VMEM((2 * TOP_K * tc * ROW_TILE, LANES), F32), pltpu.SemaphoreType.DMA((2,))],
        ),
        out_shape=jax.ShapeDtypeStruct((T, D), F32),
        compiler_params=_cparams(("arbitrary",)),
        name="moe_combine",
    )(dest_blocked, y_buf, x_mid, route, final_w.reshape(1, D))


def _layer(x2, B, S, norm_mix_w, w_in, conv_w, a_log, dt_bias, gdn_norm_w, q_norm_w, k_norm_w, w_out,
           norm_ffn_w, w_router_group, w_router_expert, w_gate, w_up, w_down, final_w):
    T, D = x2.shape
    tl = _tiles(B, S)
    perm = np.concatenate([np.arange(0, ATT_D, 2), np.arange(1, ATT_D, 2)])
    o_z = 3 * GDN_W
    o_gate = o_z + GDN_W
    o_qb = o_gate + 2 * GDN_CHAINS
    o_kb = o_qb + ATT_Q
    o_vb = o_kb + ATT_KV
    qb_cols = o_qb + (np.arange(ATT_HEADS)[:, None] * ATT_D + perm[None, :]).reshape(-1)
    kb_cols = o_kb + (np.arange(ATT_KV_HEADS)[:, None] * ATT_D + perm[None, :]).reshape(-1)
    w_all = jnp.concatenate([
        w_in[:, :o_gate], w_in[:, qb_cols], w_in[:, kb_cols], w_in[:, o_vb:o_vb + ATT_KV],
        w_in[:, o_gate:o_qb], jnp.zeros((D, LANES - 2 * GDN_CHAINS), w_in.dtype)], axis=1).astype(BF16)

    proj, gate_logits = _inproj(x2, norm_mix_w, w_all, tl.proj_rows)

    conv_w8 = jnp.concatenate([conv_w, jnp.zeros((SUBLANES - CONV_W, conv_w.shape[1]), F32)], axis=0)
    gp = jnp.zeros((SUBLANES, LANES), F32)
    gp = gp.at[0, GDN_CHAINS:2 * GDN_CHAINS].set(jnp.exp(a_log.astype(F32)).reshape(-1))
    gp = gp.at[1, GDN_CHAINS:2 * GDN_CHAINS].set(dt_bias.astype(F32).reshape(-1))
    q_a, k_a, v_a, gates = _gdn_prep(proj, gate_logits, conv_w8, gp, B, S, tl.prep_rows)
    n_chunks = T // CHUNK
    gates_t = gates[:, :2 * GDN_CHAINS].reshape(n_chunks, CHUNK, 2 * GDN_CHAINS).transpose(0, 2, 1)
    u, wq, ak, dec = _gdn_chunk(q_a, k_a, v_a, gates, gates_t, tl.gdn_chunks)
    o_f, o_b = _gdn_scan(u, wq, ak, dec, B, S, tl.scan_chunks)

    rows = S // GRID_W
    rowp = jnp.repeat(jnp.arange(rows), GRID_W).astype(F32)
    colp = jnp.tile(jnp.arange(GRID_W), rows).astype(F32)
    axis_dims = ATT_D // 2
    inv_freq = ROPE_THETA ** (-jnp.arange(0, axis_dims, 2, dtype=F32) / axis_dims)
    ang = jnp.concatenate([rowp[:, None] * inv_freq, colp[:, None] * inv_freq], axis=-1)
    cosf = jnp.tile(jnp.cos(ang), (1, LANES // axis_dims))
    sinf = jnp.tile(jnp.concatenate([-jnp.sin(ang), jnp.sin(ang)], axis=-1), (1, LANES // ATT_D))
    q_gain = jnp.max(jnp.abs(q_norm_w)).astype(F32)
    k_gain = jnp.max(jnp.abs(k_norm_w)).astype(F32)
    score_bound = ATT_D ** 0.5 * q_gain * k_gain
    safe = (2.0 * score_bound <= SOFTMAX_SAFE_SPAN).astype(jnp.int32).reshape(1)
    bias = (-LOG2E * score_bound).reshape(1, 1)
    pair = lambda w: jnp.tile(w[perm], LANES // ATT_D).reshape(1, LANES)
    qh, kh, vh = _attn_prep(proj, pair(q_norm_w), pair(k_norm_w), cosf, sinf, bias, B, S, tl.prep_rows)
    att = _attention(safe, qh, kh, vh, B, S, tl.att_q, tl.att_k).reshape(T, ATT_Q)

    w_r32 = jnp.concatenate([w_router_group, w_router_expert,
                             jnp.zeros((D, LANES - N_GROUPS - N_EXPERTS), F32)], axis=1).astype(F32)
    w_r_hi = w_r32.astype(BF16)
    w_r = jnp.concatenate([w_r_hi, (w_r32 - w_r_hi.astype(F32)).astype(BF16)], axis=1)
    w_out_bf = w_out.astype(BF16)
    x_mid, h2, route = _outproj(o_f, o_b, proj, att, x2, gdn_norm_w.reshape(1, GDN_D),
                                w_out_bf[:GDN_W], w_out_bf[GDN_W:], norm_ffn_w.reshape(1, D), w_r,
                                tl.proj_rows)

    bm = tl.moe_rows
    n_assign = T * TOP_K
    n_blocks = -(-(n_assign + N_EXPERTS * (bm - 1)) // bm)
    e_flat = route[:, :TOP_K].astype(jnp.int32).reshape(-1)
    onehot = (e_flat[:, None] == jnp.arange(N_EXPERTS)[None, :]).astype(jnp.int32)
    csum = jnp.cumsum(onehot, axis=0)
    rank = jnp.sum(csum * onehot, axis=1) - 1
    counts = csum[-1]
    padded = (counts + bm - 1) // bm * bm
    pad_end = jnp.cumsum(padded)
    pad_start = pad_end - padded
    dest = pad_start[e_flat] + rank
    filler = jnp.arange(n_blocks * bm, dtype=jnp.int32) % T
    src_tok = filler.at[dest].set(jnp.arange(n_assign, dtype=jnp.int32) // TOP_K, unique_indices=True)
    block_start = jnp.arange(n_blocks, dtype=jnp.int32) * bm
    block_expert = jnp.minimum(jnp.sum((pad_end[None, :] <= block_start[:, None]).astype(jnp.int32), axis=1),
                               N_EXPERTS - 1)
    n_used = (pad_end[-1:] // bm).astype(jnp.int32)

    y_buf = _experts(block_expert, src_tok, n_used, h2, w_gate, w_up, w_down, bm)

    tc = tl.comb_rows
    dest_blocked = dest.reshape(T // tc, tc, TOP_K).transpose(0, 2, 1).reshape(-1)
    return _combine(dest_blocked, y_buf, x_mid, route, final_w, tc)


def kernel(x, norm_mix_w, w_in, conv_w, a_log, dt_bias, gdn_norm_w, q_norm_w, k_norm_w, w_out, norm_ffn_w,
           w_router_group, w_router_expert, w_gate, w_up, w_down, final_norm_w):
    B, S, D = x.shape
    depth = w_in.shape[0]
    assert depth == 1, "the final norm is fused into the last (only) layer's combine step"
    out = _layer(x.reshape(B * S, D), B, S, norm_mix_w[0], w_in[0], conv_w[0], a_log[0], dt_bias[0],
                 gdn_norm_w[0], q_norm_w[0], k_norm_w[0], w_out[0], norm_ffn_w[0], w_router_group[0],
                 w_router_expert[0], w_gate[0], w_up[0], w_down[0], final_norm_w)
    return out.reshape(B, S, D)
```

```python
import functools
import math
from typing import NamedTuple

import jax
import jax.numpy as jnp
import numpy as np
from jax import lax
from jax.experimental import pallas as pl
from jax.experimental.pallas import tpu as pltpu

F32 = jnp.float32
BF16 = jnp.bfloat16
EPS = 1e-6

GRID_W = 64
GDN_HEADS = 4
GDN_D = 128
CONV_W = 5
CHUNK = 64
ATT_HEADS = 8
ATT_KV_HEADS = 2
ATT_GROUP = ATT_HEADS // ATT_KV_HEADS
ATT_D = 64
ROPE_THETA = 10000.0
N_GROUPS = 4
EXPERTS_PER_GROUP = 8
N_EXPERTS = N_GROUPS * EXPERTS_PER_GROUP
TOP_K = 2

GDN_W = GDN_HEADS * GDN_D
GDN_CHAINS = 2 * GDN_HEADS
ATT_Q = ATT_HEADS * ATT_D
ATT_KV = ATT_KV_HEADS * ATT_D
LANES = 128
SUBLANES = 8
ATT_DP = LANES

COL_Z = 3 * GDN_W
COL_QB = COL_Z + GDN_W
COL_KB = COL_QB + ATT_Q
COL_VB = COL_KB + ATT_KV
COL_GATE = COL_VB + ATT_KV
D_PROJ = COL_GATE + LANES

VMEM_LIMIT = 56 * 1024 * 1024
LOG2E = math.log2(math.e)
SOFTMAX_SAFE_SPAN = 60.0


class Tiles(NamedTuple):
    proj_rows: int
    prep_rows: int
    gdn_chunks: int
    scan_chunks: int
    att_q: int
    att_k: int
    moe_rows: int
    comb_rows: int


def _tile(n, want):
    t = min(n, want)
    assert n % t == 0, (n, want)
    return t


def _tiles(B, S):
    T = B * S
    n_chunks = S // CHUNK
    return Tiles(proj_rows=_tile(T, 512), prep_rows=_tile(S, 512), gdn_chunks=_tile(n_chunks, 4),
                 scan_chunks=_tile(n_chunks, 4), att_q=_tile(S, 512), att_k=_tile(S, 2048),
                 moe_rows=256, comb_rows=_tile(T, 256))


def _cparams(sem):
    return pltpu.CompilerParams(dimension_semantics=sem, vmem_limit_bytes=VMEM_LIMIT)


def _silu(x):
    return x * jax.nn.sigmoid(x)


ROW_TILE = SUBLANES
DMA_QUEUES = 2


def _store_token_tiles(ref, x, base=0):
    rows, d = x.shape
    assert d == ROW_TILE * LANES
    for c in range(ROW_TILE):
        ref[pl.ds(base + c, rows, stride=ROW_TILE), :] = x[:, c * LANES:(c + 1) * LANES]


def _load_token_tiles(ref, base, rows):
    return jnp.concatenate([ref[pl.ds(base + c, rows, stride=ROW_TILE), :] for c in range(ROW_TILE)], axis=-1)


def _inproj_body(x_ref, nw_ref, w_ref, o_ref, g_ref):
    x = x_ref[...]
    h = x * lax.rsqrt(jnp.mean(x * x, axis=-1, keepdims=True) + EPS) * nw_ref[...]
    acc = jnp.dot(h.astype(BF16), w_ref[...], preferred_element_type=F32)
    o_ref[...] = acc[:, :COL_GATE]
    g_ref[...] = acc[:, COL_GATE:]


def _inproj(x2, norm_w, w_all, tm):
    T, D = x2.shape
    return pl.pallas_call(
        _inproj_body,
        grid=(T // tm,),
        in_specs=[
            pl.BlockSpec((tm, D), lambda i: (i, 0)),
            pl.BlockSpec((1, D), lambda i: (0, 0)),
            pl.BlockSpec((D, D_PROJ), lambda i: (0, 0)),
        ],
        out_specs=[pl.BlockSpec((tm, COL_GATE), lambda i: (i, 0)), pl.BlockSpec((tm, LANES), lambda i: (i, 0))],
        out_shape=[jax.ShapeDtypeStruct((T, COL_GATE), F32), jax.ShapeDtypeStruct((T, LANES), F32)],
        compiler_params=_cparams(("parallel",)),
        name="inproj",
    )(x2, norm_w.reshape(1, D), w_all)


def _gdn_prep_body(cur_ref, prev_ref, next_ref, cw_ref, gin_ref, gp_ref,
                   q_ref, k_ref, v_ref, g_ref, ext_ref, *, tr):
    i = pl.program_id(1)
    nr = pl.num_programs(1)
    halo = prev_ref.shape[0]
    pad = CONV_W // 2
    ext_ref[0:halo, :] = jnp.where(i > 0, prev_ref[...], 0.0)
    ext_ref[halo:halo + tr, :] = cur_ref[...]
    ext_ref[halo + tr:2 * halo + tr, :] = jnp.where(i < nr - 1, next_ref[...], 0.0)
    acc = cw_ref[0:1, :] * ext_ref[pl.ds(halo - pad, tr), :]
    for j in range(1, CONV_W):
        acc = acc + cw_ref[j:j + 1, :] * ext_ref[pl.ds(halo - pad + j, tr), :]
    y = _silu(acc)
    for h in range(GDN_HEADS):
        for base, ref, scale in ((0, q_ref, GDN_D ** -0.5), (GDN_W, k_ref, 1.0)):
            t = y[:, base + h * GDN_D: base + (h + 1) * GDN_D]
            t = t * (lax.rsqrt(jnp.sum(t * t, axis=-1, keepdims=True) + EPS) * scale)
            ref[:, h * GDN_D:(h + 1) * GDN_D] = t
    v_ref[...] = y[:, 2 * GDN_W:]
    gin = gin_ref[...]
    lane = lax.broadcasted_iota(jnp.int32, gin.shape, 1)
    a = gin + gp_ref[1:2, :]
    softplus = jnp.maximum(a, 0.0) + jnp.log1p(jnp.exp(-jnp.abs(a)))
    g_ref[...] = jnp.where(lane < GDN_CHAINS, jax.nn.sigmoid(gin), -gp_ref[0:1, :] * softplus)


def _gdn_prep(proj, gate_logits, conv_w8, gate_params, B, S, tr):
    T = B * S
    nr = S // tr
    C = 3 * GDN_W
    halo = SUBLANES
    rb = tr // halo
    n_halo = T // halo
    kern = functools.partial(_gdn_prep_body, tr=tr)
    out_sd = jax.ShapeDtypeStruct((T, GDN_W), F32)
    return pl.pallas_call(
        kern,
        grid=(B, nr),
        in_specs=[
            pl.BlockSpec((tr, C), lambda b, i: (b * nr + i, 0)),
            pl.BlockSpec((halo, C), lambda b, i: (jnp.maximum((b * nr + i) * rb - 1, 0), 0)),
            pl.BlockSpec((halo, C), lambda b, i: (jnp.minimum((b * nr + i + 1) * rb, n_halo - 1), 0)),
            pl.BlockSpec((SUBLANES, C), lambda b, i: (0, 0)),
            pl.BlockSpec((tr, LANES), lambda b, i: (b * nr + i, 0)),
            pl.BlockSpec((SUBLANES, LANES), lambda b, i: (0, 0)),
        ],
        out_specs=[
            pl.BlockSpec((tr, GDN_W), lambda b, i: (b * nr + i, 0)),
            pl.BlockSpec((tr, GDN_W), lambda b, i: (b * nr + i, 0)),
            pl.BlockSpec((tr, GDN_W), lambda b, i: (b * nr + i, 0)),
            pl.BlockSpec((tr, LANES), lambda b, i: (b * nr + i, 0)),
        ],
        out_shape=[out_sd, out_sd, out_sd, jax.ShapeDtypeStruct((T, LANES), F32)],
        scratch_shapes=[pltpu.VMEM((tr + 2 * halo, C), F32)],
        compiler_params=_cparams(("parallel", "parallel")),
        name="gdn_prep",
    )(proj, proj, proj, conv_w8, gate_logits, gate_params)


def _bdot(a, b):
    return jnp.dot(a.astype(BF16), b.astype(BF16), preferred_element_type=F32)


def _bdot_nt(a, b):
    return lax.dot_general(a.astype(BF16), b.astype(BF16), (((1,), (1,)), ((), ())),
                           preferred_element_type=F32)


def _gdn_chunk_body(q_ref, k_ref, v_ref, g_ref, gt_ref, u_ref, wq_ref, ak_ref, dec_ref, *, cb):
    C = CHUNK
    row = lax.broadcasted_iota(jnp.int32, (C, C), 0)
    col = lax.broadcasted_iota(jnp.int32, (C, C), 1)
    eye = (row == col).astype(F32)
    masks = ((row >= col, row > col), (row <= col, row < col))
    hi = lax.Precision.HIGHEST
    chains = []
    for c in range(cb):
        rs = slice(c * C, (c + 1) * C)
        gates = g_ref[rs, :]
        gates_t = gt_ref[c]
        g_tot = jnp.sum(gates, axis=0, keepdims=True)
        tot_rows = jnp.sum(gates_t, axis=1, keepdims=True)
        dec_ref[c] = jnp.broadcast_to(jnp.exp(tot_rows[GDN_CHAINS:2 * GDN_CHAINS]), (GDN_CHAINS, LANES))
        for d in range(2):
            incl, strict = masks[d]
            cum = incl.astype(F32)
            gc_cols = jnp.dot(cum, gates, precision=hi, preferred_element_type=F32)
            gc_rows = lax.dot_general(gates_t, cum, (((1,), (1,)), ((), ())), precision=hi,
                                      preferred_element_type=F32)
            for h in range(GDN_HEADS):
                lane_b = d * GDN_HEADS + h
                lane_g = GDN_CHAINS + lane_b
                sl = slice(h * GDN_D, (h + 1) * GDN_D)
                q = q_ref[rs, sl]
                k = k_ref[rs, sl]
                v = v_ref[rs, sl]
                beta = gates[:, lane_b:lane_b + 1]
                gc_col = gc_cols[:, lane_g:lane_g + 1]
                gc_row = gc_rows[lane_g:lane_g + 1, :]
                g_last = g_tot[:, lane_g:lane_g + 1]
                decay = jnp.where(incl, jnp.exp(jnp.where(incl, gc_col - gc_row, 0.0)), 0.0)
                e_col = jnp.exp(gc_col)
                k_beta = k * beta
                kk = _bdot_nt(jnp.concatenate([k_beta, q], axis=0), k)
                neg_l = jnp.where(strict, -kk[:C] * decay, 0.0)
                attn = jnp.where(incl, kk[C:] * decay, 0.0)
                rhs = jnp.concatenate([v * beta, k_beta * e_col], axis=1).astype(BF16)
                wq_ref[d, c, C:2 * C, sl] = (q * e_col).astype(BF16)
                ak_ref[d, c, h, 0:C, :] = attn.astype(BF16)
                ak_ref[d, c, h, C:C + GDN_D, :] = (k * jnp.exp(g_last - gc_col)).T.astype(BF16)
                chains.append((d, c, rs, sl, neg_l, rhs))
    zs = [jnp.concatenate([ch[4], eye], axis=1) for ch in chains]
    keep_s = lax.broadcasted_iota(jnp.int32, (C, 2 * C), 1) >= C
    for _ in range(int(math.log2(C))):
        zs = [_bdot(z[:, :C], z) + jnp.where(keep_s, z, 0.0) for z in zs]
    for (d, c, rs, sl, _, rhs), z in zip(chains, zs):
        sol = jnp.dot(z[:, C:].astype(BF16), rhs, preferred_element_type=F32)
        u_ref[d, rs, sl] = sol[:, :GDN_D]
        wq_ref[d, c, 0:C, sl] = sol[:, GDN_D:].astype(BF16)


def _gdn_chunk(q, k, v, gates, gates_t, cb):
    T = q.shape[0]
    nc = T // CHUNK
    rows = cb * CHUNK
    kern = functools.partial(_gdn_chunk_body, cb=cb)
    wide = pl.BlockSpec((rows, GDN_W), lambda i: (i, 0))
    return pl.pallas_call(
        kern,
        grid=(nc // cb,),
        in_specs=[wide, wide, wide, pl.BlockSpec((rows, LANES), lambda i: (i, 0)),
                  pl.BlockSpec((cb, 2 * GDN_CHAINS, CHUNK), lambda i: (i, 0, 0))],
        out_specs=[
            pl.BlockSpec((2, rows, GDN_W), lambda i: (0, i, 0)),
            pl.BlockSpec((2, cb, 2 * CHUNK, GDN_W), lambda i: (0, i, 0, 0)),
            pl.BlockSpec((2, cb, GDN_HEADS, CHUNK + GDN_D, CHUNK), lambda i: (0, i, 0, 0, 0)),
            pl.BlockSpec((cb, GDN_CHAINS, LANES), lambda i: (i, 0, 0)),
        ],
        out_shape=[
            jax.ShapeDtypeStruct((2, T, GDN_W), F32),
            jax.ShapeDtypeStruct((2, nc, 2 * CHUNK, GDN_W), BF16),
            jax.ShapeDtypeStruct((2, nc, GDN_HEADS, CHUNK + GDN_D, CHUNK), BF16),
            jax.ShapeDtypeStruct((nc, GDN_CHAINS, LANES), F32),
        ],
        compiler_params=_cparams(("parallel",)),
        name="gdn_chunk",
    )(q, k, v, gates, gates_t)


def _gdn_scan_body(uf_ref, wqf_ref, akf_ref, decf_ref, ub_ref, wqb_ref, akb_ref, decb_ref,
                   of_ref, ob_ref, state_ref, *, cs):
    @pl.when(pl.program_id(1) == 0)
    def _():
        state_ref[...] = jnp.zeros_like(state_ref)

    C = CHUNK
    refs = ((uf_ref, wqf_ref, akf_ref, decf_ref, of_ref), (ub_ref, wqb_ref, akb_ref, decb_ref, ob_ref))
    chains = [(d, h) for d in range(2) for h in range(GDN_HEADS)]
    st = [state_ref[j] for j in range(GDN_CHAINS)]
    for step in range(cs):
        cidx = (step, cs - 1 - step)
        ws = [jnp.dot(refs[d][1][cidx[d], :, h * GDN_D:(h + 1) * GDN_D], st[j].astype(BF16),
                      preferred_element_type=F32) for j, (d, h) in enumerate(chains)]
        vn = [(refs[d][0][cidx[d] * C:(cidx[d] + 1) * C, h * GDN_D:(h + 1) * GDN_D] - ws[j][:C]).astype(BF16)
              for j, (d, h) in enumerate(chains)]
        rr = [jnp.dot(refs[d][2][cidx[d], h], vn[j], preferred_element_type=F32)
              for j, (d, h) in enumerate(chains)]
        for j, (d, h) in enumerate(chains):
            c = cidx[d]
            refs[d][4][c * C:(c + 1) * C, h * GDN_D:(h + 1) * GDN_D] = ws[j][C:] + rr[j][:C]
            st[j] = st[j] * refs[d][3][c, j:j + 1, :] + rr[j][C:]
    for j in range(GDN_CHAINS):
        state_ref[j] = st[j]


def _gdn_scan(u, wq, ak, dec, B, S, cs):
    T = B * S
    nb = S // (CHUNK * cs)
    rows = cs * CHUNK
    kern = functools.partial(_gdn_scan_body, cs=cs)
    fwd = lambda b, i: b * nb + i
    bwd = lambda b, i: b * nb + nb - 1 - i

    def specs(d, pos):
        return [
            pl.BlockSpec((None, rows, GDN_W), lambda b, i: (d, pos(b, i), 0)),
            pl.BlockSpec((None, cs, 2 * CHUNK, GDN_W), lambda b, i: (d, pos(b, i), 0, 0)),
            pl.BlockSpec((None, cs, GDN_HEADS, CHUNK + GDN_D, CHUNK), lambda b, i: (d, pos(b, i), 0, 0, 0)),
            pl.BlockSpec((cs, GDN_CHAINS, LANES), lambda b, i: (pos(b, i), 0, 0)),
        ]

    out_sd = jax.ShapeDtypeStruct((T, GDN_W), F32)
    return pl.pallas_call(
        kern,
        grid=(B, nb),
        in_specs=specs(0, fwd) + specs(1, bwd),
        out_specs=[pl.BlockSpec((rows, GDN_W), lambda b, i: (fwd(b, i), 0)),
                   pl.BlockSpec((rows, GDN_W), lambda b, i: (bwd(b, i), 0))],
        out_shape=[out_sd, out_sd],
        scratch_shapes=[pltpu.VMEM((GDN_CHAINS, GDN_D, GDN_D), F32)],
        compiler_params=_cparams(("parallel", "arbitrary")),
        name="gdn_scan",
    )(u, wq, ak, dec, u, wq, ak, dec)


def _attn_prep_body(q_ref, k_ref, v_ref, qw_ref, kw_ref, cos_ref, sin_ref, bias_ref,
                    qo_ref, ko_ref, vo_ref):
    cos = cos_ref[...]
    sin = sin_ref[...]
    rows = cos.shape[0]
    half = ATT_D // 2
    lane = lax.broadcasted_iota(jnp.int32, (rows, LANES), 1)
    low_half = (lane % ATT_D) < half
    first_head = lane < ATT_D
    extra_lane = lane == ATT_D
    mi = lax.broadcasted_iota(jnp.int32, (LANES, LANES), 0) // ATT_D
    mj = lax.broadcasted_iota(jnp.int32, (LANES, LANES), 1) // ATT_D
    head_mean = jnp.where(mi == mj, 1.0 / ATT_D, 0.0).astype(BF16)

    def norm_rope(x, w):
        sq = x * x
        hi = sq.astype(BF16)
        lo = (sq - hi.astype(F32)).astype(BF16)
        ms = (jnp.dot(hi, head_mean, preferred_element_type=F32)
              + jnp.dot(lo, head_mean, preferred_element_type=F32))
        y = x * lax.rsqrt(ms + EPS) * w
        partner = jnp.where(low_half, pltpu.roll(y, LANES - half, 1), pltpu.roll(y, half, 1))
        return y * cos + partner * sin

    def split_heads(r, extra):
        tail = jnp.where(extra_lane, extra, 0.0)
        return (jnp.where(first_head, r, tail).astype(BF16),
                jnp.where(first_head, pltpu.roll(r, ATT_D, 1), tail).astype(BF16))

    scale = LOG2E * ATT_D ** -0.5
    for c in range(ATT_HEADS // 2):
        r = norm_rope(q_ref[:, c * LANES:(c + 1) * LANES], qw_ref[...]) * scale
        qo_ref[2 * c], qo_ref[2 * c + 1] = split_heads(r, bias_ref[...])
    for c in range(ATT_KV_HEADS // 2):
        r = norm_rope(k_ref[:, c * LANES:(c + 1) * LANES], kw_ref[...])
        ko_ref[2 * c], ko_ref[2 * c + 1] = split_heads(r, 1.0)
        vo_ref[2 * c], vo_ref[2 * c + 1] = split_heads(v_ref[:, c * LANES:(c + 1) * LANES], 1.0)


def _attn_prep(proj, qw, kw, cosf, sinf, bias, B, S, tr):
    nr = S // tr
    row = lambda b, i: b * nr + i
    return pl.pallas_call(
        _attn_prep_body,
        grid=(B, nr),
        in_specs=[
            pl.BlockSpec((tr, ATT_Q), lambda b, i: (row(b, i), COL_QB // ATT_Q)),
            pl.BlockSpec((tr, ATT_KV), lambda b, i: (row(b, i), COL_KB // ATT_KV)),
            pl.BlockSpec((tr, ATT_KV), lambda b, i: (row(b, i), COL_VB // ATT_KV)),
            pl.BlockSpec((1, LANES), lambda b, i: (0, 0)),
            pl.BlockSpec((1, LANES), lambda b, i: (0, 0)),
            pl.BlockSpec((tr, LANES), lambda b, i: (i, 0)),
            pl.BlockSpec((tr, LANES), lambda b, i: (i, 0)),
            pl.BlockSpec((1, 1), lambda b, i: (0, 0)),
        ],
        out_specs=[
            pl.BlockSpec((None, ATT_HEADS, tr, ATT_DP), lambda b, i: (b, 0, i, 0)),
            pl.BlockSpec((None, ATT_KV_HEADS, tr, ATT_DP), lambda b, i: (b, 0, i, 0)),
            pl.BlockSpec((None, ATT_KV_HEADS, tr, ATT_DP), lambda b, i: (b, 0, i, 0)),
        ],
        out_shape=[
            jax.ShapeDtypeStruct((B, ATT_HEADS, S, ATT_DP), BF16),
            jax.ShapeDtypeStruct((B, ATT_KV_HEADS, S, ATT_DP), BF16),
            jax.ShapeDtypeStruct((B, ATT_KV_HEADS, S, ATT_DP), BF16),
        ],
        compiler_params=_cparams(("parallel", "parallel")),
        name="attn_prep",
    )(proj, proj, proj, qw, kw, cosf, sinf, bias)


def _attn_body(safe_ref, q_ref, k_ref, v_ref, o_ref, m_ref, acc_ref, *, tq, tk):
    safe = safe_ref[0] != 0
    nk = k_ref.shape[0] // tk
    q = q_ref[...].reshape(ATT_GROUP * tq, ATT_DP)
    acc_ref[...] = jnp.zeros_like(acc_ref)

    def scores(j):
        keys = pl.ds(pl.multiple_of(j * tk, tk), tk)
        s = lax.dot_general(q, k_ref[keys, :], (((1,), (1,)), ((), ())), preferred_element_type=F32)
        return s, v_ref[keys, :]

    @pl.when(safe)
    def _():
        def step(j, carry):
            s, v = scores(j)
            acc_ref[...] += jnp.dot(jnp.exp2(s).astype(BF16), v, preferred_element_type=F32)
            return carry
        lax.fori_loop(0, nk, step, 0)

    @pl.when(jnp.logical_not(safe))
    def _():
        m_ref[...] = jnp.full_like(m_ref, -jnp.inf)

        def step(j, carry):
            s, v = scores(j)
            m_prev = m_ref[...]
            m_new = jnp.maximum(m_prev, jnp.max(s, axis=-1, keepdims=True))
            p = jnp.exp2(s - m_new).astype(BF16)
            acc_ref[...] = jnp.exp2(m_prev - m_new) * acc_ref[...] + jnp.dot(p, v, preferred_element_type=F32)
            m_ref[...] = m_new
            return carry
        lax.fori_loop(0, nk, step, 0)

    acc = acc_ref[...]
    o = acc[:, :ATT_D] / acc[:, ATT_D:ATT_D + 1]
    for h in range(ATT_GROUP):
        o_ref[:, h * ATT_D:(h + 1) * ATT_D] = o[h * tq:(h + 1) * tq].astype(o_ref.dtype)


def _attention(safe, q, k, v, B, S, tq, tk):
    kern = functools.partial(_attn_body, tq=tq, tk=tk)
    gw = ATT_GROUP * ATT_D
    return pl.pallas_call(
        kern,
        grid_spec=pltpu.PrefetchScalarGridSpec(
            num_scalar_prefetch=1,
            grid=(B, ATT_KV_HEADS, S // tq),
            in_specs=[
                pl.BlockSpec((None, ATT_GROUP, tq, ATT_DP), lambda b, g, i, s: (b, g, i, 0)),
                pl.BlockSpec((None, None, S, ATT_DP), lambda b, g, i, s: (b, g, 0, 0)),
                pl.BlockSpec((None, None, S, ATT_DP), lambda b, g, i, s: (b, g, 0, 0)),
            ],
            out_specs=pl.BlockSpec((None, tq, gw), lambda b, g, i, s: (b, i, g)),
            scratch_shapes=[
                pltpu.VMEM((ATT_GROUP * tq, 1), F32),
                pltpu.VMEM((ATT_GROUP * tq, ATT_DP), F32),
            ],
        ),
        out_shape=jax.ShapeDtypeStruct((B, S, ATT_Q), BF16),
        compiler_params=_cparams(("parallel", "parallel", "parallel")),
        name="attention",
    )(safe, q, k, v)


def _outproj_body(of_ref, ob_ref, z_ref, att_ref, x_ref, gnw_ref, wa_ref, wb_ref,
                  fnw_ref, wr_ref, xm_ref, h_ref, r_ref):
    o = of_ref[...] + ob_ref[...]
    z = z_ref[...]
    parts = []
    for h in range(GDN_HEADS):
        sl = slice(h * GDN_D, (h + 1) * GDN_D)
        t = o[:, sl]
        t = t * lax.rsqrt(jnp.mean(t * t, axis=-1, keepdims=True) + EPS) * gnw_ref[...]
        parts.append((t * _silu(z[:, sl])).astype(BF16))
    mix_a = jnp.concatenate(parts, axis=-1)
    xm = x_ref[...] + jnp.dot(mix_a, wa_ref[...], preferred_element_type=F32)
    xm = xm + jnp.dot(att_ref[...], wb_ref[...], preferred_element_type=F32)
    xm_ref[...] = xm
    hn = xm * lax.rsqrt(jnp.mean(xm * xm, axis=-1, keepdims=True) + EPS) * fnw_ref[...]
    _store_token_tiles(h_ref, hn)
    hn_hi = hn.astype(BF16)
    hn_lo = (hn - hn_hi.astype(F32)).astype(BF16)
    both = jnp.dot(hn_hi, wr_ref[...], preferred_element_type=F32)
    logits = (both[:, :LANES] + both[:, LANES:]
              + jnp.dot(hn_lo, wr_ref[:, :LANES], preferred_element_type=F32))

    lane = lax.broadcasted_iota(jnp.int32, logits.shape, 1)
    big = jnp.int32(LANES)
    neg = -jnp.inf

    def masked_top(vals, mask):
        m = jnp.max(jnp.where(mask, vals, neg), axis=-1, keepdims=True)
        idx = jnp.min(jnp.where(mask & (vals == m), lane, big), axis=-1, keepdims=True)
        return m, idx

    gmask = lane < N_GROUPS
    gmax, gsel = masked_top(logits, gmask)
    gp_top = 1.0 / jnp.sum(jnp.where(gmask, jnp.exp(logits - gmax), 0.0), axis=-1, keepdims=True)
    lo = N_GROUPS + gsel * EXPERTS_PER_GROUP
    emask = (lane >= lo) & (lane < lo + EXPERTS_PER_GROUP)
    m1, i1 = masked_top(logits, emask)
    ex = jnp.where(emask, jnp.exp(logits - m1), 0.0)
    pf = ex / jnp.sum(ex, axis=-1, keepdims=True)
    p1, _ = masked_top(pf, emask)
    p2, i2 = masked_top(pf, emask & (lane != i1))
    denom = p1 + p2
    g1 = gp_top * p1 / denom
    g2 = gp_top * p2 / denom
    e1 = (i1 - N_GROUPS).astype(F32)
    e2 = (i2 - N_GROUPS).astype(F32)
    r_ref[...] = jnp.where(lane == 0, e1, jnp.where(lane == 1, e2, jnp.where(lane == 2, g1, jnp.where(lane == 3, g2, 0.0))))


def _outproj(o_f, o_b, proj, att, x2, gnw, w_a, w_b, fnw, w_r, tm):
    T, D = x2.shape
    row = lambda i: (i, 0)
    const = lambda i: (0, 0)
    return pl.pallas_call(
        _outproj_body,
        grid=(T // tm,),
        in_specs=[
            pl.BlockSpec((tm, GDN_W), row),
            pl.BlockSpec((tm, GDN_W), row),
            pl.BlockSpec((tm, GDN_W), lambda i: (i, COL_Z // GDN_W)),
            pl.BlockSpec((tm, ATT_Q), row),
            pl.BlockSpec((tm, D), row),
            pl.BlockSpec((1, GDN_D), const),
            pl.BlockSpec((GDN_W, D), const),
            pl.BlockSpec((ATT_Q, D), const),
            pl.BlockSpec((1, D), const),
            pl.BlockSpec((D, 2 * LANES), const),
        ],
        out_specs=[pl.BlockSpec((tm, D), row), pl.BlockSpec((tm * ROW_TILE, LANES), row),
                   pl.BlockSpec((tm, LANES), row)],
        out_shape=[jax.ShapeDtypeStruct((T, D), F32), jax.ShapeDtypeStruct((T * ROW_TILE, LANES), F32),
                   jax.ShapeDtypeStruct((T, LANES), F32)],
        compiler_params=_cparams(("parallel",)),
        name="outproj_router",
    )(o_f, o_b, proj, att, x2, gnw, w_a, w_b, fnw, w_r)


def _gather_start(src_hbm, dst_buf, sem, idx_ref, base, slot, n):
    for r in range(n):
        src = pl.multiple_of(idx_ref[base + r] * ROW_TILE, ROW_TILE)
        dst = pl.multiple_of((slot * n + r) * ROW_TILE, ROW_TILE)
        pltpu.make_async_copy(src_hbm.at[pl.ds(src, ROW_TILE)], dst_buf.at[pl.ds(dst, ROW_TILE)],
                              sem.at[slot]).start(priority=r % DMA_QUEUES)


def _gather_wait(src_hbm, dst_buf, sem, slot, n):
    dst = pl.multiple_of(slot * n * ROW_TILE, ROW_TILE)
    pltpu.make_async_copy(src_hbm.at[pl.ds(0, n * ROW_TILE)], dst_buf.at[pl.ds(dst, n * ROW_TILE)],
                          sem.at[slot]).wait()


def _experts_body(be_ref, src_ref, nused_ref, wslot_ref, first_ref, nxt_ref,
                  h_hbm, wg_hbm, wu_hbm, wd_hbm, y_ref, xbuf, wg_buf, wu_buf, wd_buf, sem, wsem, *, bm):
    i = pl.program_id(0)
    n_used = nused_ref[0]
    slot = i % 2
    ws = wslot_ref[i]

    def weight_copies(e, s):
        return [pltpu.make_async_copy(hbm.at[e], buf.at[s], wsem.at[s])
                for hbm, buf in ((wg_hbm, wg_buf), (wu_hbm, wu_buf), (wd_hbm, wd_buf))]

    def compute(prefetch_next):
        @pl.when(first_ref[i] == 1)
        def _():
            for c in weight_copies(0, ws):
                c.wait()

            @pl.when(nxt_ref[i] >= 0)
            def _():
                for c in weight_copies(nxt_ref[i], 1 - ws):
                    c.start()

        _gather_wait(h_hbm, xbuf, sem, slot, bm)
        x = _load_token_tiles(xbuf, slot * (bm * ROW_TILE), bm).astype(BF16)
        if prefetch_next:
            _gather_start(h_hbm, xbuf, sem, src_ref, (i + 1) * bm, 1 - slot, bm)
        gate = jnp.dot(x, wg_buf[ws].astype(BF16), preferred_element_type=F32)
        up = jnp.dot(x, wu_buf[ws].astype(BF16), preferred_element_type=F32)
        hid = (_silu(gate) * up).astype(BF16)
        _store_token_tiles(y_ref, jnp.dot(hid, wd_buf[ws].astype(BF16), preferred_element_type=F32))

    @pl.when((i == 0) & (n_used > 0))
    def _():
        for c in weight_copies(be_ref[0], 0):
            c.start()
        _gather_start(h_hbm, xbuf, sem, src_ref, 0, 0, bm)

    @pl.when(i + 1 < n_used)
    def _():
        compute(True)

    @pl.when(i + 1 == n_used)
    def _():
        compute(False)

    @pl.when(i >= n_used)
    def _():
        y_ref[...] = jnp.zeros_like(y_ref)


def _experts(block_expert, src_tok, n_used, h2, w_gate, w_up, w_down, bm):
    D = ROW_TILE * LANES
    P = src_tok.shape[0]
    n_blocks = P // bm
    FF = w_gate.shape[-1]
    pos = jnp.arange(n_blocks, dtype=jnp.int32)
    used = pos < n_used[0]
    first = used & jnp.concatenate([jnp.ones((1,), bool), block_expert[1:] != block_expert[:-1]])
    wslot = ((jnp.cumsum(first.astype(jnp.int32)) - 1) % 2).astype(jnp.int32)
    first_pos = jnp.where(first, pos, n_blocks)
    next_first = jnp.concatenate([lax.cummin(first_pos, reverse=True)[1:], jnp.full((1,), n_blocks, jnp.int32)])
    nxt = jnp.where(next_first < n_blocks, block_expert[jnp.minimum(next_first, n_blocks - 1)], -1).astype(jnp.int32)
    kern = functools.partial(_experts_body, bm=bm)
    hbm = pl.BlockSpec(memory_space=pl.ANY)
    return pl.pallas_call(
        kern,
        grid_spec=pltpu.PrefetchScalarGridSpec(
            num_scalar_prefetch=6,
            grid=(n_blocks,),
            in_specs=[hbm, hbm, hbm, hbm],
            out_specs=pl.BlockSpec((bm * ROW_TILE, LANES), lambda i, *_: (i, 0)),
            scratch_shapes=[
                pltpu.VMEM((2 * bm * ROW_TILE, LANES), F32),
                pltpu.VMEM((2, D, FF), F32), pltpu.VMEM((2, D, FF), F32), pltpu.VMEM((2, FF, D), F32),
                pltpu.SemaphoreType.DMA((2,)), pltpu.SemaphoreType.DMA((2,)),
            ],
        ),
        out_shape=jax.ShapeDtypeStruct((P * ROW_TILE, LANES), F32),
        compiler_params=_cparams(("arbitrary",)),
        name="moe_experts",
    )(block_expert, src_tok, n_used, wslot, first.astype(jnp.int32), nxt, h2, w_gate, w_up, w_down)


def _combine_body(dest_ref, y_hbm, xm_ref, r_ref, fw_ref, o_ref, ybuf, sem, *, tc):
    i = pl.program_id(0)
    n = pl.num_programs(0)
    slot = i % 2
    rows = TOP_K * tc

    def compute(prefetch_next):
        _gather_wait(y_hbm, ybuf, sem, slot, rows)
        route = r_ref[...]
        y0 = _load_token_tiles(ybuf, slot * (rows * ROW_TILE), tc)
        y1 = _load_token_tiles(ybuf, (slot * rows + tc) * ROW_TILE, tc)
        if prefetch_next:
            _gather_start(y_hbm, ybuf, sem, dest_ref, (i + 1) * rows, 1 - slot, rows)
        xo = xm_ref[...] + route[:, 2:3] * y0 + route[:, 3:4] * y1
        o_ref[...] = xo * lax.rsqrt(jnp.mean(xo * xo, axis=-1, keepdims=True) + EPS) * fw_ref[...]

    @pl.when(i == 0)
    def _():
        _gather_start(y_hbm, ybuf, sem, dest_ref, 0, 0, rows)

    @pl.when(i + 1 < n)
    def _():
        compute(True)

    @pl.when(i + 1 == n)
    def _():
        compute(False)


def _combine(dest_blocked, y_buf, x_mid, route, final_w, tc):
    T, D = x_mid.shape
    kern = functools.partial(_combine_body, tc=tc)
    return pl.pallas_call(
        kern,
        grid_spec=pltpu.PrefetchScalarGridSpec(
            num_scalar_prefetch=1,
            grid=(T // tc,),
            in_specs=[
                pl.BlockSpec(memory_space=pl.ANY),
                pl.BlockSpec((tc, D), lambda i, d: (i, 0)),
                pl.BlockSpec((tc, LANES), lambda i, d: (i, 0)),
                pl.BlockSpec((1, D), lambda i, d: (0, 0)),
            ],
            out_specs=pl.BlockSpec((tc, D), lambda i, d: (i, 0)),
            scratch_shapes=[pltpu.VMEM((2 * TOP_K * tc * ROW_TILE, LANES), F32), pltpu.SemaphoreType.DMA((2,))],
        ),
        out_shape=jax.ShapeDtypeStruct((T, D), F32),
        compiler_params=_cparams(("arbitrary",)),
        name="moe_combine",
    )(dest_blocked, y_buf, x_mid, route, final_w.reshape(1, D))


def _layer(x2, B, S, norm_mix_w, w_in, conv_w, a_log, dt_bias, gdn_norm_w, q_norm_w, k_norm_w, w_out,
           norm_ffn_w, w_router_group, w_router_expert, w_gate, w_up, w_down, final_w):
    T, D = x2.shape
    tl = _tiles(B, S)
    perm = np.concatenate([np.arange(0, ATT_D, 2), np.arange(1, ATT_D, 2)])
    o_z = 3 * GDN_W
    o_gate = o_z + GDN_W
    o_qb = o_gate + 2 * GDN_CHAINS
    o_kb = o_qb + ATT_Q
    o_vb = o_kb + ATT_KV
    qb_cols = o_qb + (np.arange(ATT_HEADS)[:, None] * ATT_D + perm[None, :]).reshape(-1)
    kb_cols = o_kb + (np.arange(ATT_KV_HEADS)[:, None] * ATT_D + perm[None, :]).reshape(-1)
    w_all = jnp.concatenate([
        w_in[:, :o_gate], w_in[:, qb_cols], w_in[:, kb_cols], w_in[:, o_vb:o_vb + ATT_KV],
        w_in[:, o_gate:o_qb], jnp.zeros((D, LANES - 2 * GDN_CHAINS), w_in.dtype)], axis=1).astype(BF16)

    proj, gate_logits = _inproj(x2, norm_mix_w, w_all, tl.proj_rows)

    conv_w8 = jnp.concatenate([conv_w, jnp.zeros((SUBLANES - CONV_W, conv_w.shape[1]), F32)], axis=0)
    gp = jnp.zeros((SUBLANES, LANES), F32)
    gp = gp.at[0, GDN_CHAINS:2 * GDN_CHAINS].set(jnp.exp(a_log.astype(F32)).reshape(-1))
    gp = gp.at[1, GDN_CHAINS:2 * GDN_CHAINS].set(dt_bias.astype(F32).reshape(-1))
    q_a, k_a, v_a, gates = _gdn_prep(proj, gate_logits, conv_w8, gp, B, S, tl.prep_rows)
    n_chunks = T // CHUNK
    gates_t = gates[:, :2 * GDN_CHAINS].reshape(n_chunks, CHUNK, 2 * GDN_CHAINS).transpose(0, 2, 1)
    u, wq, ak, dec = _gdn_chunk(q_a, k_a, v_a, gates, gates_t, tl.gdn_chunks)
    o_f, o_b = _gdn_scan(u, wq, ak, dec, B, S, tl.scan_chunks)

    rows = S // GRID_W
    rowp = jnp.repeat(jnp.arange(rows), GRID_W).astype(F32)
    colp = jnp.tile(jnp.arange(GRID_W), rows).astype(F32)
    axis_dims = ATT_D // 2
    inv_freq = ROPE_THETA ** (-jnp.arange(0, axis_dims, 2, dtype=F32) / axis_dims)
    ang = jnp.concatenate([rowp[:, None] * inv_freq, colp[:, None] * inv_freq], axis=-1)
    cosf = jnp.tile(jnp.cos(ang), (1, LANES // axis_dims))
    sinf = jnp.tile(jnp.concatenate([-jnp.sin(ang), jnp.sin(ang)], axis=-1), (1, LANES // ATT_D))
    q_gain = jnp.max(jnp.abs(q_norm_w)).astype(F32)
    k_gain = jnp.max(jnp.abs(k_norm_w)).astype(F32)
    score_bound = ATT_D ** 0.5 * q_gain * k_gain
    safe = (2.0 * score_bound <= SOFTMAX_SAFE_SPAN).astype(jnp.int32).reshape(1)
    bias = (-LOG2E * score_bound).reshape(1, 1)
    pair = lambda w: jnp.tile(w[perm], LANES // ATT_D).reshape(1, LANES)
    qh, kh, vh = _attn_prep(proj, pair(q_norm_w), pair(k_norm_w), cosf, sinf, bias, B, S, tl.prep_rows)
    att = _attention(safe, qh, kh, vh, B, S, tl.att_q, tl.att_k).reshape(T, ATT_Q)

    w_r32 = jnp.concatenate([w_router_group, w_router_expert,
                             jnp.zeros((D, LANES - N_GROUPS - N_EXPERTS), F32)], axis=1).astype(F32)
    w_r_hi = w_r32.astype(BF16)
    w_r = jnp.concatenate([w_r_hi, (w_r32 - w_r_hi.astype(F32)).astype(BF16)], axis=1)
    w_out_bf = w_out.astype(BF16)
    x_mid, h2, route = _outproj(o_f, o_b, proj, att, x2, gdn_norm_w.reshape(1, GDN_D),
                                w_out_bf[:GDN_W], w_out_bf[GDN_W:], norm_ffn_w.reshape(1, D), w_r,
                                tl.proj_rows)

    bm = tl.moe_rows
    n_assign = T * TOP_K
    n_blocks = -(-(n_assign + N_EXPERTS * (bm - 1)) // bm)
    e_flat = route[:, :TOP_K].astype(jnp.int32).reshape(-1)
    onehot = (e_flat[:, None] == jnp.arange(N_EXPERTS)[None, :]).astype(jnp.int32)
    csum = jnp.cumsum(onehot, axis=0)
    rank = jnp.sum(csum * onehot, axis=1) - 1
    counts = csum[-1]
    padded = (counts + bm - 1) // bm * bm
    pad_end = jnp.cumsum(padded)
    pad_start = pad_end - padded
    dest = pad_start[e_flat] + rank
    block_start = jnp.arange(n_blocks, dtype=jnp.int32) * bm
    block_expert = jnp.minimum(jnp.sum((pad_end[None, :] <= block_start[:, None]).astype(jnp.int32), axis=1),
                               N_EXPERTS - 1)
    n_used = (pad_end[-1:] // bm).astype(jnp.int32)
    _, compact = lax.sort((dest, jnp.arange(n_assign, dtype=jnp.int32) // TOP_K), num_keys=1)
    count_start = jnp.cumsum(counts) - counts
    row = block_start[:, None] + jnp.arange(bm, dtype=jnp.int32)[None, :]
    seg_row = row - pad_start[block_expert][:, None]
    holds_token = seg_row < counts[block_expert][:, None]
    entry = jnp.clip(count_start[block_expert][:, None] + seg_row, 0, n_assign - 1)
    src_tok = jnp.where(holds_token, compact[entry], row % T).reshape(-1)

    y_buf = _experts(block_expert, src_tok, n_used, h2, w_gate, w_up, w_down, bm)

    tc = tl.comb_rows
    dest_blocked = dest.reshape(T // tc, tc, TOP_K).transpose(0, 2, 1).reshape(-1)
    return _combine(dest_blocked, y_buf, x_mid, route, final_w, tc)


def kernel(x, norm_mix_w, w_in, conv_w, a_log, dt_bias, gdn_norm_w, q_norm_w, k_norm_w, w_out, norm_ffn_w,
           w_router_group, w_router_expert, w_gate, w_up, w_down, final_norm_w):
    B, S, D = x.shape
    depth = w_in.shape[0]
    assert depth == 1, "the final norm is fused into the last (only) layer's combine step"
    out = _layer(x.reshape(B * S, D), B, S, norm_mix_w[0], w_in[0], conv_w[0], a_log[0], dt_bias[0],
                 gdn_norm_w[0], q_norm_w[0], k_norm_w[0], w_out[0], norm_ffn_w[0], w_router_group[0],
                 w_router_expert[0], w_gate[0], w_up[0], w_down[0], final_norm_w)
    return out.reshape(B, S, D)
```

```python
import functools
import math
from typing import NamedTuple

import jax
import jax.numpy as jnp
import numpy as np
from jax import lax
from jax.experimental import pallas as pl
from jax.experimental.pallas import tpu as pltpu

F32 = jnp.float32
BF16 = jnp.bfloat16
EPS = 1e-6

GRID_W = 64
GDN_HEADS = 4
GDN_D = 128
CONV_W = 5
CHUNK = 64
ATT_HEADS = 8
ATT_KV_HEADS = 2
ATT_GROUP = ATT_HEADS // ATT_KV_HEADS
ATT_D = 64
ROPE_THETA = 10000.0
N_GROUPS = 4
EXPERTS_PER_GROUP = 8
N_EXPERTS = N_GROUPS * EXPERTS_PER_GROUP
TOP_K = 2

GDN_W = GDN_HEADS * GDN_D
GDN_CHAINS = 2 * GDN_HEADS
ATT_Q = ATT_HEADS * ATT_D
ATT_KV = ATT_KV_HEADS * ATT_D
LANES = 128
SUBLANES = 8
ATT_DP = LANES

COL_Z = 3 * GDN_W
COL_QB = COL_Z + GDN_W
COL_KB = COL_QB + ATT_Q
COL_VB = COL_KB + ATT_KV
COL_GATE = COL_VB + ATT_KV
D_PROJ = COL_GATE + LANES

VMEM_LIMIT = 56 * 1024 * 1024
LOG2E = math.log2(math.e)
SOFTMAX_SAFE_SPAN = 60.0


class Tiles(NamedTuple):
    proj_rows: int
    prep_rows: int
    gdn_chunks: int
    scan_chunks: int
    att_q: int
    att_k: int
    moe_rows: int
    comb_rows: int


def _tile(n, want):
    t = min(n, want)
    assert n % t == 0, (n, want)
    return t


def _tiles(B, S):
    T = B * S
    n_chunks = S // CHUNK
    return Tiles(proj_rows=_tile(T, 512), prep_rows=_tile(S, 512), gdn_chunks=_tile(n_chunks, 4),
                 scan_chunks=_tile(n_chunks, 4), att_q=_tile(S, 512), att_k=_tile(S, 2048),
                 moe_rows=256, comb_rows=_tile(T, 256))


def _cparams(sem):
    return pltpu.CompilerParams(dimension_semantics=sem, vmem_limit_bytes=VMEM_LIMIT)


def _silu(x):
    return x * jax.nn.sigmoid(x)


ROW_TILE = SUBLANES
DMA_QUEUES = 2


def _store_token_tiles(ref, x, base=0):
    rows, d = x.shape
    assert d == ROW_TILE * LANES
    for c in range(ROW_TILE):
        ref[pl.ds(base + c, rows, stride=ROW_TILE), :] = x[:, c * LANES:(c + 1) * LANES]


def _load_token_tiles(ref, base, rows):
    return jnp.concatenate([ref[pl.ds(base + c, rows, stride=ROW_TILE), :] for c in range(ROW_TILE)], axis=-1)


def _inproj_body(x_ref, nw_ref, w_ref, o_ref, g_ref):
    x = x_ref[...]
    h = x * lax.rsqrt(jnp.mean(x * x, axis=-1, keepdims=True) + EPS) * nw_ref[...]
    acc = jnp.dot(h.astype(BF16), w_ref[...], preferred_element_type=F32)
    o_ref[...] = acc[:, :COL_GATE]
    g_ref[...] = acc[:, COL_GATE:]


def _inproj(x2, norm_w, w_all, tm):
    T, D = x2.shape
    return pl.pallas_call(
        _inproj_body,
        grid=(T // tm,),
        in_specs=[
            pl.BlockSpec((tm, D), lambda i: (i, 0)),
            pl.BlockSpec((1, D), lambda i: (0, 0)),
            pl.BlockSpec((D, D_PROJ), lambda i: (0, 0)),
        ],
        out_specs=[pl.BlockSpec((tm, COL_GATE), lambda i: (i, 0)), pl.BlockSpec((tm, LANES), lambda i: (i, 0))],
        out_shape=[jax.ShapeDtypeStruct((T, COL_GATE), F32), jax.ShapeDtypeStruct((T, LANES), F32)],
        compiler_params=_cparams(("parallel",)),
        name="inproj",
    )(x2, norm_w.reshape(1, D), w_all)


def _gdn_prep_body(cur_ref, prev_ref, next_ref, cw_ref, gin_ref, gp_ref,
                   q_ref, k_ref, v_ref, g_ref, gt_ref, ext_ref, *, tr):
    i = pl.program_id(1)
    nr = pl.num_programs(1)
    halo = prev_ref.shape[0]
    pad = CONV_W // 2
    ext_ref[0:halo, :] = jnp.where(i > 0, prev_ref[...], 0.0)
    ext_ref[halo:halo + tr, :] = cur_ref[...]
    ext_ref[halo + tr:2 * halo + tr, :] = jnp.where(i < nr - 1, next_ref[...], 0.0)
    acc = cw_ref[0:1, :] * ext_ref[pl.ds(halo - pad, tr), :]
    for j in range(1, CONV_W):
        acc = acc + cw_ref[j:j + 1, :] * ext_ref[pl.ds(halo - pad + j, tr), :]
    y = _silu(acc)
    for h in range(GDN_HEADS):
        for base, ref, scale in ((0, q_ref, GDN_D ** -0.5), (GDN_W, k_ref, 1.0)):
            t = y[:, base + h * GDN_D: base + (h + 1) * GDN_D]
            t = t * (lax.rsqrt(jnp.sum(t * t, axis=-1, keepdims=True) + EPS) * scale)
            ref[:, h * GDN_D:(h + 1) * GDN_D] = t
    v_ref[...] = y[:, 2 * GDN_W:]
    gin = gin_ref[...]
    lane = lax.broadcasted_iota(jnp.int32, gin.shape, 1)
    a = gin + gp_ref[1:2, :]
    softplus = jnp.maximum(a, 0.0) + jnp.log1p(jnp.exp(-jnp.abs(a)))
    g = jnp.where(lane < GDN_CHAINS, jax.nn.sigmoid(gin), -gp_ref[0:1, :] * softplus)
    g_ref[...] = g
    gt_ref[...] = g.T[0:2 * GDN_CHAINS, :]


def _gdn_prep(proj, gate_logits, conv_w8, gate_params, B, S, tr):
    T = B * S
    nr = S // tr
    C = 3 * GDN_W
    halo = SUBLANES
    rb = tr // halo
    n_halo = T // halo
    kern = functools.partial(_gdn_prep_body, tr=tr)
    out_sd = jax.ShapeDtypeStruct((T, GDN_W), F32)
    return pl.pallas_call(
        kern,
        grid=(B, nr),
        in_specs=[
            pl.BlockSpec((tr, C), lambda b, i: (b * nr + i, 0)),
            pl.BlockSpec((halo, C), lambda b, i: (jnp.maximum((b * nr + i) * rb - 1, 0), 0)),
            pl.BlockSpec((halo, C), lambda b, i: (jnp.minimum((b * nr + i + 1) * rb, n_halo - 1), 0)),
            pl.BlockSpec((SUBLANES, C), lambda b, i: (0, 0)),
            pl.BlockSpec((tr, LANES), lambda b, i: (b * nr + i, 0)),
            pl.BlockSpec((SUBLANES, LANES), lambda b, i: (0, 0)),
        ],
        out_specs=[
            pl.BlockSpec((tr, GDN_W), lambda b, i: (b * nr + i, 0)),
            pl.BlockSpec((tr, GDN_W), lambda b, i: (b * nr + i, 0)),
            pl.BlockSpec((tr, GDN_W), lambda b, i: (b * nr + i, 0)),
            pl.BlockSpec((tr, LANES), lambda b, i: (b * nr + i, 0)),
            pl.BlockSpec((2 * GDN_CHAINS, tr), lambda b, i: (0, b * nr + i)),
        ],
        out_shape=[out_sd, out_sd, out_sd, jax.ShapeDtypeStruct((T, LANES), F32),
                   jax.ShapeDtypeStruct((2 * GDN_CHAINS, T), F32)],
        scratch_shapes=[pltpu.VMEM((tr + 2 * halo, C), F32)],
        compiler_params=_cparams(("parallel", "parallel")),
        name="gdn_prep",
    )(proj, proj, proj, conv_w8, gate_logits, gate_params)


def _bdot(a, b):
    return jnp.dot(a.astype(BF16), b.astype(BF16), preferred_element_type=F32)


def _bdot_nt(a, b):
    return lax.dot_general(a.astype(BF16), b.astype(BF16), (((1,), (1,)), ((), ())),
                           preferred_element_type=F32)


def _gdn_chunk_body(q_ref, k_ref, v_ref, g_ref, gt_ref, u_ref, wq_ref, ak_ref, dec_ref, *, cb):
    C = CHUNK
    row = lax.broadcasted_iota(jnp.int32, (C, C), 0)
    col = lax.broadcasted_iota(jnp.int32, (C, C), 1)
    eye = (row == col).astype(F32)
    masks = ((row >= col, row > col), (row <= col, row < col))
    hi = lax.Precision.HIGHEST
    chains = []
    for c in range(cb):
        rs = slice(c * C, (c + 1) * C)
        gates = g_ref[rs, :]
        gates_t = gt_ref[:, rs]
        g_tot = jnp.sum(gates, axis=0, keepdims=True)
        tot_rows = jnp.sum(gates_t, axis=1, keepdims=True)
        dec_ref[c] = jnp.broadcast_to(jnp.exp(tot_rows[GDN_CHAINS:2 * GDN_CHAINS]), (GDN_CHAINS, LANES))
        for d in range(2):
            incl, strict = masks[d]
            cum = incl.astype(F32)
            gc_cols = jnp.dot(cum, gates, precision=hi, preferred_element_type=F32)
            gc_rows = lax.dot_general(gates_t, cum, (((1,), (1,)), ((), ())), precision=hi,
                                      preferred_element_type=F32)
            for h in range(GDN_HEADS):
                lane_b = d * GDN_HEADS + h
                lane_g = GDN_CHAINS + lane_b
                sl = slice(h * GDN_D, (h + 1) * GDN_D)
                q = q_ref[rs, sl]
                k = k_ref[rs, sl]
                v = v_ref[rs, sl]
                beta = gates[:, lane_b:lane_b + 1]
                gc_col = gc_cols[:, lane_g:lane_g + 1]
                gc_row = gc_rows[lane_g:lane_g + 1, :]
                g_last = g_tot[:, lane_g:lane_g + 1]
                decay = jnp.where(incl, jnp.exp(jnp.where(incl, gc_col - gc_row, 0.0)), 0.0)
                e_col = jnp.exp(gc_col)
                k_beta = k * beta
                kk = _bdot_nt(jnp.concatenate([k_beta, q], axis=0), k)
                neg_l = jnp.where(strict, -kk[:C] * decay, 0.0)
                attn = jnp.where(incl, kk[C:] * decay, 0.0)
                rhs = jnp.concatenate([v * beta, k_beta * e_col], axis=1).astype(BF16)
                wq_ref[d, c, C:2 * C, sl] = (q * e_col).astype(BF16)
                ak_ref[d, c, h, 0:C, :] = attn.astype(BF16)
                ak_ref[d, c, h, C:C + GDN_D, :] = (k * jnp.exp(g_last - gc_col)).T.astype(BF16)
                chains.append((d, c, rs, sl, neg_l, rhs))
    zs = [jnp.concatenate([ch[4], eye], axis=1) for ch in chains]
    keep_s = lax.broadcasted_iota(jnp.int32, (C, 2 * C), 1) >= C
    for _ in range(int(math.log2(C))):
        zs = [_bdot(z[:, :C], z) + jnp.where(keep_s, z, 0.0) for z in zs]
    for (d, c, rs, sl, _, rhs), z in zip(chains, zs):
        sol = jnp.dot(z[:, C:].astype(BF16), rhs, preferred_element_type=F32)
        u_ref[d, rs, sl] = sol[:, :GDN_D]
        wq_ref[d, c, 0:C, sl] = sol[:, GDN_D:].astype(BF16)


def _gdn_chunk(q, k, v, gates, gates_t, cb):
    T = q.shape[0]
    nc = T // CHUNK
    rows = cb * CHUNK
    kern = functools.partial(_gdn_chunk_body, cb=cb)
    wide = pl.BlockSpec((rows, GDN_W), lambda i: (i, 0))
    return pl.pallas_call(
        kern,
        grid=(nc // cb,),
        in_specs=[wide, wide, wide, pl.BlockSpec((rows, LANES), lambda i: (i, 0)),
                  pl.BlockSpec((2 * GDN_CHAINS, rows), lambda i: (0, i))],
        out_specs=[
            pl.BlockSpec((2, rows, GDN_W), lambda i: (0, i, 0)),
            pl.BlockSpec((2, cb, 2 * CHUNK, GDN_W), lambda i: (0, i, 0, 0)),
            pl.BlockSpec((2, cb, GDN_HEADS, CHUNK + GDN_D, CHUNK), lambda i: (0, i, 0, 0, 0)),
            pl.BlockSpec((cb, GDN_CHAINS, LANES), lambda i: (i, 0, 0)),
        ],
        out_shape=[
            jax.ShapeDtypeStruct((2, T, GDN_W), F32),
            jax.ShapeDtypeStruct((2, nc, 2 * CHUNK, GDN_W), BF16),
            jax.ShapeDtypeStruct((2, nc, GDN_HEADS, CHUNK + GDN_D, CHUNK), BF16),
            jax.ShapeDtypeStruct((nc, GDN_CHAINS, LANES), F32),
        ],
        compiler_params=_cparams(("parallel",)),
        name="gdn_chunk",
    )(q, k, v, gates, gates_t)


def _gdn_scan_body(uf_ref, wqf_ref, akf_ref, decf_ref, ub_ref, wqb_ref, akb_ref, decb_ref,
                   of_ref, ob_ref, state_ref, *, cs):
    @pl.when(pl.program_id(1) == 0)
    def _():
        state_ref[...] = jnp.zeros_like(state_ref)

    C = CHUNK
    refs = ((uf_ref, wqf_ref, akf_ref, decf_ref, of_ref), (ub_ref, wqb_ref, akb_ref, decb_ref, ob_ref))
    chains = [(d, h) for d in range(2) for h in range(GDN_HEADS)]
    st = [state_ref[j] for j in range(GDN_CHAINS)]
    for step in range(cs):
        cidx = (step, cs - 1 - step)
        ws = [jnp.dot(refs[d][1][cidx[d], :, h * GDN_D:(h + 1) * GDN_D], st[j].astype(BF16),
                      preferred_element_type=F32) for j, (d, h) in enumerate(chains)]
        vn = [(refs[d][0][cidx[d] * C:(cidx[d] + 1) * C, h * GDN_D:(h + 1) * GDN_D] - ws[j][:C]).astype(BF16)
              for j, (d, h) in enumerate(chains)]
        rr = [jnp.dot(refs[d][2][cidx[d], h], vn[j], preferred_element_type=F32)
              for j, (d, h) in enumerate(chains)]
        for j, (d, h) in enumerate(chains):
            c = cidx[d]
            refs[d][4][c * C:(c + 1) * C, h * GDN_D:(h + 1) * GDN_D] = ws[j][C:] + rr[j][:C]
            st[j] = st[j] * refs[d][3][c, j:j + 1, :] + rr[j][C:]
    for j in range(GDN_CHAINS):
        state_ref[j] = st[j]


def _gdn_scan(u, wq, ak, dec, B, S, cs):
    T = B * S
    nb = S // (CHUNK * cs)
    rows = cs * CHUNK
    kern = functools.partial(_gdn_scan_body, cs=cs)
    fwd = lambda b, i: b * nb + i
    bwd = lambda b, i: b * nb + nb - 1 - i

    def specs(d, pos):
        return [
            pl.BlockSpec((None, rows, GDN_W), lambda b, i: (d, pos(b, i), 0)),
            pl.BlockSpec((None, cs, 2 * CHUNK, GDN_W), lambda b, i: (d, pos(b, i), 0, 0)),
            pl.BlockSpec((None, cs, GDN_HEADS, CHUNK + GDN_D, CHUNK), lambda b, i: (d, pos(b, i), 0, 0, 0)),
            pl.BlockSpec((cs, GDN_CHAINS, LANES), lambda b, i: (pos(b, i), 0, 0)),
        ]

    out_sd = jax.ShapeDtypeStruct((T, GDN_W), F32)
    return pl.pallas_call(
        kern,
        grid=(B, nb),
        in_specs=specs(0, fwd) + specs(1, bwd),
        out_specs=[pl.BlockSpec((rows, GDN_W), lambda b, i: (fwd(b, i), 0)),
                   pl.BlockSpec((rows, GDN_W), lambda b, i: (bwd(b, i), 0))],
        out_shape=[out_sd, out_sd],
        scratch_shapes=[pltpu.VMEM((GDN_CHAINS, GDN_D, GDN_D), F32)],
        compiler_params=_cparams(("parallel", "arbitrary")),
        name="gdn_scan",
    )(u, wq, ak, dec, u, wq, ak, dec)


def _attn_prep_body(q_ref, k_ref, v_ref, qw_ref, kw_ref, cos_ref, sin_ref, bias_ref,
                    qo_ref, ko_ref, vo_ref):
    cos = cos_ref[...]
    sin = sin_ref[...]
    rows = cos.shape[0]
    half = ATT_D // 2
    lane = lax.broadcasted_iota(jnp.int32, (rows, LANES), 1)
    low_half = (lane % ATT_D) < half
    first_head = lane < ATT_D
    extra_lane = lane == ATT_D
    mi = lax.broadcasted_iota(jnp.int32, (LANES, LANES), 0) // ATT_D
    mj = lax.broadcasted_iota(jnp.int32, (LANES, LANES), 1) // ATT_D
    head_mean = jnp.where(mi == mj, 1.0 / ATT_D, 0.0).astype(BF16)

    def norm_rope(x, w):
        sq = x * x
        hi = sq.astype(BF16)
        lo = (sq - hi.astype(F32)).astype(BF16)
        ms = (jnp.dot(hi, head_mean, preferred_element_type=F32)
              + jnp.dot(lo, head_mean, preferred_element_type=F32))
        y = x * lax.rsqrt(ms + EPS) * w
        partner = jnp.where(low_half, pltpu.roll(y, LANES - half, 1), pltpu.roll(y, half, 1))
        return y * cos + partner * sin

    def split_heads(r, extra):
        tail = jnp.where(extra_lane, extra, 0.0)
        return (jnp.where(first_head, r, tail).astype(BF16),
                jnp.where(first_head, pltpu.roll(r, ATT_D, 1), tail).astype(BF16))

    scale = LOG2E * ATT_D ** -0.5
    for c in range(ATT_HEADS // 2):
        r = norm_rope(q_ref[:, c * LANES:(c + 1) * LANES], qw_ref[...]) * scale
        qo_ref[2 * c], qo_ref[2 * c + 1] = split_heads(r, bias_ref[...])
    for c in range(ATT_KV_HEADS // 2):
        r = norm_rope(k_ref[:, c * LANES:(c + 1) * LANES], kw_ref[...])
        ko_ref[2 * c], ko_ref[2 * c + 1] = split_heads(r, 1.0)
        vo_ref[2 * c], vo_ref[2 * c + 1] = split_heads(v_ref[:, c * LANES:(c + 1) * LANES], 1.0)


def _attn_prep(proj, qw, kw, cosf, sinf, bias, B, S, tr):
    nr = S // tr
    row = lambda b, i: b * nr + i
    return pl.pallas_call(
        _attn_prep_body,
        grid=(B, nr),
        in_specs=[
            pl.BlockSpec((tr, ATT_Q), lambda b, i: (row(b, i), COL_QB // ATT_Q)),
            pl.BlockSpec((tr, ATT_KV), lambda b, i: (row(b, i), COL_KB // ATT_KV)),
            pl.BlockSpec((tr, ATT_KV), lambda b, i: (row(b, i), COL_VB // ATT_KV)),
            pl.BlockSpec((1, LANES), lambda b, i: (0, 0)),
            pl.BlockSpec((1, LANES), lambda b, i: (0, 0)),
            pl.BlockSpec((tr, LANES), lambda b, i: (i, 0)),
            pl.BlockSpec((tr, LANES), lambda b, i: (i, 0)),
            pl.BlockSpec((1, 1), lambda b, i: (0, 0)),
        ],
        out_specs=[
            pl.BlockSpec((None, ATT_HEADS, tr, ATT_DP), lambda b, i: (b, 0, i, 0)),
            pl.BlockSpec((None, ATT_KV_HEADS, tr, ATT_DP), lambda b, i: (b, 0, i, 0)),
            pl.BlockSpec((None, ATT_KV_HEADS, tr, ATT_DP), lambda b, i: (b, 0, i, 0)),
        ],
        out_shape=[
            jax.ShapeDtypeStruct((B, ATT_HEADS, S, ATT_DP), BF16),
            jax.ShapeDtypeStruct((B, ATT_KV_HEADS, S, ATT_DP), BF16),
            jax.ShapeDtypeStruct((B, ATT_KV_HEADS, S, ATT_DP), BF16),
        ],
        compiler_params=_cparams(("parallel", "parallel")),
        name="attn_prep",
    )(proj, proj, proj, qw, kw, cosf, sinf, bias)


def _attn_body(safe_ref, q_ref, k_ref, v_ref, o_ref, m_ref, acc_ref, *, tq, tk):
    safe = safe_ref[0] != 0
    nk = k_ref.shape[0] // tk
    q = q_ref[...].reshape(ATT_GROUP * tq, ATT_DP)
    acc_ref[...] = jnp.zeros_like(acc_ref)

    def scores(j):
        keys = pl.ds(pl.multiple_of(j * tk, tk), tk)
        s = lax.dot_general(q, k_ref[keys, :], (((1,), (1,)), ((), ())), preferred_element_type=F32)
        return s, v_ref[keys, :]

    @pl.when(safe)
    def _():
        def step(j, carry):
            s, v = scores(j)
            acc_ref[...] += jnp.dot(jnp.exp2(s).astype(BF16), v, preferred_element_type=F32)
            return carry
        lax.fori_loop(0, nk, step, 0)

    @pl.when(jnp.logical_not(safe))
    def _():
        m_ref[...] = jnp.full_like(m_ref, -jnp.inf)

        def step(j, carry):
            s, v = scores(j)
            m_prev = m_ref[...]
            m_new = jnp.maximum(m_prev, jnp.max(s, axis=-1, keepdims=True))
            p = jnp.exp2(s - m_new).astype(BF16)
            acc_ref[...] = jnp.exp2(m_prev - m_new) * acc_ref[...] + jnp.dot(p, v, preferred_element_type=F32)
            m_ref[...] = m_new
            return carry
        lax.fori_loop(0, nk, step, 0)

    acc = acc_ref[...]
    o = acc[:, :ATT_D] / acc[:, ATT_D:ATT_D + 1]
    for h in range(ATT_GROUP):
        o_ref[:, h * ATT_D:(h + 1) * ATT_D] = o[h * tq:(h + 1) * tq].astype(o_ref.dtype)


def _attention(safe, q, k, v, B, S, tq, tk):
    kern = functools.partial(_attn_body, tq=tq, tk=tk)
    gw = ATT_GROUP * ATT_D
    return pl.pallas_call(
        kern,
        grid_spec=pltpu.PrefetchScalarGridSpec(
            num_scalar_prefetch=1,
            grid=(B, ATT_KV_HEADS, S // tq),
            in_specs=[
                pl.BlockSpec((None, ATT_GROUP, tq, ATT_DP), lambda b, g, i, s: (b, g, i, 0)),
                pl.BlockSpec((None, None, S, ATT_DP), lambda b, g, i, s: (b, g, 0, 0)),
                pl.BlockSpec((None, None, S, ATT_DP), lambda b, g, i, s: (b, g, 0, 0)),
            ],
            out_specs=pl.BlockSpec((None, tq, gw), lambda b, g, i, s: (b, i, g)),
            scratch_shapes=[
                pltpu.VMEM((ATT_GROUP * tq, 1), F32),
                pltpu.VMEM((ATT_GROUP * tq, ATT_DP), F32),
            ],
        ),
        out_shape=jax.ShapeDtypeStruct((B, S, ATT_Q), BF16),
        compiler_params=_cparams(("parallel", "parallel", "parallel")),
        name="attention",
    )(safe, q, k, v)


def _outproj_body(of_ref, ob_ref, z_ref, att_ref, x_ref, gnw_ref, wa_ref, wb_ref,
                  fnw_ref, wr_ref, tril_ref, xm_ref, h_ref, r_ref, rt_ref, c_ref, count_ref):
    o = of_ref[...] + ob_ref[...]
    z = z_ref[...]
    parts = []
    for h in range(GDN_HEADS):
        sl = slice(h * GDN_D, (h + 1) * GDN_D)
        t = o[:, sl]
        t = t * lax.rsqrt(jnp.mean(t * t, axis=-1, keepdims=True) + EPS) * gnw_ref[...]
        parts.append((t * _silu(z[:, sl])).astype(BF16))
    mix_a = jnp.concatenate(parts, axis=-1)
    xm = x_ref[...] + jnp.dot(mix_a, wa_ref[...], preferred_element_type=F32)
    xm = xm + jnp.dot(att_ref[...], wb_ref[...], preferred_element_type=F32)
    xm_ref[...] = xm
    hn = xm * lax.rsqrt(jnp.mean(xm * xm, axis=-1, keepdims=True) + EPS) * fnw_ref[...]
    _store_token_tiles(h_ref, hn)
    hn_hi = hn.astype(BF16)
    hn_lo = (hn - hn_hi.astype(F32)).astype(BF16)
    both = jnp.dot(hn_hi, wr_ref[...], preferred_element_type=F32)
    logits = (both[:, :LANES] + both[:, LANES:]
              + jnp.dot(hn_lo, wr_ref[:, :LANES], preferred_element_type=F32))

    lane = lax.broadcasted_iota(jnp.int32, logits.shape, 1)
    big = jnp.int32(LANES)
    neg = -jnp.inf

    def masked_top(vals, mask):
        m = jnp.max(jnp.where(mask, vals, neg), axis=-1, keepdims=True)
        idx = jnp.min(jnp.where(mask & (vals == m), lane, big), axis=-1, keepdims=True)
        return m, idx

    gmask = lane < N_GROUPS
    gmax, gsel = masked_top(logits, gmask)
    gp_top = 1.0 / jnp.sum(jnp.where(gmask, jnp.exp(logits - gmax), 0.0), axis=-1, keepdims=True)
    lo = N_GROUPS + gsel * EXPERTS_PER_GROUP
    emask = (lane >= lo) & (lane < lo + EXPERTS_PER_GROUP)
    m1, i1 = masked_top(logits, emask)
    ex = jnp.where(emask, jnp.exp(logits - m1), 0.0)
    pf = ex / jnp.sum(ex, axis=-1, keepdims=True)
    p1, _ = masked_top(pf, emask)
    p2, i2 = masked_top(pf, emask & (lane != i1))
    denom = p1 + p2
    g1 = gp_top * p1 / denom
    g2 = gp_top * p2 / denom
    e1 = (i1 - N_GROUPS).astype(F32)
    e2 = (i2 - N_GROUPS).astype(F32)

    @pl.when(pl.program_id(0) == 0)
    def _():
        count_ref[...] = jnp.zeros_like(count_ref)

    hit1 = lane == i1
    hit2 = lane == i2
    picked = jnp.where(hit1 | hit2, 1.0, 0.0)
    seen = jnp.dot(tril_ref[...], picked.astype(BF16), preferred_element_type=F32)
    before = count_ref[0:1, :] + seen - picked
    rank1 = jnp.sum(jnp.where(hit1, before, 0.0), axis=-1, keepdims=True)
    rank2 = jnp.sum(jnp.where(hit2, before, 0.0), axis=-1, keepdims=True)
    total = count_ref[0:1, :] + seen[-1:, :]
    count_ref[...] = jnp.broadcast_to(total, count_ref.shape)
    c_ref[...] = jnp.broadcast_to(total, c_ref.shape)
    out = jnp.zeros_like(logits)
    for col, val in enumerate((e1, e2, g1, g2, rank1, rank2)):
        out = jnp.where(lane == col, val, out)
    r_ref[...] = out
    rt_ref[...] = out.T[0:SUBLANES, :]


def _outproj(o_f, o_b, proj, att, x2, gnw, w_a, w_b, fnw, w_r, tm):
    T, D = x2.shape
    row = lambda i: (i, 0)
    const = lambda i: (0, 0)
    return pl.pallas_call(
        _outproj_body,
        grid=(T // tm,),
        in_specs=[
            pl.BlockSpec((tm, GDN_W), row),
            pl.BlockSpec((tm, GDN_W), row),
            pl.BlockSpec((tm, GDN_W), lambda i: (i, COL_Z // GDN_W)),
            pl.BlockSpec((tm, ATT_Q), row),
            pl.BlockSpec((tm, D), row),
            pl.BlockSpec((1, GDN_D), const),
            pl.BlockSpec((GDN_W, D), const),
            pl.BlockSpec((ATT_Q, D), lambda i: (GDN_W // ATT_Q, 0)),
            pl.BlockSpec((1, D), const),
            pl.BlockSpec((D, 2 * LANES), const),
            pl.BlockSpec((tm, tm), const),
        ],
        out_specs=[pl.BlockSpec((tm, D), row), pl.BlockSpec((tm * ROW_TILE, LANES), row),
                   pl.BlockSpec((tm, LANES), row), pl.BlockSpec((SUBLANES, tm), lambda i: (0, i)),
                   pl.BlockSpec((SUBLANES, LANES), const)],
        out_shape=[jax.ShapeDtypeStruct((T, D), F32), jax.ShapeDtypeStruct((T * ROW_TILE, LANES), F32),
                   jax.ShapeDtypeStruct((T, LANES), F32), jax.ShapeDtypeStruct((SUBLANES, T), F32),
                   jax.ShapeDtypeStruct((SUBLANES, LANES), F32)],
        scratch_shapes=[pltpu.VMEM((SUBLANES, LANES), F32)],
        compiler_params=_cparams(("arbitrary",)),
        name="outproj_router",
    )(o_f, o_b, proj, att, x2, gnw, w_a, w_b, fnw, w_r, jnp.asarray(np.tri(tm, dtype=np.float32), BF16))


def _gather_start(src_hbm, dst_buf, sem, idx_ref, base, slot, n):
    for r in range(n):
        src = pl.multiple_of(idx_ref[base + r] * ROW_TILE, ROW_TILE)
        dst = pl.multiple_of((slot * n + r) * ROW_TILE, ROW_TILE)
        pltpu.make_async_copy(src_hbm.at[pl.ds(src, ROW_TILE)], dst_buf.at[pl.ds(dst, ROW_TILE)],
                              sem.at[slot]).start(priority=r % DMA_QUEUES)


def _gather_wait(src_hbm, dst_buf, sem, slot, n):
    dst = pl.multiple_of(slot * n * ROW_TILE, ROW_TILE)
    pltpu.make_async_copy(src_hbm.at[pl.ds(0, n * ROW_TILE)], dst_buf.at[pl.ds(dst, n * ROW_TILE)],
                          sem.at[slot]).wait()


def _experts_body(be_ref, src_ref, nused_ref,
                  h_hbm, wg_hbm, wu_hbm, wd_hbm, y_ref, xbuf, wg_buf, wu_buf, wd_buf, ws_ref, sem, wsem, *, bm):
    i = pl.program_id(0)
    n_blocks = pl.num_programs(0)
    n_used = nused_ref[0]
    slot = i % 2
    expert = be_ref[i]
    first = (i == 0) | (expert != be_ref[jnp.maximum(i - 1, 0)])

    def weight_copies(e, s):
        return [pltpu.make_async_copy(hbm.at[e], buf.at[s], wsem.at[s])
                for hbm, buf in ((wg_hbm, wg_buf), (wu_hbm, wu_buf), (wd_hbm, wd_buf))]

    def compute(prefetch_next):
        @pl.when(first)
        def _():
            cur = 1 - ws_ref[0]
            ws_ref[0] = cur
            for c in weight_copies(0, cur):
                c.wait()
            nxt = lax.while_loop(lambda j: (j < n_used) & (be_ref[jnp.minimum(j, n_blocks - 1)] == expert),
                                 lambda j: j + 1, i + 1)

            @pl.when(nxt < n_used)
            def _():
                for c in weight_copies(be_ref[jnp.minimum(nxt, n_blocks - 1)], 1 - cur):
                    c.start()

        ws = ws_ref[0]

        _gather_wait(h_hbm, xbuf, sem, slot, bm)
        x = _load_token_tiles(xbuf, slot * (bm * ROW_TILE), bm).astype(BF16)
        if prefetch_next:
            _gather_start(h_hbm, xbuf, sem, src_ref, (i + 1) * bm, 1 - slot, bm)
        gate = jnp.dot(x, wg_buf[ws].astype(BF16), preferred_element_type=F32)
        up = jnp.dot(x, wu_buf[ws].astype(BF16), preferred_element_type=F32)
        hid = (_silu(gate) * up).astype(BF16)
        _store_token_tiles(y_ref, jnp.dot(hid, wd_buf[ws].astype(BF16), preferred_element_type=F32))

    @pl.when((i == 0) & (n_used > 0))
    def _():
        ws_ref[0] = 1
        for c in weight_copies(be_ref[0], 0):
            c.start()
        _gather_start(h_hbm, xbuf, sem, src_ref, 0, 0, bm)

    @pl.when(i + 1 < n_used)
    def _():
        compute(True)

    @pl.when(i + 1 == n_used)
    def _():
        compute(False)

    @pl.when(i >= n_used)
    def _():
        y_ref[...] = jnp.zeros_like(y_ref)


def _experts(block_expert, src_tok, n_used, h2, w_gate, w_up, w_down, bm):
    D = ROW_TILE * LANES
    P = src_tok.shape[0]
    n_blocks = P // bm
    FF = w_gate.shape[-1]
    kern = functools.partial(_experts_body, bm=bm)
    hbm = pl.BlockSpec(memory_space=pl.ANY)
    return pl.pallas_call(
        kern,
        grid_spec=pltpu.PrefetchScalarGridSpec(
            num_scalar_prefetch=3,
            grid=(n_blocks,),
            in_specs=[hbm, hbm, hbm, hbm],
            out_specs=pl.BlockSpec((bm * ROW_TILE, LANES), lambda i, *_: (i, 0)),
            scratch_shapes=[
                pltpu.VMEM((2 * bm * ROW_TILE, LANES), F32),
                pltpu.VMEM((2, D, FF), F32), pltpu.VMEM((2, D, FF), F32), pltpu.VMEM((2, FF, D), F32),
                pltpu.SMEM((1,), jnp.int32),
                pltpu.SemaphoreType.DMA((2,)), pltpu.SemaphoreType.DMA((2,)),
            ],
        ),
        out_shape=jax.ShapeDtypeStruct((P * ROW_TILE, LANES), F32),
        compiler_params=_cparams(("arbitrary",)),
        name="moe_experts",
    )(block_expert, src_tok, n_used, h2, w_gate, w_up, w_down)


def _combine_body(dest_ref, y_hbm, xm_ref, r_ref, fw_ref, o_ref, ybuf, sem, *, tc):
    i = pl.program_id(0)
    n = pl.num_programs(0)
    slot = i % 2
    rows = TOP_K * tc

    def compute(prefetch_next):
        _gather_wait(y_hbm, ybuf, sem, slot, rows)
        route = r_ref[...]
        y0 = _load_token_tiles(ybuf, slot * (rows * ROW_TILE), tc)
        y1 = _load_token_tiles(ybuf, (slot * rows + tc) * ROW_TILE, tc)
        if prefetch_next:
            _gather_start(y_hbm, ybuf, sem, dest_ref, (i + 1) * rows, 1 - slot, rows)
        xo = xm_ref[...] + route[:, 2:3] * y0 + route[:, 3:4] * y1
        o_ref[...] = xo * lax.rsqrt(jnp.mean(xo * xo, axis=-1, keepdims=True) + EPS) * fw_ref[...]

    @pl.when(i == 0)
    def _():
        _gather_start(y_hbm, ybuf, sem, dest_ref, 0, 0, rows)

    @pl.when(i + 1 < n)
    def _():
        compute(True)

    @pl.when(i + 1 == n)
    def _():
        compute(False)


def _combine(dest_blocked, y_buf, x_mid, route, final_w, tc):
    T, D = x_mid.shape
    kern = functools.partial(_combine_body, tc=tc)
    return pl.pallas_call(
        kern,
        grid_spec=pltpu.PrefetchScalarGridSpec(
            num_scalar_prefetch=1,
            grid=(T // tc,),
            in_specs=[
                pl.BlockSpec(memory_space=pl.ANY),
                pl.BlockSpec((tc, D), lambda i, d: (i, 0)),
                pl.BlockSpec((tc, LANES), lambda i, d: (i, 0)),
                pl.BlockSpec((1, D), lambda i, d: (0, 0)),
            ],
            out_specs=pl.BlockSpec((tc, D), lambda i, d: (i, 0)),
            scratch_shapes=[pltpu.VMEM((2 * TOP_K * tc * ROW_TILE, LANES), F32), pltpu.SemaphoreType.DMA((2,))],
        ),
        out_shape=jax.ShapeDtypeStruct((T, D), F32),
        compiler_params=_cparams(("arbitrary",)),
        name="moe_combine",
    )(dest_blocked, y_buf, x_mid, route, final_w.reshape(1, D))


def _layer(x2, B, S, norm_mix_w, w_in, conv_w, a_log, dt_bias, gdn_norm_w, q_norm_w, k_norm_w, w_out,
           norm_ffn_w, w_router_group, w_router_expert, w_gate, w_up, w_down, final_w):
    T, D = x2.shape
    tl = _tiles(B, S)
    perm = np.concatenate([np.arange(0, ATT_D, 2), np.arange(1, ATT_D, 2)])
    o_z = 3 * GDN_W
    o_gate = o_z + GDN_W
    o_qb = o_gate + 2 * GDN_CHAINS
    o_kb = o_qb + ATT_Q
    o_vb = o_kb + ATT_KV
    qb_cols = o_qb + (np.arange(ATT_HEADS)[:, None] * ATT_D + perm[None, :]).reshape(-1)
    kb_cols = o_kb + (np.arange(ATT_KV_HEADS)[:, None] * ATT_D + perm[None, :]).reshape(-1)
    w_all = jnp.concatenate([
        w_in[:, :o_gate], w_in[:, qb_cols], w_in[:, kb_cols], w_in[:, o_vb:o_vb + ATT_KV],
        w_in[:, o_gate:o_qb], jnp.zeros((D, LANES - 2 * GDN_CHAINS), w_in.dtype)], axis=1).astype(BF16)

    proj, gate_logits = _inproj(x2, norm_mix_w, w_all, tl.proj_rows)

    conv_w8 = jnp.concatenate([conv_w, jnp.zeros((SUBLANES - CONV_W, conv_w.shape[1]), F32)], axis=0)
    gp = jnp.zeros((SUBLANES, LANES), F32)
    gp = gp.at[0, GDN_CHAINS:2 * GDN_CHAINS].set(jnp.exp(a_log.astype(F32)).reshape(-1))
    gp = gp.at[1, GDN_CHAINS:2 * GDN_CHAINS].set(dt_bias.astype(F32).reshape(-1))
    q_a, k_a, v_a, gates, gates_t = _gdn_prep(proj, gate_logits, conv_w8, gp, B, S, tl.prep_rows)
    u, wq, ak, dec = _gdn_chunk(q_a, k_a, v_a, gates, gates_t, tl.gdn_chunks)
    o_f, o_b = _gdn_scan(u, wq, ak, dec, B, S, tl.scan_chunks)

    rows = S // GRID_W
    rowp = np.repeat(np.arange(rows), GRID_W).astype(np.float64)
    colp = np.tile(np.arange(GRID_W), rows).astype(np.float64)
    axis_dims = ATT_D // 2
    inv_freq = ROPE_THETA ** (-np.arange(0, axis_dims, 2, dtype=np.float64) / axis_dims)
    ang = np.concatenate([rowp[:, None] * inv_freq, colp[:, None] * inv_freq], axis=-1)
    cosf = jnp.asarray(np.tile(np.cos(ang), (1, LANES // axis_dims)), F32)
    sinf = jnp.asarray(np.tile(np.concatenate([-np.sin(ang), np.sin(ang)], axis=-1), (1, LANES // ATT_D)), F32)
    q_gain = jnp.max(jnp.abs(q_norm_w)).astype(F32)
    k_gain = jnp.max(jnp.abs(k_norm_w)).astype(F32)
    score_bound = ATT_D ** 0.5 * q_gain * k_gain
    safe = (2.0 * score_bound <= SOFTMAX_SAFE_SPAN).astype(jnp.int32).reshape(1)
    bias = (-LOG2E * score_bound).reshape(1, 1)
    pair = lambda w: jnp.tile(w[perm], LANES // ATT_D).reshape(1, LANES)
    qh, kh, vh = _attn_prep(proj, pair(q_norm_w), pair(k_norm_w), cosf, sinf, bias, B, S, tl.prep_rows)
    att = _attention(safe, qh, kh, vh, B, S, tl.att_q, tl.att_k).reshape(T, ATT_Q)

    w_r32 = jnp.concatenate([w_router_group, w_router_expert,
                             jnp.zeros((D, LANES - N_GROUPS - N_EXPERTS), F32)], axis=1).astype(F32)
    w_r_hi = w_r32.astype(BF16)
    w_r = jnp.concatenate([w_r_hi, (w_r32 - w_r_hi.astype(F32)).astype(BF16)], axis=1)
    w_out_bf = w_out.astype(BF16)
    x_mid, h2, route, route_t, count_rows = _outproj(
        o_f, o_b, proj, att, x2, gdn_norm_w.reshape(1, GDN_D), w_out_bf, w_out_bf,
        norm_ffn_w.reshape(1, D), w_r, tl.proj_rows)

    bm = tl.moe_rows
    n_assign = T * TOP_K
    n_blocks = -(-(n_assign + N_EXPERTS * (bm - 1)) // bm)
    experts = jnp.arange(N_EXPERTS, dtype=jnp.int32)
    counts = count_rows[0, N_GROUPS:N_GROUPS + N_EXPERTS].astype(jnp.int32)
    padded = (counts + bm - 1) // bm * bm
    pad_end = jnp.cumsum(padded)
    pad_start = pad_end - padded
    block_start = jnp.arange(n_blocks, dtype=jnp.int32) * bm
    done = (pad_end[None, :] <= block_start[:, None]).astype(jnp.int32)
    begun = (pad_start[None, :] <= block_start[:, None]).astype(jnp.int32)
    block_expert = jnp.minimum(jnp.sum(done, axis=1), N_EXPERTS - 1)
    n_used = (pad_end[-1:] // bm).astype(jnp.int32)
    seg_start = jnp.sum(done * padded[None, :], axis=1)
    seg_entry = jnp.sum(done * counts[None, :], axis=1)
    seg_count = jnp.sum(begun * counts[None, :], axis=1) - seg_entry
    e_rows = route_t[0:TOP_K].astype(jnp.int32)
    rank_rows = route_t[4:4 + TOP_K].astype(jnp.int32)
    dest = jnp.sum(jnp.where(e_rows[:, :, None] == experts, pad_start, 0), axis=-1) + rank_rows
    tokens = jnp.tile(jnp.arange(T, dtype=jnp.int32), TOP_K)
    _, compact = lax.sort((dest.reshape(-1), tokens), num_keys=1)
    row = block_start[:, None] + jnp.arange(bm, dtype=jnp.int32)[None, :]
    seg_row = row - seg_start[:, None]
    holds_token = seg_row < seg_count[:, None]
    entry = jnp.clip(seg_entry[:, None] + seg_row, 0, n_assign - 1)
    src_tok = jnp.where(holds_token, compact[entry], lax.rem(row, jnp.full_like(row, T))).reshape(-1)

    y_buf = _experts(block_expert, src_tok, n_used, h2, w_gate, w_up, w_down, bm)

    tc = tl.comb_rows
    dest_blocked = dest.reshape(TOP_K, T // tc, tc).transpose(1, 0, 2).reshape(-1)
    return _combine(dest_blocked, y_buf, x_mid, route, final_w, tc)


def kernel(x, norm_mix_w, w_in, conv_w, a_log, dt_bias, gdn_norm_w, q_norm_w, k_norm_w, w_out, norm_ffn_w,
           w_router_group, w_router_expert, w_gate, w_up, w_down, final_norm_w):
    B, S, D = x.shape
    depth = w_in.shape[0]
    assert depth == 1, "the final norm is fused into the last (only) layer's combine step"
    out = _layer(x.reshape(B * S, D), B, S, norm_mix_w[0], w_in[0], conv_w[0], a_log[0], dt_bias[0],
                 gdn_norm_w[0], q_norm_w[0], k_norm_w[0], w_out[0], norm_ffn_w[0], w_router_group[0],
                 w_router_expert[0], w_gate[0], w_up[0], w_down[0], final_norm_w)
    return out.reshape(B, S, D)
```

```python
import functools
import math
from typing import NamedTuple

import jax
import jax.numpy as jnp
import numpy as np
from jax import lax
from jax.experimental import pallas as pl
from jax.experimental.pallas import tpu as pltpu

F32 = jnp.float32
BF16 = jnp.bfloat16
EPS = 1e-6

GRID_W = 64
GDN_HEADS = 4
GDN_D = 128
CONV_W = 5
CHUNK = 64
ATT_HEADS = 8
ATT_KV_HEADS = 2
ATT_GROUP = ATT_HEADS // ATT_KV_HEADS
ATT_D = 64
ROPE_THETA = 10000.0
N_GROUPS = 4
EXPERTS_PER_GROUP = 8
N_EXPERTS = N_GROUPS * EXPERTS_PER_GROUP
TOP_K = 2

GDN_W = GDN_HEADS * GDN_D
GDN_CHAINS = 2 * GDN_HEADS
ATT_Q = ATT_HEADS * ATT_D
ATT_KV = ATT_KV_HEADS * ATT_D
LANES = 128
SUBLANES = 8
ATT_DP = LANES

COL_Z = 3 * GDN_W
COL_QB = COL_Z + GDN_W
COL_KB = COL_QB + ATT_Q
COL_VB = COL_KB + ATT_KV
COL_GATE = COL_VB + ATT_KV
D_PROJ = COL_GATE + LANES

VMEM_LIMIT = 56 * 1024 * 1024
LOG2E = math.log2(math.e)
SOFTMAX_SAFE_SPAN = 60.0


class Tiles(NamedTuple):
    proj_rows: int
    prep_rows: int
    gdn_chunks: int
    scan_chunks: int
    att_q: int
    att_k: int
    moe_rows: int
    comb_rows: int


def _tile(n, want):
    t = min(n, want)
    assert n % t == 0, (n, want)
    return t


def _tiles(B, S):
    T = B * S
    n_chunks = S // CHUNK
    return Tiles(proj_rows=_tile(T, 512), prep_rows=_tile(S, 512), gdn_chunks=_tile(n_chunks, 4),
                 scan_chunks=_tile(n_chunks, 4), att_q=_tile(S, 512), att_k=_tile(S, 2048),
                 moe_rows=256, comb_rows=_tile(T, 256))


def _cparams(sem):
    return pltpu.CompilerParams(dimension_semantics=sem, vmem_limit_bytes=VMEM_LIMIT)


def _silu(x):
    return x * jax.nn.sigmoid(x)


ROW_TILE = SUBLANES
DMA_QUEUES = 2


def _store_token_tiles(ref, x, base=0):
    rows, d = x.shape
    assert d == ROW_TILE * LANES
    for c in range(ROW_TILE):
        ref[pl.ds(base + c, rows, stride=ROW_TILE), :] = x[:, c * LANES:(c + 1) * LANES]


def _load_token_tiles(ref, base, rows):
    return jnp.concatenate([ref[pl.ds(base + c, rows, stride=ROW_TILE), :] for c in range(ROW_TILE)], axis=-1)


def _inproj_body(x_ref, nw_ref, w_ref, qw_ref, kw_ref, cos_ref, sin_ref, bias_ref,
                 o_ref, g_ref, qo_ref, ko_ref, vo_ref):
    x = x_ref[...]
    h = x * lax.rsqrt(jnp.mean(x * x, axis=-1, keepdims=True) + EPS) * nw_ref[...]
    acc = jnp.dot(h.astype(BF16), w_ref[...], preferred_element_type=F32)
    o_ref[...] = acc[:, :COL_QB]
    g_ref[...] = acc[:, COL_GATE:]
    _attn_heads(acc[:, COL_QB:COL_KB], acc[:, COL_KB:COL_VB], acc[:, COL_VB:COL_GATE], qw_ref[...], kw_ref[...],
                cos_ref[...], sin_ref[...], bias_ref[...], qo_ref, ko_ref, vo_ref)


def _inproj(x2, norm_w, w_all, qw, kw, cosf, sinf, bias, B, S, tm):
    T, D = x2.shape
    nr = S // tm
    const = lambda i: (0, 0)
    heads = lambda i: (i // nr, 0, i % nr, 0)
    return pl.pallas_call(
        _inproj_body,
        grid=(T // tm,),
        in_specs=[
            pl.BlockSpec((tm, D), lambda i: (i, 0)),
            pl.BlockSpec((1, D), const),
            pl.BlockSpec((D, D_PROJ), const),
            pl.BlockSpec((1, LANES), const),
            pl.BlockSpec((1, LANES), const),
            pl.BlockSpec((tm, LANES), lambda i: (i % nr, 0)),
            pl.BlockSpec((tm, LANES), lambda i: (i % nr, 0)),
            pl.BlockSpec((1, 1), const),
        ],
        out_specs=[
            pl.BlockSpec((tm, COL_QB), lambda i: (i, 0)),
            pl.BlockSpec((tm, LANES), lambda i: (i, 0)),
            pl.BlockSpec((None, ATT_HEADS, tm, ATT_DP), heads),
            pl.BlockSpec((None, ATT_KV_HEADS, tm, ATT_DP), heads),
            pl.BlockSpec((None, ATT_KV_HEADS, tm, ATT_DP), heads),
        ],
        out_shape=[
            jax.ShapeDtypeStruct((T, COL_QB), F32),
            jax.ShapeDtypeStruct((T, LANES), F32),
            jax.ShapeDtypeStruct((B, ATT_HEADS, S, ATT_DP), BF16),
            jax.ShapeDtypeStruct((B, ATT_KV_HEADS, S, ATT_DP), BF16),
            jax.ShapeDtypeStruct((B, ATT_KV_HEADS, S, ATT_DP), BF16),
        ],
        compiler_params=_cparams(("parallel",)),
        name="inproj",
    )(x2, norm_w.reshape(1, D), w_all, qw, kw, cosf, sinf, bias)


def _gdn_prep_body(cur_ref, prev_ref, next_ref, cw_ref, gin_ref, gp_ref,
                   q_ref, k_ref, v_ref, g_ref, gt_ref, ext_ref, *, tr):
    i = pl.program_id(1)
    nr = pl.num_programs(1)
    halo = prev_ref.shape[0]
    pad = CONV_W // 2
    ext_ref[0:halo, :] = jnp.where(i > 0, prev_ref[...], 0.0)
    ext_ref[halo:halo + tr, :] = cur_ref[...]
    ext_ref[halo + tr:2 * halo + tr, :] = jnp.where(i < nr - 1, next_ref[...], 0.0)
    acc = cw_ref[0:1, :] * ext_ref[pl.ds(halo - pad, tr), :]
    for j in range(1, CONV_W):
        acc = acc + cw_ref[j:j + 1, :] * ext_ref[pl.ds(halo - pad + j, tr), :]
    y = _silu(acc)
    for h in range(GDN_HEADS):
        for base, ref, scale in ((0, q_ref, GDN_D ** -0.5), (GDN_W, k_ref, 1.0)):
            t = y[:, base + h * GDN_D: base + (h + 1) * GDN_D]
            t = t * (lax.rsqrt(jnp.sum(t * t, axis=-1, keepdims=True) + EPS) * scale)
            ref[:, h * GDN_D:(h + 1) * GDN_D] = t
    v_ref[...] = y[:, 2 * GDN_W:]
    gin = gin_ref[...]
    lane = lax.broadcasted_iota(jnp.int32, gin.shape, 1)
    a = gin + gp_ref[1:2, :]
    softplus = jnp.maximum(a, 0.0) + jnp.log1p(jnp.exp(-jnp.abs(a)))
    g = jnp.where(lane < GDN_CHAINS, jax.nn.sigmoid(gin), -gp_ref[0:1, :] * softplus)
    g_ref[...] = g
    gt_ref[...] = g.T[0:2 * GDN_CHAINS, :]


def _gdn_prep(proj, gate_logits, conv_w8, gate_params, B, S, tr):
    T = B * S
    nr = S // tr
    C = 3 * GDN_W
    halo = SUBLANES
    rb = tr // halo
    n_halo = T // halo
    kern = functools.partial(_gdn_prep_body, tr=tr)
    out_sd = jax.ShapeDtypeStruct((T, GDN_W), F32)
    return pl.pallas_call(
        kern,
        grid=(B, nr),
        in_specs=[
            pl.BlockSpec((tr, C), lambda b, i: (b * nr + i, 0)),
            pl.BlockSpec((halo, C), lambda b, i: (jnp.maximum((b * nr + i) * rb - 1, 0), 0)),
            pl.BlockSpec((halo, C), lambda b, i: (jnp.minimum((b * nr + i + 1) * rb, n_halo - 1), 0)),
            pl.BlockSpec((SUBLANES, C), lambda b, i: (0, 0)),
            pl.BlockSpec((tr, LANES), lambda b, i: (b * nr + i, 0)),
            pl.BlockSpec((SUBLANES, LANES), lambda b, i: (0, 0)),
        ],
        out_specs=[
            pl.BlockSpec((tr, GDN_W), lambda b, i: (b * nr + i, 0)),
            pl.BlockSpec((tr, GDN_W), lambda b, i: (b * nr + i, 0)),
            pl.BlockSpec((tr, GDN_W), lambda b, i: (b * nr + i, 0)),
            pl.BlockSpec((tr, LANES), lambda b, i: (b * nr + i, 0)),
            pl.BlockSpec((2 * GDN_CHAINS, tr), lambda b, i: (0, b * nr + i)),
        ],
        out_shape=[out_sd, out_sd, out_sd, jax.ShapeDtypeStruct((T, LANES), F32),
                   jax.ShapeDtypeStruct((2 * GDN_CHAINS, T), F32)],
        scratch_shapes=[pltpu.VMEM((tr + 2 * halo, C), F32)],
        compiler_params=_cparams(("parallel", "parallel")),
        name="gdn_prep",
    )(proj, proj, proj, conv_w8, gate_logits, gate_params)


def _bdot(a, b):
    return jnp.dot(a.astype(BF16), b.astype(BF16), preferred_element_type=F32)


def _bdot_nt(a, b):
    return lax.dot_general(a.astype(BF16), b.astype(BF16), (((1,), (1,)), ((), ())),
                           preferred_element_type=F32)


def _gdn_chunk_body(q_ref, k_ref, v_ref, g_ref, gt_ref, u_ref, wq_ref, ak_ref, dec_ref, *, cb):
    C = CHUNK
    row = lax.broadcasted_iota(jnp.int32, (C, C), 0)
    col = lax.broadcasted_iota(jnp.int32, (C, C), 1)
    eye = (row == col).astype(F32)
    masks = ((row >= col, row > col), (row <= col, row < col))
    hi = lax.Precision.HIGHEST
    chains = []
    for c in range(cb):
        rs = slice(c * C, (c + 1) * C)
        gates = g_ref[rs, :]
        gates_t = gt_ref[:, rs]
        g_tot = jnp.sum(gates, axis=0, keepdims=True)
        tot_rows = jnp.sum(gates_t, axis=1, keepdims=True)
        dec_ref[c] = jnp.broadcast_to(jnp.exp(tot_rows[GDN_CHAINS:2 * GDN_CHAINS]), (GDN_CHAINS, LANES))
        for d in range(2):
            incl, strict = masks[d]
            cum = incl.astype(F32)
            gc_cols = jnp.dot(cum, gates, precision=hi, preferred_element_type=F32)
            gc_rows = lax.dot_general(gates_t, cum, (((1,), (1,)), ((), ())), precision=hi,
                                      preferred_element_type=F32)
            for h in range(GDN_HEADS):
                lane_b = d * GDN_HEADS + h
                lane_g = GDN_CHAINS + lane_b
                sl = slice(h * GDN_D, (h + 1) * GDN_D)
                q = q_ref[rs, sl]
                k = k_ref[rs, sl]
                v = v_ref[rs, sl]
                beta = gates[:, lane_b:lane_b + 1]
                gc_col = gc_cols[:, lane_g:lane_g + 1]
                gc_row = gc_rows[lane_g:lane_g + 1, :]
                g_last = g_tot[:, lane_g:lane_g + 1]
                decay = jnp.where(incl, jnp.exp(jnp.where(incl, gc_col - gc_row, 0.0)), 0.0)
                e_col = jnp.exp(gc_col)
                k_beta = k * beta
                kk = _bdot_nt(jnp.concatenate([k_beta, q], axis=0), k)
                neg_l = jnp.where(strict, -kk[:C] * decay, 0.0)
                attn = jnp.where(incl, kk[C:] * decay, 0.0)
                rhs = jnp.concatenate([v * beta, k_beta * e_col], axis=1).astype(BF16)
                wq_ref[d, c, C:2 * C, sl] = (q * e_col).astype(BF16)
                ak_ref[d, c, h, 0:C, :] = attn.astype(BF16)
                ak_ref[d, c, h, C:C + GDN_D, :] = (k * jnp.exp(g_last - gc_col)).T.astype(BF16)
                chains.append((d, c, rs, sl, neg_l, rhs))
    zs = [jnp.concatenate([ch[4], eye], axis=1) for ch in chains]
    keep_s = lax.broadcasted_iota(jnp.int32, (C, 2 * C), 1) >= C
    for _ in range(int(math.log2(C))):
        zs = [_bdot(z[:, :C], z) + jnp.where(keep_s, z, 0.0) for z in zs]
    for (d, c, rs, sl, _, rhs), z in zip(chains, zs):
        sol = jnp.dot(z[:, C:].astype(BF16), rhs, preferred_element_type=F32)
        u_ref[d, rs, sl] = sol[:, :GDN_D]
        wq_ref[d, c, 0:C, sl] = sol[:, GDN_D:].astype(BF16)


def _gdn_chunk(q, k, v, gates, gates_t, cb):
    T = q.shape[0]
    nc = T // CHUNK
    rows = cb * CHUNK
    kern = functools.partial(_gdn_chunk_body, cb=cb)
    wide = pl.BlockSpec((rows, GDN_W), lambda i: (i, 0))
    return pl.pallas_call(
        kern,
        grid=(nc // cb,),
        in_specs=[wide, wide, wide, pl.BlockSpec((rows, LANES), lambda i: (i, 0)),
                  pl.BlockSpec((2 * GDN_CHAINS, rows), lambda i: (0, i))],
        out_specs=[
            pl.BlockSpec((2, rows, GDN_W), lambda i: (0, i, 0)),
            pl.BlockSpec((2, cb, 2 * CHUNK, GDN_W), lambda i: (0, i, 0, 0)),
            pl.BlockSpec((2, cb, GDN_HEADS, CHUNK + GDN_D, CHUNK), lambda i: (0, i, 0, 0, 0)),
            pl.BlockSpec((cb, GDN_CHAINS, LANES), lambda i: (i, 0, 0)),
        ],
        out_shape=[
            jax.ShapeDtypeStruct((2, T, GDN_W), F32),
            jax.ShapeDtypeStruct((2, nc, 2 * CHUNK, GDN_W), BF16),
            jax.ShapeDtypeStruct((2, nc, GDN_HEADS, CHUNK + GDN_D, CHUNK), BF16),
            jax.ShapeDtypeStruct((nc, GDN_CHAINS, LANES), F32),
        ],
        compiler_params=_cparams(("parallel",)),
        name="gdn_chunk",
    )(q, k, v, gates, gates_t)


def _gdn_scan_body(uf_ref, wqf_ref, akf_ref, decf_ref, ub_ref, wqb_ref, akb_ref, decb_ref,
                   of_ref, ob_ref, state_ref, *, cs):
    @pl.when(pl.program_id(1) == 0)
    def _():
        state_ref[...] = jnp.zeros_like(state_ref)

    C = CHUNK
    refs = ((uf_ref, wqf_ref, akf_ref, decf_ref, of_ref), (ub_ref, wqb_ref, akb_ref, decb_ref, ob_ref))
    chains = [(d, h) for d in range(2) for h in range(GDN_HEADS)]
    st = [state_ref[j] for j in range(GDN_CHAINS)]
    for step in range(cs):
        cidx = (step, cs - 1 - step)
        ws = [jnp.dot(refs[d][1][cidx[d], :, h * GDN_D:(h + 1) * GDN_D], st[j].astype(BF16),
                      preferred_element_type=F32) for j, (d, h) in enumerate(chains)]
        vn = [(refs[d][0][cidx[d] * C:(cidx[d] + 1) * C, h * GDN_D:(h + 1) * GDN_D] - ws[j][:C]).astype(BF16)
              for j, (d, h) in enumerate(chains)]
        rr = [jnp.dot(refs[d][2][cidx[d], h], vn[j], preferred_element_type=F32)
              for j, (d, h) in enumerate(chains)]
        for j, (d, h) in enumerate(chains):
            c = cidx[d]
            refs[d][4][c * C:(c + 1) * C, h * GDN_D:(h + 1) * GDN_D] = ws[j][C:] + rr[j][:C]
            st[j] = st[j] * refs[d][3][c, j:j + 1, :] + rr[j][C:]
    for j in range(GDN_CHAINS):
        state_ref[j] = st[j]


def _gdn_scan(u, wq, ak, dec, B, S, cs):
    T = B * S
    nb = S // (CHUNK * cs)
    rows = cs * CHUNK
    kern = functools.partial(_gdn_scan_body, cs=cs)
    fwd = lambda b, i: b * nb + i
    bwd = lambda b, i: b * nb + nb - 1 - i

    def specs(d, pos):
        return [
            pl.BlockSpec((None, rows, GDN_W), lambda b, i: (d, pos(b, i), 0)),
            pl.BlockSpec((None, cs, 2 * CHUNK, GDN_W), lambda b, i: (d, pos(b, i), 0, 0)),
            pl.BlockSpec((None, cs, GDN_HEADS, CHUNK + GDN_D, CHUNK), lambda b, i: (d, pos(b, i), 0, 0, 0)),
            pl.BlockSpec((cs, GDN_CHAINS, LANES), lambda b, i: (pos(b, i), 0, 0)),
        ]

    out_sd = jax.ShapeDtypeStruct((T, GDN_W), F32)
    return pl.pallas_call(
        kern,
        grid=(B, nb),
        in_specs=specs(0, fwd) + specs(1, bwd),
        out_specs=[pl.BlockSpec((rows, GDN_W), lambda b, i: (fwd(b, i), 0)),
                   pl.BlockSpec((rows, GDN_W), lambda b, i: (bwd(b, i), 0))],
        out_shape=[out_sd, out_sd],
        scratch_shapes=[pltpu.VMEM((GDN_CHAINS, GDN_D, GDN_D), F32)],
        compiler_params=_cparams(("parallel", "arbitrary")),
        name="gdn_scan",
    )(u, wq, ak, dec, u, wq, ak, dec)


def _attn_heads(q, k, v, qw, kw, cos, sin, bias, qo_ref, ko_ref, vo_ref):
    rows = cos.shape[0]
    lane = lax.broadcasted_iota(jnp.int32, (rows, LANES), 1)
    even_lane = lane % 2 == 0
    first_head = lane < ATT_D
    extra_lane = lane == ATT_D
    mi = lax.broadcasted_iota(jnp.int32, (LANES, LANES), 0) // ATT_D
    mj = lax.broadcasted_iota(jnp.int32, (LANES, LANES), 1) // ATT_D
    head_mean = jnp.where(mi == mj, 1.0 / ATT_D, 0.0).astype(BF16)

    def norm_rope(x, w):
        sq = x * x
        hi = sq.astype(BF16)
        lo = (sq - hi.astype(F32)).astype(BF16)
        ms = (jnp.dot(hi, head_mean, preferred_element_type=F32)
              + jnp.dot(lo, head_mean, preferred_element_type=F32))
        y = x * lax.rsqrt(ms + EPS) * w
        partner = jnp.where(even_lane, pltpu.roll(y, LANES - 1, 1), pltpu.roll(y, 1, 1))
        return y * cos + partner * sin

    def split_heads(r, extra):
        tail = jnp.where(extra_lane, extra, 0.0)
        return (jnp.where(first_head, r, tail).astype(BF16),
                jnp.where(first_head, pltpu.roll(r, ATT_D, 1), tail).astype(BF16))

    scale = LOG2E * ATT_D ** -0.5
    for c in range(ATT_HEADS // 2):
        r = norm_rope(q[:, c * LANES:(c + 1) * LANES], qw) * scale
        qo_ref[2 * c], qo_ref[2 * c + 1] = split_heads(r, bias)
    for c in range(ATT_KV_HEADS // 2):
        r = norm_rope(k[:, c * LANES:(c + 1) * LANES], kw)
        ko_ref[2 * c], ko_ref[2 * c + 1] = split_heads(r, 1.0)
        vo_ref[2 * c], vo_ref[2 * c + 1] = split_heads(v[:, c * LANES:(c + 1) * LANES], 1.0)


def _attn_body(safe_ref, q_ref, k_ref, v_ref, o_ref, m_ref, acc_ref, *, tq, tk):
    safe = safe_ref[0] != 0
    nk = k_ref.shape[0] // tk
    q = q_ref[...].reshape(ATT_GROUP * tq, ATT_DP)
    acc_ref[...] = jnp.zeros_like(acc_ref)

    def scores(j):
        keys = pl.ds(pl.multiple_of(j * tk, tk), tk)
        s = lax.dot_general(q, k_ref[keys, :], (((1,), (1,)), ((), ())), preferred_element_type=F32)
        return s, v_ref[keys, :]

    @pl.when(safe)
    def _():
        def step(j, carry):
            s, v = scores(j)
            acc_ref[...] += jnp.dot(jnp.exp2(s).astype(BF16), v, preferred_element_type=F32)
            return carry
        lax.fori_loop(0, nk, step, 0)

    @pl.when(jnp.logical_not(safe))
    def _():
        m_ref[...] = jnp.full_like(m_ref, -jnp.inf)

        def step(j, carry):
            s, v = scores(j)
            m_prev = m_ref[...]
            m_new = jnp.maximum(m_prev, jnp.max(s, axis=-1, keepdims=True))
            p = jnp.exp2(s - m_new).astype(BF16)
            acc_ref[...] = jnp.exp2(m_prev - m_new) * acc_ref[...] + jnp.dot(p, v, preferred_element_type=F32)
            m_ref[...] = m_new
            return carry
        lax.fori_loop(0, nk, step, 0)

    acc = acc_ref[...]
    o = acc[:, :ATT_D] / acc[:, ATT_D:ATT_D + 1]
    for h in range(ATT_GROUP):
        o_ref[:, h * ATT_D:(h + 1) * ATT_D] = o[h * tq:(h + 1) * tq].astype(o_ref.dtype)


def _attention(safe, q, k, v, B, S, tq, tk):
    kern = functools.partial(_attn_body, tq=tq, tk=tk)
    gw = ATT_GROUP * ATT_D
    return pl.pallas_call(
        kern,
        grid_spec=pltpu.PrefetchScalarGridSpec(
            num_scalar_prefetch=1,
            grid=(B, ATT_KV_HEADS, S // tq),
            in_specs=[
                pl.BlockSpec((None, ATT_GROUP, tq, ATT_DP), lambda b, g, i, s: (b, g, i, 0)),
                pl.BlockSpec((None, None, S, ATT_DP), lambda b, g, i, s: (b, g, 0, 0)),
                pl.BlockSpec((None, None, S, ATT_DP), lambda b, g, i, s: (b, g, 0, 0)),
            ],
            out_specs=pl.BlockSpec((None, tq, gw), lambda b, g, i, s: (b, i, g)),
            scratch_shapes=[
                pltpu.VMEM((ATT_GROUP * tq, 1), F32),
                pltpu.VMEM((ATT_GROUP * tq, ATT_DP), F32),
            ],
        ),
        out_shape=jax.ShapeDtypeStruct((B, S, ATT_Q), BF16),
        compiler_params=_cparams(("parallel", "parallel", "parallel")),
        name="attention",
    )(safe, q, k, v)


def _outproj_body(of_ref, ob_ref, z_ref, att_ref, x_ref, gnw_ref, wa_ref, wb_ref,
                  fnw_ref, wr_ref, tril_ref, xm_ref, h_ref, r_ref, rt_ref, c_ref, count_ref):
    o = of_ref[...] + ob_ref[...]
    z = z_ref[...]
    parts = []
    for h in range(GDN_HEADS):
        sl = slice(h * GDN_D, (h + 1) * GDN_D)
        t = o[:, sl]
        t = t * lax.rsqrt(jnp.mean(t * t, axis=-1, keepdims=True) + EPS) * gnw_ref[...]
        parts.append((t * _silu(z[:, sl])).astype(BF16))
    mix_a = jnp.concatenate(parts, axis=-1)
    xm = x_ref[...] + jnp.dot(mix_a, wa_ref[...], preferred_element_type=F32)
    xm = xm + jnp.dot(att_ref[...], wb_ref[...], preferred_element_type=F32)
    xm_ref[...] = xm
    hn = xm * lax.rsqrt(jnp.mean(xm * xm, axis=-1, keepdims=True) + EPS) * fnw_ref[...]
    _store_token_tiles(h_ref, hn)
    hn_hi = hn.astype(BF16)
    hn_lo = (hn - hn_hi.astype(F32)).astype(BF16)
    both = jnp.dot(hn_hi, wr_ref[...], preferred_element_type=F32)
    logits = (both[:, :LANES] + both[:, LANES:]
              + jnp.dot(hn_lo, wr_ref[:, :LANES], preferred_element_type=F32))

    lane = lax.broadcasted_iota(jnp.int32, logits.shape, 1)
    big = jnp.int32(LANES)
    neg = -jnp.inf

    def masked_top(vals, mask):
        m = jnp.max(jnp.where(mask, vals, neg), axis=-1, keepdims=True)
        idx = jnp.min(jnp.where(mask & (vals == m), lane, big), axis=-1, keepdims=True)
        return m, idx

    gmask = lane < N_GROUPS
    gmax, gsel = masked_top(logits, gmask)
    gp_top = 1.0 / jnp.sum(jnp.where(gmask, jnp.exp(logits - gmax), 0.0), axis=-1, keepdims=True)
    lo = N_GROUPS + gsel * EXPERTS_PER_GROUP
    emask = (lane >= lo) & (lane < lo + EXPERTS_PER_GROUP)
    m1, i1 = masked_top(logits, emask)
    ex = jnp.where(emask, jnp.exp(logits - m1), 0.0)
    pf = ex / jnp.sum(ex, axis=-1, keepdims=True)
    p1, _ = masked_top(pf, emask)
    p2, i2 = masked_top(pf, emask & (lane != i1))
    denom = p1 + p2
    g1 = gp_top * p1 / denom
    g2 = gp_top * p2 / denom
    e1 = (i1 - N_GROUPS).astype(F32)
    e2 = (i2 - N_GROUPS).astype(F32)

    @pl.when(pl.program_id(0) == 0)
    def _():
        count_ref[...] = jnp.zeros_like(count_ref)

    hit1 = lane == i1
    hit2 = lane == i2
    picked = jnp.where(hit1 | hit2, 1.0, 0.0)
    seen = jnp.dot(tril_ref[...], picked.astype(BF16), preferred_element_type=F32)
    before = count_ref[0:1, :] + seen - picked
    rank1 = jnp.sum(jnp.where(hit1, before, 0.0), axis=-1, keepdims=True)
    rank2 = jnp.sum(jnp.where(hit2, before, 0.0), axis=-1, keepdims=True)
    total = count_ref[0:1, :] + seen[-1:, :]
    count_ref[...] = jnp.broadcast_to(total, count_ref.shape)
    c_ref[...] = jnp.broadcast_to(total, c_ref.shape)
    out = jnp.zeros_like(logits)
    for col, val in enumerate((e1, e2, g1, g2, rank1, rank2)):
        out = jnp.where(lane == col, val, out)
    r_ref[...] = out
    rt_ref[...] = out.T[0:SUBLANES, :]


def _outproj(o_f, o_b, proj, att, x2, gnw, w_a, w_b, fnw, w_r, tm):
    T, D = x2.shape
    row = lambda i: (i, 0)
    const = lambda i: (0, 0)
    return pl.pallas_call(
        _outproj_body,
        grid=(T // tm,),
        in_specs=[
            pl.BlockSpec((tm, GDN_W), row),
            pl.BlockSpec((tm, GDN_W), row),
            pl.BlockSpec((tm, GDN_W), lambda i: (i, COL_Z // GDN_W)),
            pl.BlockSpec((tm, ATT_Q), row),
            pl.BlockSpec((tm, D), row),
            pl.BlockSpec((1, GDN_D), const),
            pl.BlockSpec((GDN_W, D), const),
            pl.BlockSpec((ATT_Q, D), lambda i: (GDN_W // ATT_Q, 0)),
            pl.BlockSpec((1, D), const),
            pl.BlockSpec((D, 2 * LANES), const),
            pl.BlockSpec((tm, tm), const),
        ],
        out_specs=[pl.BlockSpec((tm, D), row), pl.BlockSpec((tm * ROW_TILE, LANES), row),
                   pl.BlockSpec((tm, LANES), row), pl.BlockSpec((SUBLANES, tm), lambda i: (0, i)),
                   pl.BlockSpec((SUBLANES, LANES), const)],
        out_shape=[jax.ShapeDtypeStruct((T, D), F32), jax.ShapeDtypeStruct((T * ROW_TILE, LANES), F32),
                   jax.ShapeDtypeStruct((T, LANES), F32), jax.ShapeDtypeStruct((SUBLANES, T), F32),
                   jax.ShapeDtypeStruct((SUBLANES, LANES), F32)],
        scratch_shapes=[pltpu.VMEM((SUBLANES, LANES), F32)],
        compiler_params=_cparams(("arbitrary",)),
        name="outproj_router",
    )(o_f, o_b, proj, att, x2, gnw, w_a, w_b, fnw, w_r, jnp.asarray(np.tri(tm, dtype=np.float32), BF16))


def _gather_start(src_hbm, dst_buf, sem, idx_ref, base, slot, n):
    for r in range(n):
        src = pl.multiple_of(idx_ref[base + r] * ROW_TILE, ROW_TILE)
        dst = pl.multiple_of((slot * n + r) * ROW_TILE, ROW_TILE)
        pltpu.make_async_copy(src_hbm.at[pl.ds(src, ROW_TILE)], dst_buf.at[pl.ds(dst, ROW_TILE)],
                              sem.at[slot]).start(priority=r % DMA_QUEUES)


def _gather_wait(src_hbm, dst_buf, sem, slot, n):
    dst = pl.multiple_of(slot * n * ROW_TILE, ROW_TILE)
    pltpu.make_async_copy(src_hbm.at[pl.ds(0, n * ROW_TILE)], dst_buf.at[pl.ds(dst, n * ROW_TILE)],
                          sem.at[slot]).wait()


def _experts_body(be_ref, src_ref, nused_ref,
                  h_hbm, wg_hbm, wu_hbm, wd_hbm, y_ref, xbuf, wg_buf, wu_buf, wd_buf, ws_ref, sem, wsem, *, bm):
    i = pl.program_id(0)
    n_blocks = pl.num_programs(0)
    n_used = nused_ref[0]
    slot = i % 2
    expert = be_ref[i]
    first = (i == 0) | (expert != be_ref[jnp.maximum(i - 1, 0)])

    def weight_copies(e, s):
        return [pltpu.make_async_copy(hbm.at[e], buf.at[s], wsem.at[s])
                for hbm, buf in ((wg_hbm, wg_buf), (wu_hbm, wu_buf), (wd_hbm, wd_buf))]

    def compute(prefetch_next):
        @pl.when(first)
        def _():
            cur = 1 - ws_ref[0]
            ws_ref[0] = cur
            for c in weight_copies(0, cur):
                c.wait()
            nxt = lax.while_loop(lambda j: (j < n_used) & (be_ref[jnp.minimum(j, n_blocks - 1)] == expert),
                                 lambda j: j + 1, i + 1)

            @pl.when(nxt < n_used)
            def _():
                for c in weight_copies(be_ref[jnp.minimum(nxt, n_blocks - 1)], 1 - cur):
                    c.start()

        ws = ws_ref[0]

        _gather_wait(h_hbm, xbuf, sem, slot, bm)
        x = _load_token_tiles(xbuf, slot * (bm * ROW_TILE), bm).astype(BF16)
        if prefetch_next:
            _gather_start(h_hbm, xbuf, sem, src_ref, (i + 1) * bm, 1 - slot, bm)
        gate = jnp.dot(x, wg_buf[ws].astype(BF16), preferred_element_type=F32)
        up = jnp.dot(x, wu_buf[ws].astype(BF16), preferred_element_type=F32)
        hid = (_silu(gate) * up).astype(BF16)
        _store_token_tiles(y_ref, jnp.dot(hid, wd_buf[ws].astype(BF16), preferred_element_type=F32))

    @pl.when((i == 0) & (n_used > 0))
    def _():
        ws_ref[0] = 1
        for c in weight_copies(be_ref[0], 0):
            c.start()
        _gather_start(h_hbm, xbuf, sem, src_ref, 0, 0, bm)

    @pl.when(i + 1 < n_used)
    def _():
        compute(True)

    @pl.when(i + 1 == n_used)
    def _():
        compute(False)

    @pl.when(i >= n_used)
    def _():
        y_ref[...] = jnp.zeros_like(y_ref)


def _experts(block_expert, src_tok, n_used, h2, w_gate, w_up, w_down, bm):
    D = ROW_TILE * LANES
    P = src_tok.shape[0]
    n_blocks = P // bm
    FF = w_gate.shape[-1]
    kern = functools.partial(_experts_body, bm=bm)
    hbm = pl.BlockSpec(memory_space=pl.ANY)
    return pl.pallas_call(
        kern,
        grid_spec=pltpu.PrefetchScalarGridSpec(
            num_scalar_prefetch=3,
            grid=(n_blocks,),
            in_specs=[hbm, hbm, hbm, hbm],
            out_specs=pl.BlockSpec((bm * ROW_TILE, LANES), lambda i, *_: (i, 0)),
            scratch_shapes=[
                pltpu.VMEM((2 * bm * ROW_TILE, LANES), F32),
                pltpu.VMEM((2, D, FF), F32), pltpu.VMEM((2, D, FF), F32), pltpu.VMEM((2, FF, D), F32),
                pltpu.SMEM((1,), jnp.int32),
                pltpu.SemaphoreType.DMA((2,)), pltpu.SemaphoreType.DMA((2,)),
            ],
        ),
        out_shape=jax.ShapeDtypeStruct((P * ROW_TILE, LANES), F32),
        compiler_params=_cparams(("arbitrary",)),
        name="moe_experts",
    )(block_expert, src_tok, n_used, h2, w_gate, w_up, w_down)


def _combine_body(dest_ref, y_hbm, xm_ref, r_ref, fw_ref, o_ref, ybuf, sem, *, tc):
    i = pl.program_id(0)
    n = pl.num_programs(0)
    slot = i % 2
    rows = TOP_K * tc

    def compute(prefetch_next):
        _gather_wait(y_hbm, ybuf, sem, slot, rows)
        route = r_ref[...]
        y0 = _load_token_tiles(ybuf, slot * (rows * ROW_TILE), tc)
        y1 = _load_token_tiles(ybuf, (slot * rows + tc) * ROW_TILE, tc)
        if prefetch_next:
            _gather_start(y_hbm, ybuf, sem, dest_ref, (i + 1) * rows, 1 - slot, rows)
        xo = xm_ref[...] + route[:, 2:3] * y0 + route[:, 3:4] * y1
        o_ref[...] = xo * lax.rsqrt(jnp.mean(xo * xo, axis=-1, keepdims=True) + EPS) * fw_ref[...]

    @pl.when(i == 0)
    def _():
        _gather_start(y_hbm, ybuf, sem, dest_ref, 0, 0, rows)

    @pl.when(i + 1 < n)
    def _():
        compute(True)

    @pl.when(i + 1 == n)
    def _():
        compute(False)


def _combine(dest_blocked, y_buf, x_mid, route, final_w, tc):
    T, D = x_mid.shape
    kern = functools.partial(_combine_body, tc=tc)
    return pl.pallas_call(
        kern,
        grid_spec=pltpu.PrefetchScalarGridSpec(
            num_scalar_prefetch=1,
            grid=(T // tc,),
            in_specs=[
                pl.BlockSpec(memory_space=pl.ANY),
                pl.BlockSpec((tc, D), lambda i, d: (i, 0)),
                pl.BlockSpec((tc, LANES), lambda i, d: (i, 0)),
                pl.BlockSpec((1, D), lambda i, d: (0, 0)),
            ],
            out_specs=pl.BlockSpec((tc, D), lambda i, d: (i, 0)),
            scratch_shapes=[pltpu.VMEM((2 * TOP_K * tc * ROW_TILE, LANES), F32), pltpu.SemaphoreType.DMA((2,))],
        ),
        out_shape=jax.ShapeDtypeStruct((T, D), F32),
        compiler_params=_cparams(("arbitrary",)),
        name="moe_combine",
    )(dest_blocked, y_buf, x_mid, route, final_w.reshape(1, D))


def _layer(x2, B, S, norm_mix_w, w_in, conv_w, a_log, dt_bias, gdn_norm_w, q_norm_w, k_norm_w, w_out,
           norm_ffn_w, w_router_group, w_router_expert, w_gate, w_up, w_down, final_w):
    T, D = x2.shape
    tl = _tiles(B, S)
    o_gate = 4 * GDN_W
    o_qb = o_gate + 2 * GDN_CHAINS
    w_all = jnp.concatenate([w_in[:, :o_gate], w_in[:, o_qb:], w_in[:, o_gate:o_qb],
                             jnp.zeros((D, LANES - 2 * GDN_CHAINS), w_in.dtype)], axis=1).astype(BF16)

    rows = S // GRID_W
    rowp = np.repeat(np.arange(rows), GRID_W).astype(np.float64)
    colp = np.tile(np.arange(GRID_W), rows).astype(np.float64)
    axis_dims = ATT_D // 2
    inv_freq = ROPE_THETA ** (-np.arange(0, axis_dims, 2, dtype=np.float64) / axis_dims)
    ang = np.concatenate([rowp[:, None] * inv_freq, colp[:, None] * inv_freq], axis=-1)
    pair_sign = np.tile(np.array([-1.0, 1.0]), axis_dims)
    cosf = jnp.asarray(np.tile(np.repeat(np.cos(ang), 2, axis=1), (1, LANES // ATT_D)), F32)
    sinf = jnp.asarray(np.tile(np.repeat(np.sin(ang), 2, axis=1) * pair_sign, (1, LANES // ATT_D)), F32)
    q_gain = jnp.max(jnp.abs(q_norm_w)).astype(F32)
    k_gain = jnp.max(jnp.abs(k_norm_w)).astype(F32)
    score_bound = ATT_D ** 0.5 * q_gain * k_gain
    safe = (2.0 * score_bound <= SOFTMAX_SAFE_SPAN).astype(jnp.int32).reshape(1)
    bias = (-LOG2E * score_bound).reshape(1, 1)
    pair = lambda w: jnp.tile(w, LANES // ATT_D).reshape(1, LANES)

    proj, gate_logits, qh, kh, vh = _inproj(x2, norm_mix_w, w_all, pair(q_norm_w), pair(k_norm_w),
                                            cosf, sinf, bias, B, S, tl.proj_rows)

    conv_w8 = jnp.concatenate([conv_w, jnp.zeros((SUBLANES - CONV_W, conv_w.shape[1]), F32)], axis=0)
    gp = jnp.zeros((SUBLANES, LANES), F32)
    gp = gp.at[0, GDN_CHAINS:2 * GDN_CHAINS].set(jnp.exp(a_log.astype(F32)).reshape(-1))
    gp = gp.at[1, GDN_CHAINS:2 * GDN_CHAINS].set(dt_bias.astype(F32).reshape(-1))
    q_a, k_a, v_a, gates, gates_t = _gdn_prep(proj, gate_logits, conv_w8, gp, B, S, tl.prep_rows)
    u, wq, ak, dec = _gdn_chunk(q_a, k_a, v_a, gates, gates_t, tl.gdn_chunks)
    o_f, o_b = _gdn_scan(u, wq, ak, dec, B, S, tl.scan_chunks)

    att = _attention(safe, qh, kh, vh, B, S, tl.att_q, tl.att_k).reshape(T, ATT_Q)

    w_r32 = jnp.concatenate([w_router_group, w_router_expert,
                             jnp.zeros((D, LANES - N_GROUPS - N_EXPERTS), F32)], axis=1).astype(F32)
    w_r_hi = w_r32.astype(BF16)
    w_r = jnp.concatenate([w_r_hi, (w_r32 - w_r_hi.astype(F32)).astype(BF16)], axis=1)
    w_out_bf = w_out.astype(BF16)
    x_mid, h2, route, route_t, count_rows = _outproj(
        o_f, o_b, proj, att, x2, gdn_norm_w.reshape(1, GDN_D), w_out_bf, w_out_bf,
        norm_ffn_w.reshape(1, D), w_r, tl.proj_rows)

    bm = tl.moe_rows
    n_assign = T * TOP_K
    n_blocks = -(-(n_assign + N_EXPERTS * (bm - 1)) // bm)
    experts = jnp.arange(N_EXPERTS, dtype=jnp.int32)
    counts = count_rows[0, N_GROUPS:N_GROUPS + N_EXPERTS].astype(jnp.int32)
    padded = (counts + bm - 1) // bm * bm
    pad_end = jnp.cumsum(padded)
    pad_start = pad_end - padded
    block_start = jnp.arange(n_blocks, dtype=jnp.int32) * bm
    done = (pad_end[None, :] <= block_start[:, None]).astype(jnp.int32)
    begun = (pad_start[None, :] <= block_start[:, None]).astype(jnp.int32)
    block_expert = jnp.minimum(jnp.sum(done, axis=1), N_EXPERTS - 1)
    n_used = (pad_end[-1:] // bm).astype(jnp.int32)
    seg_start = jnp.sum(done * padded[None, :], axis=1)
    seg_entry = jnp.sum(done * counts[None, :], axis=1)
    seg_count = jnp.sum(begun * counts[None, :], axis=1) - seg_entry
    e_rows = route_t[0:TOP_K].astype(jnp.int32)
    rank_rows = route_t[4:4 + TOP_K].astype(jnp.int32)
    dest = jnp.sum(jnp.where(e_rows[:, :, None] == experts, pad_start, 0), axis=-1) + rank_rows
    tokens = jnp.tile(jnp.arange(T, dtype=jnp.int32), TOP_K)
    _, compact = lax.sort((dest.reshape(-1), tokens), num_keys=1)
    row = block_start[:, None] + jnp.arange(bm, dtype=jnp.int32)[None, :]
    seg_row = row - seg_start[:, None]
    holds_token = seg_row < seg_count[:, None]
    entry = jnp.clip(seg_entry[:, None] + seg_row, 0, n_assign - 1)
    src_tok = jnp.where(holds_token, compact[entry], lax.rem(row, jnp.full_like(row, T))).reshape(-1)

    y_buf = _experts(block_expert, src_tok, n_used, h2, w_gate, w_up, w_down, bm)

    tc = tl.comb_rows
    dest_blocked = dest.reshape(TOP_K, T // tc, tc).transpose(1, 0, 2).reshape(-1)
    return _combine(dest_blocked, y_buf, x_mid, route, final_w, tc)


def kernel(x, norm_mix_w, w_in, conv_w, a_log, dt_bias, gdn_norm_w, q_norm_w, k_norm_w, w_out, norm_ffn_w,
           w_router_group, w_router_expert, w_gate, w_up, w_down, final_norm_w):
    B, S, D = x.shape
    depth = w_in.shape[0]
    assert depth == 1, "the final norm is fused into the last (only) layer's combine step"
    out = _layer(x.reshape(B * S, D), B, S, norm_mix_w[0], w_in[0], conv_w[0], a_log[0], dt_bias[0],
                 gdn_norm_w[0], q_norm_w[0], k_norm_w[0], w_out[0], norm_ffn_w[0], w_router_group[0],
                 w_router_expert[0], w_gate[0], w_up[0], w_down[0], final_norm_w)
    return out.reshape(B, S, D)
```

```python
import functools
import math
from typing import NamedTuple

import jax
import jax.numpy as jnp
import numpy as np
from jax import lax
from jax.experimental import pallas as pl
from jax.experimental.pallas import tpu as pltpu

F32 = jnp.float32
BF16 = jnp.bfloat16
EPS = 1e-6

GRID_W = 64
GDN_HEADS = 4
GDN_D = 128
CONV_W = 5
CHUNK = 64
ATT_HEADS = 8
ATT_KV_HEADS = 2
ATT_GROUP = ATT_HEADS // ATT_KV_HEADS
ATT_D = 64
ROPE_THETA = 10000.0
N_GROUPS = 4
EXPERTS_PER_GROUP = 8
N_EXPERTS = N_GROUPS * EXPERTS_PER_GROUP
TOP_K = 2

GDN_W = GDN_HEADS * GDN_D
GDN_CHAINS = 2 * GDN_HEADS
ATT_Q = ATT_HEADS * ATT_D
ATT_KV = ATT_KV_HEADS * ATT_D
LANES = 128
SUBLANES = 8
ATT_DP = LANES
ROUTER_ROWS = -(-(N_GROUPS + N_EXPERTS) // SUBLANES) * SUBLANES

COL_Z = 3 * GDN_W
COL_QB = COL_Z + GDN_W
COL_KB = COL_QB + ATT_Q
COL_VB = COL_KB + ATT_KV
COL_GATE = COL_VB + ATT_KV
D_PROJ = COL_GATE + LANES

VMEM_LIMIT = 56 * 1024 * 1024
LOG2E = math.log2(math.e)
SOFTMAX_SAFE_SPAN = 60.0


class Tiles(NamedTuple):
    proj_rows: int
    prep_rows: int
    gdn_chunks: int
    scan_chunks: int
    att_q: int
    att_k: int
    moe_rows: int
    comb_rows: int


def _tile(n, want):
    t = min(n, want)
    assert n % t == 0, (n, want)
    return t


def _tiles(B, S):
    T = B * S
    n_chunks = S // CHUNK
    return Tiles(proj_rows=_tile(T, 512), prep_rows=_tile(S, 512), gdn_chunks=_tile(n_chunks, 8),
                 scan_chunks=_tile(n_chunks, 4), att_q=_tile(S, 512), att_k=_tile(S, 2048),
                 moe_rows=256, comb_rows=_tile(T, 256))


def _cparams(sem):
    return pltpu.CompilerParams(dimension_semantics=sem, vmem_limit_bytes=VMEM_LIMIT)


def _silu(x):
    return x * jax.nn.sigmoid(x)


ROW_TILE = SUBLANES
DMA_QUEUES = 2


def _store_token_tiles(ref, x, base=0):
    rows, d = x.shape
    assert d == ROW_TILE * LANES
    for c in range(ROW_TILE):
        ref[pl.ds(base + c, rows, stride=ROW_TILE), :] = x[:, c * LANES:(c + 1) * LANES]


def _load_token_tiles(ref, base, rows):
    return jnp.concatenate([ref[pl.ds(base + c, rows, stride=ROW_TILE), :] for c in range(ROW_TILE)], axis=-1)


def _inproj_body(x_ref, nw_ref, w_ref, qw_ref, kw_ref, cos_ref, sin_ref, bias_ref,
                 o_ref, g_ref, qo_ref, ko_ref, vo_ref):
    x = x_ref[...]
    h = x * lax.rsqrt(jnp.mean(x * x, axis=-1, keepdims=True) + EPS) * nw_ref[...]
    acc = jnp.dot(h.astype(BF16), w_ref[...], preferred_element_type=F32)
    o_ref[...] = acc[:, :COL_QB]
    g_ref[...] = acc[:, COL_GATE:]
    _attn_heads(acc[:, COL_QB:COL_KB], acc[:, COL_KB:COL_VB], acc[:, COL_VB:COL_GATE], qw_ref[...], kw_ref[...],
                cos_ref[...], sin_ref[...], bias_ref[...], qo_ref, ko_ref, vo_ref)


def _inproj(x2, norm_w, w_all, qw, kw, cosf, sinf, bias, B, S, tm):
    T, D = x2.shape
    nr = S // tm
    const = lambda i: (0, 0)
    heads = lambda i: (i // nr, 0, i % nr, 0)
    return pl.pallas_call(
        _inproj_body,
        grid=(T // tm,),
        in_specs=[
            pl.BlockSpec((tm, D), lambda i: (i, 0)),
            pl.BlockSpec((1, D), const),
            pl.BlockSpec((D, D_PROJ), const),
            pl.BlockSpec((1, LANES), const),
            pl.BlockSpec((1, LANES), const),
            pl.BlockSpec((tm, LANES), lambda i: (i % nr, 0)),
            pl.BlockSpec((tm, LANES), lambda i: (i % nr, 0)),
            pl.BlockSpec((1, 1), const),
        ],
        out_specs=[
            pl.BlockSpec((tm, COL_QB), lambda i: (i, 0)),
            pl.BlockSpec((tm, LANES), lambda i: (i, 0)),
            pl.BlockSpec((None, ATT_HEADS, tm, ATT_DP), heads),
            pl.BlockSpec((None, ATT_KV_HEADS, tm, ATT_DP), heads),
            pl.BlockSpec((None, ATT_KV_HEADS, tm, ATT_DP), heads),
        ],
        out_shape=[
            jax.ShapeDtypeStruct((T, COL_QB), F32),
            jax.ShapeDtypeStruct((T, LANES), F32),
            jax.ShapeDtypeStruct((B, ATT_HEADS, S, ATT_DP), BF16),
            jax.ShapeDtypeStruct((B, ATT_KV_HEADS, S, ATT_DP), BF16),
            jax.ShapeDtypeStruct((B, ATT_KV_HEADS, S, ATT_DP), BF16),
        ],
        compiler_params=_cparams(("parallel",)),
        name="inproj",
    )(x2, norm_w.reshape(1, D), w_all, qw, kw, cosf, sinf, bias)


def _gdn_prep_body(cur_ref, prev_ref, next_ref, cw_ref, gin_ref, gp_ref,
                   q_ref, k_ref, v_ref, g_ref, gt_ref, ext_ref, *, tr):
    i = pl.program_id(1)
    nr = pl.num_programs(1)
    halo = prev_ref.shape[0]
    pad = CONV_W // 2
    ext_ref[0:halo, :] = jnp.where(i > 0, prev_ref[...], 0.0)
    ext_ref[halo:halo + tr, :] = cur_ref[...]
    ext_ref[halo + tr:2 * halo + tr, :] = jnp.where(i < nr - 1, next_ref[...], 0.0)
    acc = cw_ref[0:1, :] * ext_ref[pl.ds(halo - pad, tr), :]
    for j in range(1, CONV_W):
        acc = acc + cw_ref[j:j + 1, :] * ext_ref[pl.ds(halo - pad + j, tr), :]
    y = _silu(acc)
    for h in range(GDN_HEADS):
        for base, ref, scale in ((0, q_ref, GDN_D ** -0.5), (GDN_W, k_ref, 1.0)):
            t = y[:, base + h * GDN_D: base + (h + 1) * GDN_D]
            t = t * (lax.rsqrt(jnp.sum(t * t, axis=-1, keepdims=True) + EPS) * scale)
            ref[:, h * GDN_D:(h + 1) * GDN_D] = t
    v_ref[...] = y[:, 2 * GDN_W:]
    gin = gin_ref[...]
    lane = lax.broadcasted_iota(jnp.int32, gin.shape, 1)
    a = gin + gp_ref[1:2, :]
    softplus = jnp.maximum(a, 0.0) + jnp.log1p(jnp.exp(-jnp.abs(a)))
    g = jnp.where(lane < GDN_CHAINS, jax.nn.sigmoid(gin), -gp_ref[0:1, :] * softplus)
    g_ref[...] = g
    gt_ref[...] = g.T[0:2 * GDN_CHAINS, :]


def _gdn_prep(proj, gate_logits, conv_w8, gate_params, B, S, tr):
    T = B * S
    nr = S // tr
    C = 3 * GDN_W
    halo = SUBLANES
    rb = tr // halo
    n_halo = T // halo
    kern = functools.partial(_gdn_prep_body, tr=tr)
    out_sd = jax.ShapeDtypeStruct((T, GDN_W), F32)
    return pl.pallas_call(
        kern,
        grid=(B, nr),
        in_specs=[
            pl.BlockSpec((tr, C), lambda b, i: (b * nr + i, 0)),
            pl.BlockSpec((halo, C), lambda b, i: (jnp.maximum((b * nr + i) * rb - 1, 0), 0)),
            pl.BlockSpec((halo, C), lambda b, i: (jnp.minimum((b * nr + i + 1) * rb, n_halo - 1), 0)),
            pl.BlockSpec((SUBLANES, C), lambda b, i: (0, 0)),
            pl.BlockSpec((tr, LANES), lambda b, i: (b * nr + i, 0)),
            pl.BlockSpec((SUBLANES, LANES), lambda b, i: (0, 0)),
        ],
        out_specs=[
            pl.BlockSpec((tr, GDN_W), lambda b, i: (b * nr + i, 0)),
            pl.BlockSpec((tr, GDN_W), lambda b, i: (b * nr + i, 0)),
            pl.BlockSpec((tr, GDN_W), lambda b, i: (b * nr + i, 0)),
            pl.BlockSpec((tr, LANES), lambda b, i: (b * nr + i, 0)),
            pl.BlockSpec((2 * GDN_CHAINS, tr), lambda b, i: (0, b * nr + i)),
        ],
        out_shape=[out_sd, out_sd, out_sd, jax.ShapeDtypeStruct((T, LANES), F32),
                   jax.ShapeDtypeStruct((2 * GDN_CHAINS, T), F32)],
        scratch_shapes=[pltpu.VMEM((tr + 2 * halo, C), F32)],
        compiler_params=_cparams(("parallel", "parallel")),
        name="gdn_prep",
    )(proj, proj, proj, conv_w8, gate_logits, gate_params)


def _bdot(a, b):
    return jnp.dot(a.astype(BF16), b.astype(BF16), preferred_element_type=F32)


def _bdot_nt(a, b):
    return lax.dot_general(a.astype(BF16), b.astype(BF16), (((1,), (1,)), ((), ())),
                           preferred_element_type=F32)


def _gdn_chunk_body(q_ref, k_ref, v_ref, g_ref, gt_ref, u_ref, wq_ref, ak_ref, dec_ref, *, cb):
    C = CHUNK
    row = lax.broadcasted_iota(jnp.int32, (C, C), 0)
    col = lax.broadcasted_iota(jnp.int32, (C, C), 1)
    eye = (row == col).astype(F32)
    masks = ((row >= col, row > col), (row <= col, row < col))
    hi = lax.Precision.HIGHEST
    chains = []
    gate_rows = gt_ref[...]
    for c in range(cb):
        rs = slice(c * C, (c + 1) * C)
        gates = g_ref[rs, :]
        gates_t = gate_rows[:, rs]
        g_tot = jnp.sum(gates, axis=0, keepdims=True)
        tot_rows = jnp.sum(gates_t, axis=1, keepdims=True)
        dec_ref[c] = jnp.broadcast_to(jnp.exp(tot_rows[GDN_CHAINS:2 * GDN_CHAINS]), (GDN_CHAINS, LANES))
        for d in range(2):
            incl, strict = masks[d]
            cum = incl.astype(F32)
            gc_cols = jnp.dot(cum, gates, precision=hi, preferred_element_type=F32)
            gc_rows = lax.dot_general(gates_t, cum, (((1,), (1,)), ((), ())), precision=hi,
                                      preferred_element_type=F32)
            for h in range(GDN_HEADS):
                lane_b = d * GDN_HEADS + h
                lane_g = GDN_CHAINS + lane_b
                sl = slice(h * GDN_D, (h + 1) * GDN_D)
                q = q_ref[rs, sl]
                k = k_ref[rs, sl]
                v = v_ref[rs, sl]
                beta = gates[:, lane_b:lane_b + 1]
                gc_col = gc_cols[:, lane_g:lane_g + 1]
                gc_row = gc_rows[lane_g:lane_g + 1, :]
                g_last = g_tot[:, lane_g:lane_g + 1]
                decay = jnp.where(incl, jnp.exp(jnp.where(incl, gc_col - gc_row, 0.0)), 0.0)
                e_col = jnp.exp(gc_col)
                k_beta = k * beta
                kk = _bdot_nt(jnp.concatenate([k_beta, q], axis=0), k)
                neg_l = jnp.where(strict, -kk[:C] * decay, 0.0)
                attn = jnp.where(incl, kk[C:] * decay, 0.0)
                rhs = jnp.concatenate([v * beta, k_beta * e_col], axis=1).astype(BF16)
                wq_ref[d, c, C:2 * C, sl] = (q * e_col).astype(BF16)
                ak_ref[d, c, h, 0:C, :] = attn.astype(BF16)
                ak_ref[d, c, h, C:C + GDN_D, :] = (k * jnp.exp(g_last - gc_col)).T.astype(BF16)
                chains.append((d, c, rs, sl, neg_l, rhs))
    zs = [jnp.concatenate([ch[4], eye], axis=1) for ch in chains]
    keep_s = lax.broadcasted_iota(jnp.int32, (C, 2 * C), 1) >= C
    for _ in range(int(math.log2(C))):
        zs = [_bdot(z[:, :C], z) + jnp.where(keep_s, z, 0.0) for z in zs]
    for (d, c, rs, sl, _, rhs), z in zip(chains, zs):
        sol = jnp.dot(z[:, C:].astype(BF16), rhs, preferred_element_type=F32)
        u_ref[d, rs, sl] = sol[:, :GDN_D]
        wq_ref[d, c, 0:C, sl] = sol[:, GDN_D:].astype(BF16)


def _gdn_chunk(q, k, v, gates, gates_t, cb):
    T = q.shape[0]
    nc = T // CHUNK
    rows = cb * CHUNK
    kern = functools.partial(_gdn_chunk_body, cb=cb)
    wide = pl.BlockSpec((rows, GDN_W), lambda i: (i, 0))
    return pl.pallas_call(
        kern,
        grid=(nc // cb,),
        in_specs=[wide, wide, wide, pl.BlockSpec((rows, LANES), lambda i: (i, 0)),
                  pl.BlockSpec((2 * GDN_CHAINS, rows), lambda i: (0, i))],
        out_specs=[
            pl.BlockSpec((2, rows, GDN_W), lambda i: (0, i, 0)),
            pl.BlockSpec((2, cb, 2 * CHUNK, GDN_W), lambda i: (0, i, 0, 0)),
            pl.BlockSpec((2, cb, GDN_HEADS, CHUNK + GDN_D, CHUNK), lambda i: (0, i, 0, 0, 0)),
            pl.BlockSpec((cb, GDN_CHAINS, LANES), lambda i: (i, 0, 0)),
        ],
        out_shape=[
            jax.ShapeDtypeStruct((2, T, GDN_W), F32),
            jax.ShapeDtypeStruct((2, nc, 2 * CHUNK, GDN_W), BF16),
            jax.ShapeDtypeStruct((2, nc, GDN_HEADS, CHUNK + GDN_D, CHUNK), BF16),
            jax.ShapeDtypeStruct((nc, GDN_CHAINS, LANES), F32),
        ],
        compiler_params=_cparams(("parallel",)),
        name="gdn_chunk",
    )(q, k, v, gates, gates_t)


def _gdn_scan_body(uf_ref, wqf_ref, akf_ref, decf_ref, ub_ref, wqb_ref, akb_ref, decb_ref,
                   of_ref, ob_ref, state_ref, *, cs):
    @pl.when(pl.program_id(1) == 0)
    def _():
        state_ref[...] = jnp.zeros_like(state_ref)

    C = CHUNK
    refs = ((uf_ref, wqf_ref, akf_ref, decf_ref, of_ref), (ub_ref, wqb_ref, akb_ref, decb_ref, ob_ref))
    chains = [(d, h) for d in range(2) for h in range(GDN_HEADS)]
    st = [state_ref[j] for j in range(GDN_CHAINS)]
    for step in range(cs):
        cidx = (step, cs - 1 - step)
        ws = [jnp.dot(refs[d][1][cidx[d], :, h * GDN_D:(h + 1) * GDN_D], st[j].astype(BF16),
                      preferred_element_type=F32) for j, (d, h) in enumerate(chains)]
        vn = [(refs[d][0][cidx[d] * C:(cidx[d] + 1) * C, h * GDN_D:(h + 1) * GDN_D] - ws[j][:C]).astype(BF16)
              for j, (d, h) in enumerate(chains)]
        rr = [jnp.dot(refs[d][2][cidx[d], h], vn[j], preferred_element_type=F32)
              for j, (d, h) in enumerate(chains)]
        for j, (d, h) in enumerate(chains):
            c = cidx[d]
            refs[d][4][c * C:(c + 1) * C, h * GDN_D:(h + 1) * GDN_D] = ws[j][C:] + rr[j][:C]
            st[j] = st[j] * refs[d][3][c, j:j + 1, :] + rr[j][C:]
    for j in range(GDN_CHAINS):
        state_ref[j] = st[j]


def _gdn_scan(u, wq, ak, dec, B, S, cs):
    T = B * S
    nb = S // (CHUNK * cs)
    rows = cs * CHUNK
    kern = functools.partial(_gdn_scan_body, cs=cs)
    fwd = lambda b, i: b * nb + i
    bwd = lambda b, i: b * nb + nb - 1 - i

    def specs(d, pos):
        return [
            pl.BlockSpec((None, rows, GDN_W), lambda b, i: (d, pos(b, i), 0)),
            pl.BlockSpec((None, cs, 2 * CHUNK, GDN_W), lambda b, i: (d, pos(b, i), 0, 0)),
            pl.BlockSpec((None, cs, GDN_HEADS, CHUNK + GDN_D, CHUNK), lambda b, i: (d, pos(b, i), 0, 0, 0)),
            pl.BlockSpec((cs, GDN_CHAINS, LANES), lambda b, i: (pos(b, i), 0, 0)),
        ]

    out_sd = jax.ShapeDtypeStruct((T, GDN_W), F32)
    return pl.pallas_call(
        kern,
        grid=(B, nb),
        in_specs=specs(0, fwd) + specs(1, bwd),
        out_specs=[pl.BlockSpec((rows, GDN_W), lambda b, i: (fwd(b, i), 0)),
                   pl.BlockSpec((rows, GDN_W), lambda b, i: (bwd(b, i), 0))],
        out_shape=[out_sd, out_sd],
        scratch_shapes=[pltpu.VMEM((GDN_CHAINS, GDN_D, GDN_D), F32)],
        compiler_params=_cparams(("parallel", "arbitrary")),
        name="gdn_scan",
    )(u, wq, ak, dec, u, wq, ak, dec)


def _attn_heads(q, k, v, qw, kw, cos, sin, bias, qo_ref, ko_ref, vo_ref):
    rows = cos.shape[0]
    lane = lax.broadcasted_iota(jnp.int32, (rows, LANES), 1)
    even_lane = lane % 2 == 0
    first_head = lane < ATT_D
    extra_lane = lane == ATT_D
    mi = lax.broadcasted_iota(jnp.int32, (LANES, LANES), 0) // ATT_D
    mj = lax.broadcasted_iota(jnp.int32, (LANES, LANES), 1) // ATT_D
    head_mean = jnp.where(mi == mj, 1.0 / ATT_D, 0.0).astype(BF16)

    def norm_rope(x, w):
        sq = x * x
        hi = sq.astype(BF16)
        lo = (sq - hi.astype(F32)).astype(BF16)
        ms = (jnp.dot(hi, head_mean, preferred_element_type=F32)
              + jnp.dot(lo, head_mean, preferred_element_type=F32))
        y = x * lax.rsqrt(ms + EPS) * w
        partner = jnp.where(even_lane, pltpu.roll(y, LANES - 1, 1), pltpu.roll(y, 1, 1))
        return y * cos + partner * sin

    def split_heads(r, extra):
        tail = jnp.where(extra_lane, extra, 0.0)
        return (jnp.where(first_head, r, tail).astype(BF16),
                jnp.where(first_head, pltpu.roll(r, ATT_D, 1), tail).astype(BF16))

    scale = LOG2E * ATT_D ** -0.5
    for c in range(ATT_HEADS // 2):
        r = norm_rope(q[:, c * LANES:(c + 1) * LANES], qw) * scale
        qo_ref[2 * c], qo_ref[2 * c + 1] = split_heads(r, bias)
    for c in range(ATT_KV_HEADS // 2):
        r = norm_rope(k[:, c * LANES:(c + 1) * LANES], kw)
        ko_ref[2 * c], ko_ref[2 * c + 1] = split_heads(r, 1.0)
        vo_ref[2 * c], vo_ref[2 * c + 1] = split_heads(v[:, c * LANES:(c + 1) * LANES], 1.0)


def _attn_body(safe_ref, q_ref, k_ref, v_ref, o_ref, m_ref, acc_ref, *, tq, tk):
    safe = safe_ref[0] != 0
    nk = k_ref.shape[0] // tk
    q = q_ref[...].reshape(ATT_GROUP * tq, ATT_DP)
    acc_ref[...] = jnp.zeros_like(acc_ref)

    def scores(j):
        keys = pl.ds(pl.multiple_of(j * tk, tk), tk)
        s = lax.dot_general(q, k_ref[keys, :], (((1,), (1,)), ((), ())), preferred_element_type=F32)
        return s, v_ref[keys, :]

    @pl.when(safe)
    def _():
        def step(j, carry):
            s, v = scores(j)
            acc_ref[...] += jnp.dot(jnp.exp2(s).astype(BF16), v, preferred_element_type=F32)
            return carry
        lax.fori_loop(0, nk, step, 0)

    @pl.when(jnp.logical_not(safe))
    def _():
        m_ref[...] = jnp.full_like(m_ref, -jnp.inf)

        def step(j, carry):
            s, v = scores(j)
            m_prev = m_ref[...]
            m_new = jnp.maximum(m_prev, jnp.max(s, axis=-1, keepdims=True))
            p = jnp.exp2(s - m_new).astype(BF16)
            acc_ref[...] = jnp.exp2(m_prev - m_new) * acc_ref[...] + jnp.dot(p, v, preferred_element_type=F32)
            m_ref[...] = m_new
            return carry
        lax.fori_loop(0, nk, step, 0)

    acc = acc_ref[...]
    o = acc[:, :ATT_D] / acc[:, ATT_D:ATT_D + 1]
    for h in range(ATT_GROUP):
        o_ref[:, h * ATT_D:(h + 1) * ATT_D] = o[h * tq:(h + 1) * tq].astype(o_ref.dtype)


def _attention(safe, q, k, v, B, S, tq, tk):
    kern = functools.partial(_attn_body, tq=tq, tk=tk)
    gw = ATT_GROUP * ATT_D
    return pl.pallas_call(
        kern,
        grid_spec=pltpu.PrefetchScalarGridSpec(
            num_scalar_prefetch=1,
            grid=(B, ATT_KV_HEADS, S // tq),
            in_specs=[
                pl.BlockSpec((None, ATT_GROUP, tq, ATT_DP), lambda b, g, i, s: (b, g, i, 0)),
                pl.BlockSpec((None, None, S, ATT_DP), lambda b, g, i, s: (b, g, 0, 0)),
                pl.BlockSpec((None, None, S, ATT_DP), lambda b, g, i, s: (b, g, 0, 0)),
            ],
            out_specs=pl.BlockSpec((None, tq, gw), lambda b, g, i, s: (b, i, g)),
            scratch_shapes=[
                pltpu.VMEM((ATT_GROUP * tq, 1), F32),
                pltpu.VMEM((ATT_GROUP * tq, ATT_DP), F32),
            ],
        ),
        out_shape=jax.ShapeDtypeStruct((B, S, ATT_Q), BF16),
        compiler_params=_cparams(("parallel", "parallel", "parallel")),
        name="attention",
    )(safe, q, k, v)


def _outproj_body(of_ref, ob_ref, z_ref, att_ref, x_ref, gnw_ref, wa_ref, wb_ref,
                  fnw_ref, wr_ref, triu_ref, xm_ref, h_ref, rt_ref, c_ref, count_ref):
    o = of_ref[...] + ob_ref[...]
    z = z_ref[...]
    parts = []
    for h in range(GDN_HEADS):
        sl = slice(h * GDN_D, (h + 1) * GDN_D)
        t = o[:, sl]
        t = t * lax.rsqrt(jnp.mean(t * t, axis=-1, keepdims=True) + EPS) * gnw_ref[...]
        parts.append((t * _silu(z[:, sl])).astype(BF16))
    mix_a = jnp.concatenate(parts, axis=-1)
    xm = x_ref[...] + jnp.dot(mix_a, wa_ref[...], preferred_element_type=F32)
    xm = xm + jnp.dot(att_ref[...], wb_ref[...], preferred_element_type=F32)
    xm_ref[...] = xm
    hn = xm * lax.rsqrt(jnp.mean(xm * xm, axis=-1, keepdims=True) + EPS) * fnw_ref[...]
    _store_token_tiles(h_ref, hn)
    hn_hi = hn.astype(BF16)
    hn_lo = (hn - hn_hi.astype(F32)).astype(BF16)
    both = jnp.dot(hn_hi, wr_ref[...], preferred_element_type=F32)
    logits = (both[:, :LANES] + both[:, LANES:]
              + jnp.dot(hn_lo, wr_ref[:, :LANES], preferred_element_type=F32))

    lt = logits.T[0:ROUTER_ROWS, :]
    tok = lt.shape[1]
    slot = lax.broadcasted_iota(jnp.int32, (ROUTER_ROWS, tok), 0)
    big = jnp.int32(ROUTER_ROWS)
    neg = -jnp.inf

    def masked_top(vals, mask):
        m = jnp.max(jnp.where(mask, vals, neg), axis=0, keepdims=True)
        idx = jnp.min(jnp.where(mask & (vals == m), slot, big), axis=0, keepdims=True)
        return m, idx

    gmask = slot < N_GROUPS
    gmax, gsel = masked_top(lt, gmask)
    gp_top = 1.0 / jnp.sum(jnp.where(gmask, jnp.exp(lt - gmax), 0.0), axis=0, keepdims=True)
    lo = N_GROUPS + gsel * EXPERTS_PER_GROUP
    emask = (slot >= lo) & (slot < lo + EXPERTS_PER_GROUP)
    m1, i1 = masked_top(lt, emask)
    ex = jnp.where(emask, jnp.exp(lt - m1), 0.0)
    pf = ex / jnp.sum(ex, axis=0, keepdims=True)
    p1, _ = masked_top(pf, emask)
    p2, i2 = masked_top(pf, emask & (slot != i1))
    denom = p1 + p2
    g1 = gp_top * p1 / denom
    g2 = gp_top * p2 / denom
    e1 = (i1 - N_GROUPS).astype(F32)
    e2 = (i2 - N_GROUPS).astype(F32)

    @pl.when(pl.program_id(0) == 0)
    def _():
        count_ref[...] = jnp.zeros_like(count_ref)

    hit1 = slot == i1
    hit2 = slot == i2
    picked = jnp.where(hit1 | hit2, 1.0, 0.0)
    seen = jnp.dot(picked.astype(BF16), triu_ref[...], preferred_element_type=F32)
    counted = count_ref[:, 0:1]
    before = counted + seen - picked
    rank1 = jnp.sum(jnp.where(hit1, before, 0.0), axis=0, keepdims=True)
    rank2 = jnp.sum(jnp.where(hit2, before, 0.0), axis=0, keepdims=True)
    total = jnp.broadcast_to(counted + seen[:, tok - 1:tok], count_ref.shape)
    count_ref[...] = total
    c_ref[...] = total
    rt_ref[...] = jnp.concatenate([e1, e2, g1, g2, rank1, rank2, jnp.zeros((SUBLANES - 6, tok), F32)], axis=0)


def _outproj(o_f, o_b, proj, att, x2, gnw, w_a, w_b, fnw, w_r, tm):
    T, D = x2.shape
    row = lambda i: (i, 0)
    const = lambda i: (0, 0)
    return pl.pallas_call(
        _outproj_body,
        grid=(T // tm,),
        in_specs=[
            pl.BlockSpec((tm, GDN_W), row),
            pl.BlockSpec((tm, GDN_W), row),
            pl.BlockSpec((tm, GDN_W), lambda i: (i, COL_Z // GDN_W)),
            pl.BlockSpec((tm, ATT_Q), row),
            pl.BlockSpec((tm, D), row),
            pl.BlockSpec((1, GDN_D), const),
            pl.BlockSpec((GDN_W, D), const),
            pl.BlockSpec((ATT_Q, D), lambda i: (GDN_W // ATT_Q, 0)),
            pl.BlockSpec((1, D), const),
            pl.BlockSpec((D, 2 * LANES), const),
            pl.BlockSpec((tm, tm), const),
        ],
        out_specs=[pl.BlockSpec((tm, D), row), pl.BlockSpec((tm * ROW_TILE, LANES), row),
                   pl.BlockSpec((SUBLANES, tm), lambda i: (0, i)),
                   pl.BlockSpec((ROUTER_ROWS, LANES), const)],
        out_shape=[jax.ShapeDtypeStruct((T, D), F32), jax.ShapeDtypeStruct((T * ROW_TILE, LANES), F32),
                   jax.ShapeDtypeStruct((SUBLANES, T), F32),
                   jax.ShapeDtypeStruct((ROUTER_ROWS, LANES), F32)],
        scratch_shapes=[pltpu.VMEM((ROUTER_ROWS, LANES), F32)],
        compiler_params=_cparams(("arbitrary",)),
        name="outproj_router",
    )(o_f, o_b, proj, att, x2, gnw, w_a, w_b, fnw, w_r, jnp.asarray(np.tri(tm, dtype=np.float32).T, BF16))


def _gather_start(src_hbm, dst_buf, sem, idx_ref, base, slot, n):
    for r in range(n):
        src = pl.multiple_of(idx_ref[base + r] * ROW_TILE, ROW_TILE)
        dst = pl.multiple_of((slot * n + r) * ROW_TILE, ROW_TILE)
        pltpu.make_async_copy(src_hbm.at[pl.ds(src, ROW_TILE)], dst_buf.at[pl.ds(dst, ROW_TILE)],
                              sem.at[slot]).start(priority=r % DMA_QUEUES)


def _gather_wait(src_hbm, dst_buf, sem, slot, n):
    dst = pl.multiple_of(slot * n * ROW_TILE, ROW_TILE)
    pltpu.make_async_copy(src_hbm.at[pl.ds(0, n * ROW_TILE)], dst_buf.at[pl.ds(dst, n * ROW_TILE)],
                          sem.at[slot]).wait()


def _experts_body(be_ref, src_ref, nused_ref,
                  h_hbm, wg_hbm, wu_hbm, wd_hbm, y_ref, xbuf, wg_buf, wu_buf, wd_buf, ws_ref, sem, wsem, *, bm):
    i = pl.program_id(0)
    n_blocks = pl.num_programs(0)
    n_used = nused_ref[0]
    slot = i % 2
    expert = be_ref[i]
    first = (i == 0) | (expert != be_ref[jnp.maximum(i - 1, 0)])

    def weight_copies(e, s):
        return [pltpu.make_async_copy(hbm.at[e], buf.at[s], wsem.at[s])
                for hbm, buf in ((wg_hbm, wg_buf), (wu_hbm, wu_buf), (wd_hbm, wd_buf))]

    def compute(prefetch_next):
        @pl.when(first)
        def _():
            cur = 1 - ws_ref[0]
            ws_ref[0] = cur
            for c in weight_copies(0, cur):
                c.wait()
            nxt = lax.while_loop(lambda j: (j < n_used) & (be_ref[jnp.minimum(j, n_blocks - 1)] == expert),
                                 lambda j: j + 1, i + 1)

            @pl.when(nxt < n_used)
            def _():
                for c in weight_copies(be_ref[jnp.minimum(nxt, n_blocks - 1)], 1 - cur):
                    c.start()

        ws = ws_ref[0]

        _gather_wait(h_hbm, xbuf, sem, slot, bm)
        x = _load_token_tiles(xbuf, slot * (bm * ROW_TILE), bm).astype(BF16)
        if prefetch_next:
            _gather_start(h_hbm, xbuf, sem, src_ref, (i + 1) * bm, 1 - slot, bm)
        gate = jnp.dot(x, wg_buf[ws].astype(BF16), preferred_element_type=F32)
        up = jnp.dot(x, wu_buf[ws].astype(BF16), preferred_element_type=F32)
        hid = (_silu(gate) * up).astype(BF16)
        _store_token_tiles(y_ref, jnp.dot(hid, wd_buf[ws].astype(BF16), preferred_element_type=F32))

    @pl.when((i == 0) & (n_used > 0))
    def _():
        ws_ref[0] = 1
        for c in weight_copies(be_ref[0], 0):
            c.start()
        _gather_start(h_hbm, xbuf, sem, src_ref, 0, 0, bm)

    @pl.when(i + 1 < n_used)
    def _():
        compute(True)

    @pl.when(i + 1 == n_used)
    def _():
        compute(False)

    @pl.when(i >= n_used)
    def _():
        y_ref[...] = jnp.zeros_like(y_ref)


def _experts(block_expert, src_tok, n_used, h2, w_gate, w_up, w_down, bm):
    D = ROW_TILE * LANES
    P = src_tok.shape[0]
    n_blocks = P // bm
    FF = w_gate.shape[-1]
    kern = functools.partial(_experts_body, bm=bm)
    hbm = pl.BlockSpec(memory_space=pl.ANY)
    return pl.pallas_call(
        kern,
        grid_spec=pltpu.PrefetchScalarGridSpec(
            num_scalar_prefetch=3,
            grid=(n_blocks,),
            in_specs=[hbm, hbm, hbm, hbm],
            out_specs=pl.BlockSpec((bm * ROW_TILE, LANES), lambda i, *_: (i, 0)),
            scratch_shapes=[
                pltpu.VMEM((2 * bm * ROW_TILE, LANES), F32),
                pltpu.VMEM((2, D, FF), F32), pltpu.VMEM((2, D, FF), F32), pltpu.VMEM((2, FF, D), F32),
                pltpu.SMEM((1,), jnp.int32),
                pltpu.SemaphoreType.DMA((2,)), pltpu.SemaphoreType.DMA((2,)),
            ],
        ),
        out_shape=jax.ShapeDtypeStruct((P * ROW_TILE, LANES), F32),
        compiler_params=_cparams(("arbitrary",)),
        name="moe_experts",
    )(block_expert, src_tok, n_used, h2, w_gate, w_up, w_down)


def _combine_body(dest_ref, y_hbm, xm_ref, r_ref, fw_ref, o_ref, ybuf, sem, *, tc):
    i = pl.program_id(0)
    n = pl.num_programs(0)
    slot = i % 2
    rows = TOP_K * tc

    def compute(prefetch_next):
        _gather_wait(y_hbm, ybuf, sem, slot, rows)
        route = r_ref[...].T
        y0 = _load_token_tiles(ybuf, slot * (rows * ROW_TILE), tc)
        y1 = _load_token_tiles(ybuf, (slot * rows + tc) * ROW_TILE, tc)
        if prefetch_next:
            _gather_start(y_hbm, ybuf, sem, dest_ref, (i + 1) * rows, 1 - slot, rows)
        xo = xm_ref[...] + route[:, 2:3] * y0 + route[:, 3:4] * y1
        o_ref[...] = xo * lax.rsqrt(jnp.mean(xo * xo, axis=-1, keepdims=True) + EPS) * fw_ref[...]

    @pl.when(i == 0)
    def _():
        _gather_start(y_hbm, ybuf, sem, dest_ref, 0, 0, rows)

    @pl.when(i + 1 < n)
    def _():
        compute(True)

    @pl.when(i + 1 == n)
    def _():
        compute(False)


def _combine(dest_blocked, y_buf, x_mid, route, final_w, tc):
    T, D = x_mid.shape
    kern = functools.partial(_combine_body, tc=tc)
    return pl.pallas_call(
        kern,
        grid_spec=pltpu.PrefetchScalarGridSpec(
            num_scalar_prefetch=1,
            grid=(T // tc,),
            in_specs=[
                pl.BlockSpec(memory_space=pl.ANY),
                pl.BlockSpec((tc, D), lambda i, d: (i, 0)),
                pl.BlockSpec((SUBLANES, tc), lambda i, d: (0, i)),
                pl.BlockSpec((1, D), lambda i, d: (0, 0)),
            ],
            out_specs=pl.BlockSpec((tc, D), lambda i, d: (i, 0)),
            scratch_shapes=[pltpu.VMEM((2 * TOP_K * tc * ROW_TILE, LANES), F32), pltpu.SemaphoreType.DMA((2,))],
        ),
        out_shape=jax.ShapeDtypeStruct((T, D), F32),
        compiler_params=_cparams(("arbitrary",)),
        name="moe_combine",
    )(dest_blocked, y_buf, x_mid, route, final_w.reshape(1, D))


def _layer(x2, B, S, norm_mix_w, w_in, conv_w, a_log, dt_bias, gdn_norm_w, q_norm_w, k_norm_w, w_out,
           norm_ffn_w, w_router_group, w_router_expert, w_gate, w_up, w_down, final_w):
    T, D = x2.shape
    tl = _tiles(B, S)
    o_gate = 4 * GDN_W
    o_qb = o_gate + 2 * GDN_CHAINS
    w_all = jnp.concatenate([w_in[:, :o_gate], w_in[:, o_qb:], w_in[:, o_gate:o_qb],
                             jnp.zeros((D, LANES - 2 * GDN_CHAINS), w_in.dtype)], axis=1).astype(BF16)

    rows = S // GRID_W
    rowp = np.repeat(np.arange(rows), GRID_W).astype(np.float64)
    colp = np.tile(np.arange(GRID_W), rows).astype(np.float64)
    axis_dims = ATT_D // 2
    inv_freq = ROPE_THETA ** (-np.arange(0, axis_dims, 2, dtype=np.float64) / axis_dims)
    ang = np.concatenate([rowp[:, None] * inv_freq, colp[:, None] * inv_freq], axis=-1)
    pair_sign = np.tile(np.array([-1.0, 1.0]), axis_dims)
    cosf = jnp.asarray(np.tile(np.repeat(np.cos(ang), 2, axis=1), (1, LANES // ATT_D)), F32)
    sinf = jnp.asarray(np.tile(np.repeat(np.sin(ang), 2, axis=1) * pair_sign, (1, LANES // ATT_D)), F32)
    q_gain = jnp.max(jnp.abs(q_norm_w)).astype(F32)
    k_gain = jnp.max(jnp.abs(k_norm_w)).astype(F32)
    score_bound = ATT_D ** 0.5 * q_gain * k_gain
    safe = (2.0 * score_bound <= SOFTMAX_SAFE_SPAN).astype(jnp.int32).reshape(1)
    bias = (-LOG2E * score_bound).reshape(1, 1)
    pair = lambda w: jnp.tile(w, LANES // ATT_D).reshape(1, LANES)

    proj, gate_logits, qh, kh, vh = _inproj(x2, norm_mix_w, w_all, pair(q_norm_w), pair(k_norm_w),
                                            cosf, sinf, bias, B, S, tl.proj_rows)

    conv_w8 = jnp.concatenate([conv_w, jnp.zeros((SUBLANES - CONV_W, conv_w.shape[1]), F32)], axis=0)
    gp = jnp.zeros((SUBLANES, LANES), F32)
    gp = gp.at[0, GDN_CHAINS:2 * GDN_CHAINS].set(jnp.exp(a_log.astype(F32)).reshape(-1))
    gp = gp.at[1, GDN_CHAINS:2 * GDN_CHAINS].set(dt_bias.astype(F32).reshape(-1))
    q_a, k_a, v_a, gates, gates_t = _gdn_prep(proj, gate_logits, conv_w8, gp, B, S, tl.prep_rows)
    u, wq, ak, dec = _gdn_chunk(q_a, k_a, v_a, gates, gates_t, tl.gdn_chunks)
    o_f, o_b = _gdn_scan(u, wq, ak, dec, B, S, tl.scan_chunks)

    att = _attention(safe, qh, kh, vh, B, S, tl.att_q, tl.att_k).reshape(T, ATT_Q)

    w_r32 = jnp.concatenate([w_router_group, w_router_expert,
                             jnp.zeros((D, LANES - N_GROUPS - N_EXPERTS), F32)], axis=1).astype(F32)
    w_r_hi = w_r32.astype(BF16)
    w_r = jnp.concatenate([w_r_hi, (w_r32 - w_r_hi.astype(F32)).astype(BF16)], axis=1)
    w_out_bf = w_out.astype(BF16)
    x_mid, h2, route_t, count_rows = _outproj(
        o_f, o_b, proj, att, x2, gdn_norm_w.reshape(1, GDN_D), w_out_bf, w_out_bf,
        norm_ffn_w.reshape(1, D), w_r, tl.proj_rows)

    bm = tl.moe_rows
    n_assign = T * TOP_K
    n_blocks = -(-(n_assign + N_EXPERTS * (bm - 1)) // bm)
    experts = jnp.arange(N_EXPERTS, dtype=jnp.int32)
    counts = count_rows[N_GROUPS:N_GROUPS + N_EXPERTS, 0].astype(jnp.int32)
    padded = (counts + bm - 1) // bm * bm
    pad_end = jnp.cumsum(padded)
    pad_start = pad_end - padded
    block_start = jnp.arange(n_blocks, dtype=jnp.int32) * bm
    done = (pad_end[None, :] <= block_start[:, None]).astype(jnp.int32)
    begun = (pad_start[None, :] <= block_start[:, None]).astype(jnp.int32)
    block_expert = jnp.minimum(jnp.sum(done, axis=1), N_EXPERTS - 1)
    n_used = (pad_end[-1:] // bm).astype(jnp.int32)
    seg_start = jnp.sum(done * padded[None, :], axis=1)
    seg_entry = jnp.sum(done * counts[None, :], axis=1)
    seg_count = jnp.sum(begun * counts[None, :], axis=1) - seg_entry
    e_rows = route_t[0:TOP_K].astype(jnp.int32)
    rank_rows = route_t[4:4 + TOP_K].astype(jnp.int32)
    dest = jnp.sum(jnp.where(e_rows[:, :, None] == experts, pad_start, 0), axis=-1) + rank_rows
    tokens = jnp.tile(jnp.arange(T, dtype=jnp.int32), TOP_K)
    _, compact = lax.sort((dest.reshape(-1), tokens), num_keys=1)
    row = block_start[:, None] + jnp.arange(bm, dtype=jnp.int32)[None, :]
    seg_row = row - seg_start[:, None]
    holds_token = seg_row < seg_count[:, None]
    entry = jnp.clip(seg_entry[:, None] + seg_row, 0, n_assign - 1)
    src_tok = jnp.where(holds_token, compact[entry], lax.rem(row, jnp.full_like(row, T))).reshape(-1)

    y_buf = _experts(block_expert, src_tok, n_used, h2, w_gate, w_up, w_down, bm)

    tc = tl.comb_rows
    dest_blocked = dest.reshape(TOP_K, T // tc, tc).transpose(1, 0, 2).reshape(-1)
    return _combine(dest_blocked, y_buf, x_mid, route_t, final_w, tc)


def kernel(x, norm_mix_w, w_in, conv_w, a_log, dt_bias, gdn_norm_w, q_norm_w, k_norm_w, w_out, norm_ffn_w,
           w_router_group, w_router_expert, w_gate, w_up, w_down, final_norm_w):
    B, S, D = x.shape
    depth = w_in.shape[0]
    assert depth == 1, "the final norm is fused into the last (only) layer's combine step"
    out = _layer(x.reshape(B * S, D), B, S, norm_mix_w[0], w_in[0], conv_w[0], a_log[0], dt_bias[0],
                 gdn_norm_w[0], q_norm_w[0], k_norm_w[0], w_out[0], norm_ffn_w[0], w_router_group[0],
                 w_router_expert[0], w_gate[0], w_up[0], w_down[0], final_norm_w)
    return out.reshape(B, S, D)
```

```python
import functools
import math
from typing import NamedTuple

import jax
import jax.numpy as jnp
import numpy as np
from jax import lax
from jax.experimental import pallas as pl
from jax.experimental.pallas import tpu as pltpu

F32 = jnp.float32
BF16 = jnp.bfloat16
EPS = 1e-6

GRID_W = 64
GDN_HEADS = 4
GDN_D = 128
CONV_W = 5
CHUNK = 64
ATT_HEADS = 8
ATT_KV_HEADS = 2
ATT_GROUP = ATT_HEADS // ATT_KV_HEADS
ATT_D = 64
ROPE_THETA = 10000.0
N_GROUPS = 4
EXPERTS_PER_GROUP = 8
N_EXPERTS = N_GROUPS * EXPERTS_PER_GROUP
TOP_K = 2

GDN_W = GDN_HEADS * GDN_D
GDN_CHAINS = 2 * GDN_HEADS
ATT_Q = ATT_HEADS * ATT_D
ATT_KV = ATT_KV_HEADS * ATT_D
LANES = 128
SUBLANES = 8
ATT_DP = LANES
ROUTER_ROWS = -(-(N_GROUPS + N_EXPERTS) // SUBLANES) * SUBLANES

COL_Z = 3 * GDN_W
COL_QB = COL_Z + GDN_W
COL_KB = COL_QB + ATT_Q
COL_VB = COL_KB + ATT_KV
COL_GATE = COL_VB + ATT_KV
D_PROJ = COL_GATE + LANES

VMEM_LIMIT = 56 * 1024 * 1024
LOG2E = math.log2(math.e)
SOFTMAX_SAFE_SPAN = 60.0


class Tiles(NamedTuple):
    proj_rows: int
    prep_rows: int
    gdn_chunks: int
    scan_chunks: int
    att_q: int
    att_k: int
    moe_rows: int
    comb_rows: int


def _tile(n, want):
    t = min(n, want)
    assert n % t == 0, (n, want)
    return t


def _tiles(B, S):
    T = B * S
    n_chunks = S // CHUNK
    return Tiles(proj_rows=_tile(T, 512), prep_rows=_tile(S, 512), gdn_chunks=_tile(n_chunks, 8),
                 scan_chunks=_tile(n_chunks, 4), att_q=_tile(S, 512), att_k=_tile(S, 2048),
                 moe_rows=256, comb_rows=_tile(T, 256))


def _cparams(sem):
    return pltpu.CompilerParams(dimension_semantics=sem, vmem_limit_bytes=VMEM_LIMIT)


def _silu(x):
    return x * jax.nn.sigmoid(x)


ROW_TILE = SUBLANES
DMA_QUEUES = 2


def _store_token_tiles(ref, x, base=0):
    rows, d = x.shape
    assert d == ROW_TILE * LANES
    for c in range(ROW_TILE):
        ref[pl.ds(base + c, rows, stride=ROW_TILE), :] = x[:, c * LANES:(c + 1) * LANES]


def _load_token_tiles(ref, base, rows):
    return jnp.concatenate([ref[pl.ds(base + c, rows, stride=ROW_TILE), :] for c in range(ROW_TILE)], axis=-1)


def _inproj_body(x_ref, nw_ref, w_ref, qw_ref, kw_ref, cos_ref, sin_ref, bias_ref,
                 o_ref, g_ref, qo_ref, ko_ref, vo_ref):
    x = x_ref[...]
    h = x * lax.rsqrt(jnp.mean(x * x, axis=-1, keepdims=True) + EPS) * nw_ref[...]
    acc = jnp.dot(h.astype(BF16), w_ref[...], preferred_element_type=F32)
    o_ref[...] = acc[:, :COL_QB]
    g_ref[...] = acc[:, COL_GATE:]
    _attn_heads(acc[:, COL_QB:COL_KB], acc[:, COL_KB:COL_VB], acc[:, COL_VB:COL_GATE], qw_ref[...], kw_ref[...],
                cos_ref[...], sin_ref[...], bias_ref[...], qo_ref, ko_ref, vo_ref)


def _inproj(x2, norm_w, w_all, qw, kw, cosf, sinf, bias, B, S, tm):
    T, D = x2.shape
    nr = S // tm
    const = lambda i: (0, 0)
    heads = lambda i: (i // nr, 0, i % nr, 0)
    return pl.pallas_call(
        _inproj_body,
        grid=(T // tm,),
        in_specs=[
            pl.BlockSpec((tm, D), lambda i: (i, 0)),
            pl.BlockSpec((1, D), const),
            pl.BlockSpec((D, D_PROJ), const),
            pl.BlockSpec((1, LANES), const),
            pl.BlockSpec((1, LANES), const),
            pl.BlockSpec((tm, LANES), lambda i: (i % nr, 0)),
            pl.BlockSpec((tm, LANES), lambda i: (i % nr, 0)),
            pl.BlockSpec((1, 1), const),
        ],
        out_specs=[
            pl.BlockSpec((tm, COL_QB), lambda i: (i, 0)),
            pl.BlockSpec((tm, LANES), lambda i: (i, 0)),
            pl.BlockSpec((None, ATT_HEADS, tm, ATT_DP), heads),
            pl.BlockSpec((None, ATT_KV_HEADS, tm, ATT_DP), heads),
            pl.BlockSpec((None, ATT_KV_HEADS, tm, ATT_DP), heads),
        ],
        out_shape=[
            jax.ShapeDtypeStruct((T, COL_QB), F32),
            jax.ShapeDtypeStruct((T, LANES), F32),
            jax.ShapeDtypeStruct((B, ATT_HEADS, S, ATT_DP), BF16),
            jax.ShapeDtypeStruct((B, ATT_KV_HEADS, S, ATT_DP), BF16),
            jax.ShapeDtypeStruct((B, ATT_KV_HEADS, S, ATT_DP), BF16),
        ],
        compiler_params=_cparams(("parallel",)),
        name="inproj",
    )(x2, norm_w.reshape(1, D), w_all, qw, kw, cosf, sinf, bias)


def _gdn_prep_body(cur_ref, prev_ref, next_ref, cw_ref, gin_ref, gp_ref,
                   q_ref, k_ref, v_ref, g_ref, gt_ref, ext_ref, *, tr):
    i = pl.program_id(1)
    nr = pl.num_programs(1)
    halo = prev_ref.shape[0]
    pad = CONV_W // 2
    ext_ref[0:halo, :] = jnp.where(i > 0, prev_ref[...], 0.0)
    ext_ref[halo:halo + tr, :] = cur_ref[...]
    ext_ref[halo + tr:2 * halo + tr, :] = jnp.where(i < nr - 1, next_ref[...], 0.0)
    acc = cw_ref[0:1, :] * ext_ref[pl.ds(halo - pad, tr), :]
    for j in range(1, CONV_W):
        acc = acc + cw_ref[j:j + 1, :] * ext_ref[pl.ds(halo - pad + j, tr), :]
    y = _silu(acc)
    for h in range(GDN_HEADS):
        for base, ref, scale in ((0, q_ref, GDN_D ** -0.5), (GDN_W, k_ref, 1.0)):
            t = y[:, base + h * GDN_D: base + (h + 1) * GDN_D]
            t = t * (lax.rsqrt(jnp.sum(t * t, axis=-1, keepdims=True) + EPS) * scale)
            ref[:, h * GDN_D:(h + 1) * GDN_D] = t
    v_ref[...] = y[:, 2 * GDN_W:]
    gin = gin_ref[...]
    lane = lax.broadcasted_iota(jnp.int32, gin.shape, 1)
    a = gin + gp_ref[1:2, :]
    softplus = jnp.maximum(a, 0.0) + jnp.log1p(jnp.exp(-jnp.abs(a)))
    g = jnp.where(lane < GDN_CHAINS, jax.nn.sigmoid(gin), -gp_ref[0:1, :] * softplus)
    g_ref[...] = g
    gt_ref[...] = g.T[0:2 * GDN_CHAINS, :]


def _gdn_prep(proj, gate_logits, conv_w8, gate_params, B, S, tr):
    T = B * S
    nr = S // tr
    C = 3 * GDN_W
    halo = SUBLANES
    rb = tr // halo
    n_halo = T // halo
    kern = functools.partial(_gdn_prep_body, tr=tr)
    out_sd = jax.ShapeDtypeStruct((T, GDN_W), F32)
    return pl.pallas_call(
        kern,
        grid=(B, nr),
        in_specs=[
            pl.BlockSpec((tr, C), lambda b, i: (b * nr + i, 0)),
            pl.BlockSpec((halo, C), lambda b, i: (jnp.maximum((b * nr + i) * rb - 1, 0), 0)),
            pl.BlockSpec((halo, C), lambda b, i: (jnp.minimum((b * nr + i + 1) * rb, n_halo - 1), 0)),
            pl.BlockSpec((SUBLANES, C), lambda b, i: (0, 0)),
            pl.BlockSpec((tr, LANES), lambda b, i: (b * nr + i, 0)),
            pl.BlockSpec((SUBLANES, LANES), lambda b, i: (0, 0)),
        ],
        out_specs=[
            pl.BlockSpec((tr, GDN_W), lambda b, i: (b * nr + i, 0)),
            pl.BlockSpec((tr, GDN_W), lambda b, i: (b * nr + i, 0)),
            pl.BlockSpec((tr, GDN_W), lambda b, i: (b * nr + i, 0)),
            pl.BlockSpec((tr, LANES), lambda b, i: (b * nr + i, 0)),
            pl.BlockSpec((2 * GDN_CHAINS, tr), lambda b, i: (0, b * nr + i)),
        ],
        out_shape=[out_sd, out_sd, out_sd, jax.ShapeDtypeStruct((T, LANES), F32),
                   jax.ShapeDtypeStruct((2 * GDN_CHAINS, T), F32)],
        scratch_shapes=[pltpu.VMEM((tr + 2 * halo, C), F32)],
        compiler_params=_cparams(("parallel", "parallel")),
        name="gdn_prep",
    )(proj, proj, proj, conv_w8, gate_logits, gate_params)


def _bdot(a, b):
    return jnp.dot(a.astype(BF16), b.astype(BF16), preferred_element_type=F32)


def _bdot_nt(a, b):
    return lax.dot_general(a.astype(BF16), b.astype(BF16), (((1,), (1,)), ((), ())),
                           preferred_element_type=F32)


def _gdn_chunk_body(q_ref, k_ref, v_ref, g_ref, gt_ref, u_ref, wq_ref, ak_ref, dec_ref, *, cb):
    C = CHUNK
    row = lax.broadcasted_iota(jnp.int32, (C, C), 0)
    col = lax.broadcasted_iota(jnp.int32, (C, C), 1)
    eye = (row == col).astype(F32)
    masks = ((row >= col, row > col), (row <= col, row < col))
    hi = lax.Precision.HIGHEST
    chains = []
    gate_rows = gt_ref[...]
    for c in range(cb):
        rs = slice(c * C, (c + 1) * C)
        gates = g_ref[rs, :]
        gates_t = gate_rows[:, rs]
        g_tot = jnp.sum(gates, axis=0, keepdims=True)
        tot_rows = jnp.sum(gates_t, axis=1, keepdims=True)
        dec_ref[c] = jnp.broadcast_to(jnp.exp(tot_rows[GDN_CHAINS:2 * GDN_CHAINS]), (GDN_CHAINS, LANES))
        for d in range(2):
            incl, strict = masks[d]
            cum = incl.astype(F32)
            gc_cols = jnp.dot(cum, gates, precision=hi, preferred_element_type=F32)
            gc_rows = lax.dot_general(gates_t, cum, (((1,), (1,)), ((), ())), precision=hi,
                                      preferred_element_type=F32)
            for h in range(GDN_HEADS):
                lane_b = d * GDN_HEADS + h
                lane_g = GDN_CHAINS + lane_b
                sl = slice(h * GDN_D, (h + 1) * GDN_D)
                q = q_ref[rs, sl]
                k = k_ref[rs, sl]
                v = v_ref[rs, sl]
                beta = gates[:, lane_b:lane_b + 1]
                gc_col = gc_cols[:, lane_g:lane_g + 1]
                gc_row = gc_rows[lane_g:lane_g + 1, :]
                g_last = g_tot[:, lane_g:lane_g + 1]
                decay = jnp.where(incl, jnp.exp(jnp.where(incl, gc_col - gc_row, 0.0)), 0.0)
                e_col = jnp.exp(gc_col)
                k_beta = k * beta
                kk = _bdot_nt(jnp.concatenate([k_beta, q], axis=0), k)
                neg_l = jnp.where(strict, -kk[:C] * decay, 0.0)
                attn = jnp.where(incl, kk[C:] * decay, 0.0)
                rhs = jnp.concatenate([v * beta, k_beta * e_col], axis=1).astype(BF16)
                wq_ref[d, c, C:2 * C, sl] = (q * e_col).astype(BF16)
                k_tail = k * jnp.exp(g_last - gc_col)
                if h % 2 == 0:
                    even_attn, even_tail = attn, k_tail
                else:
                    ak_ref[d, c, h // 2, 0:C, :] = jnp.concatenate([even_attn, attn], axis=1).astype(BF16)
                    ak_ref[d, c, h // 2, C:C + GDN_D, :] = jnp.concatenate([even_tail, k_tail], axis=0).T.astype(BF16)
                chains.append((d, c, rs, sl, neg_l, rhs))
    zs = [jnp.concatenate([ch[4], eye], axis=1) for ch in chains]
    keep_s = lax.broadcasted_iota(jnp.int32, (C, 2 * C), 1) >= C
    for _ in range(int(math.log2(C))):
        zs = [_bdot(z[:, :C], z) + jnp.where(keep_s, z, 0.0) for z in zs]
    for (d, c, rs, sl, _, rhs), z in zip(chains, zs):
        sol = jnp.dot(z[:, C:].astype(BF16), rhs, preferred_element_type=F32)
        u_ref[d, rs, sl] = sol[:, :GDN_D]
        wq_ref[d, c, 0:C, sl] = sol[:, GDN_D:].astype(BF16)


def _gdn_chunk(q, k, v, gates, gates_t, cb):
    T = q.shape[0]
    nc = T // CHUNK
    rows = cb * CHUNK
    kern = functools.partial(_gdn_chunk_body, cb=cb)
    wide = pl.BlockSpec((rows, GDN_W), lambda i: (i, 0))
    return pl.pallas_call(
        kern,
        grid=(nc // cb,),
        in_specs=[wide, wide, wide, pl.BlockSpec((rows, LANES), lambda i: (i, 0)),
                  pl.BlockSpec((2 * GDN_CHAINS, rows), lambda i: (0, i))],
        out_specs=[
            pl.BlockSpec((2, rows, GDN_W), lambda i: (0, i, 0)),
            pl.BlockSpec((2, cb, 2 * CHUNK, GDN_W), lambda i: (0, i, 0, 0)),
            pl.BlockSpec((2, cb, GDN_HEADS // 2, CHUNK + GDN_D, 2 * CHUNK), lambda i: (0, i, 0, 0, 0)),
            pl.BlockSpec((cb, GDN_CHAINS, LANES), lambda i: (i, 0, 0)),
        ],
        out_shape=[
            jax.ShapeDtypeStruct((2, T, GDN_W), F32),
            jax.ShapeDtypeStruct((2, nc, 2 * CHUNK, GDN_W), BF16),
            jax.ShapeDtypeStruct((2, nc, GDN_HEADS // 2, CHUNK + GDN_D, 2 * CHUNK), BF16),
            jax.ShapeDtypeStruct((nc, GDN_CHAINS, LANES), F32),
        ],
        compiler_params=_cparams(("parallel",)),
        name="gdn_chunk",
    )(q, k, v, gates, gates_t)


def _gdn_scan_body(uf_ref, wqf_ref, akf_ref, decf_ref, ub_ref, wqb_ref, akb_ref, decb_ref,
                   of_ref, ob_ref, state_ref, *, cs):
    @pl.when(pl.program_id(1) == 0)
    def _():
        state_ref[...] = jnp.zeros_like(state_ref)

    C = CHUNK
    refs = ((uf_ref, wqf_ref, akf_ref, decf_ref, of_ref), (ub_ref, wqb_ref, akb_ref, decb_ref, ob_ref))
    chains = [(d, h) for d in range(2) for h in range(GDN_HEADS)]
    st = [state_ref[j] for j in range(GDN_CHAINS)]
    for step in range(cs):
        cidx = (step, cs - 1 - step)
        ws = [jnp.dot(refs[d][1][cidx[d], :, h * GDN_D:(h + 1) * GDN_D], st[j].astype(BF16),
                      preferred_element_type=F32) for j, (d, h) in enumerate(chains)]
        vn = [(refs[d][0][cidx[d] * C:(cidx[d] + 1) * C, h * GDN_D:(h + 1) * GDN_D] - ws[j][:C]).astype(BF16)
              for j, (d, h) in enumerate(chains)]
        zero = jnp.zeros((C, GDN_D), BF16)
        rr = [None] * GDN_CHAINS
        for j in range(0, GDN_CHAINS, 2):
            d, h = chains[j]
            both = jnp.concatenate([jnp.concatenate([vn[j], zero], axis=1),
                                    jnp.concatenate([zero, vn[j + 1]], axis=1)], axis=0)
            out = jnp.dot(refs[d][2][cidx[d], h // 2], both, preferred_element_type=F32)
            rr[j], rr[j + 1] = out[:, :GDN_D], out[:, GDN_D:]
        for j, (d, h) in enumerate(chains):
            c = cidx[d]
            refs[d][4][c * C:(c + 1) * C, h * GDN_D:(h + 1) * GDN_D] = ws[j][C:] + rr[j][:C]
            st[j] = st[j] * refs[d][3][c, j:j + 1, :] + rr[j][C:]
    for j in range(GDN_CHAINS):
        state_ref[j] = st[j]


def _gdn_scan(u, wq, ak, dec, B, S, cs):
    T = B * S
    nb = S // (CHUNK * cs)
    rows = cs * CHUNK
    kern = functools.partial(_gdn_scan_body, cs=cs)
    fwd = lambda b, i: b * nb + i
    bwd = lambda b, i: b * nb + nb - 1 - i

    def specs(d, pos):
        return [
            pl.BlockSpec((None, rows, GDN_W), lambda b, i: (d, pos(b, i), 0)),
            pl.BlockSpec((None, cs, 2 * CHUNK, GDN_W), lambda b, i: (d, pos(b, i), 0, 0)),
            pl.BlockSpec((None, cs, GDN_HEADS // 2, CHUNK + GDN_D, 2 * CHUNK), lambda b, i: (d, pos(b, i), 0, 0, 0)),
            pl.BlockSpec((cs, GDN_CHAINS, LANES), lambda b, i: (pos(b, i), 0, 0)),
        ]

    out_sd = jax.ShapeDtypeStruct((T, GDN_W), F32)
    return pl.pallas_call(
        kern,
        grid=(B, nb),
        in_specs=specs(0, fwd) + specs(1, bwd),
        out_specs=[pl.BlockSpec((rows, GDN_W), lambda b, i: (fwd(b, i), 0)),
                   pl.BlockSpec((rows, GDN_W), lambda b, i: (bwd(b, i), 0))],
        out_shape=[out_sd, out_sd],
        scratch_shapes=[pltpu.VMEM((GDN_CHAINS, GDN_D, GDN_D), F32)],
        compiler_params=_cparams(("parallel", "arbitrary")),
        name="gdn_scan",
    )(u, wq, ak, dec, u, wq, ak, dec)


def _attn_heads(q, k, v, qw, kw, cos, sin, bias, qo_ref, ko_ref, vo_ref):
    rows = cos.shape[0]
    lane = lax.broadcasted_iota(jnp.int32, (rows, LANES), 1)
    even_lane = lane % 2 == 0
    first_head = lane < ATT_D
    extra_lane = lane == ATT_D
    mi = lax.broadcasted_iota(jnp.int32, (LANES, LANES), 0) // ATT_D
    mj = lax.broadcasted_iota(jnp.int32, (LANES, LANES), 1) // ATT_D
    head_mean = jnp.where(mi == mj, 1.0 / ATT_D, 0.0).astype(BF16)

    def norm_rope(x, w):
        sq = x * x
        hi = sq.astype(BF16)
        lo = (sq - hi.astype(F32)).astype(BF16)
        ms = (jnp.dot(hi, head_mean, preferred_element_type=F32)
              + jnp.dot(lo, head_mean, preferred_element_type=F32))
        y = x * lax.rsqrt(ms + EPS) * w
        partner = jnp.where(even_lane, pltpu.roll(y, LANES - 1, 1), pltpu.roll(y, 1, 1))
        return y * cos + partner * sin

    def split_heads(r, extra):
        tail = jnp.where(extra_lane, extra, 0.0)
        return (jnp.where(first_head, r, tail).astype(BF16),
                jnp.where(first_head, pltpu.roll(r, ATT_D, 1), tail).astype(BF16))

    scale = LOG2E * ATT_D ** -0.5
    for c in range(ATT_HEADS // 2):
        r = norm_rope(q[:, c * LANES:(c + 1) * LANES], qw) * scale
        qo_ref[2 * c], qo_ref[2 * c + 1] = split_heads(r, bias)
    for c in range(ATT_KV_HEADS // 2):
        r = norm_rope(k[:, c * LANES:(c + 1) * LANES], kw)
        ko_ref[2 * c], ko_ref[2 * c + 1] = split_heads(r, 1.0)
        vo_ref[2 * c], vo_ref[2 * c + 1] = split_heads(v[:, c * LANES:(c + 1) * LANES], 1.0)


def _attn_body(safe_ref, q_ref, k_ref, v_ref, o_ref, m_ref, acc_ref, *, tq, tk):
    safe = safe_ref[0] != 0
    nk = k_ref.shape[0] // tk
    q = q_ref[...].reshape(ATT_GROUP * tq, ATT_DP)
    acc_ref[...] = jnp.zeros_like(acc_ref)

    def scores(j):
        keys = pl.ds(pl.multiple_of(j * tk, tk), tk)
        s = lax.dot_general(q, k_ref[keys, :], (((1,), (1,)), ((), ())), preferred_element_type=F32)
        return s, v_ref[keys, :]

    @pl.when(safe)
    def _():
        def step(j, carry):
            s, v = scores(j)
            acc_ref[...] += jnp.dot(jnp.exp2(s).astype(BF16), v, preferred_element_type=F32)
            return carry
        lax.fori_loop(0, nk, step, 0)

    @pl.when(jnp.logical_not(safe))
    def _():
        m_ref[...] = jnp.full_like(m_ref, -jnp.inf)

        def step(j, carry):
            s, v = scores(j)
            m_prev = m_ref[...]
            m_new = jnp.maximum(m_prev, jnp.max(s, axis=-1, keepdims=True))
            p = jnp.exp2(s - m_new).astype(BF16)
            acc_ref[...] = jnp.exp2(m_prev - m_new) * acc_ref[...] + jnp.dot(p, v, preferred_element_type=F32)
            m_ref[...] = m_new
            return carry
        lax.fori_loop(0, nk, step, 0)

    acc = acc_ref[...]
    o = acc[:, :ATT_D] / acc[:, ATT_D:ATT_D + 1]
    for h in range(ATT_GROUP):
        o_ref[:, h * ATT_D:(h + 1) * ATT_D] = o[h * tq:(h + 1) * tq].astype(o_ref.dtype)


def _attention(safe, q, k, v, B, S, tq, tk):
    kern = functools.partial(_attn_body, tq=tq, tk=tk)
    gw = ATT_GROUP * ATT_D
    return pl.pallas_call(
        kern,
        grid_spec=pltpu.PrefetchScalarGridSpec(
            num_scalar_prefetch=1,
            grid=(B, ATT_KV_HEADS, S // tq),
            in_specs=[
                pl.BlockSpec((None, ATT_GROUP, tq, ATT_DP), lambda b, g, i, s: (b, g, i, 0)),
                pl.BlockSpec((None, None, S, ATT_DP), lambda b, g, i, s: (b, g, 0, 0)),
                pl.BlockSpec((None, None, S, ATT_DP), lambda b, g, i, s: (b, g, 0, 0)),
            ],
            out_specs=pl.BlockSpec((None, tq, gw), lambda b, g, i, s: (b, i, g)),
            scratch_shapes=[
                pltpu.VMEM((ATT_GROUP * tq, 1), F32),
                pltpu.VMEM((ATT_GROUP * tq, ATT_DP), F32),
            ],
        ),
        out_shape=jax.ShapeDtypeStruct((B, S, ATT_Q), BF16),
        compiler_params=_cparams(("parallel", "parallel", "parallel")),
        name="attention",
    )(safe, q, k, v)


def _outproj_body(of_ref, ob_ref, z_ref, att_ref, x_ref, gnw_ref, wa_ref, wb_ref,
                  fnw_ref, wr_ref, triu_ref, xm_ref, h_ref, rt_ref, c_ref, count_ref):
    o = of_ref[...] + ob_ref[...]
    z = z_ref[...]
    parts = []
    for h in range(GDN_HEADS):
        sl = slice(h * GDN_D, (h + 1) * GDN_D)
        t = o[:, sl]
        t = t * lax.rsqrt(jnp.mean(t * t, axis=-1, keepdims=True) + EPS) * gnw_ref[...]
        parts.append((t * _silu(z[:, sl])).astype(BF16))
    mix_a = jnp.concatenate(parts, axis=-1)
    xm = x_ref[...] + jnp.dot(mix_a, wa_ref[...], preferred_element_type=F32)
    xm = xm + jnp.dot(att_ref[...], wb_ref[...], preferred_element_type=F32)
    xm_ref[...] = xm
    hn = xm * lax.rsqrt(jnp.mean(xm * xm, axis=-1, keepdims=True) + EPS) * fnw_ref[...]
    _store_token_tiles(h_ref, hn)
    hn_hi = hn.astype(BF16)
    hn_lo = (hn - hn_hi.astype(F32)).astype(BF16)
    both = jnp.dot(hn_hi, wr_ref[...], preferred_element_type=F32)
    logits = (both[:, :LANES] + both[:, LANES:]
              + jnp.dot(hn_lo, wr_ref[:, :LANES], preferred_element_type=F32))

    lt = logits.T[0:ROUTER_ROWS, :]
    tok = lt.shape[1]
    slot = lax.broadcasted_iota(jnp.int32, (ROUTER_ROWS, tok), 0)
    big = jnp.int32(ROUTER_ROWS)
    neg = -jnp.inf

    def masked_top(vals, mask):
        m = jnp.max(jnp.where(mask, vals, neg), axis=0, keepdims=True)
        idx = jnp.min(jnp.where(mask & (vals == m), slot, big), axis=0, keepdims=True)
        return m, idx

    gmask = slot < N_GROUPS
    gmax, gsel = masked_top(lt, gmask)
    gp_top = 1.0 / jnp.sum(jnp.where(gmask, jnp.exp(lt - gmax), 0.0), axis=0, keepdims=True)
    lo = N_GROUPS + gsel * EXPERTS_PER_GROUP
    emask = (slot >= lo) & (slot < lo + EXPERTS_PER_GROUP)
    m1, i1 = masked_top(lt, emask)
    ex = jnp.where(emask, jnp.exp(lt - m1), 0.0)
    pf = ex / jnp.sum(ex, axis=0, keepdims=True)
    p1, _ = masked_top(pf, emask)
    p2, i2 = masked_top(pf, emask & (slot != i1))
    denom = p1 + p2
    g1 = gp_top * p1 / denom
    g2 = gp_top * p2 / denom
    e1 = (i1 - N_GROUPS).astype(F32)
    e2 = (i2 - N_GROUPS).astype(F32)

    @pl.when(pl.program_id(0) == 0)
    def _():
        count_ref[...] = jnp.zeros_like(count_ref)

    hit1 = slot == i1
    hit2 = slot == i2
    picked = jnp.where(hit1 | hit2, 1.0, 0.0)
    seen = jnp.dot(picked.astype(BF16), triu_ref[...], preferred_element_type=F32)
    counted = count_ref[:, 0:1]
    before = counted + seen - picked
    rank1 = jnp.sum(jnp.where(hit1, before, 0.0), axis=0, keepdims=True)
    rank2 = jnp.sum(jnp.where(hit2, before, 0.0), axis=0, keepdims=True)
    total = jnp.broadcast_to(counted + seen[:, tok - 1:tok], count_ref.shape)
    count_ref[...] = total
    c_ref[...] = total
    rt_ref[...] = jnp.concatenate([e1, e2, g1, g2, rank1, rank2, jnp.zeros((SUBLANES - 6, tok), F32)], axis=0)


def _outproj(o_f, o_b, proj, att, x2, gnw, w_a, w_b, fnw, w_r, tm):
    T, D = x2.shape
    row = lambda i: (i, 0)
    const = lambda i: (0, 0)
    return pl.pallas_call(
        _outproj_body,
        grid=(T // tm,),
        in_specs=[
            pl.BlockSpec((tm, GDN_W), row),
            pl.BlockSpec((tm, GDN_W), row),
            pl.BlockSpec((tm, GDN_W), lambda i: (i, COL_Z // GDN_W)),
            pl.BlockSpec((tm, ATT_Q), row),
            pl.BlockSpec((tm, D), row),
            pl.BlockSpec((1, GDN_D), const),
            pl.BlockSpec((GDN_W, D), const),
            pl.BlockSpec((ATT_Q, D), lambda i: (GDN_W // ATT_Q, 0)),
            pl.BlockSpec((1, D), const),
            pl.BlockSpec((D, 2 * LANES), const),
            pl.BlockSpec((tm, tm), const),
        ],
        out_specs=[pl.BlockSpec((tm, D), row), pl.BlockSpec((tm * ROW_TILE, LANES), row),
                   pl.BlockSpec((SUBLANES, tm), lambda i: (0, i)),
                   pl.BlockSpec((ROUTER_ROWS, LANES), const)],
        out_shape=[jax.ShapeDtypeStruct((T, D), F32), jax.ShapeDtypeStruct((T * ROW_TILE, LANES), F32),
                   jax.ShapeDtypeStruct((SUBLANES, T), F32),
                   jax.ShapeDtypeStruct((ROUTER_ROWS, LANES), F32)],
        scratch_shapes=[pltpu.VMEM((ROUTER_ROWS, LANES), F32)],
        compiler_params=_cparams(("arbitrary",)),
        name="outproj_router",
    )(o_f, o_b, proj, att, x2, gnw, w_a, w_b, fnw, w_r, jnp.asarray(np.tri(tm, dtype=np.float32).T, BF16))


def _gather_start(src_hbm, dst_buf, sem, idx_ref, base, slot, n):
    for r in range(n):
        src = pl.multiple_of(idx_ref[base + r] * ROW_TILE, ROW_TILE)
        dst = pl.multiple_of((slot * n + r) * ROW_TILE, ROW_TILE)
        pltpu.make_async_copy(src_hbm.at[pl.ds(src, ROW_TILE)], dst_buf.at[pl.ds(dst, ROW_TILE)],
                              sem.at[slot]).start(priority=r % DMA_QUEUES)


def _gather_wait(src_hbm, dst_buf, sem, slot, n):
    dst = pl.multiple_of(slot * n * ROW_TILE, ROW_TILE)
    pltpu.make_async_copy(src_hbm.at[pl.ds(0, n * ROW_TILE)], dst_buf.at[pl.ds(dst, n * ROW_TILE)],
                          sem.at[slot]).wait()


def _experts_body(be_ref, src_ref, nused_ref,
                  h_hbm, wg_hbm, wu_hbm, wd_hbm, y_ref, xbuf, wg_buf, wu_buf, wd_buf, ws_ref, sem, wsem, *, bm):
    i = pl.program_id(0)
    n_blocks = pl.num_programs(0)
    n_used = nused_ref[0]
    slot = i % 2
    expert = be_ref[i]
    first = (i == 0) | (expert != be_ref[jnp.maximum(i - 1, 0)])

    def weight_copies(e, s):
        return [pltpu.make_async_copy(hbm.at[e], buf.at[s], wsem.at[s])
                for hbm, buf in ((wg_hbm, wg_buf), (wu_hbm, wu_buf), (wd_hbm, wd_buf))]

    def compute(prefetch_next):
        @pl.when(first)
        def _():
            cur = 1 - ws_ref[0]
            ws_ref[0] = cur
            for c in weight_copies(0, cur):
                c.wait()
            nxt = lax.while_loop(lambda j: (j < n_used) & (be_ref[jnp.minimum(j, n_blocks - 1)] == expert),
                                 lambda j: j + 1, i + 1)

            @pl.when(nxt < n_used)
            def _():
                for c in weight_copies(be_ref[jnp.minimum(nxt, n_blocks - 1)], 1 - cur):
                    c.start()

        ws = ws_ref[0]

        _gather_wait(h_hbm, xbuf, sem, slot, bm)
        x = _load_token_tiles(xbuf, slot * (bm * ROW_TILE), bm).astype(BF16)
        if prefetch_next:
            _gather_start(h_hbm, xbuf, sem, src_ref, (i + 1) * bm, 1 - slot, bm)
        gate = jnp.dot(x, wg_buf[ws].astype(BF16), preferred_element_type=F32)
        up = jnp.dot(x, wu_buf[ws].astype(BF16), preferred_element_type=F32)
        hid = (_silu(gate) * up).astype(BF16)
        _store_token_tiles(y_ref, jnp.dot(hid, wd_buf[ws].astype(BF16), preferred_element_type=F32))

    @pl.when((i == 0) & (n_used > 0))
    def _():
        ws_ref[0] = 1
        for c in weight_copies(be_ref[0], 0):
            c.start()
        _gather_start(h_hbm, xbuf, sem, src_ref, 0, 0, bm)

    @pl.when(i + 1 < n_used)
    def _():
        compute(True)

    @pl.when(i + 1 == n_used)
    def _():
        compute(False)

    @pl.when(i >= n_used)
    def _():
        y_ref[...] = jnp.zeros_like(y_ref)


def _experts(block_expert, src_tok, n_used, h2, w_gate, w_up, w_down, bm):
    D = ROW_TILE * LANES
    P = src_tok.shape[0]
    n_blocks = P // bm
    FF = w_gate.shape[-1]
    kern = functools.partial(_experts_body, bm=bm)
    hbm = pl.BlockSpec(memory_space=pl.ANY)
    return pl.pallas_call(
        kern,
        grid_spec=pltpu.PrefetchScalarGridSpec(
            num_scalar_prefetch=3,
            grid=(n_blocks,),
            in_specs=[hbm, hbm, hbm, hbm],
            out_specs=pl.BlockSpec((bm * ROW_TILE, LANES), lambda i, *_: (i, 0)),
            scratch_shapes=[
                pltpu.VMEM((2 * bm * ROW_TILE, LANES), F32),
                pltpu.VMEM((2, D, FF), F32), pltpu.VMEM((2, D, FF), F32), pltpu.VMEM((2, FF, D), F32),
                pltpu.SMEM((1,), jnp.int32),
                pltpu.SemaphoreType.DMA((2,)), pltpu.SemaphoreType.DMA((2,)),
            ],
        ),
        out_shape=jax.ShapeDtypeStruct((P * ROW_TILE, LANES), F32),
        compiler_params=_cparams(("arbitrary",)),
        name="moe_experts",
    )(block_expert, src_tok, n_used, h2, w_gate, w_up, w_down)


def _combine_body(dest_ref, y_hbm, xm_ref, r_ref, fw_ref, o_ref, ybuf, sem, *, tc):
    i = pl.program_id(0)
    n = pl.num_programs(0)
    slot = i % 2
    rows = TOP_K * tc

    def compute(prefetch_next):
        _gather_wait(y_hbm, ybuf, sem, slot, rows)
        route = r_ref[...].T
        y0 = _load_token_tiles(ybuf, slot * (rows * ROW_TILE), tc)
        y1 = _load_token_tiles(ybuf, (slot * rows + tc) * ROW_TILE, tc)
        if prefetch_next:
            _gather_start(y_hbm, ybuf, sem, dest_ref, (i + 1) * rows, 1 - slot, rows)
        xo = xm_ref[...] + route[:, 2:3] * y0 + route[:, 3:4] * y1
        o_ref[...] = xo * lax.rsqrt(jnp.mean(xo * xo, axis=-1, keepdims=True) + EPS) * fw_ref[...]

    @pl.when(i == 0)
    def _():
        _gather_start(y_hbm, ybuf, sem, dest_ref, 0, 0, rows)

    @pl.when(i + 1 < n)
    def _():
        compute(True)

    @pl.when(i + 1 == n)
    def _():
        compute(False)


def _combine(dest_blocked, y_buf, x_mid, route, final_w, tc):
    T, D = x_mid.shape
    kern = functools.partial(_combine_body, tc=tc)
    return pl.pallas_call(
        kern,
        grid_spec=pltpu.PrefetchScalarGridSpec(
            num_scalar_prefetch=1,
            grid=(T // tc,),
            in_specs=[
                pl.BlockSpec(memory_space=pl.ANY),
                pl.BlockSpec((tc, D), lambda i, d: (i, 0)),
                pl.BlockSpec((SUBLANES, tc), lambda i, d: (0, i)),
                pl.BlockSpec((1, D), lambda i, d: (0, 0)),
            ],
            out_specs=pl.BlockSpec((tc, D), lambda i, d: (i, 0)),
            scratch_shapes=[pltpu.VMEM((2 * TOP_K * tc * ROW_TILE, LANES), F32), pltpu.SemaphoreType.DMA((2,))],
        ),
        out_shape=jax.ShapeDtypeStruct((T, D), F32),
        compiler_params=_cparams(("arbitrary",)),
        name="moe_combine",
    )(dest_blocked, y_buf, x_mid, route, final_w.reshape(1, D))


def _layer(x2, B, S, norm_mix_w, w_in, conv_w, a_log, dt_bias, gdn_norm_w, q_norm_w, k_norm_w, w_out,
           norm_ffn_w, w_router_group, w_router_expert, w_gate, w_up, w_down, final_w):
    T, D = x2.shape
    tl = _tiles(B, S)
    o_gate = 4 * GDN_W
    o_qb = o_gate + 2 * GDN_CHAINS
    w_all = jnp.concatenate([w_in[:, :o_gate], w_in[:, o_qb:], w_in[:, o_gate:o_qb],
                             jnp.zeros((D, LANES - 2 * GDN_CHAINS), w_in.dtype)], axis=1).astype(BF16)

    rows = S // GRID_W
    rowp = np.repeat(np.arange(rows), GRID_W).astype(np.float64)
    colp = np.tile(np.arange(GRID_W), rows).astype(np.float64)
    axis_dims = ATT_D // 2
    inv_freq = ROPE_THETA ** (-np.arange(0, axis_dims, 2, dtype=np.float64) / axis_dims)
    ang = np.concatenate([rowp[:, None] * inv_freq, colp[:, None] * inv_freq], axis=-1)
    pair_sign = np.tile(np.array([-1.0, 1.0]), axis_dims)
    cosf = jnp.asarray(np.tile(np.repeat(np.cos(ang), 2, axis=1), (1, LANES // ATT_D)), F32)
    sinf = jnp.asarray(np.tile(np.repeat(np.sin(ang), 2, axis=1) * pair_sign, (1, LANES // ATT_D)), F32)
    q_gain = jnp.max(jnp.abs(q_norm_w)).astype(F32)
    k_gain = jnp.max(jnp.abs(k_norm_w)).astype(F32)
    score_bound = ATT_D ** 0.5 * q_gain * k_gain
    safe = (2.0 * score_bound <= SOFTMAX_SAFE_SPAN).astype(jnp.int32).reshape(1)
    bias = (-LOG2E * score_bound).reshape(1, 1)
    pair = lambda w: jnp.tile(w, LANES // ATT_D).reshape(1, LANES)

    proj, gate_logits, qh, kh, vh = _inproj(x2, norm_mix_w, w_all, pair(q_norm_w), pair(k_norm_w),
                                            cosf, sinf, bias, B, S, tl.proj_rows)

    conv_w8 = jnp.concatenate([conv_w, jnp.zeros((SUBLANES - CONV_W, conv_w.shape[1]), F32)], axis=0)
    gp = jnp.zeros((SUBLANES, LANES), F32)
    gp = gp.at[0, GDN_CHAINS:2 * GDN_CHAINS].set(jnp.exp(a_log.astype(F32)).reshape(-1))
    gp = gp.at[1, GDN_CHAINS:2 * GDN_CHAINS].set(dt_bias.astype(F32).reshape(-1))
    q_a, k_a, v_a, gates, gates_t = _gdn_prep(proj, gate_logits, conv_w8, gp, B, S, tl.prep_rows)
    u, wq, ak, dec = _gdn_chunk(q_a, k_a, v_a, gates, gates_t, tl.gdn_chunks)
    o_f, o_b = _gdn_scan(u, wq, ak, dec, B, S, tl.scan_chunks)

    att = _attention(safe, qh, kh, vh, B, S, tl.att_q, tl.att_k).reshape(T, ATT_Q)

    w_r32 = jnp.concatenate([w_router_group, w_router_expert,
                             jnp.zeros((D, LANES - N_GROUPS - N_EXPERTS), F32)], axis=1).astype(F32)
    w_r_hi = w_r32.astype(BF16)
    w_r = jnp.concatenate([w_r_hi, (w_r32 - w_r_hi.astype(F32)).astype(BF16)], axis=1)
    w_out_bf = w_out.astype(BF16)
    x_mid, h2, route_t, count_rows = _outproj(
        o_f, o_b, proj, att, x2, gdn_norm_w.reshape(1, GDN_D), w_out_bf, w_out_bf,
        norm_ffn_w.reshape(1, D), w_r, tl.proj_rows)

    bm = tl.moe_rows
    n_assign = T * TOP_K
    n_blocks = -(-(n_assign + N_EXPERTS * (bm - 1)) // bm)
    experts = jnp.arange(N_EXPERTS, dtype=jnp.int32)
    counts = count_rows[N_GROUPS:N_GROUPS + N_EXPERTS, 0].astype(jnp.int32)
    padded = (counts + bm - 1) // bm * bm
    pad_end = jnp.cumsum(padded)
    pad_start = pad_end - padded
    block_start = jnp.arange(n_blocks, dtype=jnp.int32) * bm
    done = (pad_end[None, :] <= block_start[:, None]).astype(jnp.int32)
    begun = (pad_start[None, :] <= block_start[:, None]).astype(jnp.int32)
    block_expert = jnp.minimum(jnp.sum(done, axis=1), N_EXPERTS - 1)
    n_used = (pad_end[-1:] // bm).astype(jnp.int32)
    seg_start = jnp.sum(done * padded[None, :], axis=1)
    seg_entry = jnp.sum(done * counts[None, :], axis=1)
    seg_count = jnp.sum(begun * counts[None, :], axis=1) - seg_entry
    e_rows = route_t[0:TOP_K].astype(jnp.int32)
    rank_rows = route_t[4:4 + TOP_K].astype(jnp.int32)
    dest = jnp.sum(jnp.where(e_rows[:, :, None] == experts, pad_start, 0), axis=-1) + rank_rows
    tokens = jnp.tile(jnp.arange(T, dtype=jnp.int32), TOP_K)
    _, compact = lax.sort((dest.reshape(-1), tokens), num_keys=1)
    row = block_start[:, None] + jnp.arange(bm, dtype=jnp.int32)[None, :]
    seg_row = row - seg_start[:, None]
    holds_token = seg_row < seg_count[:, None]
    entry = jnp.clip(seg_entry[:, None] + seg_row, 0, n_assign - 1)
    src_tok = jnp.where(holds_token, compact[entry], lax.rem(row, jnp.full_like(row, T))).reshape(-1)

    y_buf = _experts(block_expert, src_tok, n_used, h2, w_gate, w_up, w_down, bm)

    tc = tl.comb_rows
    dest_blocked = dest.reshape(TOP_K, T // tc, tc).transpose(1, 0, 2).reshape(-1)
    return _combine(dest_blocked, y_buf, x_mid, route_t, final_w, tc)


def kernel(x, norm_mix_w, w_in, conv_w, a_log, dt_bias, gdn_norm_w, q_norm_w, k_norm_w, w_out, norm_ffn_w,
           w_router_group, w_router_expert, w_gate, w_up, w_down, final_norm_w):
    B, S, D = x.shape
    depth = w_in.shape[0]
    assert depth == 1, "the final norm is fused into the last (only) layer's combine step"
    out = _layer(x.reshape(B * S, D), B, S, norm_mix_w[0], w_in[0], conv_w[0], a_log[0], dt_bias[0],
                 gdn_norm_w[0], q_norm_w[0], k_norm_w[0], w_out[0], norm_ffn_w[0], w_router_group[0],
                 w_router_expert[0], w_gate[0], w_up[0], w_down[0], final_norm_w)
    return out.reshape(B, S, D)
```

```python
import functools
import math
from typing import NamedTuple

import jax
import jax.numpy as jnp
import numpy as np
from jax import lax
from jax.experimental import pallas as pl
from jax.experimental.pallas import tpu as pltpu

F32 = jnp.float32
BF16 = jnp.bfloat16
EPS = 1e-6

GRID_W = 64
GDN_HEADS = 4
GDN_D = 128
CONV_W = 5
CHUNK = 64
ATT_HEADS = 8
ATT_KV_HEADS = 2
ATT_GROUP = ATT_HEADS // ATT_KV_HEADS
ATT_D = 64
ROPE_THETA = 10000.0
N_GROUPS = 4
EXPERTS_PER_GROUP = 8
N_EXPERTS = N_GROUPS * EXPERTS_PER_GROUP
TOP_K = 2

GDN_W = GDN_HEADS * GDN_D
GDN_CHAINS = 2 * GDN_HEADS
ATT_Q = ATT_HEADS * ATT_D
ATT_KV = ATT_KV_HEADS * ATT_D
LANES = 128
SUBLANES = 8
ATT_DP = LANES
ROUTER_ROWS = -(-(N_GROUPS + N_EXPERTS) // SUBLANES) * SUBLANES

COL_Z = 3 * GDN_W
COL_QB = COL_Z + GDN_W
COL_KB = COL_QB + ATT_Q
COL_VB = COL_KB + ATT_KV
COL_GATE = COL_VB + ATT_KV
D_PROJ = COL_GATE + LANES

VMEM_LIMIT = 56 * 1024 * 1024
LOG2E = math.log2(math.e)
SOFTMAX_SAFE_SPAN = 60.0


class Tiles(NamedTuple):
    proj_rows: int
    prep_rows: int
    gdn_chunks: int
    scan_chunks: int
    att_q: int
    att_k: int
    moe_rows: int
    comb_rows: int


def _tile(n, want):
    t = min(n, want)
    assert n % t == 0, (n, want)
    return t


def _tiles(B, S):
    T = B * S
    n_chunks = S // CHUNK
    return Tiles(proj_rows=_tile(T, 512), prep_rows=_tile(S, 512), gdn_chunks=_tile(n_chunks, 8),
                 scan_chunks=_tile(n_chunks, 8), att_q=_tile(S, 512), att_k=_tile(S, 2048),
                 moe_rows=256, comb_rows=_tile(T, 256))


def _cparams(sem):
    return pltpu.CompilerParams(dimension_semantics=sem, vmem_limit_bytes=VMEM_LIMIT)


def _silu(x):
    return x * jax.nn.sigmoid(x)


ROW_TILE = SUBLANES
DMA_QUEUES = 2


def _store_token_tiles(ref, x, base=0):
    rows, d = x.shape
    assert d == ROW_TILE * LANES
    for c in range(ROW_TILE):
        ref[pl.ds(base + c, rows, stride=ROW_TILE), :] = x[:, c * LANES:(c + 1) * LANES]


def _load_token_tiles(ref, base, rows):
    return jnp.concatenate([ref[pl.ds(base + c, rows, stride=ROW_TILE), :] for c in range(ROW_TILE)], axis=-1)


def _inproj_body(x_ref, nw_ref, w_ref, qw_ref, kw_ref, cos_ref, sin_ref, bias_ref,
                 o_ref, g_ref, qo_ref, ko_ref, vo_ref):
    x = x_ref[...]
    h = x * lax.rsqrt(jnp.mean(x * x, axis=-1, keepdims=True) + EPS) * nw_ref[...]
    acc = jnp.dot(h.astype(BF16), w_ref[...], preferred_element_type=F32)
    o_ref[...] = acc[:, :COL_QB]
    g_ref[...] = acc[:, COL_GATE:]
    _attn_heads(acc[:, COL_QB:COL_KB], acc[:, COL_KB:COL_VB], acc[:, COL_VB:COL_GATE], qw_ref[...], kw_ref[...],
                cos_ref[...], sin_ref[...], bias_ref[...], qo_ref, ko_ref, vo_ref)


def _inproj(x2, norm_w, w_all, qw, kw, cosf, sinf, bias, B, S, tm):
    T, D = x2.shape
    nr = S // tm
    const = lambda i: (0, 0)
    heads = lambda i: (i // nr, 0, i % nr, 0)
    return pl.pallas_call(
        _inproj_body,
        grid=(T // tm,),
        in_specs=[
            pl.BlockSpec((tm, D), lambda i: (i, 0)),
            pl.BlockSpec((1, D), const),
            pl.BlockSpec((D, D_PROJ), const),
            pl.BlockSpec((1, LANES), const),
            pl.BlockSpec((1, LANES), const),
            pl.BlockSpec((tm, LANES), lambda i: (i % nr, 0)),
            pl.BlockSpec((tm, LANES), lambda i: (i % nr, 0)),
            pl.BlockSpec((1, 1), const),
        ],
        out_specs=[
            pl.BlockSpec((tm, COL_QB), lambda i: (i, 0)),
            pl.BlockSpec((tm, LANES), lambda i: (i, 0)),
            pl.BlockSpec((None, ATT_HEADS, tm, ATT_DP), heads),
            pl.BlockSpec((None, ATT_KV_HEADS, tm, ATT_DP), heads),
            pl.BlockSpec((None, ATT_KV_HEADS, tm, ATT_DP), heads),
        ],
        out_shape=[
            jax.ShapeDtypeStruct((T, COL_QB), F32),
            jax.ShapeDtypeStruct((T, LANES), F32),
            jax.ShapeDtypeStruct((B, ATT_HEADS, S, ATT_DP), BF16),
            jax.ShapeDtypeStruct((B, ATT_KV_HEADS, S, ATT_DP), BF16),
            jax.ShapeDtypeStruct((B, ATT_KV_HEADS, S, ATT_DP), BF16),
        ],
        compiler_params=_cparams(("parallel",)),
        name="inproj",
    )(x2, norm_w.reshape(1, D), w_all, qw, kw, cosf, sinf, bias)


def _gdn_prep_body(cur_ref, prev_ref, next_ref, cw_ref, gin_ref, gp_ref,
                   q_ref, k_ref, v_ref, g_ref, gt_ref, ext_ref, *, tr):
    i = pl.program_id(1)
    nr = pl.num_programs(1)
    halo = prev_ref.shape[0]
    pad = CONV_W // 2
    ext_ref[0:halo, :] = jnp.where(i > 0, prev_ref[...], 0.0)
    ext_ref[halo:halo + tr, :] = cur_ref[...]
    ext_ref[halo + tr:2 * halo + tr, :] = jnp.where(i < nr - 1, next_ref[...], 0.0)
    acc = cw_ref[0:1, :] * ext_ref[pl.ds(halo - pad, tr), :]
    for j in range(1, CONV_W):
        acc = acc + cw_ref[j:j + 1, :] * ext_ref[pl.ds(halo - pad + j, tr), :]
    y = _silu(acc)
    for h in range(GDN_HEADS):
        for base, ref, scale in ((0, q_ref, GDN_D ** -0.5), (GDN_W, k_ref, 1.0)):
            t = y[:, base + h * GDN_D: base + (h + 1) * GDN_D]
            t = t * (lax.rsqrt(jnp.sum(t * t, axis=-1, keepdims=True) + EPS) * scale)
            ref[:, h * GDN_D:(h + 1) * GDN_D] = t
    v_ref[...] = y[:, 2 * GDN_W:]
    gin = gin_ref[...]
    lane = lax.broadcasted_iota(jnp.int32, gin.shape, 1)
    a = gin + gp_ref[1:2, :]
    softplus = jnp.maximum(a, 0.0) + jnp.log1p(jnp.exp(-jnp.abs(a)))
    g = jnp.where(lane < GDN_CHAINS, jax.nn.sigmoid(gin), -gp_ref[0:1, :] * softplus)
    g_ref[...] = g
    gt_ref[...] = g.T[0:2 * GDN_CHAINS, :]


def _gdn_prep(proj, gate_logits, conv_w8, gate_params, B, S, tr):
    T = B * S
    nr = S // tr
    C = 3 * GDN_W
    halo = SUBLANES
    rb = tr // halo
    n_halo = T // halo
    kern = functools.partial(_gdn_prep_body, tr=tr)
    out_sd = jax.ShapeDtypeStruct((T, GDN_W), F32)
    return pl.pallas_call(
        kern,
        grid=(B, nr),
        in_specs=[
            pl.BlockSpec((tr, C), lambda b, i: (b * nr + i, 0)),
            pl.BlockSpec((halo, C), lambda b, i: (jnp.maximum((b * nr + i) * rb - 1, 0), 0)),
            pl.BlockSpec((halo, C), lambda b, i: (jnp.minimum((b * nr + i + 1) * rb, n_halo - 1), 0)),
            pl.BlockSpec((SUBLANES, C), lambda b, i: (0, 0)),
            pl.BlockSpec((tr, LANES), lambda b, i: (b * nr + i, 0)),
            pl.BlockSpec((SUBLANES, LANES), lambda b, i: (0, 0)),
        ],
        out_specs=[
            pl.BlockSpec((tr, GDN_W), lambda b, i: (b * nr + i, 0)),
            pl.BlockSpec((tr, GDN_W), lambda b, i: (b * nr + i, 0)),
            pl.BlockSpec((tr, GDN_W), lambda b, i: (b * nr + i, 0)),
            pl.BlockSpec((tr, LANES), lambda b, i: (b * nr + i, 0)),
            pl.BlockSpec((2 * GDN_CHAINS, tr), lambda b, i: (0, b * nr + i)),
        ],
        out_shape=[out_sd, out_sd, out_sd, jax.ShapeDtypeStruct((T, LANES), F32),
                   jax.ShapeDtypeStruct((2 * GDN_CHAINS, T), F32)],
        scratch_shapes=[pltpu.VMEM((tr + 2 * halo, C), F32)],
        compiler_params=_cparams(("parallel", "parallel")),
        name="gdn_prep",
    )(proj, proj, proj, conv_w8, gate_logits, gate_params)


def _bdot(a, b):
    return jnp.dot(a.astype(BF16), b.astype(BF16), preferred_element_type=F32)


def _bdot_nt(a, b):
    return lax.dot_general(a.astype(BF16), b.astype(BF16), (((1,), (1,)), ((), ())),
                           preferred_element_type=F32)


def _gdn_chunk_body(q_ref, k_ref, v_ref, g_ref, gt_ref, u_ref, wq_ref, ak_ref, dec_ref, *, cb):
    C = CHUNK
    row = lax.broadcasted_iota(jnp.int32, (C, C), 0)
    col = lax.broadcasted_iota(jnp.int32, (C, C), 1)
    eye = (row == col).astype(F32)
    masks = ((row >= col, row > col), (row <= col, row < col))
    hi = lax.Precision.HIGHEST
    chains = []
    gate_rows = gt_ref[...]
    for c in range(cb):
        rs = slice(c * C, (c + 1) * C)
        gates = g_ref[rs, :]
        gates_t = gate_rows[:, rs]
        g_tot = jnp.sum(gates, axis=0, keepdims=True)
        tot_rows = jnp.sum(gates_t, axis=1, keepdims=True)
        dec_ref[c] = jnp.broadcast_to(jnp.exp(tot_rows[GDN_CHAINS:2 * GDN_CHAINS]), (GDN_CHAINS, LANES))
        for d in range(2):
            incl, strict = masks[d]
            cum = incl.astype(F32)
            gc_cols = jnp.dot(cum, gates, precision=hi, preferred_element_type=F32)
            gc_rows = lax.dot_general(gates_t, cum, (((1,), (1,)), ((), ())), precision=hi,
                                      preferred_element_type=F32)
            for h in range(GDN_HEADS):
                lane_b = d * GDN_HEADS + h
                lane_g = GDN_CHAINS + lane_b
                sl = slice(h * GDN_D, (h + 1) * GDN_D)
                q = q_ref[rs, sl]
                k = k_ref[rs, sl]
                v = v_ref[rs, sl]
                beta = gates[:, lane_b:lane_b + 1]
                gc_col = gc_cols[:, lane_g:lane_g + 1]
                gc_row = gc_rows[lane_g:lane_g + 1, :]
                g_last = g_tot[:, lane_g:lane_g + 1]
                decay = jnp.where(incl, jnp.exp(jnp.where(incl, gc_col - gc_row, 0.0)), 0.0)
                e_col = jnp.exp(gc_col)
                k_beta = k * beta
                kk = _bdot_nt(jnp.concatenate([k_beta, q], axis=0), k)
                neg_l = jnp.where(strict, -kk[:C] * decay, 0.0)
                attn = jnp.where(incl, kk[C:] * decay, 0.0)
                rhs = jnp.concatenate([v * beta, k_beta * e_col], axis=1).astype(BF16)
                wq_ref[d, c, C:2 * C, sl] = (q * e_col).astype(BF16)
                ak_ref[d, c, h, 0:C, :] = attn.astype(BF16)
                ak_ref[d, c, h, C:C + GDN_D, :] = (k * jnp.exp(g_last - gc_col)).T.astype(BF16)
                chains.append((d, c, rs, sl, neg_l, rhs))
    zs = [jnp.concatenate([ch[4], eye], axis=1) for ch in chains]
    keep_s = lax.broadcasted_iota(jnp.int32, (C, 2 * C), 1) >= C
    for _ in range(int(math.log2(C))):
        zs = [_bdot(z[:, :C], z) + jnp.where(keep_s, z, 0.0) for z in zs]
    for (d, c, rs, sl, _, rhs), z in zip(chains, zs):
        sol = jnp.dot(z[:, C:].astype(BF16), rhs, preferred_element_type=F32)
        u_ref[d, rs, sl] = sol[:, :GDN_D]
        wq_ref[d, c, 0:C, sl] = sol[:, GDN_D:].astype(BF16)


def _gdn_chunk(q, k, v, gates, gates_t, cb):
    T = q.shape[0]
    nc = T // CHUNK
    rows = cb * CHUNK
    kern = functools.partial(_gdn_chunk_body, cb=cb)
    wide = pl.BlockSpec((rows, GDN_W), lambda i: (i, 0))
    return pl.pallas_call(
        kern,
        grid=(nc // cb,),
        in_specs=[wide, wide, wide, pl.BlockSpec((rows, LANES), lambda i: (i, 0)),
                  pl.BlockSpec((2 * GDN_CHAINS, rows), lambda i: (0, i))],
        out_specs=[
            pl.BlockSpec((2, rows, GDN_W), lambda i: (0, i, 0)),
            pl.BlockSpec((2, cb, 2 * CHUNK, GDN_W), lambda i: (0, i, 0, 0)),
            pl.BlockSpec((2, cb, GDN_HEADS, CHUNK + GDN_D, CHUNK), lambda i: (0, i, 0, 0, 0)),
            pl.BlockSpec((cb, GDN_CHAINS, LANES), lambda i: (i, 0, 0)),
        ],
        out_shape=[
            jax.ShapeDtypeStruct((2, T, GDN_W), F32),
            jax.ShapeDtypeStruct((2, nc, 2 * CHUNK, GDN_W), BF16),
            jax.ShapeDtypeStruct((2, nc, GDN_HEADS, CHUNK + GDN_D, CHUNK), BF16),
            jax.ShapeDtypeStruct((nc, GDN_CHAINS, LANES), F32),
        ],
        compiler_params=_cparams(("parallel",)),
        name="gdn_chunk",
    )(q, k, v, gates, gates_t)


def _gdn_scan_body(uf_ref, wqf_ref, akf_ref, decf_ref, ub_ref, wqb_ref, akb_ref, decb_ref,
                   of_ref, ob_ref, state_ref, *, cs):
    @pl.when(pl.program_id(1) == 0)
    def _():
        state_ref[...] = jnp.zeros_like(state_ref)

    C = CHUNK
    refs = ((uf_ref, wqf_ref, akf_ref, decf_ref, of_ref), (ub_ref, wqb_ref, akb_ref, decb_ref, ob_ref))
    chains = [(d, h) for d in range(2) for h in range(GDN_HEADS)]
    st = [state_ref[j] for j in range(GDN_CHAINS)]
    for step in range(cs):
        cidx = (step, cs - 1 - step)
        ws = [jnp.dot(refs[d][1][cidx[d], :, h * GDN_D:(h + 1) * GDN_D], st[j].astype(BF16),
                      preferred_element_type=F32) for j, (d, h) in enumerate(chains)]
        vn = [(refs[d][0][cidx[d] * C:(cidx[d] + 1) * C, h * GDN_D:(h + 1) * GDN_D] - ws[j][:C]).astype(BF16)
              for j, (d, h) in enumerate(chains)]
        rr = [jnp.dot(refs[d][2][cidx[d], h], vn[j], preferred_element_type=F32)
              for j, (d, h) in enumerate(chains)]
        for j, (d, h) in enumerate(chains):
            c = cidx[d]
            refs[d][4][c * C:(c + 1) * C, h * GDN_D:(h + 1) * GDN_D] = (ws[j][C:] + rr[j][:C]).astype(BF16)
            st[j] = st[j] * refs[d][3][c, j:j + 1, :] + rr[j][C:]
    for j in range(GDN_CHAINS):
        state_ref[j] = st[j]


def _gdn_scan(u, wq, ak, dec, B, S, cs):
    T = B * S
    nb = S // (CHUNK * cs)
    rows = cs * CHUNK
    kern = functools.partial(_gdn_scan_body, cs=cs)
    fwd = lambda b, i: b * nb + i
    bwd = lambda b, i: b * nb + nb - 1 - i

    def specs(d, pos):
        return [
            pl.BlockSpec((None, rows, GDN_W), lambda b, i: (d, pos(b, i), 0)),
            pl.BlockSpec((None, cs, 2 * CHUNK, GDN_W), lambda b, i: (d, pos(b, i), 0, 0)),
            pl.BlockSpec((None, cs, GDN_HEADS, CHUNK + GDN_D, CHUNK), lambda b, i: (d, pos(b, i), 0, 0, 0)),
            pl.BlockSpec((cs, GDN_CHAINS, LANES), lambda b, i: (pos(b, i), 0, 0)),
        ]

    out_sd = jax.ShapeDtypeStruct((T, GDN_W), BF16)
    return pl.pallas_call(
        kern,
        grid=(B, nb),
        in_specs=specs(0, fwd) + specs(1, bwd),
        out_specs=[pl.BlockSpec((rows, GDN_W), lambda b, i: (fwd(b, i), 0)),
                   pl.BlockSpec((rows, GDN_W), lambda b, i: (bwd(b, i), 0))],
        out_shape=[out_sd, out_sd],
        scratch_shapes=[pltpu.VMEM((GDN_CHAINS, GDN_D, GDN_D), F32)],
        compiler_params=_cparams(("parallel", "arbitrary")),
        name="gdn_scan",
    )(u, wq, ak, dec, u, wq, ak, dec)


def _attn_heads(q, k, v, qw, kw, cos, sin, bias, qo_ref, ko_ref, vo_ref):
    rows = cos.shape[0]
    lane = lax.broadcasted_iota(jnp.int32, (rows, LANES), 1)
    even_lane = lane % 2 == 0
    first_head = lane < ATT_D
    extra_lane = lane == ATT_D
    mi = lax.broadcasted_iota(jnp.int32, (LANES, LANES), 0) // ATT_D
    mj = lax.broadcasted_iota(jnp.int32, (LANES, LANES), 1) // ATT_D
    head_mean = jnp.where(mi == mj, 1.0 / ATT_D, 0.0).astype(BF16)

    def norm_rope(x, w):
        sq = x * x
        hi = sq.astype(BF16)
        lo = (sq - hi.astype(F32)).astype(BF16)
        ms = (jnp.dot(hi, head_mean, preferred_element_type=F32)
              + jnp.dot(lo, head_mean, preferred_element_type=F32))
        y = x * lax.rsqrt(ms + EPS) * w
        partner = jnp.where(even_lane, pltpu.roll(y, LANES - 1, 1), pltpu.roll(y, 1, 1))
        return y * cos + partner * sin

    def split_heads(r, extra):
        tail = jnp.where(extra_lane, extra, 0.0)
        return (jnp.where(first_head, r, tail).astype(BF16),
                jnp.where(first_head, pltpu.roll(r, ATT_D, 1), tail).astype(BF16))

    scale = LOG2E * ATT_D ** -0.5
    for c in range(ATT_HEADS // 2):
        r = norm_rope(q[:, c * LANES:(c + 1) * LANES], qw) * scale
        qo_ref[2 * c], qo_ref[2 * c + 1] = split_heads(r, bias)
    for c in range(ATT_KV_HEADS // 2):
        r = norm_rope(k[:, c * LANES:(c + 1) * LANES], kw)
        ko_ref[2 * c], ko_ref[2 * c + 1] = split_heads(r, 1.0)
        vo_ref[2 * c], vo_ref[2 * c + 1] = split_heads(v[:, c * LANES:(c + 1) * LANES], 1.0)


def _attn_body(safe_ref, q_ref, k_ref, v_ref, o_ref, m_ref, acc_ref, *, tq, tk):
    safe = safe_ref[0] != 0
    nk = k_ref.shape[0] // tk
    q = q_ref[...].reshape(ATT_GROUP * tq, ATT_DP)
    acc_ref[...] = jnp.zeros_like(acc_ref)

    def scores(j):
        keys = pl.ds(pl.multiple_of(j * tk, tk), tk)
        s = lax.dot_general(q, k_ref[keys, :], (((1,), (1,)), ((), ())), preferred_element_type=F32)
        return s, v_ref[keys, :]

    @pl.when(safe)
    def _():
        def step(j, carry):
            s, v = scores(j)
            acc_ref[...] += jnp.dot(jnp.exp2(s).astype(BF16), v, preferred_element_type=F32)
            return carry
        lax.fori_loop(0, nk, step, 0)

    @pl.when(jnp.logical_not(safe))
    def _():
        m_ref[...] = jnp.full_like(m_ref, -jnp.inf)

        def step(j, carry):
            s, v = scores(j)
            m_prev = m_ref[...]
            m_new = jnp.maximum(m_prev, jnp.max(s, axis=-1, keepdims=True))
            p = jnp.exp2(s - m_new).astype(BF16)
            acc_ref[...] = jnp.exp2(m_prev - m_new) * acc_ref[...] + jnp.dot(p, v, preferred_element_type=F32)
            m_ref[...] = m_new
            return carry
        lax.fori_loop(0, nk, step, 0)

    acc = acc_ref[...]
    o = acc[:, :ATT_D] / acc[:, ATT_D:ATT_D + 1]
    for h in range(ATT_GROUP):
        o_ref[:, h * ATT_D:(h + 1) * ATT_D] = o[h * tq:(h + 1) * tq].astype(o_ref.dtype)


def _attention(safe, q, k, v, B, S, tq, tk):
    kern = functools.partial(_attn_body, tq=tq, tk=tk)
    gw = ATT_GROUP * ATT_D
    return pl.pallas_call(
        kern,
        grid_spec=pltpu.PrefetchScalarGridSpec(
            num_scalar_prefetch=1,
            grid=(B, ATT_KV_HEADS, S // tq),
            in_specs=[
                pl.BlockSpec((None, ATT_GROUP, tq, ATT_DP), lambda b, g, i, s: (b, g, i, 0)),
                pl.BlockSpec((None, None, S, ATT_DP), lambda b, g, i, s: (b, g, 0, 0)),
                pl.BlockSpec((None, None, S, ATT_DP), lambda b, g, i, s: (b, g, 0, 0)),
            ],
            out_specs=pl.BlockSpec((None, tq, gw), lambda b, g, i, s: (b, i, g)),
            scratch_shapes=[
                pltpu.VMEM((ATT_GROUP * tq, 1), F32),
                pltpu.VMEM((ATT_GROUP * tq, ATT_DP), F32),
            ],
        ),
        out_shape=jax.ShapeDtypeStruct((B, S, ATT_Q), BF16),
        compiler_params=_cparams(("parallel", "parallel", "parallel")),
        name="attention",
    )(safe, q, k, v)


def _outproj_body(of_ref, ob_ref, z_ref, att_ref, x_ref, gnw_ref, wa_ref, wb_ref,
                  fnw_ref, wr_ref, triu_ref, xm_ref, h_ref, rt_ref, c_ref, count_ref):
    o = of_ref[...].astype(F32) + ob_ref[...].astype(F32)
    z = z_ref[...]
    parts = []
    for h in range(GDN_HEADS):
        sl = slice(h * GDN_D, (h + 1) * GDN_D)
        t = o[:, sl]
        t = t * lax.rsqrt(jnp.mean(t * t, axis=-1, keepdims=True) + EPS) * gnw_ref[...]
        parts.append((t * _silu(z[:, sl])).astype(BF16))
    mix_a = jnp.concatenate(parts, axis=-1)
    xm = x_ref[...] + jnp.dot(mix_a, wa_ref[...], preferred_element_type=F32)
    xm = xm + jnp.dot(att_ref[...], wb_ref[...], preferred_element_type=F32)
    xm_ref[...] = xm
    hn = xm * lax.rsqrt(jnp.mean(xm * xm, axis=-1, keepdims=True) + EPS) * fnw_ref[...]
    _store_token_tiles(h_ref, hn)
    hn_hi = hn.astype(BF16)
    hn_lo = (hn - hn_hi.astype(F32)).astype(BF16)
    both = jnp.dot(hn_hi, wr_ref[...], preferred_element_type=F32)
    logits = (both[:, :LANES] + both[:, LANES:]
              + jnp.dot(hn_lo, wr_ref[:, :LANES], preferred_element_type=F32))

    lt = logits.T[0:ROUTER_ROWS, :]
    tok = lt.shape[1]
    slot = lax.broadcasted_iota(jnp.int32, (ROUTER_ROWS, tok), 0)
    big = jnp.int32(ROUTER_ROWS)
    neg = -jnp.inf

    def masked_top(vals, mask):
        m = jnp.max(jnp.where(mask, vals, neg), axis=0, keepdims=True)
        idx = jnp.min(jnp.where(mask & (vals == m), slot, big), axis=0, keepdims=True)
        return m, idx

    gmask = slot < N_GROUPS
    gmax, gsel = masked_top(lt, gmask)
    gp_top = 1.0 / jnp.sum(jnp.where(gmask, jnp.exp(lt - gmax), 0.0), axis=0, keepdims=True)
    lo = N_GROUPS + gsel * EXPERTS_PER_GROUP
    emask = (slot >= lo) & (slot < lo + EXPERTS_PER_GROUP)
    m1, i1 = masked_top(lt, emask)
    ex = jnp.where(emask, jnp.exp(lt - m1), 0.0)
    pf = ex / jnp.sum(ex, axis=0, keepdims=True)
    p1, _ = masked_top(pf, emask)
    p2, i2 = masked_top(pf, emask & (slot != i1))
    denom = p1 + p2
    g1 = gp_top * p1 / denom
    g2 = gp_top * p2 / denom
    e1 = (i1 - N_GROUPS).astype(F32)
    e2 = (i2 - N_GROUPS).astype(F32)

    @pl.when(pl.program_id(0) == 0)
    def _():
        count_ref[...] = jnp.zeros_like(count_ref)

    hit1 = slot == i1
    hit2 = slot == i2
    picked = jnp.where(hit1 | hit2, 1.0, 0.0)
    seen = jnp.dot(picked.astype(BF16), triu_ref[...], preferred_element_type=F32)
    counted = count_ref[:, 0:1]
    before = counted + seen - picked
    rank1 = jnp.sum(jnp.where(hit1, before, 0.0), axis=0, keepdims=True)
    rank2 = jnp.sum(jnp.where(hit2, before, 0.0), axis=0, keepdims=True)
    total = jnp.broadcast_to(counted + seen[:, tok - 1:tok], count_ref.shape)
    count_ref[...] = total
    c_ref[...] = total
    rt_ref[...] = jnp.concatenate([e1, e2, g1, g2, rank1, rank2, jnp.zeros((SUBLANES - 6, tok), F32)], axis=0)


def _outproj(o_f, o_b, proj, att, x2, gnw, w_a, w_b, fnw, w_r, tm):
    T, D = x2.shape
    row = lambda i: (i, 0)
    const = lambda i: (0, 0)
    return pl.pallas_call(
        _outproj_body,
        grid=(T // tm,),
        in_specs=[
            pl.BlockSpec((tm, GDN_W), row),
            pl.BlockSpec((tm, GDN_W), row),
            pl.BlockSpec((tm, GDN_W), lambda i: (i, COL_Z // GDN_W)),
            pl.BlockSpec((tm, ATT_Q), row),
            pl.BlockSpec((tm, D), row),
            pl.BlockSpec((1, GDN_D), const),
            pl.BlockSpec((GDN_W, D), const),
            pl.BlockSpec((ATT_Q, D), lambda i: (GDN_W // ATT_Q, 0)),
            pl.BlockSpec((1, D), const),
            pl.BlockSpec((D, 2 * LANES), const),
            pl.BlockSpec((tm, tm), const),
        ],
        out_specs=[pl.BlockSpec((tm, D), row), pl.BlockSpec((tm * ROW_TILE, LANES), row),
                   pl.BlockSpec((SUBLANES, tm), lambda i: (0, i)),
                   pl.BlockSpec((ROUTER_ROWS, LANES), const)],
        out_shape=[jax.ShapeDtypeStruct((T, D), F32), jax.ShapeDtypeStruct((T * ROW_TILE, LANES), F32),
                   jax.ShapeDtypeStruct((SUBLANES, T), F32),
                   jax.ShapeDtypeStruct((ROUTER_ROWS, LANES), F32)],
        scratch_shapes=[pltpu.VMEM((ROUTER_ROWS, LANES), F32)],
        compiler_params=_cparams(("arbitrary",)),
        name="outproj_router",
    )(o_f, o_b, proj, att, x2, gnw, w_a, w_b, fnw, w_r, jnp.asarray(np.tri(tm, dtype=np.float32).T, BF16))


def _gather_start(src_hbm, dst_buf, sem, idx_ref, base, slot, n):
    for r in range(n):
        src = pl.multiple_of(idx_ref[base + r] * ROW_TILE, ROW_TILE)
        dst = pl.multiple_of((slot * n + r) * ROW_TILE, ROW_TILE)
        pltpu.make_async_copy(src_hbm.at[pl.ds(src, ROW_TILE)], dst_buf.at[pl.ds(dst, ROW_TILE)],
                              sem.at[slot]).start(priority=r % DMA_QUEUES)


def _gather_wait(src_hbm, dst_buf, sem, slot, n):
    dst = pl.multiple_of(slot * n * ROW_TILE, ROW_TILE)
    pltpu.make_async_copy(src_hbm.at[pl.ds(0, n * ROW_TILE)], dst_buf.at[pl.ds(dst, n * ROW_TILE)],
                          sem.at[slot]).wait()


def _experts_body(be_ref, src_ref, nused_ref,
                  h_hbm, wg_hbm, wu_hbm, wd_hbm, y_ref, xbuf, wg_buf, wu_buf, wd_buf, ws_ref, sem, wsem, *, bm):
    i = pl.program_id(0)
    n_blocks = pl.num_programs(0)
    n_used = nused_ref[0]
    slot = i % 2
    expert = be_ref[i]
    first = (i == 0) | (expert != be_ref[jnp.maximum(i - 1, 0)])

    def weight_copies(e, s):
        return [pltpu.make_async_copy(hbm.at[e], buf.at[s], wsem.at[s])
                for hbm, buf in ((wg_hbm, wg_buf), (wu_hbm, wu_buf), (wd_hbm, wd_buf))]

    def compute(prefetch_next):
        @pl.when(first)
        def _():
            cur = 1 - ws_ref[0]
            ws_ref[0] = cur
            for c in weight_copies(0, cur):
                c.wait()
            nxt = lax.while_loop(lambda j: (j < n_used) & (be_ref[jnp.minimum(j, n_blocks - 1)] == expert),
                                 lambda j: j + 1, i + 1)

            @pl.when(nxt < n_used)
            def _():
                for c in weight_copies(be_ref[jnp.minimum(nxt, n_blocks - 1)], 1 - cur):
                    c.start()

        ws = ws_ref[0]

        _gather_wait(h_hbm, xbuf, sem, slot, bm)
        x = _load_token_tiles(xbuf, slot * (bm * ROW_TILE), bm).astype(BF16)
        if prefetch_next:
            _gather_start(h_hbm, xbuf, sem, src_ref, (i + 1) * bm, 1 - slot, bm)
        gate = jnp.dot(x, wg_buf[ws].astype(BF16), preferred_element_type=F32)
        up = jnp.dot(x, wu_buf[ws].astype(BF16), preferred_element_type=F32)
        hid = (_silu(gate) * up).astype(BF16)
        _store_token_tiles(y_ref, jnp.dot(hid, wd_buf[ws].astype(BF16), preferred_element_type=F32))

    @pl.when((i == 0) & (n_used > 0))
    def _():
        ws_ref[0] = 1
        for c in weight_copies(be_ref[0], 0):
            c.start()
        _gather_start(h_hbm, xbuf, sem, src_ref, 0, 0, bm)

    @pl.when(i + 1 < n_used)
    def _():
        compute(True)

    @pl.when(i + 1 == n_used)
    def _():
        compute(False)

    @pl.when(i >= n_used)
    def _():
        y_ref[...] = jnp.zeros_like(y_ref)


def _experts(block_expert, src_tok, n_used, h2, w_gate, w_up, w_down, bm):
    D = ROW_TILE * LANES
    P = src_tok.shape[0]
    n_blocks = P // bm
    FF = w_gate.shape[-1]
    kern = functools.partial(_experts_body, bm=bm)
    hbm = pl.BlockSpec(memory_space=pl.ANY)
    return pl.pallas_call(
        kern,
        grid_spec=pltpu.PrefetchScalarGridSpec(
            num_scalar_prefetch=3,
            grid=(n_blocks,),
            in_specs=[hbm, hbm, hbm, hbm],
            out_specs=pl.BlockSpec((bm * ROW_TILE, LANES), lambda i, *_: (i, 0)),
            scratch_shapes=[
                pltpu.VMEM((2 * bm * ROW_TILE, LANES), F32),
                pltpu.VMEM((2, D, FF), F32), pltpu.VMEM((2, D, FF), F32), pltpu.VMEM((2, FF, D), F32),
                pltpu.SMEM((1,), jnp.int32),
                pltpu.SemaphoreType.DMA((2,)), pltpu.SemaphoreType.DMA((2,)),
            ],
        ),
        out_shape=jax.ShapeDtypeStruct((P * ROW_TILE, LANES), F32),
        compiler_params=_cparams(("arbitrary",)),
        name="moe_experts",
    )(block_expert, src_tok, n_used, h2, w_gate, w_up, w_down)


def _combine_body(dest_ref, y_hbm, xm_ref, r_ref, fw_ref, o_ref, ybuf, sem, *, tc):
    i = pl.program_id(0)
    n = pl.num_programs(0)
    slot = i % 2
    rows = TOP_K * tc

    def compute(prefetch_next):
        _gather_wait(y_hbm, ybuf, sem, slot, rows)
        route = r_ref[...].T
        y0 = _load_token_tiles(ybuf, slot * (rows * ROW_TILE), tc)
        y1 = _load_token_tiles(ybuf, (slot * rows + tc) * ROW_TILE, tc)
        if prefetch_next:
            _gather_start(y_hbm, ybuf, sem, dest_ref, (i + 1) * rows, 1 - slot, rows)
        xo = xm_ref[...] + route[:, 2:3] * y0 + route[:, 3:4] * y1
        o_ref[...] = xo * lax.rsqrt(jnp.mean(xo * xo, axis=-1, keepdims=True) + EPS) * fw_ref[...]

    @pl.when(i == 0)
    def _():
        _gather_start(y_hbm, ybuf, sem, dest_ref, 0, 0, rows)

    @pl.when(i + 1 < n)
    def _():
        compute(True)

    @pl.when(i + 1 == n)
    def _():
        compute(False)


def _combine(dest_blocked, y_buf, x_mid, route, final_w, tc):
    T, D = x_mid.shape
    kern = functools.partial(_combine_body, tc=tc)
    return pl.pallas_call(
        kern,
        grid_spec=pltpu.PrefetchScalarGridSpec(
            num_scalar_prefetch=1,
            grid=(T // tc,),
            in_specs=[
                pl.BlockSpec(memory_space=pl.ANY),
                pl.BlockSpec((tc, D), lambda i, d: (i, 0)),
                pl.BlockSpec((SUBLANES, tc), lambda i, d: (0, i)),
                pl.BlockSpec((1, D), lambda i, d: (0, 0)),
            ],
            out_specs=pl.BlockSpec((tc, D), lambda i, d: (i, 0)),
            scratch_shapes=[pltpu.VMEM((2 * TOP_K * tc * ROW_TILE, LANES), F32), pltpu.SemaphoreType.DMA((2,))],
        ),
        out_shape=jax.ShapeDtypeStruct((T, D), F32),
        compiler_params=_cparams(("arbitrary",)),
        name="moe_combine",
    )(dest_blocked, y_buf, x_mid, route, final_w.reshape(1, D))


def _layer(x2, B, S, norm_mix_w, w_in, conv_w, a_log, dt_bias, gdn_norm_w, q_norm_w, k_norm_w, w_out,
           norm_ffn_w, w_router_group, w_router_expert, w_gate, w_up, w_down, final_w):
    T, D = x2.shape
    tl = _tiles(B, S)
    o_gate = 4 * GDN_W
    o_qb = o_gate + 2 * GDN_CHAINS
    w_all = jnp.concatenate([w_in[:, :o_gate], w_in[:, o_qb:], w_in[:, o_gate:o_qb],
                             jnp.zeros((D, LANES - 2 * GDN_CHAINS), w_in.dtype)], axis=1).astype(BF16)

    rows = S // GRID_W
    rowp = np.repeat(np.arange(rows), GRID_W).astype(np.float64)
    colp = np.tile(np.arange(GRID_W), rows).astype(np.float64)
    axis_dims = ATT_D // 2
    inv_freq = ROPE_THETA ** (-np.arange(0, axis_dims, 2, dtype=np.float64) / axis_dims)
    ang = np.concatenate([rowp[:, None] * inv_freq, colp[:, None] * inv_freq], axis=-1)
    pair_sign = np.tile(np.array([-1.0, 1.0]), axis_dims)
    cosf = jnp.asarray(np.tile(np.repeat(np.cos(ang), 2, axis=1), (1, LANES // ATT_D)), F32)
    sinf = jnp.asarray(np.tile(np.repeat(np.sin(ang), 2, axis=1) * pair_sign, (1, LANES // ATT_D)), F32)
    q_gain = jnp.max(jnp.abs(q_norm_w)).astype(F32)
    k_gain = jnp.max(jnp.abs(k_norm_w)).astype(F32)
    score_bound = ATT_D ** 0.5 * q_gain * k_gain
    safe = (2.0 * score_bound <= SOFTMAX_SAFE_SPAN).astype(jnp.int32).reshape(1)
    bias = (-LOG2E * score_bound).reshape(1, 1)
    pair = lambda w: jnp.tile(w, LANES // ATT_D).reshape(1, LANES)

    proj, gate_logits, qh, kh, vh = _inproj(x2, norm_mix_w, w_all, pair(q_norm_w), pair(k_norm_w),
                                            cosf, sinf, bias, B, S, tl.proj_rows)

    conv_w8 = jnp.concatenate([conv_w, jnp.zeros((SUBLANES - CONV_W, conv_w.shape[1]), F32)], axis=0)
    gp = jnp.zeros((SUBLANES, LANES), F32)
    gp = gp.at[0, GDN_CHAINS:2 * GDN_CHAINS].set(jnp.exp(a_log.astype(F32)).reshape(-1))
    gp = gp.at[1, GDN_CHAINS:2 * GDN_CHAINS].set(dt_bias.astype(F32).reshape(-1))
    q_a, k_a, v_a, gates, gates_t = _gdn_prep(proj, gate_logits, conv_w8, gp, B, S, tl.prep_rows)
    u, wq, ak, dec = _gdn_chunk(q_a, k_a, v_a, gates, gates_t, tl.gdn_chunks)
    o_f, o_b = _gdn_scan(u, wq, ak, dec, B, S, tl.scan_chunks)

    att = _attention(safe, qh, kh, vh, B, S, tl.att_q, tl.att_k).reshape(T, ATT_Q)

    w_r32 = jnp.concatenate([w_router_group, w_router_expert,
                             jnp.zeros((D, LANES - N_GROUPS - N_EXPERTS), F32)], axis=1).astype(F32)
    w_r_hi = w_r32.astype(BF16)
    w_r = jnp.concatenate([w_r_hi, (w_r32 - w_r_hi.astype(F32)).astype(BF16)], axis=1)
    w_out_bf = w_out.astype(BF16)
    x_mid, h2, route_t, count_rows = _outproj(
        o_f, o_b, proj, att, x2, gdn_norm_w.reshape(1, GDN_D), w_out_bf, w_out_bf,
        norm_ffn_w.reshape(1, D), w_r, tl.proj_rows)

    bm = tl.moe_rows
    n_assign = T * TOP_K
    n_blocks = -(-(n_assign + N_EXPERTS * (bm - 1)) // bm)
    experts = jnp.arange(N_EXPERTS, dtype=jnp.int32)
    counts = count_rows[N_GROUPS:N_GROUPS + N_EXPERTS, 0].astype(jnp.int32)
    padded = (counts + bm - 1) // bm * bm
    pad_end = jnp.cumsum(padded)
    pad_start = pad_end - padded
    block_start = jnp.arange(n_blocks, dtype=jnp.int32) * bm
    done = (pad_end[None, :] <= block_start[:, None]).astype(jnp.int32)
    begun = (pad_start[None, :] <= block_start[:, None]).astype(jnp.int32)
    block_expert = jnp.minimum(jnp.sum(done, axis=1), N_EXPERTS - 1)
    n_used = (pad_end[-1:] // bm).astype(jnp.int32)
    seg_start = jnp.sum(done * padded[None, :], axis=1)
    seg_entry = jnp.sum(done * counts[None, :], axis=1)
    seg_count = jnp.sum(begun * counts[None, :], axis=1) - seg_entry
    e_rows = route_t[0:TOP_K].astype(jnp.int32)
    rank_rows = route_t[4:4 + TOP_K].astype(jnp.int32)
    dest = jnp.sum(jnp.where(e_rows[:, :, None] == experts, pad_start, 0), axis=-1) + rank_rows
    tokens = jnp.tile(jnp.arange(T, dtype=jnp.int32), TOP_K)
    _, compact = lax.sort((dest.reshape(-1), tokens), num_keys=1)
    row = block_start[:, None] + jnp.arange(bm, dtype=jnp.int32)[None, :]
    seg_row = row - seg_start[:, None]
    holds_token = seg_row < seg_count[:, None]
    entry = jnp.clip(seg_entry[:, None] + seg_row, 0, n_assign - 1)
    src_tok = jnp.where(holds_token, compact[entry], lax.rem(row, jnp.full_like(row, T))).reshape(-1)

    y_buf = _experts(block_expert, src_tok, n_used, h2, w_gate, w_up, w_down, bm)

    tc = tl.comb_rows
    dest_blocked = dest.reshape(TOP_K, T // tc, tc).transpose(1, 0, 2).reshape(-1)
    return _combine(dest_blocked, y_buf, x_mid, route_t, final_w, tc)


def kernel(x, norm_mix_w, w_in, conv_w, a_log, dt_bias, gdn_norm_w, q_norm_w, k_norm_w, w_out, norm_ffn_w,
           w_router_group, w_router_expert, w_gate, w_up, w_down, final_norm_w):
    B, S, D = x.shape
    depth = w_in.shape[0]
    assert depth == 1, "the final norm is fused into the last (only) layer's combine step"
    out = _layer(x.reshape(B * S, D), B, S, norm_mix_w[0], w_in[0], conv_w[0], a_log[0], dt_bias[0],
                 gdn_norm_w[0], q_norm_w[0], k_norm_w[0], w_out[0], norm_ffn_w[0], w_router_group[0],
                 w_router_expert[0], w_gate[0], w_up[0], w_down[0], final_norm_w)
    return out.reshape(B, S, D)
```

```python
import functools
import math
from typing import NamedTuple

import jax
import jax.numpy as jnp
import numpy as np
from jax import lax
from jax.experimental import pallas as pl
from jax.experimental.pallas import tpu as pltpu

F32 = jnp.float32
BF16 = jnp.bfloat16
EPS = 1e-6

GRID_W = 64
GDN_HEADS = 4
GDN_D = 128
CONV_W = 5
CHUNK = 64
ATT_HEADS = 8
ATT_KV_HEADS = 2
ATT_GROUP = ATT_HEADS // ATT_KV_HEADS
ATT_D = 64
ROPE_THETA = 10000.0
N_GROUPS = 4
EXPERTS_PER_GROUP = 8
N_EXPERTS = N_GROUPS * EXPERTS_PER_GROUP
TOP_K = 2

GDN_W = GDN_HEADS * GDN_D
GDN_CHAINS = 2 * GDN_HEADS
ATT_Q = ATT_HEADS * ATT_D
ATT_KV = ATT_KV_HEADS * ATT_D
LANES = 128
SUBLANES = 8
ATT_DP = LANES
ROUTER_ROWS = -(-(N_GROUPS + N_EXPERTS) // SUBLANES) * SUBLANES

COL_Z = 3 * GDN_W
COL_QB = COL_Z + GDN_W
COL_KB = COL_QB + ATT_Q
COL_VB = COL_KB + ATT_KV
COL_GATE = COL_VB + ATT_KV
D_PROJ = COL_GATE + LANES

VMEM_LIMIT = 56 * 1024 * 1024
LOG2E = math.log2(math.e)
SOFTMAX_SAFE_SPAN = 60.0


class Tiles(NamedTuple):
    proj_rows: int
    prep_rows: int
    gdn_chunks: int
    scan_chunks: int
    att_q: int
    att_k: int
    moe_rows: int
    comb_rows: int


def _tile(n, want):
    t = min(n, want)
    assert n % t == 0, (n, want)
    return t


def _tiles(B, S):
    T = B * S
    n_chunks = S // CHUNK
    return Tiles(proj_rows=_tile(T, 512), prep_rows=_tile(S, 512), gdn_chunks=_tile(n_chunks, 8),
                 scan_chunks=_tile(n_chunks, 8), att_q=_tile(S, 512), att_k=_tile(S, 2048),
                 moe_rows=256, comb_rows=_tile(T, 256))


def _cparams(sem):
    return pltpu.CompilerParams(dimension_semantics=sem, vmem_limit_bytes=VMEM_LIMIT)


def _silu(x):
    return x * jax.nn.sigmoid(x)


ROW_TILE = SUBLANES
DMA_QUEUES = 2


def _store_token_tiles(ref, x, base=0):
    rows, d = x.shape
    assert d == ROW_TILE * LANES
    for c in range(ROW_TILE):
        ref[pl.ds(base + c, rows, stride=ROW_TILE), :] = x[:, c * LANES:(c + 1) * LANES]


def _load_token_tiles(ref, base, rows):
    return jnp.concatenate([ref[pl.ds(base + c, rows, stride=ROW_TILE), :] for c in range(ROW_TILE)], axis=-1)


def _inproj_body(x_ref, nw_ref, w_ref, qw_ref, kw_ref, cos_ref, sin_ref, bias_ref,
                 o_ref, g_ref, qo_ref, ko_ref, vo_ref):
    x = x_ref[...]
    h = x * lax.rsqrt(jnp.mean(x * x, axis=-1, keepdims=True) + EPS) * nw_ref[...]
    acc = jnp.dot(h.astype(BF16), w_ref[...], preferred_element_type=F32)
    o_ref[...] = acc[:, :COL_QB]
    g_ref[...] = acc[:, COL_GATE:]
    _attn_heads(acc[:, COL_QB:COL_KB], acc[:, COL_KB:COL_VB], acc[:, COL_VB:COL_GATE], qw_ref[...], kw_ref[...],
                cos_ref[...], sin_ref[...], bias_ref[...], qo_ref, ko_ref, vo_ref)


def _inproj(x2, norm_w, w_all, qw, kw, cosf, sinf, bias, B, S, tm):
    T, D = x2.shape
    nr = S // tm
    const = lambda i: (0, 0)
    heads = lambda i: (i // nr, 0, i % nr, 0)
    return pl.pallas_call(
        _inproj_body,
        grid=(T // tm,),
        in_specs=[
            pl.BlockSpec((tm, D), lambda i: (i, 0)),
            pl.BlockSpec((1, D), const),
            pl.BlockSpec((D, D_PROJ), const),
            pl.BlockSpec((1, LANES), const),
            pl.BlockSpec((1, LANES), const),
            pl.BlockSpec((tm, LANES), lambda i: (i % nr, 0)),
            pl.BlockSpec((tm, LANES), lambda i: (i % nr, 0)),
            pl.BlockSpec((1, 1), const),
        ],
        out_specs=[
            pl.BlockSpec((tm, COL_QB), lambda i: (i, 0)),
            pl.BlockSpec((tm, LANES), lambda i: (i, 0)),
            pl.BlockSpec((None, ATT_HEADS, tm, ATT_DP), heads),
            pl.BlockSpec((None, ATT_KV_HEADS, tm, ATT_DP), heads),
            pl.BlockSpec((None, ATT_KV_HEADS, tm, ATT_DP), heads),
        ],
        out_shape=[
            jax.ShapeDtypeStruct((T, COL_QB), F32),
            jax.ShapeDtypeStruct((T, LANES), F32),
            jax.ShapeDtypeStruct((B, ATT_HEADS, S, ATT_DP), BF16),
            jax.ShapeDtypeStruct((B, ATT_KV_HEADS, S, ATT_DP), BF16),
            jax.ShapeDtypeStruct((B, ATT_KV_HEADS, S, ATT_DP), BF16),
        ],
        compiler_params=_cparams(("parallel",)),
        name="inproj",
    )(x2, norm_w.reshape(1, D), w_all, qw, kw, cosf, sinf, bias)


def _gdn_prep_body(cur_ref, prev_ref, next_ref, cw_ref, gin_ref, gp_ref,
                   q_ref, k_ref, v_ref, g_ref, gt_ref, ext_ref, *, tr):
    i = pl.program_id(1)
    nr = pl.num_programs(1)
    halo = prev_ref.shape[0]
    pad = CONV_W // 2
    ext_ref[0:halo, :] = jnp.where(i > 0, prev_ref[...], 0.0)
    ext_ref[halo:halo + tr, :] = cur_ref[...]
    ext_ref[halo + tr:2 * halo + tr, :] = jnp.where(i < nr - 1, next_ref[...], 0.0)
    acc = cw_ref[0:1, :] * ext_ref[pl.ds(halo - pad, tr), :]
    for j in range(1, CONV_W):
        acc = acc + cw_ref[j:j + 1, :] * ext_ref[pl.ds(halo - pad + j, tr), :]
    y = _silu(acc)
    for h in range(GDN_HEADS):
        for base, ref, scale in ((0, q_ref, GDN_D ** -0.5), (GDN_W, k_ref, 1.0)):
            t = y[:, base + h * GDN_D: base + (h + 1) * GDN_D]
            t = t * (lax.rsqrt(jnp.sum(t * t, axis=-1, keepdims=True) + EPS) * scale)
            ref[:, h * GDN_D:(h + 1) * GDN_D] = t
    v_ref[...] = y[:, 2 * GDN_W:]
    gin = gin_ref[...]
    lane = lax.broadcasted_iota(jnp.int32, gin.shape, 1)
    a = gin + gp_ref[1:2, :]
    softplus = jnp.maximum(a, 0.0) + jnp.log1p(jnp.exp(-jnp.abs(a)))
    g = jnp.where(lane < GDN_CHAINS, jax.nn.sigmoid(gin), -gp_ref[0:1, :] * softplus)
    g_ref[...] = g
    gt_ref[...] = g.T[0:2 * GDN_CHAINS, :]


def _gdn_prep(proj, gate_logits, conv_w8, gate_params, B, S, tr):
    T = B * S
    nr = S // tr
    C = 3 * GDN_W
    halo = SUBLANES
    rb = tr // halo
    n_halo = T // halo
    kern = functools.partial(_gdn_prep_body, tr=tr)
    out_sd = jax.ShapeDtypeStruct((T, GDN_W), F32)
    return pl.pallas_call(
        kern,
        grid=(B, nr),
        in_specs=[
            pl.BlockSpec((tr, C), lambda b, i: (b * nr + i, 0)),
            pl.BlockSpec((halo, C), lambda b, i: (jnp.maximum((b * nr + i) * rb - 1, 0), 0)),
            pl.BlockSpec((halo, C), lambda b, i: (jnp.minimum((b * nr + i + 1) * rb, n_halo - 1), 0)),
            pl.BlockSpec((SUBLANES, C), lambda b, i: (0, 0)),
            pl.BlockSpec((tr, LANES), lambda b, i: (b * nr + i, 0)),
            pl.BlockSpec((SUBLANES, LANES), lambda b, i: (0, 0)),
        ],
        out_specs=[
            pl.BlockSpec((tr, GDN_W), lambda b, i: (b * nr + i, 0)),
            pl.BlockSpec((tr, GDN_W), lambda b, i: (b * nr + i, 0)),
            pl.BlockSpec((tr, GDN_W), lambda b, i: (b * nr + i, 0)),
            pl.BlockSpec((tr, LANES), lambda b, i: (b * nr + i, 0)),
            pl.BlockSpec((2 * GDN_CHAINS, tr), lambda b, i: (0, b * nr + i)),
        ],
        out_shape=[out_sd, out_sd, out_sd, jax.ShapeDtypeStruct((T, LANES), F32),
                   jax.ShapeDtypeStruct((2 * GDN_CHAINS, T), F32)],
        scratch_shapes=[pltpu.VMEM((tr + 2 * halo, C), F32)],
        compiler_params=_cparams(("parallel", "parallel")),
        name="gdn_prep",
    )(proj, proj, proj, conv_w8, gate_logits, gate_params)


def _bdot(a, b):
    return jnp.dot(a.astype(BF16), b.astype(BF16), preferred_element_type=F32)


def _bdot_nt(a, b):
    return lax.dot_general(a.astype(BF16), b.astype(BF16), (((1,), (1,)), ((), ())),
                           preferred_element_type=F32)


def _gdn_chunk_body(q_ref, k_ref, v_ref, g_ref, gt_ref, u_ref, wq_ref, ak_ref, dec_ref, *, cb):
    C = CHUNK
    row = lax.broadcasted_iota(jnp.int32, (C, C), 0)
    col = lax.broadcasted_iota(jnp.int32, (C, C), 1)
    eye = (row == col).astype(F32)
    masks = ((row >= col, row > col), (row <= col, row < col))
    hi = lax.Precision.HIGHEST
    chains = []
    gate_rows = gt_ref[...]
    for c in range(cb):
        rs = slice(c * C, (c + 1) * C)
        gates = g_ref[rs, :]
        gates_t = gate_rows[:, rs]
        g_tot = jnp.sum(gates, axis=0, keepdims=True)
        tot_rows = jnp.sum(gates_t, axis=1, keepdims=True)
        dec_ref[c] = jnp.broadcast_to(jnp.exp(tot_rows[GDN_CHAINS:2 * GDN_CHAINS]), (GDN_CHAINS, LANES))
        for d in range(2):
            incl, strict = masks[d]
            cum = incl.astype(F32)
            gc_cols = jnp.dot(cum, gates, precision=hi, preferred_element_type=F32)
            gc_rows = lax.dot_general(gates_t, cum, (((1,), (1,)), ((), ())), precision=hi,
                                      preferred_element_type=F32)
            for h in range(GDN_HEADS):
                lane_b = d * GDN_HEADS + h
                lane_g = GDN_CHAINS + lane_b
                sl = slice(h * GDN_D, (h + 1) * GDN_D)
                q = q_ref[rs, sl]
                k = k_ref[rs, sl]
                v = v_ref[rs, sl]
                beta = gates[:, lane_b:lane_b + 1]
                gc_col = gc_cols[:, lane_g:lane_g + 1]
                gc_row = gc_rows[lane_g:lane_g + 1, :]
                g_last = g_tot[:, lane_g:lane_g + 1]
                decay = jnp.where(incl, jnp.exp(jnp.where(incl, gc_col - gc_row, 0.0)), 0.0)
                e_col = jnp.exp(gc_col)
                k_beta = k * beta
                kk = _bdot_nt(jnp.concatenate([k_beta, q], axis=0), k)
                neg_l = jnp.where(strict, -kk[:C] * decay, 0.0)
                attn = jnp.where(incl, kk[C:] * decay, 0.0)
                rhs = jnp.concatenate([v * beta, k_beta * e_col], axis=1).astype(BF16)
                wq_ref[d, c, C:2 * C, sl] = (q * e_col).astype(BF16)
                ak_ref[d, c, h, 0:C, :] = attn.astype(BF16)
                ak_ref[d, c, h, C:C + GDN_D, :] = (k * jnp.exp(g_last - gc_col)).T.astype(BF16)
                chains.append((d, c, rs, sl, neg_l, rhs))
    zs = [jnp.concatenate([ch[4], eye], axis=1) for ch in chains]
    keep_s = lax.broadcasted_iota(jnp.int32, (C, 2 * C), 1) >= C
    for _ in range(int(math.log2(C))):
        zs = [_bdot(z[:, :C], z) + jnp.where(keep_s, z, 0.0) for z in zs]
    for (d, c, rs, sl, _, rhs), z in zip(chains, zs):
        sol = jnp.dot(z[:, C:].astype(BF16), rhs, preferred_element_type=F32)
        u_ref[d, rs, sl] = sol[:, :GDN_D]
        wq_ref[d, c, 0:C, sl] = sol[:, GDN_D:].astype(BF16)


def _gdn_chunk(q, k, v, gates, gates_t, cb):
    T = q.shape[0]
    nc = T // CHUNK
    rows = cb * CHUNK
    kern = functools.partial(_gdn_chunk_body, cb=cb)
    wide = pl.BlockSpec((rows, GDN_W), lambda i: (i, 0))
    return pl.pallas_call(
        kern,
        grid=(nc // cb,),
        in_specs=[wide, wide, wide, pl.BlockSpec((rows, LANES), lambda i: (i, 0)),
                  pl.BlockSpec((2 * GDN_CHAINS, rows), lambda i: (0, i))],
        out_specs=[
            pl.BlockSpec((2, rows, GDN_W), lambda i: (0, i, 0)),
            pl.BlockSpec((2, cb, 2 * CHUNK, GDN_W), lambda i: (0, i, 0, 0)),
            pl.BlockSpec((2, cb, GDN_HEADS, CHUNK + GDN_D, CHUNK), lambda i: (0, i, 0, 0, 0)),
            pl.BlockSpec((cb, GDN_CHAINS, LANES), lambda i: (i, 0, 0)),
        ],
        out_shape=[
            jax.ShapeDtypeStruct((2, T, GDN_W), F32),
            jax.ShapeDtypeStruct((2, nc, 2 * CHUNK, GDN_W), BF16),
            jax.ShapeDtypeStruct((2, nc, GDN_HEADS, CHUNK + GDN_D, CHUNK), BF16),
            jax.ShapeDtypeStruct((nc, GDN_CHAINS, LANES), F32),
        ],
        compiler_params=_cparams(("parallel",)),
        name="gdn_chunk",
    )(q, k, v, gates, gates_t)


def _gdn_scan_body(uf_ref, wqf_ref, akf_ref, decf_ref, ub_ref, wqb_ref, akb_ref, decb_ref,
                   of_ref, ob_ref, state_ref, *, cs):
    @pl.when(pl.program_id(1) == 0)
    def _():
        state_ref[...] = jnp.zeros_like(state_ref)

    C = CHUNK
    refs = ((uf_ref, wqf_ref, akf_ref, decf_ref, of_ref), (ub_ref, wqb_ref, akb_ref, decb_ref, ob_ref))
    chains = [(d, h) for d in range(2) for h in range(GDN_HEADS)]
    st = [state_ref[j] for j in range(GDN_CHAINS)]
    for step in range(cs):
        cidx = (step, cs - 1 - step)
        ws = [jnp.dot(refs[d][1][cidx[d], :, h * GDN_D:(h + 1) * GDN_D], st[j].astype(BF16),
                      preferred_element_type=F32) for j, (d, h) in enumerate(chains)]
        vn = [(refs[d][0][cidx[d] * C:(cidx[d] + 1) * C, h * GDN_D:(h + 1) * GDN_D] - ws[j][:C]).astype(BF16)
              for j, (d, h) in enumerate(chains)]
        rr = [jnp.dot(refs[d][2][cidx[d], h], vn[j], preferred_element_type=F32)
              for j, (d, h) in enumerate(chains)]
        for j, (d, h) in enumerate(chains):
            c = cidx[d]
            refs[d][4][c * C:(c + 1) * C, h * GDN_D:(h + 1) * GDN_D] = (ws[j][C:] + rr[j][:C]).astype(BF16)
            st[j] = st[j] * refs[d][3][c, j:j + 1, :] + rr[j][C:]
    for j in range(GDN_CHAINS):
        state_ref[j] = st[j]


def _gdn_scan(u, wq, ak, dec, B, S, cs):
    T = B * S
    nb = S // (CHUNK * cs)
    rows = cs * CHUNK
    kern = functools.partial(_gdn_scan_body, cs=cs)
    fwd = lambda b, i: b * nb + i
    bwd = lambda b, i: b * nb + nb - 1 - i

    def specs(d, pos):
        return [
            pl.BlockSpec((None, rows, GDN_W), lambda b, i: (d, pos(b, i), 0)),
            pl.BlockSpec((None, cs, 2 * CHUNK, GDN_W), lambda b, i: (d, pos(b, i), 0, 0)),
            pl.BlockSpec((None, cs, GDN_HEADS, CHUNK + GDN_D, CHUNK), lambda b, i: (d, pos(b, i), 0, 0, 0)),
            pl.BlockSpec((cs, GDN_CHAINS, LANES), lambda b, i: (pos(b, i), 0, 0)),
        ]

    out_sd = jax.ShapeDtypeStruct((T, GDN_W), BF16)
    return pl.pallas_call(
        kern,
        grid=(B, nb),
        in_specs=specs(0, fwd) + specs(1, bwd),
        out_specs=[pl.BlockSpec((rows, GDN_W), lambda b, i: (fwd(b, i), 0)),
                   pl.BlockSpec((rows, GDN_W), lambda b, i: (bwd(b, i), 0))],
        out_shape=[out_sd, out_sd],
        scratch_shapes=[pltpu.VMEM((GDN_CHAINS, GDN_D, GDN_D), F32)],
        compiler_params=_cparams(("parallel", "arbitrary")),
        name="gdn_scan",
    )(u, wq, ak, dec, u, wq, ak, dec)


def _attn_heads(q, k, v, qw, kw, cos, sin, bias, qo_ref, ko_ref, vo_ref):
    rows = cos.shape[0]
    lane = lax.broadcasted_iota(jnp.int32, (rows, LANES), 1)
    even_lane = lane % 2 == 0
    first_head = lane < ATT_D
    extra_lane = lane == ATT_D
    mi = lax.broadcasted_iota(jnp.int32, (LANES, LANES), 0) // ATT_D
    mj = lax.broadcasted_iota(jnp.int32, (LANES, LANES), 1) // ATT_D
    head_mean = jnp.where(mi == mj, 1.0 / ATT_D, 0.0).astype(BF16)

    def norm_rope(x, w):
        sq = x * x
        hi = sq.astype(BF16)
        lo = (sq - hi.astype(F32)).astype(BF16)
        ms = (jnp.dot(hi, head_mean, preferred_element_type=F32)
              + jnp.dot(lo, head_mean, preferred_element_type=F32))
        y = x * lax.rsqrt(ms + EPS) * w
        partner = jnp.where(even_lane, pltpu.roll(y, LANES - 1, 1), pltpu.roll(y, 1, 1))
        return y * cos + partner * sin

    def split_heads(r, extra):
        tail = jnp.where(extra_lane, extra, 0.0)
        return (jnp.where(first_head, r, tail).astype(BF16),
                jnp.where(first_head, pltpu.roll(r, ATT_D, 1), tail).astype(BF16))

    scale = LOG2E * ATT_D ** -0.5
    for c in range(ATT_HEADS // 2):
        r = norm_rope(q[:, c * LANES:(c + 1) * LANES], qw) * scale
        qo_ref[2 * c], qo_ref[2 * c + 1] = split_heads(r, bias)
    for c in range(ATT_KV_HEADS // 2):
        r = norm_rope(k[:, c * LANES:(c + 1) * LANES], kw)
        ko_ref[2 * c], ko_ref[2 * c + 1] = split_heads(r, 1.0)
        vo_ref[2 * c], vo_ref[2 * c + 1] = split_heads(v[:, c * LANES:(c + 1) * LANES], 1.0)


def _attn_body(safe_ref, q_ref, k_ref, v_ref, o_ref, m_ref, acc_ref, *, tq, tk):
    safe = safe_ref[0] != 0
    nk = k_ref.shape[0] // tk
    q = q_ref[...].reshape(ATT_GROUP * tq, ATT_DP)
    acc_ref[...] = jnp.zeros_like(acc_ref)

    def scores(j):
        keys = pl.ds(pl.multiple_of(j * tk, tk), tk)
        s = lax.dot_general(q, k_ref[keys, :], (((1,), (1,)), ((), ())), preferred_element_type=F32)
        return s, v_ref[keys, :]

    @pl.when(safe)
    def _():
        def step(j, carry):
            s, v = scores(j)
            acc_ref[...] += jnp.dot(jnp.exp2(s).astype(BF16), v, preferred_element_type=F32)
            return carry
        lax.fori_loop(0, nk, step, 0)

    @pl.when(jnp.logical_not(safe))
    def _():
        m_ref[...] = jnp.full_like(m_ref, -jnp.inf)

        def step(j, carry):
            s, v = scores(j)
            m_prev = m_ref[...]
            m_new = jnp.maximum(m_prev, jnp.max(s, axis=-1, keepdims=True))
            p = jnp.exp2(s - m_new).astype(BF16)
            acc_ref[...] = jnp.exp2(m_prev - m_new) * acc_ref[...] + jnp.dot(p, v, preferred_element_type=F32)
            m_ref[...] = m_new
            return carry
        lax.fori_loop(0, nk, step, 0)

    acc = acc_ref[...]
    o = acc[:, :ATT_D] / acc[:, ATT_D:ATT_D + 1]
    for h in range(ATT_GROUP):
        o_ref[:, h * ATT_D:(h + 1) * ATT_D] = o[h * tq:(h + 1) * tq].astype(o_ref.dtype)


def _attention(safe, q, k, v, B, S, tq, tk):
    kern = functools.partial(_attn_body, tq=tq, tk=tk)
    gw = ATT_GROUP * ATT_D
    return pl.pallas_call(
        kern,
        grid_spec=pltpu.PrefetchScalarGridSpec(
            num_scalar_prefetch=1,
            grid=(B, ATT_KV_HEADS, S // tq),
            in_specs=[
                pl.BlockSpec((None, ATT_GROUP, tq, ATT_DP), lambda b, g, i, s: (b, g, i, 0)),
                pl.BlockSpec((None, None, S, ATT_DP), lambda b, g, i, s: (b, g, 0, 0)),
                pl.BlockSpec((None, None, S, ATT_DP), lambda b, g, i, s: (b, g, 0, 0)),
            ],
            out_specs=pl.BlockSpec((None, tq, gw), lambda b, g, i, s: (b, i, g)),
            scratch_shapes=[
                pltpu.VMEM((ATT_GROUP * tq, 1), F32),
                pltpu.VMEM((ATT_GROUP * tq, ATT_DP), F32),
            ],
        ),
        out_shape=jax.ShapeDtypeStruct((B, S, ATT_Q), BF16),
        compiler_params=_cparams(("parallel", "parallel", "parallel")),
        name="attention",
    )(safe, q, k, v)


def _outproj_body(of_ref, ob_ref, z_ref, att_ref, x_ref, gnw_ref, wa_ref, wb_ref,
                  fnw_ref, wr_ref, triu_ref, xm_ref, h_ref, rt_ref, c_ref, count_ref):
    o = of_ref[...].astype(F32) + ob_ref[...].astype(F32)
    z = z_ref[...]
    parts = []
    for h in range(GDN_HEADS):
        sl = slice(h * GDN_D, (h + 1) * GDN_D)
        t = o[:, sl]
        t = t * lax.rsqrt(jnp.mean(t * t, axis=-1, keepdims=True) + EPS) * gnw_ref[...]
        parts.append((t * _silu(z[:, sl])).astype(BF16))
    mix_a = jnp.concatenate(parts, axis=-1)
    xm = x_ref[...] + jnp.dot(mix_a, wa_ref[...], preferred_element_type=F32)
    xm = xm + jnp.dot(att_ref[...], wb_ref[...], preferred_element_type=F32)
    xm_ref[...] = xm
    hn = xm * lax.rsqrt(jnp.mean(xm * xm, axis=-1, keepdims=True) + EPS) * fnw_ref[...]
    _store_token_tiles(h_ref, hn)
    hn_hi = hn.astype(BF16)
    hn_lo = (hn - hn_hi.astype(F32)).astype(BF16)
    both = jnp.dot(hn_hi, wr_ref[...], preferred_element_type=F32)
    logits = (both[:, :LANES] + both[:, LANES:]
              + jnp.dot(hn_lo, wr_ref[:, :LANES], preferred_element_type=F32))

    lt = logits.T[0:ROUTER_ROWS, :]
    tok = lt.shape[1]
    slot = lax.broadcasted_iota(jnp.int32, (ROUTER_ROWS, tok), 0)
    big = jnp.int32(ROUTER_ROWS)
    neg = -jnp.inf

    def masked_top(vals, mask):
        m = jnp.max(jnp.where(mask, vals, neg), axis=0, keepdims=True)
        idx = jnp.min(jnp.where(mask & (vals == m), slot, big), axis=0, keepdims=True)
        return m, idx

    gmask = slot < N_GROUPS
    gmax, gsel = masked_top(lt, gmask)
    gp_top = 1.0 / jnp.sum(jnp.where(gmask, jnp.exp(lt - gmax), 0.0), axis=0, keepdims=True)
    lo = N_GROUPS + gsel * EXPERTS_PER_GROUP
    emask = (slot >= lo) & (slot < lo + EXPERTS_PER_GROUP)
    m1, i1 = masked_top(lt, emask)
    ex = jnp.where(emask, jnp.exp(lt - m1), 0.0)
    pf = ex / jnp.sum(ex, axis=0, keepdims=True)
    p1, _ = masked_top(pf, emask)
    p2, i2 = masked_top(pf, emask & (slot != i1))
    denom = p1 + p2
    g1 = gp_top * p1 / denom
    g2 = gp_top * p2 / denom
    e1 = (i1 - N_GROUPS).astype(F32)
    e2 = (i2 - N_GROUPS).astype(F32)

    @pl.when(pl.program_id(0) == 0)
    def _():
        count_ref[...] = jnp.zeros_like(count_ref)

    hit1 = slot == i1
    hit2 = slot == i2
    picked = jnp.where(hit1 | hit2, 1.0, 0.0)
    seen = jnp.dot(picked.astype(BF16), triu_ref[...], preferred_element_type=F32)
    counted = count_ref[:, 0:1]
    before = counted + seen - picked
    rank1 = jnp.sum(jnp.where(hit1, before, 0.0), axis=0, keepdims=True)
    rank2 = jnp.sum(jnp.where(hit2, before, 0.0), axis=0, keepdims=True)
    total = jnp.broadcast_to(counted + seen[:, tok - 1:tok], count_ref.shape)
    count_ref[...] = total
    c_ref[...] = total
    rt_ref[...] = jnp.concatenate([e1, e2, g1, g2, rank1, rank2, jnp.zeros((SUBLANES - 6, tok), F32)], axis=0)


def _outproj(o_f, o_b, proj, att, x2, gnw, w_a, w_b, fnw, w_r, tm):
    T, D = x2.shape
    row = lambda i: (i, 0)
    const = lambda i: (0, 0)
    return pl.pallas_call(
        _outproj_body,
        grid=(T // tm,),
        in_specs=[
            pl.BlockSpec((tm, GDN_W), row),
            pl.BlockSpec((tm, GDN_W), row),
            pl.BlockSpec((tm, GDN_W), lambda i: (i, COL_Z // GDN_W)),
            pl.BlockSpec((tm, ATT_Q), row),
            pl.BlockSpec((tm, D), row),
            pl.BlockSpec((1, GDN_D), const),
            pl.BlockSpec((GDN_W, D), const),
            pl.BlockSpec((ATT_Q, D), lambda i: (GDN_W // ATT_Q, 0)),
            pl.BlockSpec((1, D), const),
            pl.BlockSpec((D, 2 * LANES), const),
            pl.BlockSpec((tm, tm), const),
        ],
        out_specs=[pl.BlockSpec((tm, D), row), pl.BlockSpec((tm * ROW_TILE, LANES), row),
                   pl.BlockSpec((SUBLANES, tm), lambda i: (0, i)),
                   pl.BlockSpec((ROUTER_ROWS, LANES), const)],
        out_shape=[jax.ShapeDtypeStruct((T, D), F32), jax.ShapeDtypeStruct((T * ROW_TILE, LANES), F32),
                   jax.ShapeDtypeStruct((SUBLANES, T), F32),
                   jax.ShapeDtypeStruct((ROUTER_ROWS, LANES), F32)],
        scratch_shapes=[pltpu.VMEM((ROUTER_ROWS, LANES), F32)],
        compiler_params=_cparams(("arbitrary",)),
        name="outproj_router",
    )(o_f, o_b, proj, att, x2, gnw, w_a, w_b, fnw, w_r, jnp.asarray(np.tri(tm, dtype=np.float32).T, BF16))


def _gather_start(src_hbm, dst_buf, sem, row_of, slot, n):
    for r in range(n):
        src = pl.multiple_of(row_of(r) * ROW_TILE, ROW_TILE)
        dst = pl.multiple_of((slot * n + r) * ROW_TILE, ROW_TILE)
        pltpu.make_async_copy(src_hbm.at[pl.ds(src, ROW_TILE)], dst_buf.at[pl.ds(dst, ROW_TILE)],
                              sem.at[slot]).start(priority=r % DMA_QUEUES)


def _gather_wait(src_hbm, dst_buf, sem, slot, n):
    dst = pl.multiple_of(slot * n * ROW_TILE, ROW_TILE)
    pltpu.make_async_copy(src_hbm.at[pl.ds(0, n * ROW_TILE)], dst_buf.at[pl.ds(dst, n * ROW_TILE)],
                          sem.at[slot]).wait()


def _experts_body(be_ref, tok_ref, nused_ref, seg_start_ref, seg_count_ref, seg_entry_ref,
                  h_hbm, wg_hbm, wu_hbm, wd_hbm, y_ref, xbuf, wg_buf, wu_buf, wd_buf, ws_ref, sem, wsem, *, bm):
    i = pl.program_id(0)
    n_blocks = pl.num_programs(0)
    n_used = nused_ref[0]
    slot = i % 2
    expert = be_ref[i]
    first = (i == 0) | (expert != be_ref[jnp.maximum(i - 1, 0)])
    n_assign = tok_ref.shape[0]

    def block_rows(blk):
        first_off = blk * bm - seg_start_ref[blk]
        count = seg_count_ref[blk]
        entry = seg_entry_ref[blk]

        def row_of(r):
            off = first_off + r
            listed = tok_ref[jnp.minimum(entry + off, n_assign - 1)]
            return jnp.where(off < count, listed, jnp.minimum(off, n_assign // TOP_K - 1))
        return row_of

    def weight_copies(e, s):
        return [pltpu.make_async_copy(hbm.at[e], buf.at[s], wsem.at[s])
                for hbm, buf in ((wg_hbm, wg_buf), (wu_hbm, wu_buf), (wd_hbm, wd_buf))]

    def compute(prefetch_next):
        @pl.when(first)
        def _():
            cur = 1 - ws_ref[0]
            ws_ref[0] = cur
            for c in weight_copies(0, cur):
                c.wait()
            nxt = lax.while_loop(lambda j: (j < n_used) & (be_ref[jnp.minimum(j, n_blocks - 1)] == expert),
                                 lambda j: j + 1, i + 1)

            @pl.when(nxt < n_used)
            def _():
                for c in weight_copies(be_ref[jnp.minimum(nxt, n_blocks - 1)], 1 - cur):
                    c.start()

        ws = ws_ref[0]

        _gather_wait(h_hbm, xbuf, sem, slot, bm)
        x = _load_token_tiles(xbuf, slot * (bm * ROW_TILE), bm).astype(BF16)
        if prefetch_next:
            _gather_start(h_hbm, xbuf, sem, block_rows(i + 1), 1 - slot, bm)
        gate = jnp.dot(x, wg_buf[ws].astype(BF16), preferred_element_type=F32)
        up = jnp.dot(x, wu_buf[ws].astype(BF16), preferred_element_type=F32)
        hid = (_silu(gate) * up).astype(BF16)
        _store_token_tiles(y_ref, jnp.dot(hid, wd_buf[ws].astype(BF16), preferred_element_type=F32))

    @pl.when((i == 0) & (n_used > 0))
    def _():
        ws_ref[0] = 1
        for c in weight_copies(be_ref[0], 0):
            c.start()
        _gather_start(h_hbm, xbuf, sem, block_rows(0), 0, bm)

    @pl.when(i + 1 < n_used)
    def _():
        compute(True)

    @pl.when(i + 1 == n_used)
    def _():
        compute(False)

    @pl.when(i >= n_used)
    def _():
        y_ref[...] = jnp.zeros_like(y_ref)


def _experts(block_expert, tokens, n_used, seg_start, seg_count, seg_entry, h2, w_gate, w_up, w_down, bm):
    D = ROW_TILE * LANES
    n_blocks = block_expert.shape[0]
    P = n_blocks * bm
    FF = w_gate.shape[-1]
    kern = functools.partial(_experts_body, bm=bm)
    hbm = pl.BlockSpec(memory_space=pl.ANY)
    return pl.pallas_call(
        kern,
        grid_spec=pltpu.PrefetchScalarGridSpec(
            num_scalar_prefetch=6,
            grid=(n_blocks,),
            in_specs=[hbm, hbm, hbm, hbm],
            out_specs=pl.BlockSpec((bm * ROW_TILE, LANES), lambda i, *_: (i, 0)),
            scratch_shapes=[
                pltpu.VMEM((2 * bm * ROW_TILE, LANES), F32),
                pltpu.VMEM((2, D, FF), F32), pltpu.VMEM((2, D, FF), F32), pltpu.VMEM((2, FF, D), F32),
                pltpu.SMEM((1,), jnp.int32),
                pltpu.SemaphoreType.DMA((2,)), pltpu.SemaphoreType.DMA((2,)),
            ],
        ),
        out_shape=jax.ShapeDtypeStruct((P * ROW_TILE, LANES), F32),
        compiler_params=_cparams(("arbitrary",)),
        name="moe_experts",
    )(block_expert, tokens, n_used, seg_start, seg_count, seg_entry, h2, w_gate, w_up, w_down)


def _combine_body(dest_ref, y_hbm, xm_ref, r_ref, fw_ref, o_ref, ybuf, sem, *, tc):
    i = pl.program_id(0)
    n = pl.num_programs(0)
    slot = i % 2
    rows = TOP_K * tc

    def compute(prefetch_next):
        _gather_wait(y_hbm, ybuf, sem, slot, rows)
        route = r_ref[...].T
        y0 = _load_token_tiles(ybuf, slot * (rows * ROW_TILE), tc)
        y1 = _load_token_tiles(ybuf, (slot * rows + tc) * ROW_TILE, tc)
        if prefetch_next:
            _gather_start(y_hbm, ybuf, sem, lambda r: dest_ref[(i + 1) * rows + r], 1 - slot, rows)
        xo = xm_ref[...] + route[:, 2:3] * y0 + route[:, 3:4] * y1
        o_ref[...] = xo * lax.rsqrt(jnp.mean(xo * xo, axis=-1, keepdims=True) + EPS) * fw_ref[...]

    @pl.when(i == 0)
    def _():
        _gather_start(y_hbm, ybuf, sem, lambda r: dest_ref[r], 0, rows)

    @pl.when(i + 1 < n)
    def _():
        compute(True)

    @pl.when(i + 1 == n)
    def _():
        compute(False)


def _combine(dest_blocked, y_buf, x_mid, route, final_w, tc):
    T, D = x_mid.shape
    kern = functools.partial(_combine_body, tc=tc)
    return pl.pallas_call(
        kern,
        grid_spec=pltpu.PrefetchScalarGridSpec(
            num_scalar_prefetch=1,
            grid=(T // tc,),
            in_specs=[
                pl.BlockSpec(memory_space=pl.ANY),
                pl.BlockSpec((tc, D), lambda i, d: (i, 0)),
                pl.BlockSpec((SUBLANES, tc), lambda i, d: (0, i)),
                pl.BlockSpec((1, D), lambda i, d: (0, 0)),
            ],
            out_specs=pl.BlockSpec((tc, D), lambda i, d: (i, 0)),
            scratch_shapes=[pltpu.VMEM((2 * TOP_K * tc * ROW_TILE, LANES), F32), pltpu.SemaphoreType.DMA((2,))],
        ),
        out_shape=jax.ShapeDtypeStruct((T, D), F32),
        compiler_params=_cparams(("arbitrary",)),
        name="moe_combine",
    )(dest_blocked, y_buf, x_mid, route, final_w.reshape(1, D))


def _layer(x2, B, S, norm_mix_w, w_in, conv_w, a_log, dt_bias, gdn_norm_w, q_norm_w, k_norm_w, w_out,
           norm_ffn_w, w_router_group, w_router_expert, w_gate, w_up, w_down, final_w):
    T, D = x2.shape
    tl = _tiles(B, S)
    o_gate = 4 * GDN_W
    o_qb = o_gate + 2 * GDN_CHAINS
    w_all = jnp.concatenate([w_in[:, :o_gate], w_in[:, o_qb:], w_in[:, o_gate:o_qb],
                             jnp.zeros((D, LANES - 2 * GDN_CHAINS), w_in.dtype)], axis=1).astype(BF16)

    rows = S // GRID_W
    rowp = np.repeat(np.arange(rows), GRID_W).astype(np.float64)
    colp = np.tile(np.arange(GRID_W), rows).astype(np.float64)
    axis_dims = ATT_D // 2
    inv_freq = ROPE_THETA ** (-np.arange(0, axis_dims, 2, dtype=np.float64) / axis_dims)
    ang = np.concatenate([rowp[:, None] * inv_freq, colp[:, None] * inv_freq], axis=-1)
    pair_sign = np.tile(np.array([-1.0, 1.0]), axis_dims)
    cosf = jnp.asarray(np.tile(np.repeat(np.cos(ang), 2, axis=1), (1, LANES // ATT_D)), F32)
    sinf = jnp.asarray(np.tile(np.repeat(np.sin(ang), 2, axis=1) * pair_sign, (1, LANES // ATT_D)), F32)
    q_gain = jnp.max(jnp.abs(q_norm_w)).astype(F32)
    k_gain = jnp.max(jnp.abs(k_norm_w)).astype(F32)
    score_bound = ATT_D ** 0.5 * q_gain * k_gain
    safe = (2.0 * score_bound <= SOFTMAX_SAFE_SPAN).astype(jnp.int32).reshape(1)
    bias = (-LOG2E * score_bound).reshape(1, 1)
    pair = lambda w: jnp.tile(w, LANES // ATT_D).reshape(1, LANES)

    proj, gate_logits, qh, kh, vh = _inproj(x2, norm_mix_w, w_all, pair(q_norm_w), pair(k_norm_w),
                                            cosf, sinf, bias, B, S, tl.proj_rows)

    conv_w8 = jnp.concatenate([conv_w, jnp.zeros((SUBLANES - CONV_W, conv_w.shape[1]), F32)], axis=0)
    gp = jnp.zeros((SUBLANES, LANES), F32)
    gp = gp.at[0, GDN_CHAINS:2 * GDN_CHAINS].set(jnp.exp(a_log.astype(F32)).reshape(-1))
    gp = gp.at[1, GDN_CHAINS:2 * GDN_CHAINS].set(dt_bias.astype(F32).reshape(-1))
    q_a, k_a, v_a, gates, gates_t = _gdn_prep(proj, gate_logits, conv_w8, gp, B, S, tl.prep_rows)
    u, wq, ak, dec = _gdn_chunk(q_a, k_a, v_a, gates, gates_t, tl.gdn_chunks)
    o_f, o_b = _gdn_scan(u, wq, ak, dec, B, S, tl.scan_chunks)

    att = _attention(safe, qh, kh, vh, B, S, tl.att_q, tl.att_k).reshape(T, ATT_Q)

    w_r32 = jnp.concatenate([w_router_group, w_router_expert,
                             jnp.zeros((D, LANES - N_GROUPS - N_EXPERTS), F32)], axis=1).astype(F32)
    w_r_hi = w_r32.astype(BF16)
    w_r = jnp.concatenate([w_r_hi, (w_r32 - w_r_hi.astype(F32)).astype(BF16)], axis=1)
    w_out_bf = w_out.astype(BF16)
    x_mid, h2, route_t, count_rows = _outproj(
        o_f, o_b, proj, att, x2, gdn_norm_w.reshape(1, GDN_D), w_out_bf, w_out_bf,
        norm_ffn_w.reshape(1, D), w_r, tl.proj_rows)

    bm = tl.moe_rows
    n_assign = T * TOP_K
    n_blocks = -(-(n_assign + N_EXPERTS * (bm - 1)) // bm)
    experts = jnp.arange(N_EXPERTS, dtype=jnp.int32)
    counts = count_rows[N_GROUPS:N_GROUPS + N_EXPERTS, 0].astype(jnp.int32)
    padded = (counts + bm - 1) // bm * bm
    pad_end = jnp.cumsum(padded)
    pad_start = pad_end - padded
    block_start = jnp.arange(n_blocks, dtype=jnp.int32) * bm
    done = (pad_end[None, :] <= block_start[:, None]).astype(jnp.int32)
    begun = (pad_start[None, :] <= block_start[:, None]).astype(jnp.int32)
    block_expert = jnp.minimum(jnp.sum(done, axis=1), N_EXPERTS - 1)
    n_used = (pad_end[-1:] // bm).astype(jnp.int32)
    seg_start = jnp.sum(done * padded[None, :], axis=1)
    seg_entry = jnp.sum(done * counts[None, :], axis=1)
    seg_count = jnp.sum(begun * counts[None, :], axis=1) - seg_entry
    e_rows = route_t[0:TOP_K].astype(jnp.int32)
    rank_rows = route_t[4:4 + TOP_K].astype(jnp.int32)
    dest = jnp.sum(jnp.where(e_rows[:, :, None] == experts, pad_start, 0), axis=-1) + rank_rows
    tokens = jnp.tile(jnp.arange(T, dtype=jnp.int32), TOP_K)
    _, expert_ordered = lax.sort((dest.reshape(-1), tokens), num_keys=1)

    y_buf = _experts(block_expert, expert_ordered, n_used, seg_start, seg_count, seg_entry,
                     h2, w_gate, w_up, w_down, bm)

    tc = tl.comb_rows
    dest_blocked = dest.reshape(TOP_K, T // tc, tc).transpose(1, 0, 2).reshape(-1)
    return _combine(dest_blocked, y_buf, x_mid, route_t, final_w, tc)


def kernel(x, norm_mix_w, w_in, conv_w, a_log, dt_bias, gdn_norm_w, q_norm_w, k_norm_w, w_out, norm_ffn_w,
           w_router_group, w_router_expert, w_gate, w_up, w_down, final_norm_w):
    B, S, D = x.shape
    depth = w_in.shape[0]
    assert depth == 1, "the final norm is fused into the last (only) layer's combine step"
    out = _layer(x.reshape(B * S, D), B, S, norm_mix_w[0], w_in[0], conv_w[0], a_log[0], dt_bias[0],
                 gdn_norm_w[0], q_norm_w[0], k_norm_w[0], w_out[0], norm_ffn_w[0], w_router_group[0],
                 w_router_expert[0], w_gate[0], w_up[0], w_down[0], final_norm_w)
    return out.reshape(B, S, D)
```

```python
import functools
import math
from typing import NamedTuple

import jax
import jax.numpy as jnp
import numpy as np
from jax import lax
from jax.experimental import pallas as pl
from jax.experimental.pallas import tpu as pltpu

F32 = jnp.float32
BF16 = jnp.bfloat16
EPS = 1e-6

GRID_W = 64
GDN_HEADS = 4
GDN_D = 128
CONV_W = 5
CHUNK = 64
ATT_HEADS = 8
ATT_KV_HEADS = 2
ATT_GROUP = ATT_HEADS // ATT_KV_HEADS
ATT_D = 64
ROPE_THETA = 10000.0
N_GROUPS = 4
EXPERTS_PER_GROUP = 8
N_EXPERTS = N_GROUPS * EXPERTS_PER_GROUP
TOP_K = 2

GDN_W = GDN_HEADS * GDN_D
GDN_CHAINS = 2 * GDN_HEADS
ATT_Q = ATT_HEADS * ATT_D
ATT_KV = ATT_KV_HEADS * ATT_D
LANES = 128
SUBLANES = 8
ATT_DP = LANES
ROUTER_ROWS = -(-(N_GROUPS + N_EXPERTS) // SUBLANES) * SUBLANES

COL_Z = 3 * GDN_W
COL_QB = COL_Z + GDN_W
COL_KB = COL_QB + ATT_Q
COL_VB = COL_KB + ATT_KV
COL_GATE = COL_VB + ATT_KV
D_PROJ = COL_GATE + LANES

VMEM_LIMIT = 56 * 1024 * 1024
LOG2E = math.log2(math.e)
SOFTMAX_SAFE_SPAN = 60.0


class Tiles(NamedTuple):
    proj_rows: int
    prep_rows: int
    gdn_chunks: int
    scan_chunks: int
    att_q: int
    att_k: int
    moe_rows: int
    comb_rows: int


def _tile(n, want):
    t = min(n, want)
    assert n % t == 0, (n, want)
    return t


def _tiles(B, S):
    T = B * S
    n_chunks = S // CHUNK
    return Tiles(proj_rows=_tile(T, 512), prep_rows=_tile(S, 512), gdn_chunks=_tile(n_chunks, 8),
                 scan_chunks=_tile(n_chunks, 8), att_q=_tile(S, 512), att_k=_tile(S, 2048),
                 moe_rows=256, comb_rows=_tile(T, 256))


def _cparams(sem):
    return pltpu.CompilerParams(dimension_semantics=sem, vmem_limit_bytes=VMEM_LIMIT)


def _silu(x):
    return x * jax.nn.sigmoid(x)


ROW_TILE = SUBLANES
DMA_QUEUES = 2


def _store_token_tiles(ref, x, base=0):
    rows, d = x.shape
    assert d == ROW_TILE * LANES
    for c in range(ROW_TILE):
        ref[pl.ds(base + c, rows, stride=ROW_TILE), :] = x[:, c * LANES:(c + 1) * LANES]


def _load_token_tiles(ref, base, rows):
    return jnp.concatenate([ref[pl.ds(base + c, rows, stride=ROW_TILE), :] for c in range(ROW_TILE)], axis=-1)


def _inproj_body(x_ref, nw_ref, w_ref, qw_ref, kw_ref, cos_ref, sin_ref, bias_ref,
                 o_ref, g_ref, qo_ref, ko_ref, vo_ref):
    x = x_ref[...]
    h = x * lax.rsqrt(jnp.mean(x * x, axis=-1, keepdims=True) + EPS) * nw_ref[...]
    acc = jnp.dot(h.astype(BF16), w_ref[...], preferred_element_type=F32)
    o_ref[...] = acc[:, :COL_QB]
    g_ref[...] = acc[:, COL_GATE:]
    _attn_heads(acc[:, COL_QB:COL_KB], acc[:, COL_KB:COL_VB], acc[:, COL_VB:COL_GATE], qw_ref[...], kw_ref[...],
                cos_ref[...], sin_ref[...], bias_ref[...], qo_ref, ko_ref, vo_ref)


def _inproj(x2, norm_w, w_all, qw, kw, cosf, sinf, bias, B, S, tm):
    T, D = x2.shape
    nr = S // tm
    const = lambda i: (0, 0)
    heads = lambda i: (i // nr, 0, i % nr, 0)
    return pl.pallas_call(
        _inproj_body,
        grid=(T // tm,),
        in_specs=[
            pl.BlockSpec((tm, D), lambda i: (i, 0)),
            pl.BlockSpec((1, D), const),
            pl.BlockSpec((D, D_PROJ), const),
            pl.BlockSpec((1, LANES), const),
            pl.BlockSpec((1, LANES), const),
            pl.BlockSpec((tm, LANES), lambda i: (i % nr, 0)),
            pl.BlockSpec((tm, LANES), lambda i: (i % nr, 0)),
            pl.BlockSpec((1, 1), const),
        ],
        out_specs=[
            pl.BlockSpec((tm, COL_QB), lambda i: (i, 0)),
            pl.BlockSpec((tm, LANES), lambda i: (i, 0)),
            pl.BlockSpec((None, ATT_HEADS, tm, ATT_DP), heads),
            pl.BlockSpec((None, ATT_KV_HEADS, tm, ATT_DP), heads),
            pl.BlockSpec((None, ATT_KV_HEADS, tm, ATT_DP), heads),
        ],
        out_shape=[
            jax.ShapeDtypeStruct((T, COL_QB), F32),
            jax.ShapeDtypeStruct((T, LANES), F32),
            jax.ShapeDtypeStruct((B, ATT_HEADS, S, ATT_DP), BF16),
            jax.ShapeDtypeStruct((B, ATT_KV_HEADS, S, ATT_DP), BF16),
            jax.ShapeDtypeStruct((B, ATT_KV_HEADS, S, ATT_DP), BF16),
        ],
        compiler_params=_cparams(("parallel",)),
        name="inproj",
    )(x2, norm_w.reshape(1, D), w_all, qw, kw, cosf, sinf, bias)


def _gdn_prep_body(cur_ref, prev_ref, next_ref, cw_ref, gin_ref, gp_ref,
                   q_ref, k_ref, v_ref, g_ref, gt_ref, ext_ref, *, tr):
    i = pl.program_id(1)
    nr = pl.num_programs(1)
    halo = prev_ref.shape[0]
    pad = CONV_W // 2
    ext_ref[0:halo, :] = jnp.where(i > 0, prev_ref[...], 0.0)
    ext_ref[halo:halo + tr, :] = cur_ref[...]
    ext_ref[halo + tr:2 * halo + tr, :] = jnp.where(i < nr - 1, next_ref[...], 0.0)
    acc = cw_ref[0:1, :] * ext_ref[pl.ds(halo - pad, tr), :]
    for j in range(1, CONV_W):
        acc = acc + cw_ref[j:j + 1, :] * ext_ref[pl.ds(halo - pad + j, tr), :]
    y = _silu(acc)
    for h in range(GDN_HEADS):
        for base, ref, scale in ((0, q_ref, GDN_D ** -0.5), (GDN_W, k_ref, 1.0)):
            t = y[:, base + h * GDN_D: base + (h + 1) * GDN_D]
            t = t * (lax.rsqrt(jnp.sum(t * t, axis=-1, keepdims=True) + EPS) * scale)
            ref[:, h * GDN_D:(h + 1) * GDN_D] = t
    v_ref[...] = y[:, 2 * GDN_W:]
    gin = gin_ref[...]
    lane = lax.broadcasted_iota(jnp.int32, gin.shape, 1)
    a = gin + gp_ref[1:2, :]
    softplus = jnp.maximum(a, 0.0) + jnp.log1p(jnp.exp(-jnp.abs(a)))
    g = jnp.where(lane < GDN_CHAINS, jax.nn.sigmoid(gin), -gp_ref[0:1, :] * softplus)
    g_ref[...] = g
    gt_ref[...] = g.T[0:2 * GDN_CHAINS, :]


def _gdn_prep(proj, gate_logits, conv_w8, gate_params, B, S, tr):
    T = B * S
    nr = S // tr
    C = 3 * GDN_W
    halo = SUBLANES
    rb = tr // halo
    n_halo = T // halo
    kern = functools.partial(_gdn_prep_body, tr=tr)
    out_sd = jax.ShapeDtypeStruct((T, GDN_W), F32)
    return pl.pallas_call(
        kern,
        grid=(B, nr),
        in_specs=[
            pl.BlockSpec((tr, C), lambda b, i: (b * nr + i, 0)),
            pl.BlockSpec((halo, C), lambda b, i: (jnp.maximum((b * nr + i) * rb - 1, 0), 0)),
            pl.BlockSpec((halo, C), lambda b, i: (jnp.minimum((b * nr + i + 1) * rb, n_halo - 1), 0)),
            pl.BlockSpec((SUBLANES, C), lambda b, i: (0, 0)),
            pl.BlockSpec((tr, LANES), lambda b, i: (b * nr + i, 0)),
            pl.BlockSpec((SUBLANES, LANES), lambda b, i: (0, 0)),
        ],
        out_specs=[
            pl.BlockSpec((tr, GDN_W), lambda b, i: (b * nr + i, 0)),
            pl.BlockSpec((tr, GDN_W), lambda b, i: (b * nr + i, 0)),
            pl.BlockSpec((tr, GDN_W), lambda b, i: (b * nr + i, 0)),
            pl.BlockSpec((tr, LANES), lambda b, i: (b * nr + i, 0)),
            pl.BlockSpec((2 * GDN_CHAINS, tr), lambda b, i: (0, b * nr + i)),
        ],
        out_shape=[out_sd, out_sd, out_sd, jax.ShapeDtypeStruct((T, LANES), F32),
                   jax.ShapeDtypeStruct((2 * GDN_CHAINS, T), F32)],
        scratch_shapes=[pltpu.VMEM((tr + 2 * halo, C), F32)],
        compiler_params=_cparams(("parallel", "parallel")),
        name="gdn_prep",
    )(proj, proj, proj, conv_w8, gate_logits, gate_params)


def _bdot(a, b):
    return jnp.dot(a.astype(BF16), b.astype(BF16), preferred_element_type=F32)


def _bdot_nt(a, b):
    return lax.dot_general(a.astype(BF16), b.astype(BF16), (((1,), (1,)), ((), ())),
                           preferred_element_type=F32)


def _gdn_chunk_body(q_ref, k_ref, v_ref, g_ref, gt_ref, u_ref, wq_ref, ak_ref, dec_ref, *, cb):
    C = CHUNK
    row = lax.broadcasted_iota(jnp.int32, (C, C), 0)
    col = lax.broadcasted_iota(jnp.int32, (C, C), 1)
    eye = (row == col).astype(F32)
    masks = ((row >= col, row > col), (row <= col, row < col))
    hi = lax.Precision.HIGHEST
    chains = []
    gate_rows = gt_ref[...]
    for c in range(cb):
        rs = slice(c * C, (c + 1) * C)
        gates = g_ref[rs, :]
        gates_t = gate_rows[:, rs]
        g_tot = jnp.sum(gates, axis=0, keepdims=True)
        tot_rows = jnp.sum(gates_t, axis=1, keepdims=True)
        dec_ref[c] = jnp.broadcast_to(jnp.exp(tot_rows[GDN_CHAINS:2 * GDN_CHAINS]), (GDN_CHAINS, LANES))
        for d in range(2):
            incl, strict = masks[d]
            cum = incl.astype(F32)
            gc_cols = jnp.dot(cum, gates, precision=hi, preferred_element_type=F32)
            gc_rows = lax.dot_general(gates_t, cum, (((1,), (1,)), ((), ())), precision=hi,
                                      preferred_element_type=F32)
            for h in range(GDN_HEADS):
                lane_b = d * GDN_HEADS + h
                lane_g = GDN_CHAINS + lane_b
                sl = slice(h * GDN_D, (h + 1) * GDN_D)
                q = q_ref[rs, sl]
                k = k_ref[rs, sl]
                v = v_ref[rs, sl]
                beta = gates[:, lane_b:lane_b + 1]
                gc_col = gc_cols[:, lane_g:lane_g + 1]
                gc_row = gc_rows[lane_g:lane_g + 1, :]
                g_last = g_tot[:, lane_g:lane_g + 1]
                decay = jnp.where(incl, jnp.exp(jnp.where(incl, gc_col - gc_row, 0.0)), 0.0)
                e_col = jnp.exp(gc_col)
                k_beta = k * beta
                kk = _bdot_nt(jnp.concatenate([k_beta, q], axis=0), k)
                neg_l = jnp.where(strict, -kk[:C] * decay, 0.0)
                attn = jnp.where(incl, kk[C:] * decay, 0.0)
                rhs = jnp.concatenate([v * beta, k_beta * e_col], axis=1).astype(BF16)
                wq_ref[d, c, C:2 * C, sl] = (q * e_col).astype(BF16)
                ak_ref[d, c, h, 0:C, :] = attn.astype(BF16)
                ak_ref[d, c, h, C:C + GDN_D, :] = (k * jnp.exp(g_last - gc_col)).T.astype(BF16)
                chains.append((d, c, rs, sl, neg_l, rhs))
    zs = [jnp.concatenate([ch[4], eye], axis=1) for ch in chains]
    keep_s = lax.broadcasted_iota(jnp.int32, (C, 2 * C), 1) >= C
    for _ in range(int(math.log2(C))):
        zs = [_bdot(z[:, :C], z) + jnp.where(keep_s, z, 0.0) for z in zs]
    for (d, c, rs, sl, _, rhs), z in zip(chains, zs):
        sol = jnp.dot(z[:, C:].astype(BF16), rhs, preferred_element_type=F32)
        u_ref[d, rs, sl] = sol[:, :GDN_D]
        wq_ref[d, c, 0:C, sl] = sol[:, GDN_D:].astype(BF16)


def _gdn_chunk(q, k, v, gates, gates_t, cb):
    T = q.shape[0]
    nc = T // CHUNK
    rows = cb * CHUNK
    kern = functools.partial(_gdn_chunk_body, cb=cb)
    wide = pl.BlockSpec((rows, GDN_W), lambda i: (i, 0))
    return pl.pallas_call(
        kern,
        grid=(nc // cb,),
        in_specs=[wide, wide, wide, pl.BlockSpec((rows, LANES), lambda i: (i, 0)),
                  pl.BlockSpec((2 * GDN_CHAINS, rows), lambda i: (0, i))],
        out_specs=[
            pl.BlockSpec((2, rows, GDN_W), lambda i: (0, i, 0)),
            pl.BlockSpec((2, cb, 2 * CHUNK, GDN_W), lambda i: (0, i, 0, 0)),
            pl.BlockSpec((2, cb, GDN_HEADS, CHUNK + GDN_D, CHUNK), lambda i: (0, i, 0, 0, 0)),
            pl.BlockSpec((cb, GDN_CHAINS, LANES), lambda i: (i, 0, 0)),
        ],
        out_shape=[
            jax.ShapeDtypeStruct((2, T, GDN_W), F32),
            jax.ShapeDtypeStruct((2, nc, 2 * CHUNK, GDN_W), BF16),
            jax.ShapeDtypeStruct((2, nc, GDN_HEADS, CHUNK + GDN_D, CHUNK), BF16),
            jax.ShapeDtypeStruct((nc, GDN_CHAINS, LANES), F32),
        ],
        compiler_params=_cparams(("parallel",)),
        name="gdn_chunk",
    )(q, k, v, gates, gates_t)


def _gdn_scan_body(uf_ref, wqf_ref, akf_ref, decf_ref, ub_ref, wqb_ref, akb_ref, decb_ref,
                   of_ref, ob_ref, state_ref, *, cs):
    @pl.when(pl.program_id(1) == 0)
    def _():
        state_ref[...] = jnp.zeros_like(state_ref)

    C = CHUNK
    refs = ((uf_ref, wqf_ref, akf_ref, decf_ref, of_ref), (ub_ref, wqb_ref, akb_ref, decb_ref, ob_ref))
    chains = [(d, h) for d in range(2) for h in range(GDN_HEADS)]
    st = [state_ref[j] for j in range(GDN_CHAINS)]
    for step in range(cs):
        cidx = (step, cs - 1 - step)
        ws = [jnp.dot(refs[d][1][cidx[d], :, h * GDN_D:(h + 1) * GDN_D], st[j].astype(BF16),
                      preferred_element_type=F32) for j, (d, h) in enumerate(chains)]
        vn = [(refs[d][0][cidx[d] * C:(cidx[d] + 1) * C, h * GDN_D:(h + 1) * GDN_D] - ws[j][:C]).astype(BF16)
              for j, (d, h) in enumerate(chains)]
        rr = [jnp.dot(refs[d][2][cidx[d], h], vn[j], preferred_element_type=F32)
              for j, (d, h) in enumerate(chains)]
        for j, (d, h) in enumerate(chains):
            c = cidx[d]
            refs[d][4][c * C:(c + 1) * C, h * GDN_D:(h + 1) * GDN_D] = (ws[j][C:] + rr[j][:C]).astype(BF16)
            st[j] = st[j] * refs[d][3][c, j:j + 1, :] + rr[j][C:]
    for j in range(GDN_CHAINS):
        state_ref[j] = st[j]


def _gdn_scan(u, wq, ak, dec, B, S, cs):
    T = B * S
    nb = S // (CHUNK * cs)
    rows = cs * CHUNK
    kern = functools.partial(_gdn_scan_body, cs=cs)
    fwd = lambda b, i: b * nb + i
    bwd = lambda b, i: b * nb + nb - 1 - i

    def specs(d, pos):
        return [
            pl.BlockSpec((None, rows, GDN_W), lambda b, i: (d, pos(b, i), 0)),
            pl.BlockSpec((None, cs, 2 * CHUNK, GDN_W), lambda b, i: (d, pos(b, i), 0, 0)),
            pl.BlockSpec((None, cs, GDN_HEADS, CHUNK + GDN_D, CHUNK), lambda b, i: (d, pos(b, i), 0, 0, 0)),
            pl.BlockSpec((cs, GDN_CHAINS, LANES), lambda b, i: (pos(b, i), 0, 0)),
        ]

    out_sd = jax.ShapeDtypeStruct((T, GDN_W), BF16)
    return pl.pallas_call(
        kern,
        grid=(B, nb),
        in_specs=specs(0, fwd) + specs(1, bwd),
        out_specs=[pl.BlockSpec((rows, GDN_W), lambda b, i: (fwd(b, i), 0)),
                   pl.BlockSpec((rows, GDN_W), lambda b, i: (bwd(b, i), 0))],
        out_shape=[out_sd, out_sd],
        scratch_shapes=[pltpu.VMEM((GDN_CHAINS, GDN_D, GDN_D), F32)],
        compiler_params=_cparams(("parallel", "arbitrary")),
        name="gdn_scan",
    )(u, wq, ak, dec, u, wq, ak, dec)


def _attn_heads(q, k, v, qw, kw, cos, sin, bias, qo_ref, ko_ref, vo_ref):
    rows = cos.shape[0]
    lane = lax.broadcasted_iota(jnp.int32, (rows, LANES), 1)
    even_lane = lane % 2 == 0
    first_head = lane < ATT_D
    extra_lane = lane == ATT_D
    mi = lax.broadcasted_iota(jnp.int32, (LANES, LANES), 0) // ATT_D
    mj = lax.broadcasted_iota(jnp.int32, (LANES, LANES), 1) // ATT_D
    head_mean = jnp.where(mi == mj, 1.0 / ATT_D, 0.0).astype(BF16)

    def norm_rope(x, w):
        sq = x * x
        hi = sq.astype(BF16)
        lo = (sq - hi.astype(F32)).astype(BF16)
        ms = (jnp.dot(hi, head_mean, preferred_element_type=F32)
              + jnp.dot(lo, head_mean, preferred_element_type=F32))
        y = x * lax.rsqrt(ms + EPS) * w
        partner = jnp.where(even_lane, pltpu.roll(y, LANES - 1, 1), pltpu.roll(y, 1, 1))
        return y * cos + partner * sin

    def split_heads(r, extra):
        tail = jnp.where(extra_lane, extra, 0.0)
        return (jnp.where(first_head, r, tail).astype(BF16),
                jnp.where(first_head, pltpu.roll(r, ATT_D, 1), tail).astype(BF16))

    scale = LOG2E * ATT_D ** -0.5
    for c in range(ATT_HEADS // 2):
        r = norm_rope(q[:, c * LANES:(c + 1) * LANES], qw) * scale
        qo_ref[2 * c], qo_ref[2 * c + 1] = split_heads(r, bias)
    for c in range(ATT_KV_HEADS // 2):
        r = norm_rope(k[:, c * LANES:(c + 1) * LANES], kw)
        ko_ref[2 * c], ko_ref[2 * c + 1] = split_heads(r, 1.0)
        vo_ref[2 * c], vo_ref[2 * c + 1] = split_heads(v[:, c * LANES:(c + 1) * LANES], 1.0)


def _attn_body(safe_ref, q_ref, k_ref, v_ref, o_ref, m_ref, acc_ref, *, tq, tk):
    safe = safe_ref[0] != 0
    nk = k_ref.shape[0] // tk
    q = q_ref[...].reshape(ATT_GROUP * tq, ATT_DP)
    acc_ref[...] = jnp.zeros_like(acc_ref)

    def scores(j):
        keys = pl.ds(pl.multiple_of(j * tk, tk), tk)
        s = lax.dot_general(q, k_ref[keys, :], (((1,), (1,)), ((), ())), preferred_element_type=F32)
        return s, v_ref[keys, :]

    @pl.when(safe)
    def _():
        def step(j, carry):
            s, v = scores(j)
            acc_ref[...] += jnp.dot(jnp.exp2(s).astype(BF16), v, preferred_element_type=F32)
            return carry
        lax.fori_loop(0, nk, step, 0)

    @pl.when(jnp.logical_not(safe))
    def _():
        m_ref[...] = jnp.full_like(m_ref, -jnp.inf)

        def step(j, carry):
            s, v = scores(j)
            m_prev = m_ref[...]
            m_new = jnp.maximum(m_prev, jnp.max(s, axis=-1, keepdims=True))
            p = jnp.exp2(s - m_new).astype(BF16)
            acc_ref[...] = jnp.exp2(m_prev - m_new) * acc_ref[...] + jnp.dot(p, v, preferred_element_type=F32)
            m_ref[...] = m_new
            return carry
        lax.fori_loop(0, nk, step, 0)

    acc = acc_ref[...]
    o = acc[:, :ATT_D] / acc[:, ATT_D:ATT_D + 1]
    for h in range(ATT_GROUP):
        o_ref[:, h * ATT_D:(h + 1) * ATT_D] = o[h * tq:(h + 1) * tq].astype(o_ref.dtype)


def _attention(safe, q, k, v, B, S, tq, tk):
    kern = functools.partial(_attn_body, tq=tq, tk=tk)
    gw = ATT_GROUP * ATT_D
    return pl.pallas_call(
        kern,
        grid_spec=pltpu.PrefetchScalarGridSpec(
            num_scalar_prefetch=1,
            grid=(B, ATT_KV_HEADS, S // tq),
            in_specs=[
                pl.BlockSpec((None, ATT_GROUP, tq, ATT_DP), lambda b, g, i, s: (b, g, i, 0)),
                pl.BlockSpec((None, None, S, ATT_DP), lambda b, g, i, s: (b, g, 0, 0)),
                pl.BlockSpec((None, None, S, ATT_DP), lambda b, g, i, s: (b, g, 0, 0)),
            ],
            out_specs=pl.BlockSpec((None, tq, gw), lambda b, g, i, s: (b, i, g)),
            scratch_shapes=[
                pltpu.VMEM((ATT_GROUP * tq, 1), F32),
                pltpu.VMEM((ATT_GROUP * tq, ATT_DP), F32),
            ],
        ),
        out_shape=jax.ShapeDtypeStruct((B, S, ATT_Q), BF16),
        compiler_params=_cparams(("parallel", "parallel", "parallel")),
        name="attention",
    )(safe, q, k, v)


def _outproj_body(of_ref, ob_ref, z_ref, att_ref, x_ref, gnw_ref, wa_ref, wb_ref,
                  fnw_ref, wr_ref, triu_ref, xm_ref, h_ref, rt_ref, c_ref, count_ref):
    o = of_ref[...].astype(F32) + ob_ref[...].astype(F32)
    z = z_ref[...]
    parts = []
    for h in range(GDN_HEADS):
        sl = slice(h * GDN_D, (h + 1) * GDN_D)
        t = o[:, sl]
        t = t * lax.rsqrt(jnp.mean(t * t, axis=-1, keepdims=True) + EPS) * gnw_ref[...]
        parts.append((t * _silu(z[:, sl])).astype(BF16))
    mix_a = jnp.concatenate(parts, axis=-1)
    xm = x_ref[...] + jnp.dot(mix_a, wa_ref[...], preferred_element_type=F32)
    xm = xm + jnp.dot(att_ref[...], wb_ref[...], preferred_element_type=F32)
    xm_ref[...] = xm
    hn = xm * lax.rsqrt(jnp.mean(xm * xm, axis=-1, keepdims=True) + EPS) * fnw_ref[...]
    _store_token_tiles(h_ref, hn)
    hn_hi = hn.astype(BF16)
    hn_lo = (hn - hn_hi.astype(F32)).astype(BF16)
    both = jnp.dot(hn_hi, wr_ref[...], preferred_element_type=F32)
    logits = (both[:, :LANES] + both[:, LANES:]
              + jnp.dot(hn_lo, wr_ref[:, :LANES], preferred_element_type=F32))

    lt = logits.T[0:ROUTER_ROWS, :]
    tok = lt.shape[1]
    slot = lax.broadcasted_iota(jnp.int32, (ROUTER_ROWS, tok), 0)
    big = jnp.int32(ROUTER_ROWS)
    neg = -jnp.inf

    def masked_top(vals, mask):
        m = jnp.max(jnp.where(mask, vals, neg), axis=0, keepdims=True)
        idx = jnp.min(jnp.where(mask & (vals == m), slot, big), axis=0, keepdims=True)
        return m, idx

    gmask = slot < N_GROUPS
    gmax, gsel = masked_top(lt, gmask)
    gp_top = 1.0 / jnp.sum(jnp.where(gmask, jnp.exp(lt - gmax), 0.0), axis=0, keepdims=True)
    lo = N_GROUPS + gsel * EXPERTS_PER_GROUP
    emask = (slot >= lo) & (slot < lo + EXPERTS_PER_GROUP)
    m1, i1 = masked_top(lt, emask)
    ex = jnp.where(emask, jnp.exp(lt - m1), 0.0)
    pf = ex / jnp.sum(ex, axis=0, keepdims=True)
    p1, _ = masked_top(pf, emask)
    p2, i2 = masked_top(pf, emask & (slot != i1))
    denom = p1 + p2
    g1 = gp_top * p1 / denom
    g2 = gp_top * p2 / denom
    e1 = (i1 - N_GROUPS).astype(F32)
    e2 = (i2 - N_GROUPS).astype(F32)

    @pl.when(pl.program_id(0) == 0)
    def _():
        count_ref[...] = jnp.zeros_like(count_ref)

    hit1 = slot == i1
    hit2 = slot == i2
    picked = jnp.where(hit1 | hit2, 1.0, 0.0)
    seen = jnp.dot(picked.astype(BF16), triu_ref[...], preferred_element_type=F32)
    counted = count_ref[:, 0:1]
    before = counted + seen - picked
    rank1 = jnp.sum(jnp.where(hit1, before, 0.0), axis=0, keepdims=True)
    rank2 = jnp.sum(jnp.where(hit2, before, 0.0), axis=0, keepdims=True)
    total = jnp.broadcast_to(counted + seen[:, tok - 1:tok], count_ref.shape)
    count_ref[...] = total
    c_ref[...] = total
    rt_ref[...] = jnp.concatenate([e1, e2, g1, g2, rank1, rank2, jnp.zeros((SUBLANES - 6, tok), F32)], axis=0)


def _outproj(o_f, o_b, proj, att, x2, gnw, w_a, w_b, fnw, w_r, tm):
    T, D = x2.shape
    row = lambda i: (i, 0)
    const = lambda i: (0, 0)
    return pl.pallas_call(
        _outproj_body,
        grid=(T // tm,),
        in_specs=[
            pl.BlockSpec((tm, GDN_W), row),
            pl.BlockSpec((tm, GDN_W), row),
            pl.BlockSpec((tm, GDN_W), lambda i: (i, COL_Z // GDN_W)),
            pl.BlockSpec((tm, ATT_Q), row),
            pl.BlockSpec((tm, D), row),
            pl.BlockSpec((1, GDN_D), const),
            pl.BlockSpec((GDN_W, D), const),
            pl.BlockSpec((ATT_Q, D), lambda i: (GDN_W // ATT_Q, 0)),
            pl.BlockSpec((1, D), const),
            pl.BlockSpec((D, 2 * LANES), const),
            pl.BlockSpec((tm, tm), const),
        ],
        out_specs=[pl.BlockSpec((tm, D), row), pl.BlockSpec((tm * ROW_TILE, LANES), row),
                   pl.BlockSpec((SUBLANES, tm), lambda i: (0, i)),
                   pl.BlockSpec((ROUTER_ROWS, LANES), const)],
        out_shape=[jax.ShapeDtypeStruct((T, D), F32), jax.ShapeDtypeStruct((T * ROW_TILE, LANES), F32),
                   jax.ShapeDtypeStruct((SUBLANES, T), F32),
                   jax.ShapeDtypeStruct((ROUTER_ROWS, LANES), F32)],
        scratch_shapes=[pltpu.VMEM((ROUTER_ROWS, LANES), F32)],
        compiler_params=_cparams(("arbitrary",)),
        name="outproj_router",
    )(o_f, o_b, proj, att, x2, gnw, w_a, w_b, fnw, w_r, jnp.asarray(np.tri(tm, dtype=np.float32).T, BF16))


def _gather_start(src_hbm, dst_buf, sem, idx_ref, base, slot, n):
    for r in range(n):
        src = pl.multiple_of(idx_ref[base + r] * ROW_TILE, ROW_TILE)
        dst = pl.multiple_of((slot * n + r) * ROW_TILE, ROW_TILE)
        pltpu.make_async_copy(src_hbm.at[pl.ds(src, ROW_TILE)], dst_buf.at[pl.ds(dst, ROW_TILE)],
                              sem.at[slot]).start(priority=r % DMA_QUEUES)


def _gather_wait(src_hbm, dst_buf, sem, slot, n):
    dst = pl.multiple_of(slot * n * ROW_TILE, ROW_TILE)
    pltpu.make_async_copy(src_hbm.at[pl.ds(0, n * ROW_TILE)], dst_buf.at[pl.ds(dst, n * ROW_TILE)],
                          sem.at[slot]).wait()


def _experts_body(be_ref, src_ref, nused_ref,
                  h_hbm, wg_hbm, wu_hbm, wd_hbm, y_ref, xbuf, wg_buf, wu_buf, wd_buf, ws_ref, sem, wsem, *, bm):
    i = pl.program_id(0)
    n_blocks = pl.num_programs(0)
    n_used = nused_ref[0]
    slot = i % 2
    expert = be_ref[i]
    first = (i == 0) | (expert != be_ref[jnp.maximum(i - 1, 0)])

    def weight_copies(e, s):
        return [pltpu.make_async_copy(hbm.at[e], buf.at[s], wsem.at[s])
                for hbm, buf in ((wg_hbm, wg_buf), (wu_hbm, wu_buf), (wd_hbm, wd_buf))]

    def compute():
        @pl.when(first)
        def _():
            cur = 1 - ws_ref[0]
            ws_ref[0] = cur
            for c in weight_copies(0, cur):
                c.wait()
            nxt = lax.while_loop(lambda j: (j < n_used) & (be_ref[jnp.minimum(j, n_blocks - 1)] == expert),
                                 lambda j: j + 1, i + 1)

            @pl.when(nxt < n_used)
            def _():
                for c in weight_copies(be_ref[jnp.minimum(nxt, n_blocks - 1)], 1 - cur):
                    c.start()

        ws = ws_ref[0]

        _gather_wait(h_hbm, xbuf, sem, slot, bm)
        x = _load_token_tiles(xbuf, slot * (bm * ROW_TILE), bm).astype(BF16)
        gate = jnp.dot(x, wg_buf[ws].astype(BF16), preferred_element_type=F32)
        up = jnp.dot(x, wu_buf[ws].astype(BF16), preferred_element_type=F32)
        hid = (_silu(gate) * up).astype(BF16)
        _store_token_tiles(y_ref, jnp.dot(hid, wd_buf[ws].astype(BF16), preferred_element_type=F32))

    @pl.when((i == 0) & (n_used > 0))
    def _():
        ws_ref[0] = 1
        for c in weight_copies(be_ref[0], 0):
            c.start()
        _gather_start(h_hbm, xbuf, sem, src_ref, 0, 0, bm)

    @pl.when(i + 1 < n_used)
    def _():
        _gather_start(h_hbm, xbuf, sem, src_ref, (i + 1) * bm, 1 - slot, bm)

    @pl.when(i < n_used)
    def _():
        compute()

    @pl.when(i >= n_used)
    def _():
        y_ref[...] = jnp.zeros_like(y_ref)


def _experts(block_expert, src_tok, n_used, h2, w_gate, w_up, w_down, bm):
    D = ROW_TILE * LANES
    P = src_tok.shape[0]
    n_blocks = P // bm
    FF = w_gate.shape[-1]
    kern = functools.partial(_experts_body, bm=bm)
    hbm = pl.BlockSpec(memory_space=pl.ANY)
    return pl.pallas_call(
        kern,
        grid_spec=pltpu.PrefetchScalarGridSpec(
            num_scalar_prefetch=3,
            grid=(n_blocks,),
            in_specs=[hbm, hbm, hbm, hbm],
            out_specs=pl.BlockSpec((bm * ROW_TILE, LANES), lambda i, *_: (i, 0)),
            scratch_shapes=[
                pltpu.VMEM((2 * bm * ROW_TILE, LANES), F32),
                pltpu.VMEM((2, D, FF), F32), pltpu.VMEM((2, D, FF), F32), pltpu.VMEM((2, FF, D), F32),
                pltpu.SMEM((1,), jnp.int32),
                pltpu.SemaphoreType.DMA((2,)), pltpu.SemaphoreType.DMA((2,)),
            ],
        ),
        out_shape=jax.ShapeDtypeStruct((P * ROW_TILE, LANES), F32),
        compiler_params=_cparams(("arbitrary",)),
        name="moe_experts",
    )(block_expert, src_tok, n_used, h2, w_gate, w_up, w_down)


def _combine_body(dest_ref, y_hbm, xm_ref, r_ref, fw_ref, o_ref, ybuf, sem, *, tc):
    i = pl.program_id(0)
    n = pl.num_programs(0)
    slot = i % 2
    rows = TOP_K * tc

    @pl.when(i == 0)
    def _():
        _gather_start(y_hbm, ybuf, sem, dest_ref, 0, 0, rows)

    @pl.when(i + 1 < n)
    def _():
        _gather_start(y_hbm, ybuf, sem, dest_ref, (i + 1) * rows, 1 - slot, rows)

    _gather_wait(y_hbm, ybuf, sem, slot, rows)
    route = r_ref[...].T
    y0 = _load_token_tiles(ybuf, slot * (rows * ROW_TILE), tc)
    y1 = _load_token_tiles(ybuf, (slot * rows + tc) * ROW_TILE, tc)
    xo = xm_ref[...] + route[:, 2:3] * y0 + route[:, 3:4] * y1
    o_ref[...] = xo * lax.rsqrt(jnp.mean(xo * xo, axis=-1, keepdims=True) + EPS) * fw_ref[...]


def _combine(dest_blocked, y_buf, x_mid, route, final_w, tc):
    T, D = x_mid.shape
    kern = functools.partial(_combine_body, tc=tc)
    return pl.pallas_call(
        kern,
        grid_spec=pltpu.PrefetchScalarGridSpec(
            num_scalar_prefetch=1,
            grid=(T // tc,),
            in_specs=[
                pl.BlockSpec(memory_space=pl.ANY),
                pl.BlockSpec((tc, D), lambda i, d: (i, 0)),
                pl.BlockSpec((SUBLANES, tc), lambda i, d: (0, i)),
                pl.BlockSpec((1, D), lambda i, d: (0, 0)),
            ],
            out_specs=pl.BlockSpec((tc, D), lambda i, d: (i, 0)),
            scratch_shapes=[pltpu.VMEM((2 * TOP_K * tc * ROW_TILE, LANES), F32), pltpu.SemaphoreType.DMA((2,))],
        ),
        out_shape=jax.ShapeDtypeStruct((T, D), F32),
        compiler_params=_cparams(("arbitrary",)),
        name="moe_combine",
    )(dest_blocked, y_buf, x_mid, route, final_w.reshape(1, D))


def _layer(x2, B, S, norm_mix_w, w_in, conv_w, a_log, dt_bias, gdn_norm_w, q_norm_w, k_norm_w, w_out,
           norm_ffn_w, w_router_group, w_router_expert, w_gate, w_up, w_down, final_w):
    T, D = x2.shape
    tl = _tiles(B, S)
    o_gate = 4 * GDN_W
    o_qb = o_gate + 2 * GDN_CHAINS
    w_all = jnp.concatenate([w_in[:, :o_gate], w_in[:, o_qb:], w_in[:, o_gate:o_qb],
                             jnp.zeros((D, LANES - 2 * GDN_CHAINS), w_in.dtype)], axis=1).astype(BF16)

    rows = S // GRID_W
    rowp = np.repeat(np.arange(rows), GRID_W).astype(np.float64)
    colp = np.tile(np.arange(GRID_W), rows).astype(np.float64)
    axis_dims = ATT_D // 2
    inv_freq = ROPE_THETA ** (-np.arange(0, axis_dims, 2, dtype=np.float64) / axis_dims)
    ang = np.concatenate([rowp[:, None] * inv_freq, colp[:, None] * inv_freq], axis=-1)
    pair_sign = np.tile(np.array([-1.0, 1.0]), axis_dims)
    cosf = jnp.asarray(np.tile(np.repeat(np.cos(ang), 2, axis=1), (1, LANES // ATT_D)), F32)
    sinf = jnp.asarray(np.tile(np.repeat(np.sin(ang), 2, axis=1) * pair_sign, (1, LANES // ATT_D)), F32)
    q_gain = jnp.max(jnp.abs(q_norm_w)).astype(F32)
    k_gain = jnp.max(jnp.abs(k_norm_w)).astype(F32)
    score_bound = ATT_D ** 0.5 * q_gain * k_gain
    safe = (2.0 * score_bound <= SOFTMAX_SAFE_SPAN).astype(jnp.int32).reshape(1)
    bias = (-LOG2E * score_bound).reshape(1, 1)
    pair = lambda w: jnp.tile(w, LANES // ATT_D).reshape(1, LANES)

    proj, gate_logits, qh, kh, vh = _inproj(x2, norm_mix_w, w_all, pair(q_norm_w), pair(k_norm_w),
                                            cosf, sinf, bias, B, S, tl.proj_rows)

    conv_w8 = jnp.concatenate([conv_w, jnp.zeros((SUBLANES - CONV_W, conv_w.shape[1]), F32)], axis=0)
    gp = jnp.zeros((SUBLANES, LANES), F32)
    gp = gp.at[0, GDN_CHAINS:2 * GDN_CHAINS].set(jnp.exp(a_log.astype(F32)).reshape(-1))
    gp = gp.at[1, GDN_CHAINS:2 * GDN_CHAINS].set(dt_bias.astype(F32).reshape(-1))
    q_a, k_a, v_a, gates, gates_t = _gdn_prep(proj, gate_logits, conv_w8, gp, B, S, tl.prep_rows)
    u, wq, ak, dec = _gdn_chunk(q_a, k_a, v_a, gates, gates_t, tl.gdn_chunks)
    o_f, o_b = _gdn_scan(u, wq, ak, dec, B, S, tl.scan_chunks)

    att = _attention(safe, qh, kh, vh, B, S, tl.att_q, tl.att_k).reshape(T, ATT_Q)

    w_r32 = jnp.concatenate([w_router_group, w_router_expert,
                             jnp.zeros((D, LANES - N_GROUPS - N_EXPERTS), F32)], axis=1).astype(F32)
    w_r_hi = w_r32.astype(BF16)
    w_r = jnp.concatenate([w_r_hi, (w_r32 - w_r_hi.astype(F32)).astype(BF16)], axis=1)
    w_out_bf = w_out.astype(BF16)
    x_mid, h2, route_t, count_rows = _outproj(
        o_f, o_b, proj, att, x2, gdn_norm_w.reshape(1, GDN_D), w_out_bf, w_out_bf,
        norm_ffn_w.reshape(1, D), w_r, tl.proj_rows)

    bm = tl.moe_rows
    n_assign = T * TOP_K
    n_blocks = -(-(n_assign + N_EXPERTS * (bm - 1)) // bm)
    experts = jnp.arange(N_EXPERTS, dtype=jnp.int32)
    counts = count_rows[N_GROUPS:N_GROUPS + N_EXPERTS, 0].astype(jnp.int32)
    padded = (counts + bm - 1) // bm * bm
    pad_end = jnp.cumsum(padded)
    pad_start = pad_end - padded
    block_start = jnp.arange(n_blocks, dtype=jnp.int32) * bm
    done = (pad_end[None, :] <= block_start[:, None]).astype(jnp.int32)
    begun = (pad_start[None, :] <= block_start[:, None]).astype(jnp.int32)
    block_expert = jnp.minimum(jnp.sum(done, axis=1), N_EXPERTS - 1)
    n_used = (pad_end[-1:] // bm).astype(jnp.int32)
    seg_start = jnp.sum(done * padded[None, :], axis=1)
    seg_entry = jnp.sum(done * counts[None, :], axis=1)
    seg_count = jnp.sum(begun * counts[None, :], axis=1) - seg_entry
    e_rows = route_t[0:TOP_K].astype(jnp.int32)
    rank_rows = route_t[4:4 + TOP_K].astype(jnp.int32)
    dest = jnp.sum(jnp.where(e_rows[:, :, None] == experts, pad_start, 0), axis=-1) + rank_rows
    tokens = jnp.tile(jnp.arange(T, dtype=jnp.int32), TOP_K)
    _, compact = lax.sort((dest.reshape(-1), tokens), num_keys=1)
    row = block_start[:, None] + jnp.arange(bm, dtype=jnp.int32)[None, :]
    seg_row = row - seg_start[:, None]
    holds_token = seg_row < seg_count[:, None]
    entry = jnp.clip(seg_entry[:, None] + seg_row, 0, n_assign - 1)
    src_tok = jnp.where(holds_token, compact[entry], lax.rem(row, jnp.full_like(row, T))).reshape(-1)

    y_buf = _experts(block_expert, src_tok, n_used, h2, w_gate, w_up, w_down, bm)

    tc = tl.comb_rows
    dest_blocked = dest.reshape(TOP_K, T // tc, tc).transpose(1, 0, 2).reshape(-1)
    return _combine(dest_blocked, y_buf, x_mid, route_t, final_w, tc)


def kernel(x, norm_mix_w, w_in, conv_w, a_log, dt_bias, gdn_norm_w, q_norm_w, k_norm_w, w_out, norm_ffn_w,
           w_router_group, w_router_expert, w_gate, w_up, w_down, final_norm_w):
    B, S, D = x.shape
    depth = w_in.shape[0]
    assert depth == 1, "the final norm is fused into the last (only) layer's combine step"
    out = _layer(x.reshape(B * S, D), B, S, norm_mix_w[0], w_in[0], conv_w[0], a_log[0], dt_bias[0],
                 gdn_norm_w[0], q_norm_w[0], k_norm_w[0], w_out[0], norm_ffn_w[0], w_router_group[0],
                 w_router_expert[0], w_gate[0], w_up[0], w_down[0], final_norm_w)
    return out.reshape(B, S, D)
```

```python
import functools
import math
from typing import NamedTuple

import jax
import jax.numpy as jnp
import numpy as np
from jax import lax
from jax.experimental import pallas as pl
from jax.experimental.pallas import tpu as pltpu

F32 = jnp.float32
BF16 = jnp.bfloat16
EPS = 1e-6

GRID_W = 64
GDN_HEADS = 4
GDN_D = 128
CONV_W = 5
CHUNK = 64
ATT_HEADS = 8
ATT_KV_HEADS = 2
ATT_GROUP = ATT_HEADS // ATT_KV_HEADS
ATT_D = 64
ROPE_THETA = 10000.0
N_GROUPS = 4
EXPERTS_PER_GROUP = 8
N_EXPERTS = N_GROUPS * EXPERTS_PER_GROUP
TOP_K = 2

GDN_W = GDN_HEADS * GDN_D
GDN_CHAINS = 2 * GDN_HEADS
ATT_Q = ATT_HEADS * ATT_D
ATT_KV = ATT_KV_HEADS * ATT_D
LANES = 128
SUBLANES = 8
ATT_DP = LANES
ROUTER_ROWS = -(-(N_GROUPS + N_EXPERTS) // SUBLANES) * SUBLANES

COL_Z = 3 * GDN_W
COL_QB = COL_Z + GDN_W
COL_KB = COL_QB + ATT_Q
COL_VB = COL_KB + ATT_KV
COL_GATE = COL_VB + ATT_KV
D_PROJ = COL_GATE + LANES

VMEM_LIMIT = 56 * 1024 * 1024
LOG2E = math.log2(math.e)
SOFTMAX_SAFE_SPAN = 60.0


class Tiles(NamedTuple):
    proj_rows: int
    prep_rows: int
    gdn_chunks: int
    scan_chunks: int
    att_q: int
    att_k: int
    moe_rows: int
    comb_rows: int


def _tile(n, want):
    t = min(n, want)
    assert n % t == 0, (n, want)
    return t


def _tiles(B, S):
    T = B * S
    n_chunks = S // CHUNK
    return Tiles(proj_rows=_tile(T, 512), prep_rows=_tile(S, 512), gdn_chunks=_tile(n_chunks, 8),
                 scan_chunks=_tile(n_chunks, 8), att_q=_tile(S, 512), att_k=_tile(S, 2048),
                 moe_rows=256, comb_rows=_tile(T, 256))


def _cparams(sem):
    return pltpu.CompilerParams(dimension_semantics=sem, vmem_limit_bytes=VMEM_LIMIT)


def _silu(x):
    return x * jax.nn.sigmoid(x)


def _rms(x, w):
    return x * lax.rsqrt(jnp.mean(x * x, axis=-1, keepdims=True) + EPS) * w


ROW_TILE = SUBLANES
DMA_QUEUES = 2


def _store_token_tiles(ref, x, base=0):
    rows, d = x.shape
    assert d == ROW_TILE * LANES
    for c in range(ROW_TILE):
        ref[pl.ds(base + c, rows, stride=ROW_TILE), :] = x[:, c * LANES:(c + 1) * LANES]


def _load_token_tiles(ref, base, rows):
    return jnp.concatenate([ref[pl.ds(base + c, rows, stride=ROW_TILE), :] for c in range(ROW_TILE)], axis=-1)


def _inproj_body(x_ref, nw_ref, w_ref, qw_ref, kw_ref, cos_ref, sin_ref, bias_ref,
                 o_ref, g_ref, qo_ref, ko_ref, vo_ref):
    x = x_ref[...]
    h = x * lax.rsqrt(jnp.mean(x * x, axis=-1, keepdims=True) + EPS) * nw_ref[...]
    acc = jnp.dot(h.astype(BF16), w_ref[...], preferred_element_type=F32)
    o_ref[...] = acc[:, :COL_QB]
    g_ref[...] = acc[:, COL_GATE:]
    _attn_heads(acc[:, COL_QB:COL_KB], acc[:, COL_KB:COL_VB], acc[:, COL_VB:COL_GATE], qw_ref[...], kw_ref[...],
                cos_ref[...], sin_ref[...], bias_ref[...], qo_ref, ko_ref, vo_ref)


def _inproj(x2, norm_w, w_all, qw, kw, cosf, sinf, bias, B, S, tm):
    T, D = x2.shape
    nr = S // tm
    const = lambda i: (0, 0)
    heads = lambda i: (i // nr, 0, i % nr, 0)
    return pl.pallas_call(
        _inproj_body,
        grid=(T // tm,),
        in_specs=[
            pl.BlockSpec((tm, D), lambda i: (i, 0)),
            pl.BlockSpec((1, D), const),
            pl.BlockSpec((D, D_PROJ), const),
            pl.BlockSpec((1, LANES), const),
            pl.BlockSpec((1, LANES), const),
            pl.BlockSpec((tm, LANES), lambda i: (i % nr, 0)),
            pl.BlockSpec((tm, LANES), lambda i: (i % nr, 0)),
            pl.BlockSpec((1, 1), const),
        ],
        out_specs=[
            pl.BlockSpec((tm, COL_QB), lambda i: (i, 0)),
            pl.BlockSpec((tm, LANES), lambda i: (i, 0)),
            pl.BlockSpec((None, ATT_HEADS, tm, ATT_DP), heads),
            pl.BlockSpec((None, ATT_KV_HEADS, tm, ATT_DP), heads),
            pl.BlockSpec((None, ATT_KV_HEADS, tm, ATT_DP), heads),
        ],
        out_shape=[
            jax.ShapeDtypeStruct((T, COL_QB), F32),
            jax.ShapeDtypeStruct((T, LANES), F32),
            jax.ShapeDtypeStruct((B, ATT_HEADS, S, ATT_DP), BF16),
            jax.ShapeDtypeStruct((B, ATT_KV_HEADS, S, ATT_DP), BF16),
            jax.ShapeDtypeStruct((B, ATT_KV_HEADS, S, ATT_DP), BF16),
        ],
        compiler_params=_cparams(("parallel",)),
        name="inproj",
    )(x2, norm_w.reshape(1, D), w_all, qw, kw, cosf, sinf, bias)


def _gdn_prep_body(cur_ref, prev_ref, next_ref, cw_ref, gin_ref, gp_ref,
                   q_ref, k_ref, v_ref, g_ref, gt_ref, ext_ref, *, tr):
    i = pl.program_id(1)
    nr = pl.num_programs(1)
    halo = prev_ref.shape[0]
    pad = CONV_W // 2
    ext_ref[0:halo, :] = jnp.where(i > 0, prev_ref[...], 0.0)
    ext_ref[halo:halo + tr, :] = cur_ref[...]
    ext_ref[halo + tr:2 * halo + tr, :] = jnp.where(i < nr - 1, next_ref[...], 0.0)
    acc = cw_ref[0:1, :] * ext_ref[pl.ds(halo - pad, tr), :]
    for j in range(1, CONV_W):
        acc = acc + cw_ref[j:j + 1, :] * ext_ref[pl.ds(halo - pad + j, tr), :]
    y = _silu(acc)
    for h in range(GDN_HEADS):
        for base, ref, scale in ((0, q_ref, GDN_D ** -0.5), (GDN_W, k_ref, 1.0)):
            t = y[:, base + h * GDN_D: base + (h + 1) * GDN_D]
            t = t * (lax.rsqrt(jnp.sum(t * t, axis=-1, keepdims=True) + EPS) * scale)
            ref[:, h * GDN_D:(h + 1) * GDN_D] = t
    v_ref[...] = y[:, 2 * GDN_W:]
    gin = gin_ref[...]
    lane = lax.broadcasted_iota(jnp.int32, gin.shape, 1)
    a = gin + gp_ref[1:2, :]
    softplus = jnp.maximum(a, 0.0) + jnp.log1p(jnp.exp(-jnp.abs(a)))
    g = jnp.where(lane < GDN_CHAINS, jax.nn.sigmoid(gin), -gp_ref[0:1, :] * softplus)
    g_ref[...] = g
    gt_ref[...] = g.T[0:2 * GDN_CHAINS, :]


def _gdn_prep(proj, gate_logits, conv_w8, gate_params, B, S, tr):
    T = B * S
    nr = S // tr
    C = 3 * GDN_W
    halo = SUBLANES
    rb = tr // halo
    n_halo = T // halo
    kern = functools.partial(_gdn_prep_body, tr=tr)
    out_sd = jax.ShapeDtypeStruct((T, GDN_W), F32)
    return pl.pallas_call(
        kern,
        grid=(B, nr),
        in_specs=[
            pl.BlockSpec((tr, C), lambda b, i: (b * nr + i, 0)),
            pl.BlockSpec((halo, C), lambda b, i: (jnp.maximum((b * nr + i) * rb - 1, 0), 0)),
            pl.BlockSpec((halo, C), lambda b, i: (jnp.minimum((b * nr + i + 1) * rb, n_halo - 1), 0)),
            pl.BlockSpec((SUBLANES, C), lambda b, i: (0, 0)),
            pl.BlockSpec((tr, LANES), lambda b, i: (b * nr + i, 0)),
            pl.BlockSpec((SUBLANES, LANES), lambda b, i: (0, 0)),
        ],
        out_specs=[
            pl.BlockSpec((tr, GDN_W), lambda b, i: (b * nr + i, 0)),
            pl.BlockSpec((tr, GDN_W), lambda b, i: (b * nr + i, 0)),
            pl.BlockSpec((tr, GDN_W), lambda b, i: (b * nr + i, 0)),
            pl.BlockSpec((tr, LANES), lambda b, i: (b * nr + i, 0)),
            pl.BlockSpec((2 * GDN_CHAINS, tr), lambda b, i: (0, b * nr + i)),
        ],
        out_shape=[out_sd, out_sd, out_sd, jax.ShapeDtypeStruct((T, LANES), F32),
                   jax.ShapeDtypeStruct((2 * GDN_CHAINS, T), F32)],
        scratch_shapes=[pltpu.VMEM((tr + 2 * halo, C), F32)],
        compiler_params=_cparams(("parallel", "parallel")),
        name="gdn_prep",
    )(proj, proj, proj, conv_w8, gate_logits, gate_params)


def _bdot(a, b):
    return jnp.dot(a.astype(BF16), b.astype(BF16), preferred_element_type=F32)


def _bdot_nt(a, b):
    return lax.dot_general(a.astype(BF16), b.astype(BF16), (((1,), (1,)), ((), ())),
                           preferred_element_type=F32)


def _gdn_chunk_body(q_ref, k_ref, v_ref, g_ref, gt_ref, u_ref, wq_ref, ak_ref, dec_ref, *, cb):
    C = CHUNK
    row = lax.broadcasted_iota(jnp.int32, (C, C), 0)
    col = lax.broadcasted_iota(jnp.int32, (C, C), 1)
    eye = (row == col).astype(F32)
    masks = ((row >= col, row > col), (row <= col, row < col))
    hi = lax.Precision.HIGHEST
    chains = []
    gate_rows = gt_ref[...]
    for c in range(cb):
        rs = slice(c * C, (c + 1) * C)
        gates = g_ref[rs, :]
        gates_t = gate_rows[:, rs]
        g_tot = jnp.sum(gates, axis=0, keepdims=True)
        tot_rows = jnp.sum(gates_t, axis=1, keepdims=True)
        dec_ref[c] = jnp.broadcast_to(jnp.exp(tot_rows[GDN_CHAINS:2 * GDN_CHAINS]), (GDN_CHAINS, LANES))
        for d in range(2):
            incl, strict = masks[d]
            cum = incl.astype(F32)
            gc_cols = jnp.dot(cum, gates, precision=hi, preferred_element_type=F32)
            gc_rows = lax.dot_general(gates_t, cum, (((1,), (1,)), ((), ())), precision=hi,
                                      preferred_element_type=F32)
            for h in range(GDN_HEADS):
                lane_b = d * GDN_HEADS + h
                lane_g = GDN_CHAINS + lane_b
                sl = slice(h * GDN_D, (h + 1) * GDN_D)
                q = q_ref[rs, sl]
                k = k_ref[rs, sl]
                v = v_ref[rs, sl]
                beta = gates[:, lane_b:lane_b + 1]
                gc_col = gc_cols[:, lane_g:lane_g + 1]
                gc_row = gc_rows[lane_g:lane_g + 1, :]
                g_last = g_tot[:, lane_g:lane_g + 1]
                decay = jnp.where(incl, jnp.exp(jnp.where(incl, gc_col - gc_row, 0.0)), 0.0)
                e_col = jnp.exp(gc_col)
                k_beta = k * beta
                kk = _bdot_nt(jnp.concatenate([k_beta, q], axis=0), k)
                neg_l = jnp.where(strict, -kk[:C] * decay, 0.0)
                attn = jnp.where(incl, kk[C:] * decay, 0.0)
                rhs = jnp.concatenate([v * beta, k_beta * e_col], axis=1).astype(BF16)
                wq_ref[d, c, C:2 * C, sl] = (q * e_col).astype(BF16)
                ak_ref[d, c, h, 0:C, :] = attn.astype(BF16)
                ak_ref[d, c, h, C:C + GDN_D, :] = (k * jnp.exp(g_last - gc_col)).T.astype(BF16)
                chains.append((d, c, rs, sl, neg_l, rhs))
    zs = [jnp.concatenate([ch[4], eye], axis=1) for ch in chains]
    keep_s = lax.broadcasted_iota(jnp.int32, (C, 2 * C), 1) >= C
    for _ in range(int(math.log2(C))):
        zs = [_bdot(z[:, :C], z) + jnp.where(keep_s, z, 0.0) for z in zs]
    for (d, c, rs, sl, _, rhs), z in zip(chains, zs):
        sol = jnp.dot(z[:, C:].astype(BF16), rhs, preferred_element_type=F32)
        u_ref[d, rs, sl] = sol[:, :GDN_D]
        wq_ref[d, c, 0:C, sl] = sol[:, GDN_D:].astype(BF16)


def _gdn_chunk(q, k, v, gates, gates_t, cb):
    T = q.shape[0]
    nc = T // CHUNK
    rows = cb * CHUNK
    kern = functools.partial(_gdn_chunk_body, cb=cb)
    wide = pl.BlockSpec((rows, GDN_W), lambda i: (i, 0))
    return pl.pallas_call(
        kern,
        grid=(nc // cb,),
        in_specs=[wide, wide, wide, pl.BlockSpec((rows, LANES), lambda i: (i, 0)),
                  pl.BlockSpec((2 * GDN_CHAINS, rows), lambda i: (0, i))],
        out_specs=[
            pl.BlockSpec((2, rows, GDN_W), lambda i: (0, i, 0)),
            pl.BlockSpec((2, cb, 2 * CHUNK, GDN_W), lambda i: (0, i, 0, 0)),
            pl.BlockSpec((2, cb, GDN_HEADS, CHUNK + GDN_D, CHUNK), lambda i: (0, i, 0, 0, 0)),
            pl.BlockSpec((cb, GDN_CHAINS, LANES), lambda i: (i, 0, 0)),
        ],
        out_shape=[
            jax.ShapeDtypeStruct((2, T, GDN_W), F32),
            jax.ShapeDtypeStruct((2, nc, 2 * CHUNK, GDN_W), BF16),
            jax.ShapeDtypeStruct((2, nc, GDN_HEADS, CHUNK + GDN_D, CHUNK), BF16),
            jax.ShapeDtypeStruct((nc, GDN_CHAINS, LANES), F32),
        ],
        compiler_params=_cparams(("parallel",)),
        name="gdn_chunk",
    )(q, k, v, gates, gates_t)


def _gdn_scan_body(uf_ref, wqf_ref, akf_ref, decf_ref, ub_ref, wqb_ref, akb_ref, decb_ref,
                   of_ref, ob_ref, state_ref, *, cs):
    @pl.when(pl.program_id(1) == 0)
    def _():
        state_ref[...] = jnp.zeros_like(state_ref)

    C = CHUNK
    refs = ((uf_ref, wqf_ref, akf_ref, decf_ref, of_ref), (ub_ref, wqb_ref, akb_ref, decb_ref, ob_ref))
    chains = [(d, h) for d in range(2) for h in range(GDN_HEADS)]
    st = [state_ref[j] for j in range(GDN_CHAINS)]
    for step in range(cs):
        cidx = (step, cs - 1 - step)
        ws = [jnp.dot(refs[d][1][cidx[d], :, h * GDN_D:(h + 1) * GDN_D], st[j].astype(BF16),
                      preferred_element_type=F32) for j, (d, h) in enumerate(chains)]
        vn = [(refs[d][0][cidx[d] * C:(cidx[d] + 1) * C, h * GDN_D:(h + 1) * GDN_D] - ws[j][:C]).astype(BF16)
              for j, (d, h) in enumerate(chains)]
        rr = [jnp.dot(refs[d][2][cidx[d], h], vn[j], preferred_element_type=F32)
              for j, (d, h) in enumerate(chains)]
        for j, (d, h) in enumerate(chains):
            c = cidx[d]
            refs[d][4][c * C:(c + 1) * C, h * GDN_D:(h + 1) * GDN_D] = (ws[j][C:] + rr[j][:C]).astype(BF16)
            st[j] = st[j] * refs[d][3][c, j:j + 1, :] + rr[j][C:]
    for j in range(GDN_CHAINS):
        state_ref[j] = st[j]


def _gdn_scan(u, wq, ak, dec, B, S, cs):
    T = B * S
    nb = S // (CHUNK * cs)
    rows = cs * CHUNK
    kern = functools.partial(_gdn_scan_body, cs=cs)
    fwd = lambda b, i: b * nb + i
    bwd = lambda b, i: b * nb + nb - 1 - i

    def specs(d, pos):
        return [
            pl.BlockSpec((None, rows, GDN_W), lambda b, i: (d, pos(b, i), 0)),
            pl.BlockSpec((None, cs, 2 * CHUNK, GDN_W), lambda b, i: (d, pos(b, i), 0, 0)),
            pl.BlockSpec((None, cs, GDN_HEADS, CHUNK + GDN_D, CHUNK), lambda b, i: (d, pos(b, i), 0, 0, 0)),
            pl.BlockSpec((cs, GDN_CHAINS, LANES), lambda b, i: (pos(b, i), 0, 0)),
        ]

    out_sd = jax.ShapeDtypeStruct((T, GDN_W), BF16)
    return pl.pallas_call(
        kern,
        grid=(B, nb),
        in_specs=specs(0, fwd) + specs(1, bwd),
        out_specs=[pl.BlockSpec((rows, GDN_W), lambda b, i: (fwd(b, i), 0)),
                   pl.BlockSpec((rows, GDN_W), lambda b, i: (bwd(b, i), 0))],
        out_shape=[out_sd, out_sd],
        scratch_shapes=[pltpu.VMEM((GDN_CHAINS, GDN_D, GDN_D), F32)],
        compiler_params=_cparams(("parallel", "arbitrary")),
        name="gdn_scan",
    )(u, wq, ak, dec, u, wq, ak, dec)


def _attn_heads(q, k, v, qw, kw, cos, sin, bias, qo_ref, ko_ref, vo_ref):
    rows = cos.shape[0]
    lane = lax.broadcasted_iota(jnp.int32, (rows, LANES), 1)
    even_lane = lane % 2 == 0
    first_head = lane < ATT_D
    extra_lane = lane == ATT_D
    mi = lax.broadcasted_iota(jnp.int32, (LANES, LANES), 0) // ATT_D
    mj = lax.broadcasted_iota(jnp.int32, (LANES, LANES), 1) // ATT_D
    head_mean = jnp.where(mi == mj, 1.0 / ATT_D, 0.0).astype(BF16)

    def norm_rope(x, w):
        sq = x * x
        hi = sq.astype(BF16)
        lo = (sq - hi.astype(F32)).astype(BF16)
        ms = (jnp.dot(hi, head_mean, preferred_element_type=F32)
              + jnp.dot(lo, head_mean, preferred_element_type=F32))
        y = x * lax.rsqrt(ms + EPS) * w
        partner = jnp.where(even_lane, pltpu.roll(y, LANES - 1, 1), pltpu.roll(y, 1, 1))
        return y * cos + partner * sin

    def split_heads(r, extra):
        tail = jnp.where(extra_lane, extra, 0.0)
        return (jnp.where(first_head, r, tail).astype(BF16),
                jnp.where(first_head, pltpu.roll(r, ATT_D, 1), tail).astype(BF16))

    scale = LOG2E * ATT_D ** -0.5
    for c in range(ATT_HEADS // 2):
        r = norm_rope(q[:, c * LANES:(c + 1) * LANES], qw) * scale
        qo_ref[2 * c], qo_ref[2 * c + 1] = split_heads(r, bias)
    for c in range(ATT_KV_HEADS // 2):
        r = norm_rope(k[:, c * LANES:(c + 1) * LANES], kw)
        ko_ref[2 * c], ko_ref[2 * c + 1] = split_heads(r, 1.0)
        vo_ref[2 * c], vo_ref[2 * c + 1] = split_heads(v[:, c * LANES:(c + 1) * LANES], 1.0)


def _attn_body(safe_ref, q_ref, k_ref, v_ref, o_ref, m_ref, acc_ref, *, tq, tk):
    safe = safe_ref[0] != 0
    nk = k_ref.shape[0] // tk
    q = q_ref[...].reshape(ATT_GROUP * tq, ATT_DP)
    acc_ref[...] = jnp.zeros_like(acc_ref)

    def scores(j):
        keys = pl.ds(pl.multiple_of(j * tk, tk), tk)
        s = lax.dot_general(q, k_ref[keys, :], (((1,), (1,)), ((), ())), preferred_element_type=F32)
        return s, v_ref[keys, :]

    @pl.when(safe)
    def _():
        def step(j, carry):
            s, v = scores(j)
            acc_ref[...] += jnp.dot(jnp.exp2(s).astype(BF16), v, preferred_element_type=F32)
            return carry
        lax.fori_loop(0, nk, step, 0)

    @pl.when(jnp.logical_not(safe))
    def _():
        m_ref[...] = jnp.full_like(m_ref, -jnp.inf)

        def step(j, carry):
            s, v = scores(j)
            m_prev = m_ref[...]
            m_new = jnp.maximum(m_prev, jnp.max(s, axis=-1, keepdims=True))
            p = jnp.exp2(s - m_new).astype(BF16)
            acc_ref[...] = jnp.exp2(m_prev - m_new) * acc_ref[...] + jnp.dot(p, v, preferred_element_type=F32)
            m_ref[...] = m_new
            return carry
        lax.fori_loop(0, nk, step, 0)

    acc = acc_ref[...]
    o = acc[:, :ATT_D] / acc[:, ATT_D:ATT_D + 1]
    for h in range(ATT_GROUP):
        o_ref[:, h * ATT_D:(h + 1) * ATT_D] = o[h * tq:(h + 1) * tq].astype(o_ref.dtype)


def _attention(safe, q, k, v, B, S, tq, tk):
    kern = functools.partial(_attn_body, tq=tq, tk=tk)
    gw = ATT_GROUP * ATT_D
    return pl.pallas_call(
        kern,
        grid_spec=pltpu.PrefetchScalarGridSpec(
            num_scalar_prefetch=1,
            grid=(B, ATT_KV_HEADS, S // tq),
            in_specs=[
                pl.BlockSpec((None, ATT_GROUP, tq, ATT_DP), lambda b, g, i, s: (b, g, i, 0)),
                pl.BlockSpec((None, None, S, ATT_DP), lambda b, g, i, s: (b, g, 0, 0)),
                pl.BlockSpec((None, None, S, ATT_DP), lambda b, g, i, s: (b, g, 0, 0)),
            ],
            out_specs=pl.BlockSpec((None, tq, gw), lambda b, g, i, s: (b, i, g)),
            scratch_shapes=[
                pltpu.VMEM((ATT_GROUP * tq, 1), F32),
                pltpu.VMEM((ATT_GROUP * tq, ATT_DP), F32),
            ],
        ),
        out_shape=jax.ShapeDtypeStruct((B, S, ATT_Q), BF16),
        compiler_params=_cparams(("parallel", "parallel", "parallel")),
        name="attention",
    )(safe, q, k, v)


def _outproj_body(of_ref, ob_ref, z_ref, att_ref, x_ref, gnw_ref, wa_ref, wb_ref,
                  fnw_ref, wr_ref, triu_ref, xm_ref, rt_ref, c_ref, count_ref):
    o = of_ref[...].astype(F32) + ob_ref[...].astype(F32)
    z = z_ref[...]
    parts = []
    for h in range(GDN_HEADS):
        sl = slice(h * GDN_D, (h + 1) * GDN_D)
        t = o[:, sl]
        t = t * lax.rsqrt(jnp.mean(t * t, axis=-1, keepdims=True) + EPS) * gnw_ref[...]
        parts.append((t * _silu(z[:, sl])).astype(BF16))
    mix_a = jnp.concatenate(parts, axis=-1)
    xm = x_ref[...] + jnp.dot(mix_a, wa_ref[...], preferred_element_type=F32)
    xm = xm + jnp.dot(att_ref[...], wb_ref[...], preferred_element_type=F32)
    _store_token_tiles(xm_ref, xm)
    hn = _rms(xm, fnw_ref[...])
    hn_hi = hn.astype(BF16)
    hn_lo = (hn - hn_hi.astype(F32)).astype(BF16)
    both = jnp.dot(hn_hi, wr_ref[...], preferred_element_type=F32)
    logits = (both[:, :LANES] + both[:, LANES:]
              + jnp.dot(hn_lo, wr_ref[:, :LANES], preferred_element_type=F32))

    lt = logits.T[0:ROUTER_ROWS, :]
    tok = lt.shape[1]
    slot = lax.broadcasted_iota(jnp.int32, (ROUTER_ROWS, tok), 0)
    big = jnp.int32(ROUTER_ROWS)
    neg = -jnp.inf

    def masked_top(vals, mask):
        m = jnp.max(jnp.where(mask, vals, neg), axis=0, keepdims=True)
        idx = jnp.min(jnp.where(mask & (vals == m), slot, big), axis=0, keepdims=True)
        return m, idx

    gmask = slot < N_GROUPS
    gmax, gsel = masked_top(lt, gmask)
    gp_top = 1.0 / jnp.sum(jnp.where(gmask, jnp.exp(lt - gmax), 0.0), axis=0, keepdims=True)
    lo = N_GROUPS + gsel * EXPERTS_PER_GROUP
    emask = (slot >= lo) & (slot < lo + EXPERTS_PER_GROUP)
    m1, i1 = masked_top(lt, emask)
    ex = jnp.where(emask, jnp.exp(lt - m1), 0.0)
    pf = ex / jnp.sum(ex, axis=0, keepdims=True)
    p1, _ = masked_top(pf, emask)
    p2, i2 = masked_top(pf, emask & (slot != i1))
    denom = p1 + p2
    g1 = gp_top * p1 / denom
    g2 = gp_top * p2 / denom
    e1 = (i1 - N_GROUPS).astype(F32)
    e2 = (i2 - N_GROUPS).astype(F32)

    @pl.when(pl.program_id(0) == 0)
    def _():
        count_ref[...] = jnp.zeros_like(count_ref)

    hit1 = slot == i1
    hit2 = slot == i2
    picked = jnp.where(hit1 | hit2, 1.0, 0.0)
    seen = jnp.dot(picked.astype(BF16), triu_ref[...], preferred_element_type=F32)
    counted = count_ref[:, 0:1]
    before = counted + seen - picked
    rank1 = jnp.sum(jnp.where(hit1, before, 0.0), axis=0, keepdims=True)
    rank2 = jnp.sum(jnp.where(hit2, before, 0.0), axis=0, keepdims=True)
    total = jnp.broadcast_to(counted + seen[:, tok - 1:tok], count_ref.shape)
    count_ref[...] = total
    c_ref[...] = total
    rt_ref[...] = jnp.concatenate([e1, e2, g1, g2, rank1, rank2, jnp.zeros((SUBLANES - 6, tok), F32)], axis=0)


def _outproj(o_f, o_b, proj, att, x2, gnw, w_a, w_b, fnw, w_r, tm):
    T, D = x2.shape
    row = lambda i: (i, 0)
    const = lambda i: (0, 0)
    return pl.pallas_call(
        _outproj_body,
        grid=(T // tm,),
        in_specs=[
            pl.BlockSpec((tm, GDN_W), row),
            pl.BlockSpec((tm, GDN_W), row),
            pl.BlockSpec((tm, GDN_W), lambda i: (i, COL_Z // GDN_W)),
            pl.BlockSpec((tm, ATT_Q), row),
            pl.BlockSpec((tm, D), row),
            pl.BlockSpec((1, GDN_D), const),
            pl.BlockSpec((GDN_W, D), const),
            pl.BlockSpec((ATT_Q, D), lambda i: (GDN_W // ATT_Q, 0)),
            pl.BlockSpec((1, D), const),
            pl.BlockSpec((D, 2 * LANES), const),
            pl.BlockSpec((tm, tm), const),
        ],
        out_specs=[pl.BlockSpec((tm * ROW_TILE, LANES), row),
                   pl.BlockSpec((SUBLANES, tm), lambda i: (0, i)),
                   pl.BlockSpec((ROUTER_ROWS, LANES), const)],
        out_shape=[jax.ShapeDtypeStruct((T * ROW_TILE, LANES), F32),
                   jax.ShapeDtypeStruct((SUBLANES, T), F32),
                   jax.ShapeDtypeStruct((ROUTER_ROWS, LANES), F32)],
        scratch_shapes=[pltpu.VMEM((ROUTER_ROWS, LANES), F32)],
        compiler_params=_cparams(("arbitrary",)),
        name="outproj_router",
    )(o_f, o_b, proj, att, x2, gnw, w_a, w_b, fnw, w_r, jnp.asarray(np.tri(tm, dtype=np.float32).T, BF16))


def _gather_start(src_hbm, dst_buf, sem, idx_ref, base, slot, n):
    for r in range(n):
        src = pl.multiple_of(idx_ref[base + r] * ROW_TILE, ROW_TILE)
        dst = pl.multiple_of((slot * n + r) * ROW_TILE, ROW_TILE)
        pltpu.make_async_copy(src_hbm.at[pl.ds(src, ROW_TILE)], dst_buf.at[pl.ds(dst, ROW_TILE)],
                              sem.at[slot]).start(priority=r % DMA_QUEUES)


def _gather_wait(src_hbm, dst_buf, sem, slot, n):
    dst = pl.multiple_of(slot * n * ROW_TILE, ROW_TILE)
    pltpu.make_async_copy(src_hbm.at[pl.ds(0, n * ROW_TILE)], dst_buf.at[pl.ds(dst, n * ROW_TILE)],
                          sem.at[slot]).wait()


def _experts_body(be_ref, src_ref, nused_ref,
                  h_hbm, fnw_ref, wg_hbm, wu_hbm, wd_hbm, y_ref, xbuf, wg_buf, wu_buf, wd_buf, ws_ref, sem, wsem,
                  *, bm):
    i = pl.program_id(0)
    n_blocks = pl.num_programs(0)
    n_used = nused_ref[0]
    slot = i % 2
    expert = be_ref[i]
    first = (i == 0) | (expert != be_ref[jnp.maximum(i - 1, 0)])

    def weight_copies(e, s):
        return [pltpu.make_async_copy(hbm.at[e], buf.at[s], wsem.at[s])
                for hbm, buf in ((wg_hbm, wg_buf), (wu_hbm, wu_buf), (wd_hbm, wd_buf))]

    def compute():
        @pl.when(first)
        def _():
            cur = 1 - ws_ref[0]
            ws_ref[0] = cur
            for c in weight_copies(0, cur):
                c.wait()
            nxt = lax.while_loop(lambda j: (j < n_used) & (be_ref[jnp.minimum(j, n_blocks - 1)] == expert),
                                 lambda j: j + 1, i + 1)

            @pl.when(nxt < n_used)
            def _():
                for c in weight_copies(be_ref[jnp.minimum(nxt, n_blocks - 1)], 1 - cur):
                    c.start()

        ws = ws_ref[0]

        _gather_wait(h_hbm, xbuf, sem, slot, bm)
        x = _rms(_load_token_tiles(xbuf, slot * (bm * ROW_TILE), bm), fnw_ref[...]).astype(BF16)
        gate = jnp.dot(x, wg_buf[ws].astype(BF16), preferred_element_type=F32)
        up = jnp.dot(x, wu_buf[ws].astype(BF16), preferred_element_type=F32)
        hid = (_silu(gate) * up).astype(BF16)
        _store_token_tiles(y_ref, jnp.dot(hid, wd_buf[ws].astype(BF16), preferred_element_type=F32))

    @pl.when((i == 0) & (n_used > 0))
    def _():
        ws_ref[0] = 1
        for c in weight_copies(be_ref[0], 0):
            c.start()
        _gather_start(h_hbm, xbuf, sem, src_ref, 0, 0, bm)

    @pl.when(i + 1 < n_used)
    def _():
        _gather_start(h_hbm, xbuf, sem, src_ref, (i + 1) * bm, 1 - slot, bm)

    @pl.when(i < n_used)
    def _():
        compute()

    @pl.when(i >= n_used)
    def _():
        y_ref[...] = jnp.zeros_like(y_ref)


def _experts(block_expert, src_tok, n_used, x_rows, ffn_norm_w, w_gate, w_up, w_down, bm):
    D = ROW_TILE * LANES
    P = src_tok.shape[0]
    n_blocks = P // bm
    FF = w_gate.shape[-1]
    kern = functools.partial(_experts_body, bm=bm)
    hbm = pl.BlockSpec(memory_space=pl.ANY)
    return pl.pallas_call(
        kern,
        grid_spec=pltpu.PrefetchScalarGridSpec(
            num_scalar_prefetch=3,
            grid=(n_blocks,),
            in_specs=[hbm, pl.BlockSpec((1, D), lambda i, *_: (0, 0)), hbm, hbm, hbm],
            out_specs=pl.BlockSpec((bm * ROW_TILE, LANES), lambda i, *_: (i, 0)),
            scratch_shapes=[
                pltpu.VMEM((2 * bm * ROW_TILE, LANES), F32),
                pltpu.VMEM((2, D, FF), F32), pltpu.VMEM((2, D, FF), F32), pltpu.VMEM((2, FF, D), F32),
                pltpu.SMEM((1,), jnp.int32),
                pltpu.SemaphoreType.DMA((2,)), pltpu.SemaphoreType.DMA((2,)),
            ],
        ),
        out_shape=jax.ShapeDtypeStruct((P * ROW_TILE, LANES), F32),
        compiler_params=_cparams(("arbitrary",)),
        name="moe_experts",
    )(block_expert, src_tok, n_used, x_rows, ffn_norm_w, w_gate, w_up, w_down)


def _combine_body(dest_ref, y_hbm, xm_ref, r_ref, fw_ref, o_ref, ybuf, sem, *, tc):
    i = pl.program_id(0)
    n = pl.num_programs(0)
    slot = i % 2
    rows = TOP_K * tc

    @pl.when(i == 0)
    def _():
        _gather_start(y_hbm, ybuf, sem, dest_ref, 0, 0, rows)

    @pl.when(i + 1 < n)
    def _():
        _gather_start(y_hbm, ybuf, sem, dest_ref, (i + 1) * rows, 1 - slot, rows)

    _gather_wait(y_hbm, ybuf, sem, slot, rows)
    route = r_ref[...].T
    y0 = _load_token_tiles(ybuf, slot * (rows * ROW_TILE), tc)
    y1 = _load_token_tiles(ybuf, (slot * rows + tc) * ROW_TILE, tc)
    xo = _load_token_tiles(xm_ref, 0, tc) + route[:, 2:3] * y0 + route[:, 3:4] * y1
    o_ref[...] = xo * lax.rsqrt(jnp.mean(xo * xo, axis=-1, keepdims=True) + EPS) * fw_ref[...]


def _combine(dest_blocked, y_buf, x_rows, route, final_w, tc):
    T = x_rows.shape[0] // ROW_TILE
    D = ROW_TILE * LANES
    kern = functools.partial(_combine_body, tc=tc)
    return pl.pallas_call(
        kern,
        grid_spec=pltpu.PrefetchScalarGridSpec(
            num_scalar_prefetch=1,
            grid=(T // tc,),
            in_specs=[
                pl.BlockSpec(memory_space=pl.ANY),
                pl.BlockSpec((tc * ROW_TILE, LANES), lambda i, d: (i, 0)),
                pl.BlockSpec((SUBLANES, tc), lambda i, d: (0, i)),
                pl.BlockSpec((1, D), lambda i, d: (0, 0)),
            ],
            out_specs=pl.BlockSpec((tc, D), lambda i, d: (i, 0)),
            scratch_shapes=[pltpu.VMEM((2 * TOP_K * tc * ROW_TILE, LANES), F32), pltpu.SemaphoreType.DMA((2,))],
        ),
        out_shape=jax.ShapeDtypeStruct((T, D), F32),
        compiler_params=_cparams(("arbitrary",)),
        name="moe_combine",
    )(dest_blocked, y_buf, x_rows, route, final_w.reshape(1, D))


def _layer(x2, B, S, norm_mix_w, w_in, conv_w, a_log, dt_bias, gdn_norm_w, q_norm_w, k_norm_w, w_out,
           norm_ffn_w, w_router_group, w_router_expert, w_gate, w_up, w_down, final_w):
    T, D = x2.shape
    tl = _tiles(B, S)
    o_gate = 4 * GDN_W
    o_qb = o_gate + 2 * GDN_CHAINS
    w_all = jnp.concatenate([w_in[:, :o_gate], w_in[:, o_qb:], w_in[:, o_gate:o_qb],
                             jnp.zeros((D, LANES - 2 * GDN_CHAINS), w_in.dtype)], axis=1).astype(BF16)

    rows = S // GRID_W
    rowp = np.repeat(np.arange(rows), GRID_W).astype(np.float64)
    colp = np.tile(np.arange(GRID_W), rows).astype(np.float64)
    axis_dims = ATT_D // 2
    inv_freq = ROPE_THETA ** (-np.arange(0, axis_dims, 2, dtype=np.float64) / axis_dims)
    ang = np.concatenate([rowp[:, None] * inv_freq, colp[:, None] * inv_freq], axis=-1)
    pair_sign = np.tile(np.array([-1.0, 1.0]), axis_dims)
    cosf = jnp.asarray(np.tile(np.repeat(np.cos(ang), 2, axis=1), (1, LANES // ATT_D)), F32)
    sinf = jnp.asarray(np.tile(np.repeat(np.sin(ang), 2, axis=1) * pair_sign, (1, LANES // ATT_D)), F32)
    q_gain = jnp.max(jnp.abs(q_norm_w)).astype(F32)
    k_gain = jnp.max(jnp.abs(k_norm_w)).astype(F32)
    score_bound = ATT_D ** 0.5 * q_gain * k_gain
    safe = (2.0 * score_bound <= SOFTMAX_SAFE_SPAN).astype(jnp.int32).reshape(1)
    bias = (-LOG2E * score_bound).reshape(1, 1)
    pair = lambda w: jnp.tile(w, LANES // ATT_D).reshape(1, LANES)

    proj, gate_logits, qh, kh, vh = _inproj(x2, norm_mix_w, w_all, pair(q_norm_w), pair(k_norm_w),
                                            cosf, sinf, bias, B, S, tl.proj_rows)

    conv_w8 = jnp.concatenate([conv_w, jnp.zeros((SUBLANES - CONV_W, conv_w.shape[1]), F32)], axis=0)
    gp = jnp.zeros((SUBLANES, LANES), F32)
    gp = gp.at[0, GDN_CHAINS:2 * GDN_CHAINS].set(jnp.exp(a_log.astype(F32)).reshape(-1))
    gp = gp.at[1, GDN_CHAINS:2 * GDN_CHAINS].set(dt_bias.astype(F32).reshape(-1))
    q_a, k_a, v_a, gates, gates_t = _gdn_prep(proj, gate_logits, conv_w8, gp, B, S, tl.prep_rows)
    u, wq, ak, dec = _gdn_chunk(q_a, k_a, v_a, gates, gates_t, tl.gdn_chunks)
    o_f, o_b = _gdn_scan(u, wq, ak, dec, B, S, tl.scan_chunks)

    att = _attention(safe, qh, kh, vh, B, S, tl.att_q, tl.att_k).reshape(T, ATT_Q)

    w_r32 = jnp.concatenate([w_router_group, w_router_expert,
                             jnp.zeros((D, LANES - N_GROUPS - N_EXPERTS), F32)], axis=1).astype(F32)
    w_r_hi = w_r32.astype(BF16)
    w_r = jnp.concatenate([w_r_hi, (w_r32 - w_r_hi.astype(F32)).astype(BF16)], axis=1)
    w_out_bf = w_out.astype(BF16)
    x_mid, route_t, count_rows = _outproj(
        o_f, o_b, proj, att, x2, gdn_norm_w.reshape(1, GDN_D), w_out_bf, w_out_bf,
        norm_ffn_w.reshape(1, D), w_r, tl.proj_rows)

    bm = tl.moe_rows
    n_assign = T * TOP_K
    n_blocks = -(-(n_assign + N_EXPERTS * (bm - 1)) // bm)
    experts = jnp.arange(N_EXPERTS, dtype=jnp.int32)
    counts = count_rows[N_GROUPS:N_GROUPS + N_EXPERTS, 0].astype(jnp.int32)
    padded = (counts + bm - 1) // bm * bm
    pad_end = jnp.cumsum(padded)
    pad_start = pad_end - padded
    block_start = jnp.arange(n_blocks, dtype=jnp.int32) * bm
    done = (pad_end[None, :] <= block_start[:, None]).astype(jnp.int32)
    begun = (pad_start[None, :] <= block_start[:, None]).astype(jnp.int32)
    block_expert = jnp.minimum(jnp.sum(done, axis=1), N_EXPERTS - 1)
    n_used = (pad_end[-1:] // bm).astype(jnp.int32)
    seg_start = jnp.sum(done * padded[None, :], axis=1)
    seg_entry = jnp.sum(done * counts[None, :], axis=1)
    seg_count = jnp.sum(begun * counts[None, :], axis=1) - seg_entry
    e_rows = route_t[0:TOP_K].astype(jnp.int32)
    rank_rows = route_t[4:4 + TOP_K].astype(jnp.int32)
    dest = jnp.sum(jnp.where(e_rows[:, :, None] == experts, pad_start, 0), axis=-1) + rank_rows
    tokens = jnp.tile(jnp.arange(T, dtype=jnp.int32), TOP_K)
    _, compact = lax.sort((dest.reshape(-1), tokens), num_keys=1)
    row = block_start[:, None] + jnp.arange(bm, dtype=jnp.int32)[None, :]
    seg_row = row - seg_start[:, None]
    holds_token = seg_row < seg_count[:, None]
    entry = jnp.clip(seg_entry[:, None] + seg_row, 0, n_assign - 1)
    src_tok = jnp.where(holds_token, compact[entry], lax.rem(row, jnp.full_like(row, T))).reshape(-1)

    y_buf = _experts(block_expert, src_tok, n_used, x_mid, norm_ffn_w.reshape(1, D), w_gate, w_up, w_down, bm)

    tc = tl.comb_rows
    dest_blocked = dest.reshape(TOP_K, T // tc, tc).transpose(1, 0, 2).reshape(-1)
    return _combine(dest_blocked, y_buf, x_mid, route_t, final_w, tc)


def kernel(x, norm_mix_w, w_in, conv_w, a_log, dt_bias, gdn_norm_w, q_norm_w, k_norm_w, w_out, norm_ffn_w,
           w_router_group, w_router_expert, w_gate, w_up, w_down, final_norm_w):
    B, S, D = x.shape
    depth = w_in.shape[0]
    assert depth == 1, "the final norm is fused into the last (only) layer's combine step"
    out = _layer(x.reshape(B * S, D), B, S, norm_mix_w[0], w_in[0], conv_w[0], a_log[0], dt_bias[0],
                 gdn_norm_w[0], q_norm_w[0], k_norm_w[0], w_out[0], norm_ffn_w[0], w_router_group[0],
                 w_router_expert[0], w_gate[0], w_up[0], w_down[0], final_norm_w)
    return out.reshape(B, S, D)
```

```python
import functools
import math
from typing import NamedTuple

import jax
import jax.numpy as jnp
import numpy as np
from jax import lax
from jax.experimental import pallas as pl
from jax.experimental.pallas import tpu as pltpu

F32 = jnp.float32
BF16 = jnp.bfloat16
EPS = 1e-6

GRID_W = 64
GDN_HEADS = 4
GDN_D = 128
CONV_W = 5
CHUNK = 64
ATT_HEADS = 8
ATT_KV_HEADS = 2
ATT_GROUP = ATT_HEADS // ATT_KV_HEADS
ATT_D = 64
ROPE_THETA = 10000.0
N_GROUPS = 4
EXPERTS_PER_GROUP = 8
N_EXPERTS = N_GROUPS * EXPERTS_PER_GROUP
TOP_K = 2

GDN_W = GDN_HEADS * GDN_D
GDN_CHAINS = 2 * GDN_HEADS
ATT_Q = ATT_HEADS * ATT_D
ATT_KV = ATT_KV_HEADS * ATT_D
LANES = 128
SUBLANES = 8
ATT_DP = LANES
ROUTER_ROWS = -(-(N_GROUPS + N_EXPERTS) // SUBLANES) * SUBLANES

COL_Z = 3 * GDN_W
COL_QB = COL_Z + GDN_W
COL_KB = COL_QB + ATT_Q
COL_VB = COL_KB + ATT_KV
COL_GATE = COL_VB + ATT_KV
D_PROJ = COL_GATE + LANES

VMEM_LIMIT = 56 * 1024 * 1024
LOG2E = math.log2(math.e)
SOFTMAX_SAFE_SPAN = 60.0


class Tiles(NamedTuple):
    proj_rows: int
    prep_rows: int
    gdn_chunks: int
    scan_chunks: int
    att_q: int
    att_k: int
    moe_rows: int
    comb_rows: int


def _tile(n, want):
    t = min(n, want)
    assert n % t == 0, (n, want)
    return t


def _tiles(B, S):
    T = B * S
    n_chunks = S // CHUNK
    return Tiles(proj_rows=_tile(T, 512), prep_rows=_tile(S, 512), gdn_chunks=_tile(n_chunks, 8),
                 scan_chunks=_tile(n_chunks, 8), att_q=_tile(S, 512), att_k=_tile(S, 2048),
                 moe_rows=256, comb_rows=_tile(T, 256))


def _cparams(sem):
    return pltpu.CompilerParams(dimension_semantics=sem, vmem_limit_bytes=VMEM_LIMIT)


def _silu(x):
    return x * jax.nn.sigmoid(x)


ROW_TILE = SUBLANES
DMA_QUEUES = 2
ROW_BUFFERS = 3


def _store_token_tiles(ref, x, base=0):
    rows, d = x.shape
    assert d == ROW_TILE * LANES
    for c in range(ROW_TILE):
        ref[pl.ds(base + c, rows, stride=ROW_TILE), :] = x[:, c * LANES:(c + 1) * LANES]


def _load_token_tiles(ref, base, rows):
    return jnp.concatenate([ref[pl.ds(base + c, rows, stride=ROW_TILE), :] for c in range(ROW_TILE)], axis=-1)


def _inproj_body(x_ref, nw_ref, w_ref, qw_ref, kw_ref, cos_ref, sin_ref, bias_ref,
                 o_ref, g_ref, qo_ref, ko_ref, vo_ref):
    x = x_ref[...]
    h = x * lax.rsqrt(jnp.mean(x * x, axis=-1, keepdims=True) + EPS) * nw_ref[...]
    acc = jnp.dot(h.astype(BF16), w_ref[...], preferred_element_type=F32)
    o_ref[...] = acc[:, :COL_QB]
    g_ref[...] = acc[:, COL_GATE:]
    _attn_heads(acc[:, COL_QB:COL_KB], acc[:, COL_KB:COL_VB], acc[:, COL_VB:COL_GATE], qw_ref[...], kw_ref[...],
                cos_ref[...], sin_ref[...], bias_ref[...], qo_ref, ko_ref, vo_ref)


def _inproj(x2, norm_w, w_all, qw, kw, cosf, sinf, bias, B, S, tm):
    T, D = x2.shape
    nr = S // tm
    const = lambda i: (0, 0)
    heads = lambda i: (i // nr, 0, i % nr, 0)
    return pl.pallas_call(
        _inproj_body,
        grid=(T // tm,),
        in_specs=[
            pl.BlockSpec((tm, D), lambda i: (i, 0)),
            pl.BlockSpec((1, D), const),
            pl.BlockSpec((D, D_PROJ), const),
            pl.BlockSpec((1, LANES), const),
            pl.BlockSpec((1, LANES), const),
            pl.BlockSpec((tm, LANES), lambda i: (i % nr, 0)),
            pl.BlockSpec((tm, LANES), lambda i: (i % nr, 0)),
            pl.BlockSpec((1, 1), const),
        ],
        out_specs=[
            pl.BlockSpec((tm, COL_QB), lambda i: (i, 0)),
            pl.BlockSpec((tm, LANES), lambda i: (i, 0)),
            pl.BlockSpec((None, ATT_HEADS, tm, ATT_DP), heads),
            pl.BlockSpec((None, ATT_KV_HEADS, tm, ATT_DP), heads),
            pl.BlockSpec((None, ATT_KV_HEADS, tm, ATT_DP), heads),
        ],
        out_shape=[
            jax.ShapeDtypeStruct((T, COL_QB), F32),
            jax.ShapeDtypeStruct((T, LANES), F32),
            jax.ShapeDtypeStruct((B, ATT_HEADS, S, ATT_DP), BF16),
            jax.ShapeDtypeStruct((B, ATT_KV_HEADS, S, ATT_DP), BF16),
            jax.ShapeDtypeStruct((B, ATT_KV_HEADS, S, ATT_DP), BF16),
        ],
        compiler_params=_cparams(("parallel",)),
        name="inproj",
    )(x2, norm_w.reshape(1, D), w_all, qw, kw, cosf, sinf, bias)


def _gdn_prep_body(cur_ref, prev_ref, next_ref, cw_ref, gin_ref, gp_ref,
                   q_ref, k_ref, v_ref, g_ref, gt_ref, ext_ref, *, tr):
    i = pl.program_id(1)
    nr = pl.num_programs(1)
    halo = prev_ref.shape[0]
    pad = CONV_W // 2
    ext_ref[0:halo, :] = jnp.where(i > 0, prev_ref[...], 0.0)
    ext_ref[halo:halo + tr, :] = cur_ref[...]
    ext_ref[halo + tr:2 * halo + tr, :] = jnp.where(i < nr - 1, next_ref[...], 0.0)
    acc = cw_ref[0:1, :] * ext_ref[pl.ds(halo - pad, tr), :]
    for j in range(1, CONV_W):
        acc = acc + cw_ref[j:j + 1, :] * ext_ref[pl.ds(halo - pad + j, tr), :]
    y = _silu(acc)
    for h in range(GDN_HEADS):
        for base, ref, scale in ((0, q_ref, GDN_D ** -0.5), (GDN_W, k_ref, 1.0)):
            t = y[:, base + h * GDN_D: base + (h + 1) * GDN_D]
            t = t * (lax.rsqrt(jnp.sum(t * t, axis=-1, keepdims=True) + EPS) * scale)
            ref[:, h * GDN_D:(h + 1) * GDN_D] = t
    v_ref[...] = y[:, 2 * GDN_W:]
    gin = gin_ref[...]
    lane = lax.broadcasted_iota(jnp.int32, gin.shape, 1)
    a = gin + gp_ref[1:2, :]
    softplus = jnp.maximum(a, 0.0) + jnp.log1p(jnp.exp(-jnp.abs(a)))
    g = jnp.where(lane < GDN_CHAINS, jax.nn.sigmoid(gin), -gp_ref[0:1, :] * softplus)
    g_ref[...] = g
    gt_ref[...] = g.T[0:2 * GDN_CHAINS, :]


def _gdn_prep(proj, gate_logits, conv_w8, gate_params, B, S, tr):
    T = B * S
    nr = S // tr
    C = 3 * GDN_W
    halo = SUBLANES
    rb = tr // halo
    n_halo = T // halo
    kern = functools.partial(_gdn_prep_body, tr=tr)
    out_sd = jax.ShapeDtypeStruct((T, GDN_W), F32)
    return pl.pallas_call(
        kern,
        grid=(B, nr),
        in_specs=[
            pl.BlockSpec((tr, C), lambda b, i: (b * nr + i, 0)),
            pl.BlockSpec((halo, C), lambda b, i: (jnp.maximum((b * nr + i) * rb - 1, 0), 0)),
            pl.BlockSpec((halo, C), lambda b, i: (jnp.minimum((b * nr + i + 1) * rb, n_halo - 1), 0)),
            pl.BlockSpec((SUBLANES, C), lambda b, i: (0, 0)),
            pl.BlockSpec((tr, LANES), lambda b, i: (b * nr + i, 0)),
            pl.BlockSpec((SUBLANES, LANES), lambda b, i: (0, 0)),
        ],
        out_specs=[
            pl.BlockSpec((tr, GDN_W), lambda b, i: (b * nr + i, 0)),
            pl.BlockSpec((tr, GDN_W), lambda b, i: (b * nr + i, 0)),
            pl.BlockSpec((tr, GDN_W), lambda b, i: (b * nr + i, 0)),
            pl.BlockSpec((tr, LANES), lambda b, i: (b * nr + i, 0)),
            pl.BlockSpec((2 * GDN_CHAINS, tr), lambda b, i: (0, b * nr + i)),
        ],
        out_shape=[out_sd, out_sd, out_sd, jax.ShapeDtypeStruct((T, LANES), F32),
                   jax.ShapeDtypeStruct((2 * GDN_CHAINS, T), F32)],
        scratch_shapes=[pltpu.VMEM((tr + 2 * halo, C), F32)],
        compiler_params=_cparams(("parallel", "parallel")),
        name="gdn_prep",
    )(proj, proj, proj, conv_w8, gate_logits, gate_params)


def _bdot(a, b):
    return jnp.dot(a.astype(BF16), b.astype(BF16), preferred_element_type=F32)


def _bdot_nt(a, b):
    return lax.dot_general(a.astype(BF16), b.astype(BF16), (((1,), (1,)), ((), ())),
                           preferred_element_type=F32)


def _gdn_chunk_body(q_ref, k_ref, v_ref, g_ref, gt_ref, u_ref, wq_ref, ak_ref, dec_ref, *, cb):
    C = CHUNK
    row = lax.broadcasted_iota(jnp.int32, (C, C), 0)
    col = lax.broadcasted_iota(jnp.int32, (C, C), 1)
    eye = (row == col).astype(F32)
    masks = ((row >= col, row > col), (row <= col, row < col))
    hi = lax.Precision.HIGHEST
    chains = []
    gate_rows = gt_ref[...]
    for c in range(cb):
        rs = slice(c * C, (c + 1) * C)
        gates = g_ref[rs, :]
        gates_t = gate_rows[:, rs]
        g_tot = jnp.sum(gates, axis=0, keepdims=True)
        tot_rows = jnp.sum(gates_t, axis=1, keepdims=True)
        dec_ref[c] = jnp.broadcast_to(jnp.exp(tot_rows[GDN_CHAINS:2 * GDN_CHAINS]), (GDN_CHAINS, LANES))
        for d in range(2):
            incl, strict = masks[d]
            cum = incl.astype(F32)
            gc_cols = jnp.dot(cum, gates, precision=hi, preferred_element_type=F32)
            gc_rows = lax.dot_general(gates_t, cum, (((1,), (1,)), ((), ())), precision=hi,
                                      preferred_element_type=F32)
            for h in range(GDN_HEADS):
                lane_b = d * GDN_HEADS + h
                lane_g = GDN_CHAINS + lane_b
                sl = slice(h * GDN_D, (h + 1) * GDN_D)
                q = q_ref[rs, sl]
                k = k_ref[rs, sl]
                v = v_ref[rs, sl]
                beta = gates[:, lane_b:lane_b + 1]
                gc_col = gc_cols[:, lane_g:lane_g + 1]
                gc_row = gc_rows[lane_g:lane_g + 1, :]
                g_last = g_tot[:, lane_g:lane_g + 1]
                decay = jnp.where(incl, jnp.exp(jnp.where(incl, gc_col - gc_row, 0.0)), 0.0)
                e_col = jnp.exp(gc_col)
                k_beta = k * beta
                kk = _bdot_nt(jnp.concatenate([k_beta, q], axis=0), k)
                neg_l = jnp.where(strict, -kk[:C] * decay, 0.0)
                attn = jnp.where(incl, kk[C:] * decay, 0.0)
                rhs = jnp.concatenate([v * beta, k_beta * e_col], axis=1).astype(BF16)
                wq_ref[d, c, C:2 * C, sl] = (q * e_col).astype(BF16)
                ak_ref[d, c, h, 0:C, :] = attn.astype(BF16)
                ak_ref[d, c, h, C:C + GDN_D, :] = (k * jnp.exp(g_last - gc_col)).T.astype(BF16)
                chains.append((d, c, rs, sl, neg_l, rhs))
    zs = [jnp.concatenate([ch[4], eye], axis=1) for ch in chains]
    keep_s = lax.broadcasted_iota(jnp.int32, (C, 2 * C), 1) >= C
    for _ in range(int(math.log2(C))):
        zs = [_bdot(z[:, :C], z) + jnp.where(keep_s, z, 0.0) for z in zs]
    for (d, c, rs, sl, _, rhs), z in zip(chains, zs):
        sol = jnp.dot(z[:, C:].astype(BF16), rhs, preferred_element_type=F32)
        u_ref[d, rs, sl] = sol[:, :GDN_D]
        wq_ref[d, c, 0:C, sl] = sol[:, GDN_D:].astype(BF16)


def _gdn_chunk(q, k, v, gates, gates_t, cb):
    T = q.shape[0]
    nc = T // CHUNK
    rows = cb * CHUNK
    kern = functools.partial(_gdn_chunk_body, cb=cb)
    wide = pl.BlockSpec((rows, GDN_W), lambda i: (i, 0))
    return pl.pallas_call(
        kern,
        grid=(nc // cb,),
        in_specs=[wide, wide, wide, pl.BlockSpec((rows, LANES), lambda i: (i, 0)),
                  pl.BlockSpec((2 * GDN_CHAINS, rows), lambda i: (0, i))],
        out_specs=[
            pl.BlockSpec((2, rows, GDN_W), lambda i: (0, i, 0)),
            pl.BlockSpec((2, cb, 2 * CHUNK, GDN_W), lambda i: (0, i, 0, 0)),
            pl.BlockSpec((2, cb, GDN_HEADS, CHUNK + GDN_D, CHUNK), lambda i: (0, i, 0, 0, 0)),
            pl.BlockSpec((cb, GDN_CHAINS, LANES), lambda i: (i, 0, 0)),
        ],
        out_shape=[
            jax.ShapeDtypeStruct((2, T, GDN_W), F32),
            jax.ShapeDtypeStruct((2, nc, 2 * CHUNK, GDN_W), BF16),
            jax.ShapeDtypeStruct((2, nc, GDN_HEADS, CHUNK + GDN_D, CHUNK), BF16),
            jax.ShapeDtypeStruct((nc, GDN_CHAINS, LANES), F32),
        ],
        compiler_params=_cparams(("parallel",)),
        name="gdn_chunk",
    )(q, k, v, gates, gates_t)


def _gdn_scan_body(uf_ref, wqf_ref, akf_ref, decf_ref, ub_ref, wqb_ref, akb_ref, decb_ref,
                   of_ref, ob_ref, state_ref, *, cs):
    @pl.when(pl.program_id(1) == 0)
    def _():
        state_ref[...] = jnp.zeros_like(state_ref)

    C = CHUNK
    refs = ((uf_ref, wqf_ref, akf_ref, decf_ref, of_ref), (ub_ref, wqb_ref, akb_ref, decb_ref, ob_ref))
    chains = [(d, h) for d in range(2) for h in range(GDN_HEADS)]
    st = [state_ref[j] for j in range(GDN_CHAINS)]
    for step in range(cs):
        cidx = (step, cs - 1 - step)
        ws = [jnp.dot(refs[d][1][cidx[d], :, h * GDN_D:(h + 1) * GDN_D], st[j].astype(BF16),
                      preferred_element_type=F32) for j, (d, h) in enumerate(chains)]
        vn = [(refs[d][0][cidx[d] * C:(cidx[d] + 1) * C, h * GDN_D:(h + 1) * GDN_D] - ws[j][:C]).astype(BF16)
              for j, (d, h) in enumerate(chains)]
        rr = [jnp.dot(refs[d][2][cidx[d], h], vn[j], preferred_element_type=F32)
              for j, (d, h) in enumerate(chains)]
        for j, (d, h) in enumerate(chains):
            c = cidx[d]
            refs[d][4][c * C:(c + 1) * C, h * GDN_D:(h + 1) * GDN_D] = (ws[j][C:] + rr[j][:C]).astype(BF16)
            st[j] = st[j] * refs[d][3][c, j:j + 1, :] + rr[j][C:]
    for j in range(GDN_CHAINS):
        state_ref[j] = st[j]


def _gdn_scan(u, wq, ak, dec, B, S, cs):
    T = B * S
    nb = S // (CHUNK * cs)
    rows = cs * CHUNK
    kern = functools.partial(_gdn_scan_body, cs=cs)
    fwd = lambda b, i: b * nb + i
    bwd = lambda b, i: b * nb + nb - 1 - i

    def specs(d, pos):
        return [
            pl.BlockSpec((None, rows, GDN_W), lambda b, i: (d, pos(b, i), 0)),
            pl.BlockSpec((None, cs, 2 * CHUNK, GDN_W), lambda b, i: (d, pos(b, i), 0, 0)),
            pl.BlockSpec((None, cs, GDN_HEADS, CHUNK + GDN_D, CHUNK), lambda b, i: (d, pos(b, i), 0, 0, 0)),
            pl.BlockSpec((cs, GDN_CHAINS, LANES), lambda b, i: (pos(b, i), 0, 0)),
        ]

    out_sd = jax.ShapeDtypeStruct((T, GDN_W), BF16)
    return pl.pallas_call(
        kern,
        grid=(B, nb),
        in_specs=specs(0, fwd) + specs(1, bwd),
        out_specs=[pl.BlockSpec((rows, GDN_W), lambda b, i: (fwd(b, i), 0)),
                   pl.BlockSpec((rows, GDN_W), lambda b, i: (bwd(b, i), 0))],
        out_shape=[out_sd, out_sd],
        scratch_shapes=[pltpu.VMEM((GDN_CHAINS, GDN_D, GDN_D), F32)],
        compiler_params=_cparams(("parallel", "arbitrary")),
        name="gdn_scan",
    )(u, wq, ak, dec, u, wq, ak, dec)


def _attn_heads(q, k, v, qw, kw, cos, sin, bias, qo_ref, ko_ref, vo_ref):
    rows = cos.shape[0]
    lane = lax.broadcasted_iota(jnp.int32, (rows, LANES), 1)
    even_lane = lane % 2 == 0
    first_head = lane < ATT_D
    extra_lane = lane == ATT_D
    mi = lax.broadcasted_iota(jnp.int32, (LANES, LANES), 0) // ATT_D
    mj = lax.broadcasted_iota(jnp.int32, (LANES, LANES), 1) // ATT_D
    head_mean = jnp.where(mi == mj, 1.0 / ATT_D, 0.0).astype(BF16)

    def norm_rope(x, w):
        sq = x * x
        hi = sq.astype(BF16)
        lo = (sq - hi.astype(F32)).astype(BF16)
        ms = (jnp.dot(hi, head_mean, preferred_element_type=F32)
              + jnp.dot(lo, head_mean, preferred_element_type=F32))
        y = x * lax.rsqrt(ms + EPS) * w
        partner = jnp.where(even_lane, pltpu.roll(y, LANES - 1, 1), pltpu.roll(y, 1, 1))
        return y * cos + partner * sin

    def split_heads(r, extra):
        tail = jnp.where(extra_lane, extra, 0.0)
        return (jnp.where(first_head, r, tail).astype(BF16),
                jnp.where(first_head, pltpu.roll(r, ATT_D, 1), tail).astype(BF16))

    scale = LOG2E * ATT_D ** -0.5
    for c in range(ATT_HEADS // 2):
        r = norm_rope(q[:, c * LANES:(c + 1) * LANES], qw) * scale
        qo_ref[2 * c], qo_ref[2 * c + 1] = split_heads(r, bias)
    for c in range(ATT_KV_HEADS // 2):
        r = norm_rope(k[:, c * LANES:(c + 1) * LANES], kw)
        ko_ref[2 * c], ko_ref[2 * c + 1] = split_heads(r, 1.0)
        vo_ref[2 * c], vo_ref[2 * c + 1] = split_heads(v[:, c * LANES:(c + 1) * LANES], 1.0)


def _attn_body(safe_ref, q_ref, k_ref, v_ref, o_ref, m_ref, acc_ref, *, tq, tk):
    safe = safe_ref[0] != 0
    nk = k_ref.shape[0] // tk
    q = q_ref[...].reshape(ATT_GROUP * tq, ATT_DP)
    acc_ref[...] = jnp.zeros_like(acc_ref)

    def scores(j):
        keys = pl.ds(pl.multiple_of(j * tk, tk), tk)
        s = lax.dot_general(q, k_ref[keys, :], (((1,), (1,)), ((), ())), preferred_element_type=F32)
        return s, v_ref[keys, :]

    @pl.when(safe)
    def _():
        def step(j, carry):
            s, v = scores(j)
            acc_ref[...] += jnp.dot(jnp.exp2(s).astype(BF16), v, preferred_element_type=F32)
            return carry
        lax.fori_loop(0, nk, step, 0)

    @pl.when(jnp.logical_not(safe))
    def _():
        m_ref[...] = jnp.full_like(m_ref, -jnp.inf)

        def step(j, carry):
            s, v = scores(j)
            m_prev = m_ref[...]
            m_new = jnp.maximum(m_prev, jnp.max(s, axis=-1, keepdims=True))
            p = jnp.exp2(s - m_new).astype(BF16)
            acc_ref[...] = jnp.exp2(m_prev - m_new) * acc_ref[...] + jnp.dot(p, v, preferred_element_type=F32)
            m_ref[...] = m_new
            return carry
        lax.fori_loop(0, nk, step, 0)

    acc = acc_ref[...]
    o = acc[:, :ATT_D] / acc[:, ATT_D:ATT_D + 1]
    for h in range(ATT_GROUP):
        o_ref[:, h * ATT_D:(h + 1) * ATT_D] = o[h * tq:(h + 1) * tq].astype(o_ref.dtype)


def _attention(safe, q, k, v, B, S, tq, tk):
    kern = functools.partial(_attn_body, tq=tq, tk=tk)
    gw = ATT_GROUP * ATT_D
    return pl.pallas_call(
        kern,
        grid_spec=pltpu.PrefetchScalarGridSpec(
            num_scalar_prefetch=1,
            grid=(B, ATT_KV_HEADS, S // tq),
            in_specs=[
                pl.BlockSpec((None, ATT_GROUP, tq, ATT_DP), lambda b, g, i, s: (b, g, i, 0)),
                pl.BlockSpec((None, None, S, ATT_DP), lambda b, g, i, s: (b, g, 0, 0)),
                pl.BlockSpec((None, None, S, ATT_DP), lambda b, g, i, s: (b, g, 0, 0)),
            ],
            out_specs=pl.BlockSpec((None, tq, gw), lambda b, g, i, s: (b, i, g)),
            scratch_shapes=[
                pltpu.VMEM((ATT_GROUP * tq, 1), F32),
                pltpu.VMEM((ATT_GROUP * tq, ATT_DP), F32),
            ],
        ),
        out_shape=jax.ShapeDtypeStruct((B, S, ATT_Q), BF16),
        compiler_params=_cparams(("parallel", "parallel", "parallel")),
        name="attention",
    )(safe, q, k, v)


def _outproj_body(of_ref, ob_ref, z_ref, att_ref, x_ref, gnw_ref, wa_ref, wb_ref,
                  fnw_ref, wr_ref, triu_ref, xm_ref, h_ref, rt_ref, c_ref, count_ref):
    o = of_ref[...].astype(F32) + ob_ref[...].astype(F32)
    z = z_ref[...]
    parts = []
    for h in range(GDN_HEADS):
        sl = slice(h * GDN_D, (h + 1) * GDN_D)
        t = o[:, sl]
        t = t * lax.rsqrt(jnp.mean(t * t, axis=-1, keepdims=True) + EPS) * gnw_ref[...]
        parts.append((t * _silu(z[:, sl])).astype(BF16))
    mix_a = jnp.concatenate(parts, axis=-1)
    xm = x_ref[...] + jnp.dot(mix_a, wa_ref[...], preferred_element_type=F32)
    xm = xm + jnp.dot(att_ref[...], wb_ref[...], preferred_element_type=F32)
    xm_ref[...] = xm
    hn = xm * lax.rsqrt(jnp.mean(xm * xm, axis=-1, keepdims=True) + EPS) * fnw_ref[...]
    _store_token_tiles(h_ref, hn)
    hn_hi = hn.astype(BF16)
    hn_lo = (hn - hn_hi.astype(F32)).astype(BF16)
    both = jnp.dot(hn_hi, wr_ref[...], preferred_element_type=F32)
    logits = (both[:, :LANES] + both[:, LANES:]
              + jnp.dot(hn_lo, wr_ref[:, :LANES], preferred_element_type=F32))

    lt = logits.T[0:ROUTER_ROWS, :]
    tok = lt.shape[1]
    slot = lax.broadcasted_iota(jnp.int32, (ROUTER_ROWS, tok), 0)
    big = jnp.int32(ROUTER_ROWS)
    neg = -jnp.inf

    def masked_top(vals, mask):
        m = jnp.max(jnp.where(mask, vals, neg), axis=0, keepdims=True)
        idx = jnp.min(jnp.where(mask & (vals == m), slot, big), axis=0, keepdims=True)
        return m, idx

    gmask = slot < N_GROUPS
    gmax, gsel = masked_top(lt, gmask)
    gp_top = 1.0 / jnp.sum(jnp.where(gmask, jnp.exp(lt - gmax), 0.0), axis=0, keepdims=True)
    lo = N_GROUPS + gsel * EXPERTS_PER_GROUP
    emask = (slot >= lo) & (slot < lo + EXPERTS_PER_GROUP)
    m1, i1 = masked_top(lt, emask)
    ex = jnp.where(emask, jnp.exp(lt - m1), 0.0)
    pf = ex / jnp.sum(ex, axis=0, keepdims=True)
    p1, _ = masked_top(pf, emask)
    p2, i2 = masked_top(pf, emask & (slot != i1))
    denom = p1 + p2
    g1 = gp_top * p1 / denom
    g2 = gp_top * p2 / denom
    e1 = (i1 - N_GROUPS).astype(F32)
    e2 = (i2 - N_GROUPS).astype(F32)

    @pl.when(pl.program_id(0) == 0)
    def _():
        count_ref[...] = jnp.zeros_like(count_ref)

    hit1 = slot == i1
    hit2 = slot == i2
    picked = jnp.where(hit1 | hit2, 1.0, 0.0)
    seen = jnp.dot(picked.astype(BF16), triu_ref[...], preferred_element_type=F32)
    counted = count_ref[:, 0:1]
    before = counted + seen - picked
    rank1 = jnp.sum(jnp.where(hit1, before, 0.0), axis=0, keepdims=True)
    rank2 = jnp.sum(jnp.where(hit2, before, 0.0), axis=0, keepdims=True)
    total = jnp.broadcast_to(counted + seen[:, tok - 1:tok], count_ref.shape)
    count_ref[...] = total
    c_ref[...] = total
    rt_ref[...] = jnp.concatenate([e1, e2, g1, g2, rank1, rank2, jnp.zeros((SUBLANES - 6, tok), F32)], axis=0)


def _outproj(o_f, o_b, proj, att, x2, gnw, w_a, w_b, fnw, w_r, tm):
    T, D = x2.shape
    row = lambda i: (i, 0)
    const = lambda i: (0, 0)
    return pl.pallas_call(
        _outproj_body,
        grid=(T // tm,),
        in_specs=[
            pl.BlockSpec((tm, GDN_W), row),
            pl.BlockSpec((tm, GDN_W), row),
            pl.BlockSpec((tm, GDN_W), lambda i: (i, COL_Z // GDN_W)),
            pl.BlockSpec((tm, ATT_Q), row),
            pl.BlockSpec((tm, D), row),
            pl.BlockSpec((1, GDN_D), const),
            pl.BlockSpec((GDN_W, D), const),
            pl.BlockSpec((ATT_Q, D), lambda i: (GDN_W // ATT_Q, 0)),
            pl.BlockSpec((1, D), const),
            pl.BlockSpec((D, 2 * LANES), const),
            pl.BlockSpec((tm, tm), const),
        ],
        out_specs=[pl.BlockSpec((tm, D), row), pl.BlockSpec((tm * ROW_TILE, LANES), row),
                   pl.BlockSpec((SUBLANES, tm), lambda i: (0, i)),
                   pl.BlockSpec((ROUTER_ROWS, LANES), const)],
        out_shape=[jax.ShapeDtypeStruct((T, D), F32), jax.ShapeDtypeStruct((T * ROW_TILE, LANES), F32),
                   jax.ShapeDtypeStruct((SUBLANES, T), F32),
                   jax.ShapeDtypeStruct((ROUTER_ROWS, LANES), F32)],
        scratch_shapes=[pltpu.VMEM((ROUTER_ROWS, LANES), F32)],
        compiler_params=_cparams(("arbitrary",)),
        name="outproj_router",
    )(o_f, o_b, proj, att, x2, gnw, w_a, w_b, fnw, w_r, jnp.asarray(np.tri(tm, dtype=np.float32).T, BF16))


def _gather_start(src_hbm, dst_buf, sem, idx_ref, base, slot, n):
    for r in range(n):
        src = pl.multiple_of(idx_ref[base + r] * ROW_TILE, ROW_TILE)
        dst = pl.multiple_of((slot * n + r) * ROW_TILE, ROW_TILE)
        pltpu.make_async_copy(src_hbm.at[pl.ds(src, ROW_TILE)], dst_buf.at[pl.ds(dst, ROW_TILE)],
                              sem.at[slot]).start(priority=r % DMA_QUEUES)


def _gather_wait(src_hbm, dst_buf, sem, slot, n):
    dst = pl.multiple_of(slot * n * ROW_TILE, ROW_TILE)
    pltpu.make_async_copy(src_hbm.at[pl.ds(0, n * ROW_TILE)], dst_buf.at[pl.ds(dst, n * ROW_TILE)],
                          sem.at[slot]).wait()


def _experts_body(be_ref, src_ref, nused_ref,
                  h_hbm, wg_hbm, wu_hbm, wd_hbm, y_ref, xbuf, wg_buf, wu_buf, wd_buf, ws_ref, sem, wsem, *, bm):
    i = pl.program_id(0)
    n_blocks = pl.num_programs(0)
    n_used = nused_ref[0]
    slot = i % ROW_BUFFERS
    expert = be_ref[i]
    first = (i == 0) | (expert != be_ref[jnp.maximum(i - 1, 0)])

    def weight_copies(e, s):
        return [pltpu.make_async_copy(hbm.at[e], buf.at[s], wsem.at[s])
                for hbm, buf in ((wg_hbm, wg_buf), (wu_hbm, wu_buf), (wd_hbm, wd_buf))]

    def compute():
        @pl.when(first)
        def _():
            cur = 1 - ws_ref[0]
            ws_ref[0] = cur
            for c in weight_copies(0, cur):
                c.wait()
            nxt = lax.while_loop(lambda j: (j < n_used) & (be_ref[jnp.minimum(j, n_blocks - 1)] == expert),
                                 lambda j: j + 1, i + 1)

            @pl.when(nxt < n_used)
            def _():
                for c in weight_copies(be_ref[jnp.minimum(nxt, n_blocks - 1)], 1 - cur):
                    c.start()

        ws = ws_ref[0]

        _gather_wait(h_hbm, xbuf, sem, slot, bm)
        x = _load_token_tiles(xbuf, slot * (bm * ROW_TILE), bm).astype(BF16)
        gate = jnp.dot(x, wg_buf[ws].astype(BF16), preferred_element_type=F32)
        up = jnp.dot(x, wu_buf[ws].astype(BF16), preferred_element_type=F32)
        hid = (_silu(gate) * up).astype(BF16)
        _store_token_tiles(y_ref, jnp.dot(hid, wd_buf[ws].astype(BF16), preferred_element_type=F32))

    @pl.when((i == 0) & (n_used > 0))
    def _():
        ws_ref[0] = 1
        for c in weight_copies(be_ref[0], 0):
            c.start()
        for blk in range(ROW_BUFFERS - 1):
            @pl.when(blk < n_used)
            def _():
                _gather_start(h_hbm, xbuf, sem, src_ref, blk * bm, blk, bm)

    @pl.when(i + ROW_BUFFERS - 1 < n_used)
    def _():
        _gather_start(h_hbm, xbuf, sem, src_ref, (i + ROW_BUFFERS - 1) * bm, (i + ROW_BUFFERS - 1) % ROW_BUFFERS, bm)

    @pl.when(i < n_used)
    def _():
        compute()

    @pl.when(i >= n_used)
    def _():
        y_ref[...] = jnp.zeros_like(y_ref)


def _experts(block_expert, src_tok, n_used, h2, w_gate, w_up, w_down, bm):
    D = ROW_TILE * LANES
    P = src_tok.shape[0]
    n_blocks = P // bm
    FF = w_gate.shape[-1]
    kern = functools.partial(_experts_body, bm=bm)
    hbm = pl.BlockSpec(memory_space=pl.ANY)
    return pl.pallas_call(
        kern,
        grid_spec=pltpu.PrefetchScalarGridSpec(
            num_scalar_prefetch=3,
            grid=(n_blocks,),
            in_specs=[hbm, hbm, hbm, hbm],
            out_specs=pl.BlockSpec((bm * ROW_TILE, LANES), lambda i, *_: (i, 0)),
            scratch_shapes=[
                pltpu.VMEM((ROW_BUFFERS * bm * ROW_TILE, LANES), F32),
                pltpu.VMEM((2, D, FF), F32), pltpu.VMEM((2, D, FF), F32), pltpu.VMEM((2, FF, D), F32),
                pltpu.SMEM((1,), jnp.int32),
                pltpu.SemaphoreType.DMA((ROW_BUFFERS,)), pltpu.SemaphoreType.DMA((2,)),
            ],
        ),
        out_shape=jax.ShapeDtypeStruct((P * ROW_TILE, LANES), F32),
        compiler_params=_cparams(("arbitrary",)),
        name="moe_experts",
    )(block_expert, src_tok, n_used, h2, w_gate, w_up, w_down)


def _combine_body(dest_ref, y_hbm, xm_ref, r_ref, fw_ref, o_ref, ybuf, sem, *, tc):
    i = pl.program_id(0)
    n = pl.num_programs(0)
    slot = i % 2
    rows = TOP_K * tc

    @pl.when(i == 0)
    def _():
        _gather_start(y_hbm, ybuf, sem, dest_ref, 0, 0, rows)

    @pl.when(i + 1 < n)
    def _():
        _gather_start(y_hbm, ybuf, sem, dest_ref, (i + 1) * rows, 1 - slot, rows)

    _gather_wait(y_hbm, ybuf, sem, slot, rows)
    route = r_ref[...].T
    y0 = _load_token_tiles(ybuf, slot * (rows * ROW_TILE), tc)
    y1 = _load_token_tiles(ybuf, (slot * rows + tc) * ROW_TILE, tc)
    xo = xm_ref[...] + route[:, 2:3] * y0 + route[:, 3:4] * y1
    o_ref[...] = xo * lax.rsqrt(jnp.mean(xo * xo, axis=-1, keepdims=True) + EPS) * fw_ref[...]


def _combine(dest_blocked, y_buf, x_mid, route, final_w, tc):
    T, D = x_mid.shape
    kern = functools.partial(_combine_body, tc=tc)
    return pl.pallas_call(
        kern,
        grid_spec=pltpu.PrefetchScalarGridSpec(
            num_scalar_prefetch=1,
            grid=(T // tc,),
            in_specs=[
                pl.BlockSpec(memory_space=pl.ANY),
                pl.BlockSpec((tc, D), lambda i, d: (i, 0)),
                pl.BlockSpec((SUBLANES, tc), lambda i, d: (0, i)),
                pl.BlockSpec((1, D), lambda i, d: (0, 0)),
            ],
            out_specs=pl.BlockSpec((tc, D), lambda i, d: (i, 0)),
            scratch_shapes=[pltpu.VMEM((2 * TOP_K * tc * ROW_TILE, LANES), F32), pltpu.SemaphoreType.DMA((2,))],
        ),
        out_shape=jax.ShapeDtypeStruct((T, D), F32),
        compiler_params=_cparams(("arbitrary",)),
        name="moe_combine",
    )(dest_blocked, y_buf, x_mid, route, final_w.reshape(1, D))


def _layer(x2, B, S, norm_mix_w, w_in, conv_w, a_log, dt_bias, gdn_norm_w, q_norm_w, k_norm_w, w_out,
           norm_ffn_w, w_router_group, w_router_expert, w_gate, w_up, w_down, final_w):
    T, D = x2.shape
    tl = _tiles(B, S)
    o_gate = 4 * GDN_W
    o_qb = o_gate + 2 * GDN_CHAINS
    w_all = jnp.concatenate([w_in[:, :o_gate], w_in[:, o_qb:], w_in[:, o_gate:o_qb],
                             jnp.zeros((D, LANES - 2 * GDN_CHAINS), w_in.dtype)], axis=1).astype(BF16)

    rows = S // GRID_W
    rowp = np.repeat(np.arange(rows), GRID_W).astype(np.float64)
    colp = np.tile(np.arange(GRID_W), rows).astype(np.float64)
    axis_dims = ATT_D // 2
    inv_freq = ROPE_THETA ** (-np.arange(0, axis_dims, 2, dtype=np.float64) / axis_dims)
    ang = np.concatenate([rowp[:, None] * inv_freq, colp[:, None] * inv_freq], axis=-1)
    pair_sign = np.tile(np.array([-1.0, 1.0]), axis_dims)
    cosf = jnp.asarray(np.tile(np.repeat(np.cos(ang), 2, axis=1), (1, LANES // ATT_D)), F32)
    sinf = jnp.asarray(np.tile(np.repeat(np.sin(ang), 2, axis=1) * pair_sign, (1, LANES // ATT_D)), F32)
    q_gain = jnp.max(jnp.abs(q_norm_w)).astype(F32)
    k_gain = jnp.max(jnp.abs(k_norm_w)).astype(F32)
    score_bound = ATT_D ** 0.5 * q_gain * k_gain
    safe = (2.0 * score_bound <= SOFTMAX_SAFE_SPAN).astype(jnp.int32).reshape(1)
    bias = (-LOG2E * score_bound).reshape(1, 1)
    pair = lambda w: jnp.tile(w, LANES // ATT_D).reshape(1, LANES)

    proj, gate_logits, qh, kh, vh = _inproj(x2, norm_mix_w, w_all, pair(q_norm_w), pair(k_norm_w),
                                            cosf, sinf, bias, B, S, tl.proj_rows)

    conv_w8 = jnp.concatenate([conv_w, jnp.zeros((SUBLANES - CONV_W, conv_w.shape[1]), F32)], axis=0)
    gp = jnp.zeros((SUBLANES, LANES), F32)
    gp = gp.at[0, GDN_CHAINS:2 * GDN_CHAINS].set(jnp.exp(a_log.astype(F32)).reshape(-1))
    gp = gp.at[1, GDN_CHAINS:2 * GDN_CHAINS].set(dt_bias.astype(F32).reshape(-1))
    q_a, k_a, v_a, gates, gates_t = _gdn_prep(proj, gate_logits, conv_w8, gp, B, S, tl.prep_rows)
    u, wq, ak, dec = _gdn_chunk(q_a, k_a, v_a, gates, gates_t, tl.gdn_chunks)
    o_f, o_b = _gdn_scan(u, wq, ak, dec, B, S, tl.scan_chunks)

    att = _attention(safe, qh, kh, vh, B, S, tl.att_q, tl.att_k).reshape(T, ATT_Q)

    w_r32 = jnp.concatenate([w_router_group, w_router_expert,
                             jnp.zeros((D, LANES - N_GROUPS - N_EXPERTS), F32)], axis=1).astype(F32)
    w_r_hi = w_r32.astype(BF16)
    w_r = jnp.concatenate([w_r_hi, (w_r32 - w_r_hi.astype(F32)).astype(BF16)], axis=1)
    w_out_bf = w_out.astype(BF16)
    x_mid, h2, route_t, count_rows = _outproj(
        o_f, o_b, proj, att, x2, gdn_norm_w.reshape(1, GDN_D), w_out_bf, w_out_bf,
        norm_ffn_w.reshape(1, D), w_r, tl.proj_rows)

    bm = tl.moe_rows
    n_assign = T * TOP_K
    n_blocks = -(-(n_assign + N_EXPERTS * (bm - 1)) // bm)
    experts = jnp.arange(N_EXPERTS, dtype=jnp.int32)
    counts = count_rows[N_GROUPS:N_GROUPS + N_EXPERTS, 0].astype(jnp.int32)
    padded = (counts + bm - 1) // bm * bm
    pad_end = jnp.cumsum(padded)
    pad_start = pad_end - padded
    block_start = jnp.arange(n_blocks, dtype=jnp.int32) * bm
    done = (pad_end[None, :] <= block_start[:, None]).astype(jnp.int32)
    begun = (pad_start[None, :] <= block_start[:, None]).astype(jnp.int32)
    block_expert = jnp.minimum(jnp.sum(done, axis=1), N_EXPERTS - 1)
    n_used = (pad_end[-1:] // bm).astype(jnp.int32)
    seg_start = jnp.sum(done * padded[None, :], axis=1)
    seg_entry = jnp.sum(done * counts[None, :], axis=1)
    seg_count = jnp.sum(begun * counts[None, :], axis=1) - seg_entry
    e_rows = route_t[0:TOP_K].astype(jnp.int32)
    rank_rows = route_t[4:4 + TOP_K].astype(jnp.int32)
    dest = jnp.sum(jnp.where(e_rows[:, :, None] == experts, pad_start, 0), axis=-1) + rank_rows
    tokens = jnp.tile(jnp.arange(T, dtype=jnp.int32), TOP_K)
    _, compact = lax.sort((dest.reshape(-1), tokens), num_keys=1)
    row = block_start[:, None] + jnp.arange(bm, dtype=jnp.int32)[None, :]
    seg_row = row - seg_start[:, None]
    holds_token = seg_row < seg_count[:, None]
    entry = jnp.clip(seg_entry[:, None] + seg_row, 0, n_assign - 1)
    src_tok = jnp.where(holds_token, compact[entry], lax.rem(row, jnp.full_like(row, T))).reshape(-1)

    y_buf = _experts(block_expert, src_tok, n_used, h2, w_gate, w_up, w_down, bm)

    tc = tl.comb_rows
    dest_blocked = dest.reshape(TOP_K, T // tc, tc).transpose(1, 0, 2).reshape(-1)
    return _combine(dest_blocked, y_buf, x_mid, route_t, final_w, tc)


def kernel(x, norm_mix_w, w_in, conv_w, a_log, dt_bias, gdn_norm_w, q_norm_w, k_norm_w, w_out, norm_ffn_w,
           w_router_group, w_router_expert, w_gate, w_up, w_down, final_norm_w):
    B, S, D = x.shape
    depth = w_in.shape[0]
    assert depth == 1, "the final norm is fused into the last (only) layer's combine step"
    out = _layer(x.reshape(B * S, D), B, S, norm_mix_w[0], w_in[0], conv_w[0], a_log[0], dt_bias[0],
                 gdn_norm_w[0], q_norm_w[0], k_norm_w[0], w_out[0], norm_ffn_w[0], w_router_group[0],
                 w_router_expert[0], w_gate[0], w_up[0], w_down[0], final_norm_w)
    return out.reshape(B, S, D)
```

```python
import functools
import math
from typing import NamedTuple

import jax
import jax.numpy as jnp
import numpy as np
from jax import lax
from jax.experimental import pallas as pl
from jax.experimental.pallas import tpu as pltpu

F32 = jnp.float32
BF16 = jnp.bfloat16
EPS = 1e-6

GRID_W = 64
GDN_HEADS = 4
GDN_D = 128
CONV_W = 5
CHUNK = 64
ATT_HEADS = 8
ATT_KV_HEADS = 2
ATT_GROUP = ATT_HEADS // ATT_KV_HEADS
ATT_D = 64
ROPE_THETA = 10000.0
N_GROUPS = 4
EXPERTS_PER_GROUP = 8
N_EXPERTS = N_GROUPS * EXPERTS_PER_GROUP
TOP_K = 2

GDN_W = GDN_HEADS * GDN_D
GDN_CHAINS = 2 * GDN_HEADS
ATT_Q = ATT_HEADS * ATT_D
ATT_KV = ATT_KV_HEADS * ATT_D
LANES = 128
SUBLANES = 8
ATT_DP = LANES
ROUTER_ROWS = -(-(N_GROUPS + N_EXPERTS) // SUBLANES) * SUBLANES

COL_Z = 3 * GDN_W
COL_QB = COL_Z + GDN_W
COL_KB = COL_QB + ATT_Q
COL_VB = COL_KB + ATT_KV
COL_GATE = COL_VB + ATT_KV
D_PROJ = COL_GATE + LANES

VMEM_LIMIT = 56 * 1024 * 1024
LOG2E = math.log2(math.e)
SOFTMAX_SAFE_SPAN = 60.0


class Tiles(NamedTuple):
    proj_rows: int
    prep_rows: int
    gdn_chunks: int
    scan_chunks: int
    att_q: int
    att_k: int
    moe_rows: int
    comb_rows: int


def _tile(n, want):
    t = min(n, want)
    assert n % t == 0, (n, want)
    return t


def _tiles(B, S):
    T = B * S
    n_chunks = S // CHUNK
    return Tiles(proj_rows=_tile(T, 512), prep_rows=_tile(S, 512), gdn_chunks=_tile(n_chunks, 8),
                 scan_chunks=_tile(n_chunks, 8), att_q=_tile(S, 512), att_k=_tile(S, 2048),
                 moe_rows=256, comb_rows=_tile(T, 256))


def _cparams(sem):
    return pltpu.CompilerParams(dimension_semantics=sem, vmem_limit_bytes=VMEM_LIMIT)


def _silu(x):
    return x * jax.nn.sigmoid(x)


ROW_TILE = SUBLANES
DMA_QUEUES = 2
ROW_BUFFERS = 3


def _store_token_tiles(ref, x, base=0):
    rows, d = x.shape
    assert d == ROW_TILE * LANES
    for c in range(ROW_TILE):
        ref[pl.ds(base + c, rows, stride=ROW_TILE), :] = x[:, c * LANES:(c + 1) * LANES]


def _load_token_tiles(ref, base, rows):
    return jnp.concatenate([ref[pl.ds(base + c, rows, stride=ROW_TILE), :] for c in range(ROW_TILE)], axis=-1)


def _inproj_body(x_ref, nw_ref, w_ref, qw_ref, kw_ref, cos_ref, sin_ref, bias_ref,
                 o_ref, g_ref, qo_ref, ko_ref, vo_ref):
    x = x_ref[...]
    h = x * lax.rsqrt(jnp.mean(x * x, axis=-1, keepdims=True) + EPS) * nw_ref[...]
    acc = jnp.dot(h.astype(BF16), w_ref[...], preferred_element_type=F32)
    o_ref[...] = acc[:, :COL_QB]
    g_ref[...] = acc[:, COL_GATE:]
    _attn_heads(acc[:, COL_QB:COL_KB], acc[:, COL_KB:COL_VB], acc[:, COL_VB:COL_GATE], qw_ref[...], kw_ref[...],
                cos_ref[...], sin_ref[...], bias_ref[...], qo_ref, ko_ref, vo_ref)


def _inproj(x2, norm_w, w_all, qw, kw, cosf, sinf, bias, B, S, tm):
    T, D = x2.shape
    nr = S // tm
    const = lambda i: (0, 0)
    heads = lambda i: (i // nr, 0, i % nr, 0)
    return pl.pallas_call(
        _inproj_body,
        grid=(T // tm,),
        in_specs=[
            pl.BlockSpec((tm, D), lambda i: (i, 0)),
            pl.BlockSpec((1, D), const),
            pl.BlockSpec((D, D_PROJ), const),
            pl.BlockSpec((1, LANES), const),
            pl.BlockSpec((1, LANES), const),
            pl.BlockSpec((tm, LANES), lambda i: (i % nr, 0)),
            pl.BlockSpec((tm, LANES), lambda i: (i % nr, 0)),
            pl.BlockSpec((1, 1), const),
        ],
        out_specs=[
            pl.BlockSpec((tm, COL_QB), lambda i: (i, 0)),
            pl.BlockSpec((tm, LANES), lambda i: (i, 0)),
            pl.BlockSpec((None, ATT_HEADS, tm, ATT_DP), heads),
            pl.BlockSpec((None, ATT_KV_HEADS, tm, ATT_DP), heads),
            pl.BlockSpec((None, ATT_KV_HEADS, tm, ATT_DP), heads),
        ],
        out_shape=[
            jax.ShapeDtypeStruct((T, COL_QB), F32),
            jax.ShapeDtypeStruct((T, LANES), F32),
            jax.ShapeDtypeStruct((B, ATT_HEADS, S, ATT_DP), BF16),
            jax.ShapeDtypeStruct((B, ATT_KV_HEADS, S, ATT_DP), BF16),
            jax.ShapeDtypeStruct((B, ATT_KV_HEADS, S, ATT_DP), BF16),
        ],
        compiler_params=_cparams(("parallel",)),
        name="inproj",
    )(x2, norm_w.reshape(1, D), w_all, qw, kw, cosf, sinf, bias)


def _gdn_prep_body(cur_ref, prev_ref, next_ref, cw_ref, gin_ref, gp_ref,
                   q_ref, k_ref, v_ref, g_ref, gt_ref, ext_ref, *, tr):
    i = pl.program_id(1)
    nr = pl.num_programs(1)
    halo = prev_ref.shape[0]
    pad = CONV_W // 2
    ext_ref[0:halo, :] = jnp.where(i > 0, prev_ref[...], 0.0)
    ext_ref[halo:halo + tr, :] = cur_ref[...]
    ext_ref[halo + tr:2 * halo + tr, :] = jnp.where(i < nr - 1, next_ref[...], 0.0)
    acc = cw_ref[0:1, :] * ext_ref[pl.ds(halo - pad, tr), :]
    for j in range(1, CONV_W):
        acc = acc + cw_ref[j:j + 1, :] * ext_ref[pl.ds(halo - pad + j, tr), :]
    y = _silu(acc)
    for h in range(GDN_HEADS):
        for base, ref, scale in ((0, q_ref, GDN_D ** -0.5), (GDN_W, k_ref, 1.0)):
            t = y[:, base + h * GDN_D: base + (h + 1) * GDN_D]
            t = t * (lax.rsqrt(jnp.sum(t * t, axis=-1, keepdims=True) + EPS) * scale)
            ref[:, h * GDN_D:(h + 1) * GDN_D] = t
    v_ref[...] = y[:, 2 * GDN_W:]
    gin = gin_ref[...]
    lane = lax.broadcasted_iota(jnp.int32, gin.shape, 1)
    a = gin + gp_ref[1:2, :]
    softplus = jnp.maximum(a, 0.0) + jnp.log1p(jnp.exp(-jnp.abs(a)))
    g = jnp.where(lane < GDN_CHAINS, jax.nn.sigmoid(gin), -gp_ref[0:1, :] * softplus)
    g_ref[...] = g
    gt_ref[...] = g.T[0:2 * GDN_CHAINS, :]


def _gdn_prep(proj, gate_logits, conv_w8, gate_params, B, S, tr):
    T = B * S
    nr = S // tr
    C = 3 * GDN_W
    halo = SUBLANES
    rb = tr // halo
    n_halo = T // halo
    kern = functools.partial(_gdn_prep_body, tr=tr)
    out_sd = jax.ShapeDtypeStruct((T, GDN_W), F32)
    return pl.pallas_call(
        kern,
        grid=(B, nr),
        in_specs=[
            pl.BlockSpec((tr, C), lambda b, i: (b * nr + i, 0)),
            pl.BlockSpec((halo, C), lambda b, i: (jnp.maximum((b * nr + i) * rb - 1, 0), 0)),
            pl.BlockSpec((halo, C), lambda b, i: (jnp.minimum((b * nr + i + 1) * rb, n_halo - 1), 0)),
            pl.BlockSpec((SUBLANES, C), lambda b, i: (0, 0)),
            pl.BlockSpec((tr, LANES), lambda b, i: (b * nr + i, 0)),
            pl.BlockSpec((SUBLANES, LANES), lambda b, i: (0, 0)),
        ],
        out_specs=[
            pl.BlockSpec((tr, GDN_W), lambda b, i: (b * nr + i, 0)),
            pl.BlockSpec((tr, GDN_W), lambda b, i: (b * nr + i, 0)),
            pl.BlockSpec((tr, GDN_W), lambda b, i: (b * nr + i, 0)),
            pl.BlockSpec((tr, LANES), lambda b, i: (b * nr + i, 0)),
            pl.BlockSpec((2 * GDN_CHAINS, tr), lambda b, i: (0, b * nr + i)),
        ],
        out_shape=[out_sd, out_sd, out_sd, jax.ShapeDtypeStruct((T, LANES), F32),
                   jax.ShapeDtypeStruct((2 * GDN_CHAINS, T), F32)],
        scratch_shapes=[pltpu.VMEM((tr + 2 * halo, C), F32)],
        compiler_params=_cparams(("parallel", "parallel")),
        name="gdn_prep",
    )(proj, proj, proj, conv_w8, gate_logits, gate_params)


def _bdot(a, b):
    return jnp.dot(a.astype(BF16), b.astype(BF16), preferred_element_type=F32)


def _bdot_nt(a, b):
    return lax.dot_general(a.astype(BF16), b.astype(BF16), (((1,), (1,)), ((), ())),
                           preferred_element_type=F32)


def _gdn_chunk_body(q_ref, k_ref, v_ref, g_ref, gt_ref, u_ref, wq_ref, ak_ref, dec_ref, *, cb):
    C = CHUNK
    row = lax.broadcasted_iota(jnp.int32, (C, C), 0)
    col = lax.broadcasted_iota(jnp.int32, (C, C), 1)
    eye = (row == col).astype(F32)
    masks = ((row >= col, row > col), (row <= col, row < col))
    hi = lax.Precision.HIGHEST
    chains = []
    gate_rows = gt_ref[...]
    for c in range(cb):
        rs = slice(c * C, (c + 1) * C)
        gates = g_ref[rs, :]
        gates_t = gate_rows[:, rs]
        g_tot = jnp.sum(gates, axis=0, keepdims=True)
        tot_rows = jnp.sum(gates_t, axis=1, keepdims=True)
        dec_ref[c] = jnp.broadcast_to(jnp.exp(tot_rows[GDN_CHAINS:2 * GDN_CHAINS]), (GDN_CHAINS, LANES))
        for d in range(2):
            incl, strict = masks[d]
            cum = incl.astype(F32)
            gc_cols = jnp.dot(cum, gates, precision=hi, preferred_element_type=F32)
            gc_rows = lax.dot_general(gates_t, cum, (((1,), (1,)), ((), ())), precision=hi,
                                      preferred_element_type=F32)
            for h in range(GDN_HEADS):
                lane_b = d * GDN_HEADS + h
                lane_g = GDN_CHAINS + lane_b
                sl = slice(h * GDN_D, (h + 1) * GDN_D)
                q = q_ref[rs, sl]
                k = k_ref[rs, sl]
                v = v_ref[rs, sl]
                beta = gates[:, lane_b:lane_b + 1]
                gc_col = gc_cols[:, lane_g:lane_g + 1]
                gc_row = gc_rows[lane_g:lane_g + 1, :]
                g_last = g_tot[:, lane_g:lane_g + 1]
                decay = jnp.where(incl, jnp.exp(jnp.where(incl, gc_col - gc_row, 0.0)), 0.0)
                e_col = jnp.exp(gc_col)
                k_beta = k * beta
                kk = _bdot_nt(jnp.concatenate([k_beta, q], axis=0), k)
                neg_l = jnp.where(strict, -kk[:C] * decay, 0.0)
                attn = jnp.where(incl, kk[C:] * decay, 0.0)
                rhs = jnp.concatenate([v * beta, k_beta * e_col], axis=1).astype(BF16)
                wq_ref[d, c, C:2 * C, sl] = (q * e_col).astype(BF16)
                ak_ref[d, c, h, 0:C, :] = attn.astype(BF16)
                ak_ref[d, c, h, C:C + GDN_D, :] = (k * jnp.exp(g_last - gc_col)).T.astype(BF16)
                chains.append((d, c, rs, sl, neg_l, rhs))
    zs = [jnp.concatenate([ch[4], eye], axis=1) for ch in chains]
    keep_s = lax.broadcasted_iota(jnp.int32, (C, 2 * C), 1) >= C
    for _ in range(int(math.log2(C))):
        zs = [_bdot(z[:, :C], z) + jnp.where(keep_s, z, 0.0) for z in zs]
    for (d, c, rs, sl, _, rhs), z in zip(chains, zs):
        sol = jnp.dot(z[:, C:].astype(BF16), rhs, preferred_element_type=F32)
        u_ref[d, rs, sl] = sol[:, :GDN_D]
        wq_ref[d, c, 0:C, sl] = sol[:, GDN_D:].astype(BF16)


def _gdn_chunk(q, k, v, gates, gates_t, cb):
    T = q.shape[0]
    nc = T // CHUNK
    rows = cb * CHUNK
    kern = functools.partial(_gdn_chunk_body, cb=cb)
    wide = pl.BlockSpec((rows, GDN_W), lambda i: (i, 0))
    return pl.pallas_call(
        kern,
        grid=(nc // cb,),
        in_specs=[wide, wide, wide, pl.BlockSpec((rows, LANES), lambda i: (i, 0)),
                  pl.BlockSpec((2 * GDN_CHAINS, rows), lambda i: (0, i))],
        out_specs=[
            pl.BlockSpec((2, rows, GDN_W), lambda i: (0, i, 0)),
            pl.BlockSpec((2, cb, 2 * CHUNK, GDN_W), lambda i: (0, i, 0, 0)),
            pl.BlockSpec((2, cb, GDN_HEADS, CHUNK + GDN_D, CHUNK), lambda i: (0, i, 0, 0, 0)),
            pl.BlockSpec((cb, GDN_CHAINS, LANES), lambda i: (i, 0, 0)),
        ],
        out_shape=[
            jax.ShapeDtypeStruct((2, T, GDN_W), F32),
            jax.ShapeDtypeStruct((2, nc, 2 * CHUNK, GDN_W), BF16),
            jax.ShapeDtypeStruct((2, nc, GDN_HEADS, CHUNK + GDN_D, CHUNK), BF16),
            jax.ShapeDtypeStruct((nc, GDN_CHAINS, LANES), F32),
        ],
        compiler_params=_cparams(("parallel",)),
        name="gdn_chunk",
    )(q, k, v, gates, gates_t)


def _gdn_scan_body(uf_ref, wqf_ref, akf_ref, decf_ref, ub_ref, wqb_ref, akb_ref, decb_ref,
                   of_ref, ob_ref, state_ref, *, cs):
    @pl.when(pl.program_id(1) == 0)
    def _():
        state_ref[...] = jnp.zeros_like(state_ref)

    C = CHUNK
    refs = ((uf_ref, wqf_ref, akf_ref, decf_ref, of_ref), (ub_ref, wqb_ref, akb_ref, decb_ref, ob_ref))
    chains = [(d, h) for d in range(2) for h in range(GDN_HEADS)]
    st = [state_ref[j] for j in range(GDN_CHAINS)]
    for step in range(cs):
        cidx = (step, cs - 1 - step)
        ws = [jnp.dot(refs[d][1][cidx[d], :, h * GDN_D:(h + 1) * GDN_D], st[j].astype(BF16),
                      preferred_element_type=F32) for j, (d, h) in enumerate(chains)]
        vn = [(refs[d][0][cidx[d] * C:(cidx[d] + 1) * C, h * GDN_D:(h + 1) * GDN_D] - ws[j][:C]).astype(BF16)
              for j, (d, h) in enumerate(chains)]
        rr = [jnp.dot(refs[d][2][cidx[d], h], vn[j], preferred_element_type=F32)
              for j, (d, h) in enumerate(chains)]
        for j, (d, h) in enumerate(chains):
            c = cidx[d]
            refs[d][4][c * C:(c + 1) * C, h * GDN_D:(h + 1) * GDN_D] = (ws[j][C:] + rr[j][:C]).astype(BF16)
            st[j] = st[j] * refs[d][3][c, j:j + 1, :] + rr[j][C:]
    for j in range(GDN_CHAINS):
        state_ref[j] = st[j]


def _gdn_scan(u, wq, ak, dec, B, S, cs):
    T = B * S
    nb = S // (CHUNK * cs)
    rows = cs * CHUNK
    kern = functools.partial(_gdn_scan_body, cs=cs)
    fwd = lambda b, i: b * nb + i
    bwd = lambda b, i: b * nb + nb - 1 - i

    def specs(d, pos):
        return [
            pl.BlockSpec((None, rows, GDN_W), lambda b, i: (d, pos(b, i), 0)),
            pl.BlockSpec((None, cs, 2 * CHUNK, GDN_W), lambda b, i: (d, pos(b, i), 0, 0)),
            pl.BlockSpec((None, cs, GDN_HEADS, CHUNK + GDN_D, CHUNK), lambda b, i: (d, pos(b, i), 0, 0, 0)),
            pl.BlockSpec((cs, GDN_CHAINS, LANES), lambda b, i: (pos(b, i), 0, 0)),
        ]

    out_sd = jax.ShapeDtypeStruct((T, GDN_W), BF16)
    return pl.pallas_call(
        kern,
        grid=(B, nb),
        in_specs=specs(0, fwd) + specs(1, bwd),
        out_specs=[pl.BlockSpec((rows, GDN_W), lambda b, i: (fwd(b, i), 0)),
                   pl.BlockSpec((rows, GDN_W), lambda b, i: (bwd(b, i), 0))],
        out_shape=[out_sd, out_sd],
        scratch_shapes=[pltpu.VMEM((GDN_CHAINS, GDN_D, GDN_D), F32)],
        compiler_params=_cparams(("parallel", "arbitrary")),
        name="gdn_scan",
    )(u, wq, ak, dec, u, wq, ak, dec)


def _attn_heads(q, k, v, qw, kw, cos, sin, bias, qo_ref, ko_ref, vo_ref):
    rows = cos.shape[0]
    lane = lax.broadcasted_iota(jnp.int32, (rows, LANES), 1)
    even_lane = lane % 2 == 0
    first_head = lane < ATT_D
    extra_lane = lane == ATT_D
    mi = lax.broadcasted_iota(jnp.int32, (LANES, LANES), 0) // ATT_D
    mj = lax.broadcasted_iota(jnp.int32, (LANES, LANES), 1) // ATT_D
    head_mean = jnp.where(mi == mj, 1.0 / ATT_D, 0.0).astype(BF16)

    def norm_rope(x, w):
        sq = x * x
        hi = sq.astype(BF16)
        lo = (sq - hi.astype(F32)).astype(BF16)
        ms = (jnp.dot(hi, head_mean, preferred_element_type=F32)
              + jnp.dot(lo, head_mean, preferred_element_type=F32))
        y = x * lax.rsqrt(ms + EPS) * w
        partner = jnp.where(even_lane, pltpu.roll(y, LANES - 1, 1), pltpu.roll(y, 1, 1))
        return y * cos + partner * sin

    def split_heads(r, extra):
        tail = jnp.where(extra_lane, extra, 0.0)
        return (jnp.where(first_head, r, tail).astype(BF16),
                jnp.where(first_head, pltpu.roll(r, ATT_D, 1), tail).astype(BF16))

    scale = LOG2E * ATT_D ** -0.5
    for c in range(ATT_HEADS // 2):
        r = norm_rope(q[:, c * LANES:(c + 1) * LANES], qw) * scale
        qo_ref[2 * c], qo_ref[2 * c + 1] = split_heads(r, bias)
    for c in range(ATT_KV_HEADS // 2):
        r = norm_rope(k[:, c * LANES:(c + 1) * LANES], kw)
        ko_ref[2 * c], ko_ref[2 * c + 1] = split_heads(r, 1.0)
        vo_ref[2 * c], vo_ref[2 * c + 1] = split_heads(v[:, c * LANES:(c + 1) * LANES], 1.0)


def _attn_body(safe_ref, q_ref, k_ref, v_ref, o_ref, m_ref, acc_ref, *, tq, tk):
    safe = safe_ref[0] != 0
    nk = k_ref.shape[0] // tk
    q = q_ref[...].reshape(ATT_GROUP * tq, ATT_DP)
    acc_ref[...] = jnp.zeros_like(acc_ref)

    def scores(j):
        keys = pl.ds(pl.multiple_of(j * tk, tk), tk)
        s = lax.dot_general(q, k_ref[keys, :], (((1,), (1,)), ((), ())), preferred_element_type=F32)
        return s, v_ref[keys, :]

    @pl.when(safe)
    def _():
        def step(j, carry):
            s, v = scores(j)
            acc_ref[...] += jnp.dot(jnp.exp2(s).astype(BF16), v, preferred_element_type=F32)
            return carry
        lax.fori_loop(0, nk, step, 0)

    @pl.when(jnp.logical_not(safe))
    def _():
        m_ref[...] = jnp.full_like(m_ref, -jnp.inf)

        def step(j, carry):
            s, v = scores(j)
            m_prev = m_ref[...]
            m_new = jnp.maximum(m_prev, jnp.max(s, axis=-1, keepdims=True))
            p = jnp.exp2(s - m_new).astype(BF16)
            acc_ref[...] = jnp.exp2(m_prev - m_new) * acc_ref[...] + jnp.dot(p, v, preferred_element_type=F32)
            m_ref[...] = m_new
            return carry
        lax.fori_loop(0, nk, step, 0)

    acc = acc_ref[...]
    o = acc[:, :ATT_D] / acc[:, ATT_D:ATT_D + 1]
    for h in range(ATT_GROUP):
        o_ref[:, h * ATT_D:(h + 1) * ATT_D] = o[h * tq:(h + 1) * tq].astype(o_ref.dtype)


def _attention(safe, q, k, v, B, S, tq, tk):
    kern = functools.partial(_attn_body, tq=tq, tk=tk)
    gw = ATT_GROUP * ATT_D
    return pl.pallas_call(
        kern,
        grid_spec=pltpu.PrefetchScalarGridSpec(
            num_scalar_prefetch=1,
            grid=(B, ATT_KV_HEADS, S // tq),
            in_specs=[
                pl.BlockSpec((None, ATT_GROUP, tq, ATT_DP), lambda b, g, i, s: (b, g, i, 0)),
                pl.BlockSpec((None, None, S, ATT_DP), lambda b, g, i, s: (b, g, 0, 0)),
                pl.BlockSpec((None, None, S, ATT_DP), lambda b, g, i, s: (b, g, 0, 0)),
            ],
            out_specs=pl.BlockSpec((None, tq, gw), lambda b, g, i, s: (b, i, g)),
            scratch_shapes=[
                pltpu.VMEM((ATT_GROUP * tq, 1), F32),
                pltpu.VMEM((ATT_GROUP * tq, ATT_DP), F32),
            ],
        ),
        out_shape=jax.ShapeDtypeStruct((B, S, ATT_Q), BF16),
        compiler_params=_cparams(("parallel", "parallel", "parallel")),
        name="attention",
    )(safe, q, k, v)


def _outproj_body(of_ref, ob_ref, z_ref, att_ref, x_ref, gnw_ref, wa_ref, wb_ref,
                  fnw_ref, wr_ref, triu_ref, xm_ref, h_ref, rt_ref, c_ref, count_ref):
    o = of_ref[...].astype(F32) + ob_ref[...].astype(F32)
    z = z_ref[...]
    parts = []
    for h in range(GDN_HEADS):
        sl = slice(h * GDN_D, (h + 1) * GDN_D)
        t = o[:, sl]
        t = t * lax.rsqrt(jnp.mean(t * t, axis=-1, keepdims=True) + EPS) * gnw_ref[...]
        parts.append((t * _silu(z[:, sl])).astype(BF16))
    mix_a = jnp.concatenate(parts, axis=-1)
    xm = x_ref[...] + jnp.dot(mix_a, wa_ref[...], preferred_element_type=F32)
    xm = xm + jnp.dot(att_ref[...], wb_ref[...], preferred_element_type=F32)
    xm_ref[...] = xm
    hn = xm * lax.rsqrt(jnp.mean(xm * xm, axis=-1, keepdims=True) + EPS) * fnw_ref[...]
    _store_token_tiles(h_ref, hn)
    hn_hi = hn.astype(BF16)
    hn_lo = (hn - hn_hi.astype(F32)).astype(BF16)
    both = jnp.dot(hn_hi, wr_ref[...], preferred_element_type=F32)
    logits = (both[:, :LANES] + both[:, LANES:]
              + jnp.dot(hn_lo, wr_ref[:, :LANES], preferred_element_type=F32))

    lt = logits.T[0:ROUTER_ROWS, :]
    tok = lt.shape[1]
    slot = lax.broadcasted_iota(jnp.int32, (ROUTER_ROWS, tok), 0)
    big = jnp.int32(ROUTER_ROWS)
    neg = -jnp.inf

    def masked_top(vals, mask):
        m = jnp.max(jnp.where(mask, vals, neg), axis=0, keepdims=True)
        idx = jnp.min(jnp.where(mask & (vals == m), slot, big), axis=0, keepdims=True)
        return m, idx

    gmask = slot < N_GROUPS
    gmax, gsel = masked_top(lt, gmask)
    gp_top = 1.0 / jnp.sum(jnp.where(gmask, jnp.exp(lt - gmax), 0.0), axis=0, keepdims=True)
    lo = N_GROUPS + gsel * EXPERTS_PER_GROUP
    emask = (slot >= lo) & (slot < lo + EXPERTS_PER_GROUP)
    m1, i1 = masked_top(lt, emask)
    ex = jnp.where(emask, jnp.exp(lt - m1), 0.0)
    pf = ex / jnp.sum(ex, axis=0, keepdims=True)
    p1, _ = masked_top(pf, emask)
    p2, i2 = masked_top(pf, emask & (slot != i1))
    denom = p1 + p2
    g1 = gp_top * p1 / denom
    g2 = gp_top * p2 / denom
    e1 = (i1 - N_GROUPS).astype(F32)
    e2 = (i2 - N_GROUPS).astype(F32)

    @pl.when(pl.program_id(0) == 0)
    def _():
        count_ref[...] = jnp.zeros_like(count_ref)

    hit1 = slot == i1
    hit2 = slot == i2
    picked = jnp.where(hit1 | hit2, 1.0, 0.0)
    seen = jnp.dot(picked.astype(BF16), triu_ref[...], preferred_element_type=F32)
    counted = count_ref[:, 0:1]
    before = counted + seen - picked
    rank1 = jnp.sum(jnp.where(hit1, before, 0.0), axis=0, keepdims=True)
    rank2 = jnp.sum(jnp.where(hit2, before, 0.0), axis=0, keepdims=True)
    total = jnp.broadcast_to(counted + seen[:, tok - 1:tok], count_ref.shape)
    count_ref[...] = total
    c_ref[...] = total
    rt_ref[...] = jnp.concatenate([e1, e2, g1, g2, rank1, rank2, jnp.zeros((SUBLANES - 6, tok), F32)], axis=0)


def _outproj(o_f, o_b, proj, att, x2, gnw, w_a, w_b, fnw, w_r, tm):
    T, D = x2.shape
    row = lambda i: (i, 0)
    const = lambda i: (0, 0)
    return pl.pallas_call(
        _outproj_body,
        grid=(T // tm,),
        in_specs=[
            pl.BlockSpec((tm, GDN_W), row),
            pl.BlockSpec((tm, GDN_W), row),
            pl.BlockSpec((tm, GDN_W), lambda i: (i, COL_Z // GDN_W)),
            pl.BlockSpec((tm, ATT_Q), row),
            pl.BlockSpec((tm, D), row),
            pl.BlockSpec((1, GDN_D), const),
            pl.BlockSpec((GDN_W, D), const),
            pl.BlockSpec((ATT_Q, D), lambda i: (GDN_W // ATT_Q, 0)),
            pl.BlockSpec((1, D), const),
            pl.BlockSpec((D, 2 * LANES), const),
            pl.BlockSpec((tm, tm), const),
        ],
        out_specs=[pl.BlockSpec((tm, D), row), pl.BlockSpec((tm * ROW_TILE, LANES), row),
                   pl.BlockSpec((SUBLANES, tm), lambda i: (0, i)),
                   pl.BlockSpec((ROUTER_ROWS, LANES), const)],
        out_shape=[jax.ShapeDtypeStruct((T, D), F32), jax.ShapeDtypeStruct((T * ROW_TILE, LANES), F32),
                   jax.ShapeDtypeStruct((SUBLANES, T), F32),
                   jax.ShapeDtypeStruct((ROUTER_ROWS, LANES), F32)],
        scratch_shapes=[pltpu.VMEM((ROUTER_ROWS, LANES), F32)],
        compiler_params=_cparams(("arbitrary",)),
        name="outproj_router",
    )(o_f, o_b, proj, att, x2, gnw, w_a, w_b, fnw, w_r, jnp.asarray(np.tri(tm, dtype=np.float32).T, BF16))


def _gather_start(src_hbm, dst_buf, sem, idx_ref, base, slot, n):
    for r in range(n):
        src = pl.multiple_of(idx_ref[base + r] * ROW_TILE, ROW_TILE)
        dst = pl.multiple_of((slot * n + r) * ROW_TILE, ROW_TILE)
        pltpu.make_async_copy(src_hbm.at[pl.ds(src, ROW_TILE)], dst_buf.at[pl.ds(dst, ROW_TILE)],
                              sem.at[slot]).start(priority=r % DMA_QUEUES)


def _gather_wait(src_hbm, dst_buf, sem, slot, n):
    dst = pl.multiple_of(slot * n * ROW_TILE, ROW_TILE)
    pltpu.make_async_copy(src_hbm.at[pl.ds(0, n * ROW_TILE)], dst_buf.at[pl.ds(dst, n * ROW_TILE)],
                          sem.at[slot]).wait()


def _experts_body(be_ref, src_ref, nused_ref,
                  h_hbm, wg_hbm, wu_hbm, wd_hbm, y_ref, xbuf, wg_buf, wu_buf, wd_buf, ws_ref, sem, wsem, *, bm):
    i = pl.program_id(0)
    n_blocks = pl.num_programs(0)
    n_used = nused_ref[0]
    slot = i % ROW_BUFFERS
    expert = be_ref[i]
    first = (i == 0) | (expert != be_ref[jnp.maximum(i - 1, 0)])

    def weight_copies(e, s):
        return [pltpu.make_async_copy(hbm.at[e], buf.at[s], wsem.at[s])
                for hbm, buf in ((wg_hbm, wg_buf), (wu_hbm, wu_buf), (wd_hbm, wd_buf))]

    def compute():
        @pl.when(first)
        def _():
            cur = 1 - ws_ref[0]
            ws_ref[0] = cur
            for c in weight_copies(0, cur):
                c.wait()
            nxt = lax.while_loop(lambda j: (j < n_used) & (be_ref[jnp.minimum(j, n_blocks - 1)] == expert),
                                 lambda j: j + 1, i + 1)

            @pl.when(nxt < n_used)
            def _():
                for c in weight_copies(be_ref[jnp.minimum(nxt, n_blocks - 1)], 1 - cur):
                    c.start()

        ws = ws_ref[0]

        _gather_wait(h_hbm, xbuf, sem, slot, bm)
        x = _load_token_tiles(xbuf, slot * (bm * ROW_TILE), bm).astype(BF16)
        gate = jnp.dot(x, wg_buf[ws].astype(BF16), preferred_element_type=F32)
        up = jnp.dot(x, wu_buf[ws].astype(BF16), preferred_element_type=F32)
        hid = (_silu(gate) * up).astype(BF16)
        _store_token_tiles(y_ref, jnp.dot(hid, wd_buf[ws].astype(BF16), preferred_element_type=F32))

    @pl.when((i == 0) & (n_used > 0))
    def _():
        ws_ref[0] = 1
        for c in weight_copies(be_ref[0], 0):
            c.start()
        for blk in range(ROW_BUFFERS - 1):
            @pl.when(blk < n_used)
            def _():
                _gather_start(h_hbm, xbuf, sem, src_ref, blk * bm, blk, bm)

    @pl.when(i + ROW_BUFFERS - 1 < n_used)
    def _():
        _gather_start(h_hbm, xbuf, sem, src_ref, (i + ROW_BUFFERS - 1) * bm, (i + ROW_BUFFERS - 1) % ROW_BUFFERS, bm)

    @pl.when(i < n_used)
    def _():
        compute()

    @pl.when(i >= n_used)
    def _():
        y_ref[...] = jnp.zeros_like(y_ref)


def _experts(block_expert, src_tok, n_used, h2, w_gate, w_up, w_down, bm):
    D = ROW_TILE * LANES
    P = src_tok.shape[0]
    n_blocks = P // bm
    FF = w_gate.shape[-1]
    kern = functools.partial(_experts_body, bm=bm)
    hbm = pl.BlockSpec(memory_space=pl.ANY)
    return pl.pallas_call(
        kern,
        grid_spec=pltpu.PrefetchScalarGridSpec(
            num_scalar_prefetch=3,
            grid=(n_blocks,),
            in_specs=[hbm, hbm, hbm, hbm],
            out_specs=pl.BlockSpec((bm * ROW_TILE, LANES), lambda i, *_: (i, 0)),
            scratch_shapes=[
                pltpu.VMEM((ROW_BUFFERS * bm * ROW_TILE, LANES), F32),
                pltpu.VMEM((2, D, FF), F32), pltpu.VMEM((2, D, FF), F32), pltpu.VMEM((2, FF, D), F32),
                pltpu.SMEM((1,), jnp.int32),
                pltpu.SemaphoreType.DMA((ROW_BUFFERS,)), pltpu.SemaphoreType.DMA((2,)),
            ],
        ),
        out_shape=jax.ShapeDtypeStruct((P * ROW_TILE, LANES), F32),
        compiler_params=_cparams(("arbitrary",)),
        name="moe_experts",
    )(block_expert, src_tok, n_used, h2, w_gate, w_up, w_down)


def _combine_body(dest_ref, y_hbm, xm_ref, r_ref, fw_ref, o_ref, ybuf, sem, *, tc):
    i = pl.program_id(0)
    n = pl.num_programs(0)
    slot = i % ROW_BUFFERS
    rows = TOP_K * tc
    ahead = ROW_BUFFERS - 1

    @pl.when(i == 0)
    def _():
        for blk in range(ahead):
            @pl.when(blk < n)
            def _():
                _gather_start(y_hbm, ybuf, sem, dest_ref, blk * rows, blk, rows)

    def compute(prefetch):
        _gather_wait(y_hbm, ybuf, sem, slot, rows)
        route = r_ref[...].T
        y0 = _load_token_tiles(ybuf, slot * (rows * ROW_TILE), tc)
        y1 = _load_token_tiles(ybuf, (slot * rows + tc) * ROW_TILE, tc)
        if prefetch:
            _gather_start(y_hbm, ybuf, sem, dest_ref, (i + ahead) * rows, (i + ahead) % ROW_BUFFERS, rows)
        xo = xm_ref[...] + route[:, 2:3] * y0 + route[:, 3:4] * y1
        o_ref[...] = xo * lax.rsqrt(jnp.mean(xo * xo, axis=-1, keepdims=True) + EPS) * fw_ref[...]

    @pl.when(i + ahead < n)
    def _():
        compute(True)

    @pl.when(i + ahead >= n)
    def _():
        compute(False)


def _combine(dest_blocked, y_buf, x_mid, route, final_w, tc):
    T, D = x_mid.shape
    kern = functools.partial(_combine_body, tc=tc)
    return pl.pallas_call(
        kern,
        grid_spec=pltpu.PrefetchScalarGridSpec(
            num_scalar_prefetch=1,
            grid=(T // tc,),
            in_specs=[
                pl.BlockSpec(memory_space=pl.ANY),
                pl.BlockSpec((tc, D), lambda i, d: (i, 0)),
                pl.BlockSpec((SUBLANES, tc), lambda i, d: (0, i)),
                pl.BlockSpec((1, D), lambda i, d: (0, 0)),
            ],
            out_specs=pl.BlockSpec((tc, D), lambda i, d: (i, 0)),
            scratch_shapes=[pltpu.VMEM((ROW_BUFFERS * TOP_K * tc * ROW_TILE, LANES), F32),
                            pltpu.SemaphoreType.DMA((ROW_BUFFERS,))],
        ),
        out_shape=jax.ShapeDtypeStruct((T, D), F32),
        compiler_params=_cparams(("arbitrary",)),
        name="moe_combine",
    )(dest_blocked, y_buf, x_mid, route, final_w.reshape(1, D))


def _layer(x2, B, S, norm_mix_w, w_in, conv_w, a_log, dt_bias, gdn_norm_w, q_norm_w, k_norm_w, w_out,
           norm_ffn_w, w_router_group, w_router_expert, w_gate, w_up, w_down, final_w):
    T, D = x2.shape
    tl = _tiles(B, S)
    o_gate = 4 * GDN_W
    o_qb = o_gate + 2 * GDN_CHAINS
    w_all = jnp.concatenate([w_in[:, :o_gate], w_in[:, o_qb:], w_in[:, o_gate:o_qb],
                             jnp.zeros((D, LANES - 2 * GDN_CHAINS), w_in.dtype)], axis=1).astype(BF16)

    rows = S // GRID_W
    rowp = np.repeat(np.arange(rows), GRID_W).astype(np.float64)
    colp = np.tile(np.arange(GRID_W), rows).astype(np.float64)
    axis_dims = ATT_D // 2
    inv_freq = ROPE_THETA ** (-np.arange(0, axis_dims, 2, dtype=np.float64) / axis_dims)
    ang = np.concatenate([rowp[:, None] * inv_freq, colp[:, None] * inv_freq], axis=-1)
    pair_sign = np.tile(np.array([-1.0, 1.0]), axis_dims)
    cosf = jnp.asarray(np.tile(np.repeat(np.cos(ang), 2, axis=1), (1, LANES // ATT_D)), F32)
    sinf = jnp.asarray(np.tile(np.repeat(np.sin(ang), 2, axis=1) * pair_sign, (1, LANES // ATT_D)), F32)
    q_gain = jnp.max(jnp.abs(q_norm_w)).astype(F32)
    k_gain = jnp.max(jnp.abs(k_norm_w)).astype(F32)
    score_bound = ATT_D ** 0.5 * q_gain * k_gain
    safe = (2.0 * score_bound <= SOFTMAX_SAFE_SPAN).astype(jnp.int32).reshape(1)
    bias = (-LOG2E * score_bound).reshape(1, 1)
    pair = lambda w: jnp.tile(w, LANES // ATT_D).reshape(1, LANES)

    proj, gate_logits, qh, kh, vh = _inproj(x2, norm_mix_w, w_all, pair(q_norm_w), pair(k_norm_w),
                                            cosf, sinf, bias, B, S, tl.proj_rows)

    conv_w8 = jnp.concatenate([conv_w, jnp.zeros((SUBLANES - CONV_W, conv_w.shape[1]), F32)], axis=0)
    gp = jnp.zeros((SUBLANES, LANES), F32)
    gp = gp.at[0, GDN_CHAINS:2 * GDN_CHAINS].set(jnp.exp(a_log.astype(F32)).reshape(-1))
    gp = gp.at[1, GDN_CHAINS:2 * GDN_CHAINS].set(dt_bias.astype(F32).reshape(-1))
    q_a, k_a, v_a, gates, gates_t = _gdn_prep(proj, gate_logits, conv_w8, gp, B, S, tl.prep_rows)
    u, wq, ak, dec = _gdn_chunk(q_a, k_a, v_a, gates, gates_t, tl.gdn_chunks)
    o_f, o_b = _gdn_scan(u, wq, ak, dec, B, S, tl.scan_chunks)

    att = _attention(safe, qh, kh, vh, B, S, tl.att_q, tl.att_k).reshape(T, ATT_Q)

    w_r32 = jnp.concatenate([w_router_group, w_router_expert,
                             jnp.zeros((D, LANES - N_GROUPS - N_EXPERTS), F32)], axis=1).astype(F32)
    w_r_hi = w_r32.astype(BF16)
    w_r = jnp.concatenate([w_r_hi, (w_r32 - w_r_hi.astype(F32)).astype(BF16)], axis=1)
    w_out_bf = w_out.astype(BF16)
    x_mid, h2, route_t, count_rows = _outproj(
        o_f, o_b, proj, att, x2, gdn_norm_w.reshape(1, GDN_D), w_out_bf, w_out_bf,
        norm_ffn_w.reshape(1, D), w_r, tl.proj_rows)

    bm = tl.moe_rows
    n_assign = T * TOP_K
    n_blocks = -(-(n_assign + N_EXPERTS * (bm - 1)) // bm)
    experts = jnp.arange(N_EXPERTS, dtype=jnp.int32)
    counts = count_rows[N_GROUPS:N_GROUPS + N_EXPERTS, 0].astype(jnp.int32)
    padded = (counts + bm - 1) // bm * bm
    pad_end = jnp.cumsum(padded)
    pad_start = pad_end - padded
    block_start = jnp.arange(n_blocks, dtype=jnp.int32) * bm
    done = (pad_end[None, :] <= block_start[:, None]).astype(jnp.int32)
    begun = (pad_start[None, :] <= block_start[:, None]).astype(jnp.int32)
    block_expert = jnp.minimum(jnp.sum(done, axis=1), N_EXPERTS - 1)
    n_used = (pad_end[-1:] // bm).astype(jnp.int32)
    seg_start = jnp.sum(done * padded[None, :], axis=1)
    seg_entry = jnp.sum(done * counts[None, :], axis=1)
    seg_count = jnp.sum(begun * counts[None, :], axis=1) - seg_entry
    e_rows = route_t[0:TOP_K].astype(jnp.int32)
    rank_rows = route_t[4:4 + TOP_K].astype(jnp.int32)
    dest = jnp.sum(jnp.where(e_rows[:, :, None] == experts, pad_start, 0), axis=-1) + rank_rows
    tokens = jnp.tile(jnp.arange(T, dtype=jnp.int32), TOP_K)
    _, compact = lax.sort((dest.reshape(-1), tokens), num_keys=1)
    row = block_start[:, None] + jnp.arange(bm, dtype=jnp.int32)[None, :]
    seg_row = row - seg_start[:, None]
    holds_token = seg_row < seg_count[:, None]
    entry = jnp.clip(seg_entry[:, None] + seg_row, 0, n_assign - 1)
    src_tok = jnp.where(holds_token, compact[entry], lax.rem(row, jnp.full_like(row, T))).reshape(-1)

    y_buf = _experts(block_expert, src_tok, n_used, h2, w_gate, w_up, w_down, bm)

    tc = tl.comb_rows
    dest_blocked = dest.reshape(TOP_K, T // tc, tc).transpose(1, 0, 2).reshape(-1)
    return _combine(dest_blocked, y_buf, x_mid, route_t, final_w, tc)


def kernel(x, norm_mix_w, w_in, conv_w, a_log, dt_bias, gdn_norm_w, q_norm_w, k_norm_w, w_out, norm_ffn_w,
           w_router_group, w_router_expert, w_gate, w_up, w_down, final_norm_w):
    B, S, D = x.shape
    depth = w_in.shape[0]
    assert depth == 1, "the final norm is fused into the last (only) layer's combine step"
    out = _layer(x.reshape(B * S, D), B, S, norm_mix_w[0], w_in[0], conv_w[0], a_log[0], dt_bias[0],
                 gdn_norm_w[0], q_norm_w[0], k_norm_w[0], w_out[0], norm_ffn_w[0], w_router_group[0],
                 w_router_expert[0], w_gate[0], w_up[0], w_down[0], final_norm_w)
    return out.reshape(B, S, D)
```

```python
import functools
import math
from typing import NamedTuple

import jax
import jax.numpy as jnp
import numpy as np
from jax import lax
from jax.experimental import pallas as pl
from jax.experimental.pallas import tpu as pltpu

F32 = jnp.float32
BF16 = jnp.bfloat16
EPS = 1e-6

GRID_W = 64
GDN_HEADS = 4
GDN_D = 128
CONV_W = 5
CHUNK = 64
ATT_HEADS = 8
ATT_KV_HEADS = 2
ATT_GROUP = ATT_HEADS // ATT_KV_HEADS
ATT_D = 64
ROPE_THETA = 10000.0
N_GROUPS = 4
EXPERTS_PER_GROUP = 8
N_EXPERTS = N_GROUPS * EXPERTS_PER_GROUP
TOP_K = 2

GDN_W = GDN_HEADS * GDN_D
GDN_CHAINS = 2 * GDN_HEADS
ATT_Q = ATT_HEADS * ATT_D
ATT_KV = ATT_KV_HEADS * ATT_D
LANES = 128
SUBLANES = 8
ATT_DP = LANES
ROUTER_ROWS = -(-(N_GROUPS + N_EXPERTS) // SUBLANES) * SUBLANES

COL_Z = 3 * GDN_W
COL_QB = COL_Z + GDN_W
COL_KB = COL_QB + ATT_Q
COL_VB = COL_KB + ATT_KV
COL_GATE = COL_VB + ATT_KV
D_PROJ = COL_GATE + LANES

VMEM_LIMIT = 56 * 1024 * 1024
LOG2E = math.log2(math.e)
SOFTMAX_SAFE_SPAN = 60.0


class Tiles(NamedTuple):
    proj_rows: int
    prep_rows: int
    gdn_chunks: int
    scan_chunks: int
    att_q: int
    att_k: int
    moe_rows: int
    comb_rows: int


def _tile(n, want):
    t = min(n, want)
    assert n % t == 0, (n, want)
    return t


def _tiles(B, S):
    T = B * S
    n_chunks = S // CHUNK
    return Tiles(proj_rows=_tile(T, 512), prep_rows=_tile(S, 512), gdn_chunks=_tile(n_chunks, 8),
                 scan_chunks=_tile(n_chunks, 8), att_q=_tile(S, 512), att_k=_tile(S, 2048),
                 moe_rows=256, comb_rows=_tile(T, 256))


def _cparams(sem):
    return pltpu.CompilerParams(dimension_semantics=sem, vmem_limit_bytes=VMEM_LIMIT)


def _silu(x):
    return x * jax.nn.sigmoid(x)


ROW_TILE = SUBLANES
DMA_QUEUES = 2
ROW_BUFFERS = 3


def _store_token_tiles(ref, x, base=0):
    rows, d = x.shape
    assert d == ROW_TILE * LANES
    for c in range(ROW_TILE):
        ref[pl.ds(base + c, rows, stride=ROW_TILE), :] = x[:, c * LANES:(c + 1) * LANES]


def _load_token_tiles(ref, base, rows):
    return jnp.concatenate([ref[pl.ds(base + c, rows, stride=ROW_TILE), :] for c in range(ROW_TILE)], axis=-1)


def _inproj_body(x_ref, nw_ref, w_ref, qw_ref, kw_ref, cos_ref, sin_ref, bias_ref,
                 o_ref, g_ref, qo_ref, ko_ref, vo_ref):
    x = x_ref[...]
    h = x * lax.rsqrt(jnp.mean(x * x, axis=-1, keepdims=True) + EPS) * nw_ref[...]
    acc = jnp.dot(h.astype(BF16), w_ref[...], preferred_element_type=F32)
    o_ref[...] = acc[:, :COL_QB]
    g_ref[...] = acc[:, COL_GATE:]
    _attn_heads(acc[:, COL_QB:COL_KB], acc[:, COL_KB:COL_VB], acc[:, COL_VB:COL_GATE], qw_ref[...], kw_ref[...],
                cos_ref[...], sin_ref[...], bias_ref[...], qo_ref, ko_ref, vo_ref)


def _inproj(x2, norm_w, w_all, qw, kw, cosf, sinf, bias, B, S, tm):
    T, D = x2.shape
    nr = S // tm
    const = lambda i: (0, 0)
    heads = lambda i: (i // nr, 0, i % nr, 0)
    return pl.pallas_call(
        _inproj_body,
        grid=(T // tm,),
        in_specs=[
            pl.BlockSpec((tm, D), lambda i: (i, 0)),
            pl.BlockSpec((1, D), const),
            pl.BlockSpec((D, D_PROJ), const),
            pl.BlockSpec((1, LANES), const),
            pl.BlockSpec((1, LANES), const),
            pl.BlockSpec((tm, LANES), lambda i: (i % nr, 0)),
            pl.BlockSpec((tm, LANES), lambda i: (i % nr, 0)),
            pl.BlockSpec((1, 1), const),
        ],
        out_specs=[
            pl.BlockSpec((tm, COL_QB), lambda i: (i, 0)),
            pl.BlockSpec((tm, LANES), lambda i: (i, 0)),
            pl.BlockSpec((None, ATT_HEADS, tm, ATT_DP), heads),
            pl.BlockSpec((None, ATT_KV_HEADS, tm, ATT_DP), heads),
            pl.BlockSpec((None, ATT_KV_HEADS, tm, ATT_DP), heads),
        ],
        out_shape=[
            jax.ShapeDtypeStruct((T, COL_QB), F32),
            jax.ShapeDtypeStruct((T, LANES), F32),
            jax.ShapeDtypeStruct((B, ATT_HEADS, S, ATT_DP), BF16),
            jax.ShapeDtypeStruct((B, ATT_KV_HEADS, S, ATT_DP), BF16),
            jax.ShapeDtypeStruct((B, ATT_KV_HEADS, S, ATT_DP), BF16),
        ],
        compiler_params=_cparams(("parallel",)),
        name="inproj",
    )(x2, norm_w.reshape(1, D), w_all, qw, kw, cosf, sinf, bias)


def _gdn_prep_body(cur_ref, prev_ref, next_ref, cw_ref, gin_ref, gp_ref,
                   q_ref, k_ref, v_ref, g_ref, gt_ref, ext_ref, *, tr):
    i = pl.program_id(1)
    nr = pl.num_programs(1)
    halo = prev_ref.shape[0]
    pad = CONV_W // 2
    ext_ref[0:halo, :] = jnp.where(i > 0, prev_ref[...], 0.0)
    ext_ref[halo:halo + tr, :] = cur_ref[...]
    ext_ref[halo + tr:2 * halo + tr, :] = jnp.where(i < nr - 1, next_ref[...], 0.0)
    acc = cw_ref[0:1, :] * ext_ref[pl.ds(halo - pad, tr), :]
    for j in range(1, CONV_W):
        acc = acc + cw_ref[j:j + 1, :] * ext_ref[pl.ds(halo - pad + j, tr), :]
    y = _silu(acc)
    for h in range(GDN_HEADS):
        for base, ref, scale in ((0, q_ref, GDN_D ** -0.5), (GDN_W, k_ref, 1.0)):
            t = y[:, base + h * GDN_D: base + (h + 1) * GDN_D]
            t = t * (lax.rsqrt(jnp.sum(t * t, axis=-1, keepdims=True) + EPS) * scale)
            ref[:, h * GDN_D:(h + 1) * GDN_D] = t
    v_ref[...] = y[:, 2 * GDN_W:]
    gin = gin_ref[...]
    lane = lax.broadcasted_iota(jnp.int32, gin.shape, 1)
    a = gin + gp_ref[1:2, :]
    softplus = jnp.maximum(a, 0.0) + jnp.log1p(jnp.exp(-jnp.abs(a)))
    g = jnp.where(lane < GDN_CHAINS, jax.nn.sigmoid(gin), -gp_ref[0:1, :] * softplus)
    g_ref[...] = g
    gt_ref[...] = g.T[0:2 * GDN_CHAINS, :]


def _gdn_prep(proj, gate_logits, conv_w8, gate_params, B, S, tr):
    T = B * S
    nr = S // tr
    C = 3 * GDN_W
    halo = SUBLANES
    rb = tr // halo
    n_halo = T // halo
    kern = functools.partial(_gdn_prep_body, tr=tr)
    out_sd = jax.ShapeDtypeStruct((T, GDN_W), F32)
    return pl.pallas_call(
        kern,
        grid=(B, nr),
        in_specs=[
            pl.BlockSpec((tr, C), lambda b, i: (b * nr + i, 0)),
            pl.BlockSpec((halo, C), lambda b, i: (jnp.maximum((b * nr + i) * rb - 1, 0), 0)),
            pl.BlockSpec((halo, C), lambda b, i: (jnp.minimum((b * nr + i + 1) * rb, n_halo - 1), 0)),
            pl.BlockSpec((SUBLANES, C), lambda b, i: (0, 0)),
            pl.BlockSpec((tr, LANES), lambda b, i: (b * nr + i, 0)),
            pl.BlockSpec((SUBLANES, LANES), lambda b, i: (0, 0)),
        ],
        out_specs=[
            pl.BlockSpec((tr, GDN_W), lambda b, i: (b * nr + i, 0)),
            pl.BlockSpec((tr, GDN_W), lambda b, i: (b * nr + i, 0)),
            pl.BlockSpec((tr, GDN_W), lambda b, i: (b * nr + i, 0)),
            pl.BlockSpec((tr, LANES), lambda b, i: (b * nr + i, 0)),
            pl.BlockSpec((2 * GDN_CHAINS, tr), lambda b, i: (0, b * nr + i)),
        ],
        out_shape=[out_sd, out_sd, out_sd, jax.ShapeDtypeStruct((T, LANES), F32),
                   jax.ShapeDtypeStruct((2 * GDN_CHAINS, T), F32)],
        scratch_shapes=[pltpu.VMEM((tr + 2 * halo, C), F32)],
        compiler_params=_cparams(("parallel", "parallel")),
        name="gdn_prep",
    )(proj, proj, proj, conv_w8, gate_logits, gate_params)


def _bdot(a, b):
    return jnp.dot(a.astype(BF16), b.astype(BF16), preferred_element_type=F32)


def _bdot_nt(a, b):
    return lax.dot_general(a.astype(BF16), b.astype(BF16), (((1,), (1,)), ((), ())),
                           preferred_element_type=F32)


def _gdn_chunk_body(q_ref, k_ref, v_ref, g_ref, gt_ref, u_ref, wq_ref, ak_ref, dec_ref, *, cb):
    C = CHUNK
    row = lax.broadcasted_iota(jnp.int32, (C, C), 0)
    col = lax.broadcasted_iota(jnp.int32, (C, C), 1)
    eye = (row == col).astype(F32)
    masks = ((row >= col, row > col), (row <= col, row < col))
    hi = lax.Precision.HIGHEST
    chains = []
    gate_rows = gt_ref[...]
    for c in range(cb):
        rs = slice(c * C, (c + 1) * C)
        gates = g_ref[rs, :]
        gates_t = gate_rows[:, rs]
        g_tot = jnp.sum(gates, axis=0, keepdims=True)
        tot_rows = jnp.sum(gates_t, axis=1, keepdims=True)
        dec_ref[c] = jnp.broadcast_to(jnp.exp(tot_rows[GDN_CHAINS:2 * GDN_CHAINS]), (GDN_CHAINS, LANES))
        for d in range(2):
            incl, strict = masks[d]
            cum = incl.astype(F32)
            gc_cols = jnp.dot(cum, gates, precision=hi, preferred_element_type=F32)
            gc_rows = lax.dot_general(gates_t, cum, (((1,), (1,)), ((), ())), precision=hi,
                                      preferred_element_type=F32)
            for h in range(GDN_HEADS):
                lane_b = d * GDN_HEADS + h
                lane_g = GDN_CHAINS + lane_b
                sl = slice(h * GDN_D, (h + 1) * GDN_D)
                q = q_ref[rs, sl]
                k = k_ref[rs, sl]
                v = v_ref[rs, sl]
                beta = gates[:, lane_b:lane_b + 1]
                gc_col = gc_cols[:, lane_g:lane_g + 1]
                gc_row = gc_rows[lane_g:lane_g + 1, :]
                g_last = g_tot[:, lane_g:lane_g + 1]
                decay = jnp.where(incl, jnp.exp(jnp.where(incl, gc_col - gc_row, 0.0)), 0.0)
                e_col = jnp.exp(gc_col)
                k_beta = k * beta
                kk = _bdot_nt(jnp.concatenate([k_beta, q], axis=0), k)
                neg_l = jnp.where(strict, -kk[:C] * decay, 0.0)
                attn = jnp.where(incl, kk[C:] * decay, 0.0)
                rhs = jnp.concatenate([v * beta, k_beta * e_col], axis=1).astype(BF16)
                wq_ref[d, c, C:2 * C, sl] = (q * e_col).astype(BF16)
                ak_ref[d, c, h, 0:C, :] = attn.astype(BF16)
                ak_ref[d, c, h, C:C + GDN_D, :] = (k * jnp.exp(g_last - gc_col)).T.astype(BF16)
                chains.append((d, c, rs, sl, neg_l, rhs))
    zs = [jnp.concatenate([ch[4], eye], axis=1) for ch in chains]
    keep_s = lax.broadcasted_iota(jnp.int32, (C, 2 * C), 1) >= C
    for _ in range(int(math.log2(C))):
        zs = [_bdot(z[:, :C], z) + jnp.where(keep_s, z, 0.0) for z in zs]
    for (d, c, rs, sl, _, rhs), z in zip(chains, zs):
        sol = jnp.dot(z[:, C:].astype(BF16), rhs, preferred_element_type=F32)
        u_ref[d, rs, sl] = sol[:, :GDN_D]
        wq_ref[d, c, 0:C, sl] = sol[:, GDN_D:].astype(BF16)


def _gdn_chunk(q, k, v, gates, gates_t, cb):
    T = q.shape[0]
    nc = T // CHUNK
    rows = cb * CHUNK
    kern = functools.partial(_gdn_chunk_body, cb=cb)
    wide = pl.BlockSpec((rows, GDN_W), lambda i: (i, 0))
    return pl.pallas_call(
        kern,
        grid=(nc // cb,),
        in_specs=[wide, wide, wide, pl.BlockSpec((rows, LANES), lambda i: (i, 0)),
                  pl.BlockSpec((2 * GDN_CHAINS, rows), lambda i: (0, i))],
        out_specs=[
            pl.BlockSpec((2, rows, GDN_W), lambda i: (0, i, 0)),
            pl.BlockSpec((2, cb, 2 * CHUNK, GDN_W), lambda i: (0, i, 0, 0)),
            pl.BlockSpec((2, cb, GDN_HEADS, CHUNK + GDN_D, CHUNK), lambda i: (0, i, 0, 0, 0)),
            pl.BlockSpec((cb, GDN_CHAINS, LANES), lambda i: (i, 0, 0)),
        ],
        out_shape=[
            jax.ShapeDtypeStruct((2, T, GDN_W), F32),
            jax.ShapeDtypeStruct((2, nc, 2 * CHUNK, GDN_W), BF16),
            jax.ShapeDtypeStruct((2, nc, GDN_HEADS, CHUNK + GDN_D, CHUNK), BF16),
            jax.ShapeDtypeStruct((nc, GDN_CHAINS, LANES), F32),
        ],
        compiler_params=_cparams(("parallel",)),
        name="gdn_chunk",
    )(q, k, v, gates, gates_t)


def _gdn_scan_body(uf_ref, wqf_ref, akf_ref, decf_ref, ub_ref, wqb_ref, akb_ref, decb_ref,
                   of_ref, ob_ref, state_ref, *, cs):
    @pl.when(pl.program_id(1) == 0)
    def _():
        state_ref[...] = jnp.zeros_like(state_ref)

    C = CHUNK
    refs = ((uf_ref, wqf_ref, akf_ref, decf_ref, of_ref), (ub_ref, wqb_ref, akb_ref, decb_ref, ob_ref))
    chains = [(d, h) for d in range(2) for h in range(GDN_HEADS)]
    st = [state_ref[j] for j in range(GDN_CHAINS)]
    for step in range(cs):
        cidx = (step, cs - 1 - step)
        ws = [jnp.dot(refs[d][1][cidx[d], :, h * GDN_D:(h + 1) * GDN_D], st[j].astype(BF16),
                      preferred_element_type=F32) for j, (d, h) in enumerate(chains)]
        vn = [(refs[d][0][cidx[d] * C:(cidx[d] + 1) * C, h * GDN_D:(h + 1) * GDN_D] - ws[j][:C]).astype(BF16)
              for j, (d, h) in enumerate(chains)]
        rr = [jnp.dot(refs[d][2][cidx[d], h], vn[j], preferred_element_type=F32)
              for j, (d, h) in enumerate(chains)]
        for j, (d, h) in enumerate(chains):
            c = cidx[d]
            refs[d][4][c * C:(c + 1) * C, h * GDN_D:(h + 1) * GDN_D] = (ws[j][C:] + rr[j][:C]).astype(BF16)
            st[j] = st[j] * refs[d][3][c, j:j + 1, :] + rr[j][C:]
    for j in range(GDN_CHAINS):
        state_ref[j] = st[j]


def _gdn_scan(u, wq, ak, dec, B, S, cs):
    T = B * S
    nb = S // (CHUNK * cs)
    rows = cs * CHUNK
    kern = functools.partial(_gdn_scan_body, cs=cs)
    fwd = lambda b, i: b * nb + i
    bwd = lambda b, i: b * nb + nb - 1 - i

    def specs(d, pos):
        return [
            pl.BlockSpec((None, rows, GDN_W), lambda b, i: (d, pos(b, i), 0)),
            pl.BlockSpec((None, cs, 2 * CHUNK, GDN_W), lambda b, i: (d, pos(b, i), 0, 0)),
            pl.BlockSpec((None, cs, GDN_HEADS, CHUNK + GDN_D, CHUNK), lambda b, i: (d, pos(b, i), 0, 0, 0)),
            pl.BlockSpec((cs, GDN_CHAINS, LANES), lambda b, i: (pos(b, i), 0, 0)),
        ]

    out_sd = jax.ShapeDtypeStruct((T, GDN_W), BF16)
    return pl.pallas_call(
        kern,
        grid=(B, nb),
        in_specs=specs(0, fwd) + specs(1, bwd),
        out_specs=[pl.BlockSpec((rows, GDN_W), lambda b, i: (fwd(b, i), 0)),
                   pl.BlockSpec((rows, GDN_W), lambda b, i: (bwd(b, i), 0))],
        out_shape=[out_sd, out_sd],
        scratch_shapes=[pltpu.VMEM((GDN_CHAINS, GDN_D, GDN_D), F32)],
        compiler_params=_cparams(("parallel", "arbitrary")),
        name="gdn_scan",
    )(u, wq, ak, dec, u, wq, ak, dec)


def _attn_heads(q, k, v, qw, kw, cos, sin, bias, qo_ref, ko_ref, vo_ref):
    rows = cos.shape[0]
    lane = lax.broadcasted_iota(jnp.int32, (rows, LANES), 1)
    even_lane = lane % 2 == 0
    first_head = lane < ATT_D
    extra_lane = lane == ATT_D
    mi = lax.broadcasted_iota(jnp.int32, (LANES, LANES), 0) // ATT_D
    mj = lax.broadcasted_iota(jnp.int32, (LANES, LANES), 1) // ATT_D
    head_mean = jnp.where(mi == mj, 1.0 / ATT_D, 0.0).astype(BF16)

    def norm_rope(x, w):
        sq = x * x
        hi = sq.astype(BF16)
        lo = (sq - hi.astype(F32)).astype(BF16)
        ms = (jnp.dot(hi, head_mean, preferred_element_type=F32)
              + jnp.dot(lo, head_mean, preferred_element_type=F32))
        y = x * lax.rsqrt(ms + EPS) * w
        partner = jnp.where(even_lane, pltpu.roll(y, LANES - 1, 1), pltpu.roll(y, 1, 1))
        return y * cos + partner * sin

    def split_heads(r, extra):
        tail = jnp.where(extra_lane, extra, 0.0)
        return (jnp.where(first_head, r, tail).astype(BF16),
                jnp.where(first_head, pltpu.roll(r, ATT_D, 1), tail).astype(BF16))

    scale = LOG2E * ATT_D ** -0.5
    for c in range(ATT_HEADS // 2):
        r = norm_rope(q[:, c * LANES:(c + 1) * LANES], qw) * scale
        qo_ref[2 * c], qo_ref[2 * c + 1] = split_heads(r, bias)
    for c in range(ATT_KV_HEADS // 2):
        r = norm_rope(k[:, c * LANES:(c + 1) * LANES], kw)
        ko_ref[2 * c], ko_ref[2 * c + 1] = split_heads(r, 1.0)
        vo_ref[2 * c], vo_ref[2 * c + 1] = split_heads(v[:, c * LANES:(c + 1) * LANES], 1.0)


def _attn_body(safe_ref, q_ref, k_ref, v_ref, o_ref, m_ref, acc_ref, *, tq, tk):
    safe = safe_ref[0] != 0
    nk = k_ref.shape[0] // tk
    q = q_ref[...].reshape(ATT_GROUP * tq, ATT_DP)
    acc_ref[...] = jnp.zeros_like(acc_ref)

    def scores(j):
        keys = pl.ds(pl.multiple_of(j * tk, tk), tk)
        s = lax.dot_general(q, k_ref[keys, :], (((1,), (1,)), ((), ())), preferred_element_type=F32)
        return s, v_ref[keys, :]

    @pl.when(safe)
    def _():
        def step(j, carry):
            s, v = scores(j)
            acc_ref[...] += jnp.dot(jnp.exp2(s).astype(BF16), v, preferred_element_type=F32)
            return carry
        lax.fori_loop(0, nk, step, 0)

    @pl.when(jnp.logical_not(safe))
    def _():
        m_ref[...] = jnp.full_like(m_ref, -jnp.inf)

        def step(j, carry):
            s, v = scores(j)
            m_prev = m_ref[...]
            m_new = jnp.maximum(m_prev, jnp.max(s, axis=-1, keepdims=True))
            p = jnp.exp2(s - m_new).astype(BF16)
            acc_ref[...] = jnp.exp2(m_prev - m_new) * acc_ref[...] + jnp.dot(p, v, preferred_element_type=F32)
            m_ref[...] = m_new
            return carry
        lax.fori_loop(0, nk, step, 0)

    acc = acc_ref[...]
    o = acc[:, :ATT_D] / acc[:, ATT_D:ATT_D + 1]
    for h in range(ATT_GROUP):
        o_ref[:, h * ATT_D:(h + 1) * ATT_D] = o[h * tq:(h + 1) * tq].astype(o_ref.dtype)


def _attention(safe, q, k, v, B, S, tq, tk):
    kern = functools.partial(_attn_body, tq=tq, tk=tk)
    gw = ATT_GROUP * ATT_D
    return pl.pallas_call(
        kern,
        grid_spec=pltpu.PrefetchScalarGridSpec(
            num_scalar_prefetch=1,
            grid=(B, ATT_KV_HEADS, S // tq),
            in_specs=[
                pl.BlockSpec((None, ATT_GROUP, tq, ATT_DP), lambda b, g, i, s: (b, g, i, 0)),
                pl.BlockSpec((None, None, S, ATT_DP), lambda b, g, i, s: (b, g, 0, 0)),
                pl.BlockSpec((None, None, S, ATT_DP), lambda b, g, i, s: (b, g, 0, 0)),
            ],
            out_specs=pl.BlockSpec((None, tq, gw), lambda b, g, i, s: (b, i, g)),
            scratch_shapes=[
                pltpu.VMEM((ATT_GROUP * tq, 1), F32),
                pltpu.VMEM((ATT_GROUP * tq, ATT_DP), F32),
            ],
        ),
        out_shape=jax.ShapeDtypeStruct((B, S, ATT_Q), BF16),
        compiler_params=_cparams(("parallel", "parallel", "parallel")),
        name="attention",
    )(safe, q, k, v)


def _outproj_body(of_ref, ob_ref, z_ref, att_ref, x_ref, gnw_ref, wa_ref, wb_ref,
                  fnw_ref, wr_ref, triu_ref, xm_ref, h_ref, rt_ref, c_ref, count_ref):
    o = of_ref[...].astype(F32) + ob_ref[...].astype(F32)
    z = z_ref[...]
    parts = []
    for h in range(GDN_HEADS):
        sl = slice(h * GDN_D, (h + 1) * GDN_D)
        t = o[:, sl]
        t = t * lax.rsqrt(jnp.mean(t * t, axis=-1, keepdims=True) + EPS) * gnw_ref[...]
        parts.append((t * _silu(z[:, sl])).astype(BF16))
    mix_a = jnp.concatenate(parts, axis=-1)
    xm = x_ref[...] + jnp.dot(mix_a, wa_ref[...], preferred_element_type=F32)
    xm = xm + jnp.dot(att_ref[...], wb_ref[...], preferred_element_type=F32)
    xm_ref[...] = xm
    hn = xm * lax.rsqrt(jnp.mean(xm * xm, axis=-1, keepdims=True) + EPS) * fnw_ref[...]
    _store_token_tiles(h_ref, hn)
    hn_hi = hn.astype(BF16)
    hn_lo = (hn - hn_hi.astype(F32)).astype(BF16)
    both = jnp.dot(hn_hi, wr_ref[...], preferred_element_type=F32)
    logits = (both[:, :LANES] + both[:, LANES:]
              + jnp.dot(hn_lo, wr_ref[:, :LANES], preferred_element_type=F32))

    lt = logits.T[0:ROUTER_ROWS, :]
    tok = lt.shape[1]
    slot = lax.broadcasted_iota(jnp.int32, (ROUTER_ROWS, tok), 0)
    big = jnp.int32(ROUTER_ROWS)
    neg = -jnp.inf

    def masked_top(vals, mask):
        m = jnp.max(jnp.where(mask, vals, neg), axis=0, keepdims=True)
        idx = jnp.min(jnp.where(mask & (vals == m), slot, big), axis=0, keepdims=True)
        return m, idx

    gmask = slot < N_GROUPS
    gmax, gsel = masked_top(lt, gmask)
    gp_top = 1.0 / jnp.sum(jnp.where(gmask, jnp.exp(lt - gmax), 0.0), axis=0, keepdims=True)
    lo = N_GROUPS + gsel * EXPERTS_PER_GROUP
    emask = (slot >= lo) & (slot < lo + EXPERTS_PER_GROUP)
    m1, i1 = masked_top(lt, emask)
    ex = jnp.where(emask, jnp.exp(lt - m1), 0.0)
    pf = ex / jnp.sum(ex, axis=0, keepdims=True)
    p1, _ = masked_top(pf, emask)
    p2, i2 = masked_top(pf, emask & (slot != i1))
    denom = p1 + p2
    g1 = gp_top * p1 / denom
    g2 = gp_top * p2 / denom
    e1 = (i1 - N_GROUPS).astype(F32)
    e2 = (i2 - N_GROUPS).astype(F32)

    @pl.when(pl.program_id(0) == 0)
    def _():
        count_ref[...] = jnp.zeros_like(count_ref)

    hit1 = slot == i1
    hit2 = slot == i2
    picked = jnp.where(hit1 | hit2, 1.0, 0.0)
    seen = jnp.dot(picked.astype(BF16), triu_ref[...], preferred_element_type=F32)
    counted = count_ref[:, 0:1]
    before = counted + seen - picked
    rank1 = jnp.sum(jnp.where(hit1, before, 0.0), axis=0, keepdims=True)
    rank2 = jnp.sum(jnp.where(hit2, before, 0.0), axis=0, keepdims=True)
    total = jnp.broadcast_to(counted + seen[:, tok - 1:tok], count_ref.shape)
    count_ref[...] = total
    c_ref[...] = total
    rt_ref[...] = jnp.concatenate([e1, e2, g1, g2, rank1, rank2, jnp.zeros((SUBLANES - 6, tok), F32)], axis=0)


def _outproj(o_f, o_b, proj, att, x2, gnw, w_a, w_b, fnw, w_r, tm):
    T, D = x2.shape
    row = lambda i: (i, 0)
    const = lambda i: (0, 0)
    return pl.pallas_call(
        _outproj_body,
        grid=(T // tm,),
        in_specs=[
            pl.BlockSpec((tm, GDN_W), row),
            pl.BlockSpec((tm, GDN_W), row),
            pl.BlockSpec((tm, GDN_W), lambda i: (i, COL_Z // GDN_W)),
            pl.BlockSpec((tm, ATT_Q), row),
            pl.BlockSpec((tm, D), row),
            pl.BlockSpec((1, GDN_D), const),
            pl.BlockSpec((GDN_W, D), const),
            pl.BlockSpec((ATT_Q, D), lambda i: (GDN_W // ATT_Q, 0)),
            pl.BlockSpec((1, D), const),
            pl.BlockSpec((D, 2 * LANES), const),
            pl.BlockSpec((tm, tm), const),
        ],
        out_specs=[pl.BlockSpec((tm, D), row), pl.BlockSpec((tm * ROW_TILE, LANES), row),
                   pl.BlockSpec((SUBLANES, tm), lambda i: (0, i)),
                   pl.BlockSpec((ROUTER_ROWS, LANES), const)],
        out_shape=[jax.ShapeDtypeStruct((T, D), F32), jax.ShapeDtypeStruct((T * ROW_TILE, LANES), F32),
                   jax.ShapeDtypeStruct((SUBLANES, T), F32),
                   jax.ShapeDtypeStruct((ROUTER_ROWS, LANES), F32)],
        scratch_shapes=[pltpu.VMEM((ROUTER_ROWS, LANES), F32)],
        compiler_params=_cparams(("arbitrary",)),
        name="outproj_router",
    )(o_f, o_b, proj, att, x2, gnw, w_a, w_b, fnw, w_r, jnp.asarray(np.tri(tm, dtype=np.float32).T, BF16))


def _gather_start(src_hbm, dst_buf, sem, idx_ref, base, slot, n):
    for r in range(n):
        src = pl.multiple_of(idx_ref[base + r] * ROW_TILE, ROW_TILE)
        dst = pl.multiple_of((slot * n + r) * ROW_TILE, ROW_TILE)
        pltpu.make_async_copy(src_hbm.at[pl.ds(src, ROW_TILE)], dst_buf.at[pl.ds(dst, ROW_TILE)],
                              sem.at[slot]).start(priority=r % DMA_QUEUES)


def _gather_wait(src_hbm, dst_buf, sem, slot, n):
    dst = pl.multiple_of(slot * n * ROW_TILE, ROW_TILE)
    pltpu.make_async_copy(src_hbm.at[pl.ds(0, n * ROW_TILE)], dst_buf.at[pl.ds(dst, n * ROW_TILE)],
                          sem.at[slot]).wait()


def _experts_body(be_ref, src_ref, nused_ref,
                  h_hbm, wg_hbm, wu_hbm, wd_hbm, y_ref, xbuf, wg_buf, wu_buf, wd_buf, ws_ref, sem, wsem, *, bm):
    i = pl.program_id(0)
    n_blocks = pl.num_programs(0)
    n_used = nused_ref[0]
    slot = i % ROW_BUFFERS
    expert = be_ref[i]
    first = (i == 0) | (expert != be_ref[jnp.maximum(i - 1, 0)])

    def weight_copies(e, s):
        return [pltpu.make_async_copy(hbm.at[e], buf.at[s], wsem.at[s])
                for hbm, buf in ((wg_hbm, wg_buf), (wu_hbm, wu_buf), (wd_hbm, wd_buf))]

    def compute(prefetch):
        @pl.when(first)
        def _():
            cur = 1 - ws_ref[0]
            ws_ref[0] = cur
            for c in weight_copies(0, cur):
                c.wait()
            nxt = lax.while_loop(lambda j: (j < n_used) & (be_ref[jnp.minimum(j, n_blocks - 1)] == expert),
                                 lambda j: j + 1, i + 1)

            @pl.when(nxt < n_used)
            def _():
                for c in weight_copies(be_ref[jnp.minimum(nxt, n_blocks - 1)], 1 - cur):
                    c.start()

        ws = ws_ref[0]

        _gather_wait(h_hbm, xbuf, sem, slot, bm)
        x = _load_token_tiles(xbuf, slot * (bm * ROW_TILE), bm).astype(BF16)
        if prefetch:
            ahead = i + ROW_BUFFERS - 1
            _gather_start(h_hbm, xbuf, sem, src_ref, ahead * bm, ahead % ROW_BUFFERS, bm)
        gate = jnp.dot(x, wg_buf[ws].astype(BF16), preferred_element_type=F32)
        up = jnp.dot(x, wu_buf[ws].astype(BF16), preferred_element_type=F32)
        hid = (_silu(gate) * up).astype(BF16)
        _store_token_tiles(y_ref, jnp.dot(hid, wd_buf[ws].astype(BF16), preferred_element_type=F32))

    @pl.when((i == 0) & (n_used > 0))
    def _():
        ws_ref[0] = 1
        for c in weight_copies(be_ref[0], 0):
            c.start()
        for blk in range(ROW_BUFFERS - 1):
            @pl.when(blk < n_used)
            def _():
                _gather_start(h_hbm, xbuf, sem, src_ref, blk * bm, blk, bm)

    @pl.when(i + ROW_BUFFERS - 1 < n_used)
    def _():
        compute(True)

    @pl.when((i < n_used) & (i + ROW_BUFFERS - 1 >= n_used))
    def _():
        compute(False)

    @pl.when(i >= n_used)
    def _():
        y_ref[...] = jnp.zeros_like(y_ref)


def _experts(block_expert, src_tok, n_used, h2, w_gate, w_up, w_down, bm):
    D = ROW_TILE * LANES
    P = src_tok.shape[0]
    n_blocks = P // bm
    FF = w_gate.shape[-1]
    kern = functools.partial(_experts_body, bm=bm)
    hbm = pl.BlockSpec(memory_space=pl.ANY)
    return pl.pallas_call(
        kern,
        grid_spec=pltpu.PrefetchScalarGridSpec(
            num_scalar_prefetch=3,
            grid=(n_blocks,),
            in_specs=[hbm, hbm, hbm, hbm],
            out_specs=pl.BlockSpec((bm * ROW_TILE, LANES), lambda i, *_: (i, 0)),
            scratch_shapes=[
                pltpu.VMEM((ROW_BUFFERS * bm * ROW_TILE, LANES), F32),
                pltpu.VMEM((2, D, FF), F32), pltpu.VMEM((2, D, FF), F32), pltpu.VMEM((2, FF, D), F32),
                pltpu.SMEM((1,), jnp.int32),
                pltpu.SemaphoreType.DMA((ROW_BUFFERS,)), pltpu.SemaphoreType.DMA((2,)),
            ],
        ),
        out_shape=jax.ShapeDtypeStruct((P * ROW_TILE, LANES), F32),
        compiler_params=_cparams(("arbitrary",)),
        name="moe_experts",
    )(block_expert, src_tok, n_used, h2, w_gate, w_up, w_down)


def _combine_body(dest_ref, y_hbm, xm_ref, r_ref, fw_ref, o_ref, ybuf, sem, *, tc):
    i = pl.program_id(0)
    n = pl.num_programs(0)
    slot = i % ROW_BUFFERS
    rows = TOP_K * tc
    ahead = ROW_BUFFERS - 1

    @pl.when(i == 0)
    def _():
        for blk in range(ahead):
            @pl.when(blk < n)
            def _():
                _gather_start(y_hbm, ybuf, sem, dest_ref, blk * rows, blk, rows)

    def compute(prefetch):
        _gather_wait(y_hbm, ybuf, sem, slot, rows)
        route = r_ref[...].T
        y0 = _load_token_tiles(ybuf, slot * (rows * ROW_TILE), tc)
        y1 = _load_token_tiles(ybuf, (slot * rows + tc) * ROW_TILE, tc)
        if prefetch:
            _gather_start(y_hbm, ybuf, sem, dest_ref, (i + ahead) * rows, (i + ahead) % ROW_BUFFERS, rows)
        xo = xm_ref[...] + route[:, 2:3] * y0 + route[:, 3:4] * y1
        o_ref[...] = xo * lax.rsqrt(jnp.mean(xo * xo, axis=-1, keepdims=True) + EPS) * fw_ref[...]

    @pl.when(i + ahead < n)
    def _():
        compute(True)

    @pl.when(i + ahead >= n)
    def _():
        compute(False)


def _combine(dest_blocked, y_buf, x_mid, route, final_w, tc):
    T, D = x_mid.shape
    kern = functools.partial(_combine_body, tc=tc)
    return pl.pallas_call(
        kern,
        grid_spec=pltpu.PrefetchScalarGridSpec(
            num_scalar_prefetch=1,
            grid=(T // tc,),
            in_specs=[
                pl.BlockSpec(memory_space=pl.ANY),
                pl.BlockSpec((tc, D), lambda i, d: (i, 0)),
                pl.BlockSpec((SUBLANES, tc), lambda i, d: (0, i)),
                pl.BlockSpec((1, D), lambda i, d: (0, 0)),
            ],
            out_specs=pl.BlockSpec((tc, D), lambda i, d: (i, 0)),
            scratch_shapes=[pltpu.VMEM((ROW_BUFFERS * TOP_K * tc * ROW_TILE, LANES), F32),
                            pltpu.SemaphoreType.DMA((ROW_BUFFERS,))],
        ),
        out_shape=jax.ShapeDtypeStruct((T, D), F32),
        compiler_params=_cparams(("arbitrary",)),
        name="moe_combine",
    )(dest_blocked, y_buf, x_mid, route, final_w.reshape(1, D))


def _layer(x2, B, S, norm_mix_w, w_in, conv_w, a_log, dt_bias, gdn_norm_w, q_norm_w, k_norm_w, w_out,
           norm_ffn_w, w_router_group, w_router_expert, w_gate, w_up, w_down, final_w):
    T, D = x2.shape
    tl = _tiles(B, S)
    o_gate = 4 * GDN_W
    o_qb = o_gate + 2 * GDN_CHAINS
    w_all = jnp.concatenate([w_in[:, :o_gate], w_in[:, o_qb:], w_in[:, o_gate:o_qb],
                             jnp.zeros((D, LANES - 2 * GDN_CHAINS), w_in.dtype)], axis=1).astype(BF16)

    rows = S // GRID_W
    rowp = np.repeat(np.arange(rows), GRID_W).astype(np.float64)
    colp = np.tile(np.arange(GRID_W), rows).astype(np.float64)
    axis_dims = ATT_D // 2
    inv_freq = ROPE_THETA ** (-np.arange(0, axis_dims, 2, dtype=np.float64) / axis_dims)
    ang = np.concatenate([rowp[:, None] * inv_freq, colp[:, None] * inv_freq], axis=-1)
    pair_sign = np.tile(np.array([-1.0, 1.0]), axis_dims)
    cosf = jnp.asarray(np.tile(np.repeat(np.cos(ang), 2, axis=1), (1, LANES // ATT_D)), F32)
    sinf = jnp.asarray(np.tile(np.repeat(np.sin(ang), 2, axis=1) * pair_sign, (1, LANES // ATT_D)), F32)
    q_gain = jnp.max(jnp.abs(q_norm_w)).astype(F32)
    k_gain = jnp.max(jnp.abs(k_norm_w)).astype(F32)
    score_bound = ATT_D ** 0.5 * q_gain * k_gain
    safe = (2.0 * score_bound <= SOFTMAX_SAFE_SPAN).astype(jnp.int32).reshape(1)
    bias = (-LOG2E * score_bound).reshape(1, 1)
    pair = lambda w: jnp.tile(w, LANES // ATT_D).reshape(1, LANES)

    proj, gate_logits, qh, kh, vh = _inproj(x2, norm_mix_w, w_all, pair(q_norm_w), pair(k_norm_w),
                                            cosf, sinf, bias, B, S, tl.proj_rows)

    conv_w8 = jnp.concatenate([conv_w, jnp.zeros((SUBLANES - CONV_W, conv_w.shape[1]), F32)], axis=0)
    gp = jnp.zeros((SUBLANES, LANES), F32)
    gp = gp.at[0, GDN_CHAINS:2 * GDN_CHAINS].set(jnp.exp(a_log.astype(F32)).reshape(-1))
    gp = gp.at[1, GDN_CHAINS:2 * GDN_CHAINS].set(dt_bias.astype(F32).reshape(-1))
    q_a, k_a, v_a, gates, gates_t = _gdn_prep(proj, gate_logits, conv_w8, gp, B, S, tl.prep_rows)
    u, wq, ak, dec = _gdn_chunk(q_a, k_a, v_a, gates, gates_t, tl.gdn_chunks)
    o_f, o_b = _gdn_scan(u, wq, ak, dec, B, S, tl.scan_chunks)

    att = _attention(safe, qh, kh, vh, B, S, tl.att_q, tl.att_k).reshape(T, ATT_Q)

    w_r32 = jnp.concatenate([w_router_group, w_router_expert,
                             jnp.zeros((D, LANES - N_GROUPS - N_EXPERTS), F32)], axis=1).astype(F32)
    w_r_hi = w_r32.astype(BF16)
    w_r = jnp.concatenate([w_r_hi, (w_r32 - w_r_hi.astype(F32)).astype(BF16)], axis=1)
    w_out_bf = w_out.astype(BF16)
    x_mid, h2, route_t, count_rows = _outproj(
        o_f, o_b, proj, att, x2, gdn_norm_w.reshape(1, GDN_D), w_out_bf, w_out_bf,
        norm_ffn_w.reshape(1, D), w_r, tl.proj_rows)

    bm = tl.moe_rows
    n_assign = T * TOP_K
    n_blocks = -(-(n_assign + N_EXPERTS * (bm - 1)) // bm)
    experts = jnp.arange(N_EXPERTS, dtype=jnp.int32)
    counts = count_rows[N_GROUPS:N_GROUPS + N_EXPERTS, 0].astype(jnp.int32)
    padded = (counts + bm - 1) // bm * bm
    pad_end = jnp.cumsum(padded)
    pad_start = pad_end - padded
    block_start = jnp.arange(n_blocks, dtype=jnp.int32) * bm
    done = (pad_end[None, :] <= block_start[:, None]).astype(jnp.int32)
    begun = (pad_start[None, :] <= block_start[:, None]).astype(jnp.int32)
    block_expert = jnp.minimum(jnp.sum(done, axis=1), N_EXPERTS - 1)
    n_used = (pad_end[-1:] // bm).astype(jnp.int32)
    seg_start = jnp.sum(done * padded[None, :], axis=1)
    seg_entry = jnp.sum(done * counts[None, :], axis=1)
    seg_count = jnp.sum(begun * counts[None, :], axis=1) - seg_entry
    e_rows = route_t[0:TOP_K].astype(jnp.int32)
    rank_rows = route_t[4:4 + TOP_K].astype(jnp.int32)
    dest = jnp.sum(jnp.where(e_rows[:, :, None] == experts, pad_start, 0), axis=-1) + rank_rows
    tokens = jnp.tile(jnp.arange(T, dtype=jnp.int32), TOP_K)
    _, compact = lax.sort((dest.reshape(-1), tokens), num_keys=1)
    row = block_start[:, None] + jnp.arange(bm, dtype=jnp.int32)[None, :]
    seg_row = row - seg_start[:, None]
    holds_token = seg_row < seg_count[:, None]
    entry = jnp.clip(seg_entry[:, None] + seg_row, 0, n_assign - 1)
    src_tok = jnp.where(holds_token, compact[entry], lax.rem(row, jnp.full_like(row, T))).reshape(-1)

    y_buf = _experts(block_expert, src_tok, n_used, h2, w_gate, w_up, w_down, bm)

    tc = tl.comb_rows
    dest_blocked = dest.reshape(TOP_K, T // tc, tc).transpose(1, 0, 2).reshape(-1)
    return _combine(dest_blocked, y_buf, x_mid, route_t, final_w, tc)


def kernel(x, norm_mix_w, w_in, conv_w, a_log, dt_bias, gdn_norm_w, q_norm_w, k_norm_w, w_out, norm_ffn_w,
           w_router_group, w_router_expert, w_gate, w_up, w_down, final_norm_w):
    B, S, D = x.shape
    depth = w_in.shape[0]
    assert depth == 1, "the final norm is fused into the last (only) layer's combine step"
    out = _layer(x.reshape(B * S, D), B, S, norm_mix_w[0], w_in[0], conv_w[0], a_log[0], dt_bias[0],
                 gdn_norm_w[0], q_norm_w[0], k_norm_w[0], w_out[0], norm_ffn_w[0], w_router_group[0],
                 w_router_expert[0], w_gate[0], w_up[0], w_down[0], final_norm_w)
    return out.reshape(B, S, D)
```

```python
import functools
import math
from typing import NamedTuple

import jax
import jax.numpy as jnp
import numpy as np
from jax import lax
from jax.experimental import pallas as pl
from jax.experimental.pallas import tpu as pltpu

F32 = jnp.float32
BF16 = jnp.bfloat16
EPS = 1e-6

GRID_W = 64
GDN_HEADS = 4
GDN_D = 128
CONV_W = 5
CHUNK = 64
ATT_HEADS = 8
ATT_KV_HEADS = 2
ATT_GROUP = ATT_HEADS // ATT_KV_HEADS
ATT_D = 64
ROPE_THETA = 10000.0
N_GROUPS = 4
EXPERTS_PER_GROUP = 8
N_EXPERTS = N_GROUPS * EXPERTS_PER_GROUP
TOP_K = 2

GDN_W = GDN_HEADS * GDN_D
GDN_CHAINS = 2 * GDN_HEADS
ATT_Q = ATT_HEADS * ATT_D
ATT_KV = ATT_KV_HEADS * ATT_D
LANES = 128
SUBLANES = 8
ATT_DP = LANES
ROUTER_ROWS = -(-(N_GROUPS + N_EXPERTS) // SUBLANES) * SUBLANES

COL_Z = 3 * GDN_W
COL_QB = COL_Z + GDN_W
COL_KB = COL_QB + ATT_Q
COL_VB = COL_KB + ATT_KV
COL_GATE = COL_VB + ATT_KV
D_PROJ = COL_GATE + LANES

VMEM_LIMIT = 56 * 1024 * 1024
LOG2E = math.log2(math.e)
SOFTMAX_SAFE_SPAN = 60.0


class Tiles(NamedTuple):
    proj_rows: int
    prep_rows: int
    gdn_chunks: int
    scan_chunks: int
    att_q: int
    att_k: int
    moe_rows: int
    comb_rows: int


def _tile(n, want):
    t = min(n, want)
    assert n % t == 0, (n, want)
    return t


def _tiles(B, S):
    T = B * S
    n_chunks = S // CHUNK
    return Tiles(proj_rows=_tile(T, 512), prep_rows=_tile(S, 512), gdn_chunks=_tile(n_chunks, 8),
                 scan_chunks=_tile(n_chunks, 8), att_q=_tile(S, 512), att_k=_tile(S, 2048),
                 moe_rows=256, comb_rows=_tile(T, 256))


def _cparams(sem):
    return pltpu.CompilerParams(dimension_semantics=sem, vmem_limit_bytes=VMEM_LIMIT)


def _silu(x):
    return x * jax.nn.sigmoid(x)


ROW_TILE = SUBLANES
DMA_QUEUES = 2
ROW_BUFFERS = 4


def _store_token_tiles(ref, x, base=0):
    rows, d = x.shape
    assert d == ROW_TILE * LANES
    for c in range(ROW_TILE):
        ref[pl.ds(base + c, rows, stride=ROW_TILE), :] = x[:, c * LANES:(c + 1) * LANES]


def _load_token_tiles(ref, base, rows):
    return jnp.concatenate([ref[pl.ds(base + c, rows, stride=ROW_TILE), :] for c in range(ROW_TILE)], axis=-1)


def _inproj_body(x_ref, nw_ref, w_ref, qw_ref, kw_ref, cos_ref, sin_ref, bias_ref,
                 o_ref, g_ref, qo_ref, ko_ref, vo_ref):
    x = x_ref[...]
    h = x * lax.rsqrt(jnp.mean(x * x, axis=-1, keepdims=True) + EPS) * nw_ref[...]
    acc = jnp.dot(h.astype(BF16), w_ref[...], preferred_element_type=F32)
    o_ref[...] = acc[:, :COL_QB]
    g_ref[...] = acc[:, COL_GATE:]
    _attn_heads(acc[:, COL_QB:COL_KB], acc[:, COL_KB:COL_VB], acc[:, COL_VB:COL_GATE], qw_ref[...], kw_ref[...],
                cos_ref[...], sin_ref[...], bias_ref[...], qo_ref, ko_ref, vo_ref)


def _inproj(x2, norm_w, w_all, qw, kw, cosf, sinf, bias, B, S, tm):
    T, D = x2.shape
    nr = S // tm
    const = lambda i: (0, 0)
    heads = lambda i: (i // nr, 0, i % nr, 0)
    return pl.pallas_call(
        _inproj_body,
        grid=(T // tm,),
        in_specs=[
            pl.BlockSpec((tm, D), lambda i: (i, 0)),
            pl.BlockSpec((1, D), const),
            pl.BlockSpec((D, D_PROJ), const),
            pl.BlockSpec((1, LANES), const),
            pl.BlockSpec((1, LANES), const),
            pl.BlockSpec((tm, LANES), lambda i: (i % nr, 0)),
            pl.BlockSpec((tm, LANES), lambda i: (i % nr, 0)),
            pl.BlockSpec((1, 1), const),
        ],
        out_specs=[
            pl.BlockSpec((tm, COL_QB), lambda i: (i, 0)),
            pl.BlockSpec((tm, LANES), lambda i: (i, 0)),
            pl.BlockSpec((None, ATT_HEADS, tm, ATT_DP), heads),
            pl.BlockSpec((None, ATT_KV_HEADS, tm, ATT_DP), heads),
            pl.BlockSpec((None, ATT_KV_HEADS, tm, ATT_DP), heads),
        ],
        out_shape=[
            jax.ShapeDtypeStruct((T, COL_QB), F32),
            jax.ShapeDtypeStruct((T, LANES), F32),
            jax.ShapeDtypeStruct((B, ATT_HEADS, S, ATT_DP), BF16),
            jax.ShapeDtypeStruct((B, ATT_KV_HEADS, S, ATT_DP), BF16),
            jax.ShapeDtypeStruct((B, ATT_KV_HEADS, S, ATT_DP), BF16),
        ],
        compiler_params=_cparams(("parallel",)),
        name="inproj",
    )(x2, norm_w.reshape(1, D), w_all, qw, kw, cosf, sinf, bias)


def _gdn_prep_body(cur_ref, prev_ref, next_ref, cw_ref, gin_ref, gp_ref,
                   q_ref, k_ref, v_ref, g_ref, gt_ref, ext_ref, *, tr):
    i = pl.program_id(1)
    nr = pl.num_programs(1)
    halo = prev_ref.shape[0]
    pad = CONV_W // 2
    ext_ref[0:halo, :] = jnp.where(i > 0, prev_ref[...], 0.0)
    ext_ref[halo:halo + tr, :] = cur_ref[...]
    ext_ref[halo + tr:2 * halo + tr, :] = jnp.where(i < nr - 1, next_ref[...], 0.0)
    acc = cw_ref[0:1, :] * ext_ref[pl.ds(halo - pad, tr), :]
    for j in range(1, CONV_W):
        acc = acc + cw_ref[j:j + 1, :] * ext_ref[pl.ds(halo - pad + j, tr), :]
    y = _silu(acc)
    for h in range(GDN_HEADS):
        for base, ref, scale in ((0, q_ref, GDN_D ** -0.5), (GDN_W, k_ref, 1.0)):
            t = y[:, base + h * GDN_D: base + (h + 1) * GDN_D]
            t = t * (lax.rsqrt(jnp.sum(t * t, axis=-1, keepdims=True) + EPS) * scale)
            ref[:, h * GDN_D:(h + 1) * GDN_D] = t
    v_ref[...] = y[:, 2 * GDN_W:]
    gin = gin_ref[...]
    lane = lax.broadcasted_iota(jnp.int32, gin.shape, 1)
    a = gin + gp_ref[1:2, :]
    softplus = jnp.maximum(a, 0.0) + jnp.log1p(jnp.exp(-jnp.abs(a)))
    g = jnp.where(lane < GDN_CHAINS, jax.nn.sigmoid(gin), -gp_ref[0:1, :] * softplus)
    g_ref[...] = g
    gt_ref[...] = g.T[0:2 * GDN_CHAINS, :]


def _gdn_prep(proj, gate_logits, conv_w8, gate_params, B, S, tr):
    T = B * S
    nr = S // tr
    C = 3 * GDN_W
    halo = SUBLANES
    rb = tr // halo
    n_halo = T // halo
    kern = functools.partial(_gdn_prep_body, tr=tr)
    out_sd = jax.ShapeDtypeStruct((T, GDN_W), F32)
    return pl.pallas_call(
        kern,
        grid=(B, nr),
        in_specs=[
            pl.BlockSpec((tr, C), lambda b, i: (b * nr + i, 0)),
            pl.BlockSpec((halo, C), lambda b, i: (jnp.maximum((b * nr + i) * rb - 1, 0), 0)),
            pl.BlockSpec((halo, C), lambda b, i: (jnp.minimum((b * nr + i + 1) * rb, n_halo - 1), 0)),
            pl.BlockSpec((SUBLANES, C), lambda b, i: (0, 0)),
            pl.BlockSpec((tr, LANES), lambda b, i: (b * nr + i, 0)),
            pl.BlockSpec((SUBLANES, LANES), lambda b, i: (0, 0)),
        ],
        out_specs=[
            pl.BlockSpec((tr, GDN_W), lambda b, i: (b * nr + i, 0)),
            pl.BlockSpec((tr, GDN_W), lambda b, i: (b * nr + i, 0)),
            pl.BlockSpec((tr, GDN_W), lambda b, i: (b * nr + i, 0)),
            pl.BlockSpec((tr, LANES), lambda b, i: (b * nr + i, 0)),
            pl.BlockSpec((2 * GDN_CHAINS, tr), lambda b, i: (0, b * nr + i)),
        ],
        out_shape=[out_sd, out_sd, out_sd, jax.ShapeDtypeStruct((T, LANES), F32),
                   jax.ShapeDtypeStruct((2 * GDN_CHAINS, T), F32)],
        scratch_shapes=[pltpu.VMEM((tr + 2 * halo, C), F32)],
        compiler_params=_cparams(("parallel", "parallel")),
        name="gdn_prep",
    )(proj, proj, proj, conv_w8, gate_logits, gate_params)


def _bdot(a, b):
    return jnp.dot(a.astype(BF16), b.astype(BF16), preferred_element_type=F32)


def _bdot_nt(a, b):
    return lax.dot_general(a.astype(BF16), b.astype(BF16), (((1,), (1,)), ((), ())),
                           preferred_element_type=F32)


def _gdn_chunk_body(q_ref, k_ref, v_ref, g_ref, gt_ref, u_ref, wq_ref, ak_ref, dec_ref, *, cb):
    C = CHUNK
    row = lax.broadcasted_iota(jnp.int32, (C, C), 0)
    col = lax.broadcasted_iota(jnp.int32, (C, C), 1)
    eye = (row == col).astype(F32)
    masks = ((row >= col, row > col), (row <= col, row < col))
    hi = lax.Precision.HIGHEST
    chains = []
    gate_rows = gt_ref[...]
    for c in range(cb):
        rs = slice(c * C, (c + 1) * C)
        gates = g_ref[rs, :]
        gates_t = gate_rows[:, rs]
        g_tot = jnp.sum(gates, axis=0, keepdims=True)
        tot_rows = jnp.sum(gates_t, axis=1, keepdims=True)
        dec_ref[c] = jnp.broadcast_to(jnp.exp(tot_rows[GDN_CHAINS:2 * GDN_CHAINS]), (GDN_CHAINS, LANES))
        for d in range(2):
            incl, strict = masks[d]
            cum = incl.astype(F32)
            gc_cols = jnp.dot(cum, gates, precision=hi, preferred_element_type=F32)
            gc_rows = lax.dot_general(gates_t, cum, (((1,), (1,)), ((), ())), precision=hi,
                                      preferred_element_type=F32)
            for h in range(GDN_HEADS):
                lane_b = d * GDN_HEADS + h
                lane_g = GDN_CHAINS + lane_b
                sl = slice(h * GDN_D, (h + 1) * GDN_D)
                q = q_ref[rs, sl]
                k = k_ref[rs, sl]
                v = v_ref[rs, sl]
                beta = gates[:, lane_b:lane_b + 1]
                gc_col = gc_cols[:, lane_g:lane_g + 1]
                gc_row = gc_rows[lane_g:lane_g + 1, :]
                g_last = g_tot[:, lane_g:lane_g + 1]
                decay = jnp.where(incl, jnp.exp(jnp.where(incl, gc_col - gc_row, 0.0)), 0.0)
                e_col = jnp.exp(gc_col)
                k_beta = k * beta
                kk = _bdot_nt(jnp.concatenate([k_beta, q], axis=0), k)
                neg_l = jnp.where(strict, -kk[:C] * decay, 0.0)
                attn = jnp.where(incl, kk[C:] * decay, 0.0)
                rhs = jnp.concatenate([v * beta, k_beta * e_col], axis=1).astype(BF16)
                wq_ref[d, c, C:2 * C, sl] = (q * e_col).astype(BF16)
                ak_ref[d, c, h, 0:C, :] = attn.astype(BF16)
                ak_ref[d, c, h, C:C + GDN_D, :] = (k * jnp.exp(g_last - gc_col)).T.astype(BF16)
                chains.append((d, c, rs, sl, neg_l, rhs))
    zs = [jnp.concatenate([ch[4], eye], axis=1) for ch in chains]
    keep_s = lax.broadcasted_iota(jnp.int32, (C, 2 * C), 1) >= C
    for _ in range(int(math.log2(C))):
        zs = [_bdot(z[:, :C], z) + jnp.where(keep_s, z, 0.0) for z in zs]
    for (d, c, rs, sl, _, rhs), z in zip(chains, zs):
        sol = jnp.dot(z[:, C:].astype(BF16), rhs, preferred_element_type=F32)
        u_ref[d, rs, sl] = sol[:, :GDN_D]
        wq_ref[d, c, 0:C, sl] = sol[:, GDN_D:].astype(BF16)


def _gdn_chunk(q, k, v, gates, gates_t, cb):
    T = q.shape[0]
    nc = T // CHUNK
    rows = cb * CHUNK
    kern = functools.partial(_gdn_chunk_body, cb=cb)
    wide = pl.BlockSpec((rows, GDN_W), lambda i: (i, 0))
    return pl.pallas_call(
        kern,
        grid=(nc // cb,),
        in_specs=[wide, wide, wide, pl.BlockSpec((rows, LANES), lambda i: (i, 0)),
                  pl.BlockSpec((2 * GDN_CHAINS, rows), lambda i: (0, i))],
        out_specs=[
            pl.BlockSpec((2, rows, GDN_W), lambda i: (0, i, 0)),
            pl.BlockSpec((2, cb, 2 * CHUNK, GDN_W), lambda i: (0, i, 0, 0)),
            pl.BlockSpec((2, cb, GDN_HEADS, CHUNK + GDN_D, CHUNK), lambda i: (0, i, 0, 0, 0)),
            pl.BlockSpec((cb, GDN_CHAINS, LANES), lambda i: (i, 0, 0)),
        ],
        out_shape=[
            jax.ShapeDtypeStruct((2, T, GDN_W), F32),
            jax.ShapeDtypeStruct((2, nc, 2 * CHUNK, GDN_W), BF16),
            jax.ShapeDtypeStruct((2, nc, GDN_HEADS, CHUNK + GDN_D, CHUNK), BF16),
            jax.ShapeDtypeStruct((nc, GDN_CHAINS, LANES), F32),
        ],
        compiler_params=_cparams(("parallel",)),
        name="gdn_chunk",
    )(q, k, v, gates, gates_t)


def _gdn_scan_body(uf_ref, wqf_ref, akf_ref, decf_ref, ub_ref, wqb_ref, akb_ref, decb_ref,
                   of_ref, ob_ref, state_ref, *, cs):
    @pl.when(pl.program_id(1) == 0)
    def _():
        state_ref[...] = jnp.zeros_like(state_ref)

    C = CHUNK
    refs = ((uf_ref, wqf_ref, akf_ref, decf_ref, of_ref), (ub_ref, wqb_ref, akb_ref, decb_ref, ob_ref))
    chains = [(d, h) for d in range(2) for h in range(GDN_HEADS)]
    st = [state_ref[j] for j in range(GDN_CHAINS)]
    for step in range(cs):
        cidx = (step, cs - 1 - step)
        ws = [jnp.dot(refs[d][1][cidx[d], :, h * GDN_D:(h + 1) * GDN_D], st[j].astype(BF16),
                      preferred_element_type=F32) for j, (d, h) in enumerate(chains)]
        vn = [(refs[d][0][cidx[d] * C:(cidx[d] + 1) * C, h * GDN_D:(h + 1) * GDN_D] - ws[j][:C]).astype(BF16)
              for j, (d, h) in enumerate(chains)]
        rr = [jnp.dot(refs[d][2][cidx[d], h], vn[j], preferred_element_type=F32)
              for j, (d, h) in enumerate(chains)]
        for j, (d, h) in enumerate(chains):
            c = cidx[d]
            refs[d][4][c * C:(c + 1) * C, h * GDN_D:(h + 1) * GDN_D] = (ws[j][C:] + rr[j][:C]).astype(BF16)
            st[j] = st[j] * refs[d][3][c, j:j + 1, :] + rr[j][C:]
    for j in range(GDN_CHAINS):
        state_ref[j] = st[j]


def _gdn_scan(u, wq, ak, dec, B, S, cs):
    T = B * S
    nb = S // (CHUNK * cs)
    rows = cs * CHUNK
    kern = functools.partial(_gdn_scan_body, cs=cs)
    fwd = lambda b, i: b * nb + i
    bwd = lambda b, i: b * nb + nb - 1 - i

    def specs(d, pos):
        return [
            pl.BlockSpec((None, rows, GDN_W), lambda b, i: (d, pos(b, i), 0)),
            pl.BlockSpec((None, cs, 2 * CHUNK, GDN_W), lambda b, i: (d, pos(b, i), 0, 0)),
            pl.BlockSpec((None, cs, GDN_HEADS, CHUNK + GDN_D, CHUNK), lambda b, i: (d, pos(b, i), 0, 0, 0)),
            pl.BlockSpec((cs, GDN_CHAINS, LANES), lambda b, i: (pos(b, i), 0, 0)),
        ]

    out_sd = jax.ShapeDtypeStruct((T, GDN_W), BF16)
    return pl.pallas_call(
        kern,
        grid=(B, nb),
        in_specs=specs(0, fwd) + specs(1, bwd),
        out_specs=[pl.BlockSpec((rows, GDN_W), lambda b, i: (fwd(b, i), 0)),
                   pl.BlockSpec((rows, GDN_W), lambda b, i: (bwd(b, i), 0))],
        out_shape=[out_sd, out_sd],
        scratch_shapes=[pltpu.VMEM((GDN_CHAINS, GDN_D, GDN_D), F32)],
        compiler_params=_cparams(("parallel", "arbitrary")),
        name="gdn_scan",
    )(u, wq, ak, dec, u, wq, ak, dec)


def _attn_heads(q, k, v, qw, kw, cos, sin, bias, qo_ref, ko_ref, vo_ref):
    rows = cos.shape[0]
    lane = lax.broadcasted_iota(jnp.int32, (rows, LANES), 1)
    even_lane = lane % 2 == 0
    first_head = lane < ATT_D
    extra_lane = lane == ATT_D
    mi = lax.broadcasted_iota(jnp.int32, (LANES, LANES), 0) // ATT_D
    mj = lax.broadcasted_iota(jnp.int32, (LANES, LANES), 1) // ATT_D
    head_mean = jnp.where(mi == mj, 1.0 / ATT_D, 0.0).astype(BF16)

    def norm_rope(x, w):
        sq = x * x
        hi = sq.astype(BF16)
        lo = (sq - hi.astype(F32)).astype(BF16)
        ms = (jnp.dot(hi, head_mean, preferred_element_type=F32)
              + jnp.dot(lo, head_mean, preferred_element_type=F32))
        y = x * lax.rsqrt(ms + EPS) * w
        partner = jnp.where(even_lane, pltpu.roll(y, LANES - 1, 1), pltpu.roll(y, 1, 1))
        return y * cos + partner * sin

    def split_heads(r, extra):
        tail = jnp.where(extra_lane, extra, 0.0)
        return (jnp.where(first_head, r, tail).astype(BF16),
                jnp.where(first_head, pltpu.roll(r, ATT_D, 1), tail).astype(BF16))

    scale = LOG2E * ATT_D ** -0.5
    for c in range(ATT_HEADS // 2):
        r = norm_rope(q[:, c * LANES:(c + 1) * LANES], qw) * scale
        qo_ref[2 * c], qo_ref[2 * c + 1] = split_heads(r, bias)
    for c in range(ATT_KV_HEADS // 2):
        r = norm_rope(k[:, c * LANES:(c + 1) * LANES], kw)
        ko_ref[2 * c], ko_ref[2 * c + 1] = split_heads(r, 1.0)
        vo_ref[2 * c], vo_ref[2 * c + 1] = split_heads(v[:, c * LANES:(c + 1) * LANES], 1.0)


def _attn_body(safe_ref, q_ref, k_ref, v_ref, o_ref, m_ref, acc_ref, *, tq, tk):
    safe = safe_ref[0] != 0
    nk = k_ref.shape[0] // tk
    q = q_ref[...].reshape(ATT_GROUP * tq, ATT_DP)
    acc_ref[...] = jnp.zeros_like(acc_ref)

    def scores(j):
        keys = pl.ds(pl.multiple_of(j * tk, tk), tk)
        s = lax.dot_general(q, k_ref[keys, :], (((1,), (1,)), ((), ())), preferred_element_type=F32)
        return s, v_ref[keys, :]

    @pl.when(safe)
    def _():
        def step(j, carry):
            s, v = scores(j)
            acc_ref[...] += jnp.dot(jnp.exp2(s).astype(BF16), v, preferred_element_type=F32)
            return carry
        lax.fori_loop(0, nk, step, 0)

    @pl.when(jnp.logical_not(safe))
    def _():
        m_ref[...] = jnp.full_like(m_ref, -jnp.inf)

        def step(j, carry):
            s, v = scores(j)
            m_prev = m_ref[...]
            m_new = jnp.maximum(m_prev, jnp.max(s, axis=-1, keepdims=True))
            p = jnp.exp2(s - m_new).astype(BF16)
            acc_ref[...] = jnp.exp2(m_prev - m_new) * acc_ref[...] + jnp.dot(p, v, preferred_element_type=F32)
            m_ref[...] = m_new
            return carry
        lax.fori_loop(0, nk, step, 0)

    acc = acc_ref[...]
    o = acc[:, :ATT_D] / acc[:, ATT_D:ATT_D + 1]
    for h in range(ATT_GROUP):
        o_ref[:, h * ATT_D:(h + 1) * ATT_D] = o[h * tq:(h + 1) * tq].astype(o_ref.dtype)


def _attention(safe, q, k, v, B, S, tq, tk):
    kern = functools.partial(_attn_body, tq=tq, tk=tk)
    gw = ATT_GROUP * ATT_D
    return pl.pallas_call(
        kern,
        grid_spec=pltpu.PrefetchScalarGridSpec(
            num_scalar_prefetch=1,
            grid=(B, ATT_KV_HEADS, S // tq),
            in_specs=[
                pl.BlockSpec((None, ATT_GROUP, tq, ATT_DP), lambda b, g, i, s: (b, g, i, 0)),
                pl.BlockSpec((None, None, S, ATT_DP), lambda b, g, i, s: (b, g, 0, 0)),
                pl.BlockSpec((None, None, S, ATT_DP), lambda b, g, i, s: (b, g, 0, 0)),
            ],
            out_specs=pl.BlockSpec((None, tq, gw), lambda b, g, i, s: (b, i, g)),
            scratch_shapes=[
                pltpu.VMEM((ATT_GROUP * tq, 1), F32),
                pltpu.VMEM((ATT_GROUP * tq, ATT_DP), F32),
            ],
        ),
        out_shape=jax.ShapeDtypeStruct((B, S, ATT_Q), BF16),
        compiler_params=_cparams(("parallel", "parallel", "parallel")),
        name="attention",
    )(safe, q, k, v)


def _outproj_body(of_ref, ob_ref, z_ref, att_ref, x_ref, gnw_ref, wa_ref, wb_ref,
                  fnw_ref, wr_ref, triu_ref, xm_ref, h_ref, rt_ref, c_ref, count_ref):
    o = of_ref[...].astype(F32) + ob_ref[...].astype(F32)
    z = z_ref[...]
    parts = []
    for h in range(GDN_HEADS):
        sl = slice(h * GDN_D, (h + 1) * GDN_D)
        t = o[:, sl]
        t = t * lax.rsqrt(jnp.mean(t * t, axis=-1, keepdims=True) + EPS) * gnw_ref[...]
        parts.append((t * _silu(z[:, sl])).astype(BF16))
    mix_a = jnp.concatenate(parts, axis=-1)
    xm = x_ref[...] + jnp.dot(mix_a, wa_ref[...], preferred_element_type=F32)
    xm = xm + jnp.dot(att_ref[...], wb_ref[...], preferred_element_type=F32)
    xm_ref[...] = xm
    hn = xm * lax.rsqrt(jnp.mean(xm * xm, axis=-1, keepdims=True) + EPS) * fnw_ref[...]
    _store_token_tiles(h_ref, hn)
    hn_hi = hn.astype(BF16)
    hn_lo = (hn - hn_hi.astype(F32)).astype(BF16)
    both = jnp.dot(hn_hi, wr_ref[...], preferred_element_type=F32)
    logits = (both[:, :LANES] + both[:, LANES:]
              + jnp.dot(hn_lo, wr_ref[:, :LANES], preferred_element_type=F32))

    lt = logits.T[0:ROUTER_ROWS, :]
    tok = lt.shape[1]
    slot = lax.broadcasted_iota(jnp.int32, (ROUTER_ROWS, tok), 0)
    big = jnp.int32(ROUTER_ROWS)
    neg = -jnp.inf

    def masked_top(vals, mask):
        m = jnp.max(jnp.where(mask, vals, neg), axis=0, keepdims=True)
        idx = jnp.min(jnp.where(mask & (vals == m), slot, big), axis=0, keepdims=True)
        return m, idx

    gmask = slot < N_GROUPS
    gmax, gsel = masked_top(lt, gmask)
    gp_top = 1.0 / jnp.sum(jnp.where(gmask, jnp.exp(lt - gmax), 0.0), axis=0, keepdims=True)
    lo = N_GROUPS + gsel * EXPERTS_PER_GROUP
    emask = (slot >= lo) & (slot < lo + EXPERTS_PER_GROUP)
    m1, i1 = masked_top(lt, emask)
    ex = jnp.where(emask, jnp.exp(lt - m1), 0.0)
    pf = ex / jnp.sum(ex, axis=0, keepdims=True)
    p1, _ = masked_top(pf, emask)
    p2, i2 = masked_top(pf, emask & (slot != i1))
    denom = p1 + p2
    g1 = gp_top * p1 / denom
    g2 = gp_top * p2 / denom
    e1 = (i1 - N_GROUPS).astype(F32)
    e2 = (i2 - N_GROUPS).astype(F32)

    @pl.when(pl.program_id(0) == 0)
    def _():
        count_ref[...] = jnp.zeros_like(count_ref)

    hit1 = slot == i1
    hit2 = slot == i2
    picked = jnp.where(hit1 | hit2, 1.0, 0.0)
    seen = jnp.dot(picked.astype(BF16), triu_ref[...], preferred_element_type=F32)
    counted = count_ref[:, 0:1]
    before = counted + seen - picked
    rank1 = jnp.sum(jnp.where(hit1, before, 0.0), axis=0, keepdims=True)
    rank2 = jnp.sum(jnp.where(hit2, before, 0.0), axis=0, keepdims=True)
    total = jnp.broadcast_to(counted + seen[:, tok - 1:tok], count_ref.shape)
    count_ref[...] = total
    c_ref[...] = total
    rt_ref[...] = jnp.concatenate([e1, e2, g1, g2, rank1, rank2, jnp.zeros((SUBLANES - 6, tok), F32)], axis=0)


def _outproj(o_f, o_b, proj, att, x2, gnw, w_a, w_b, fnw, w_r, tm):
    T, D = x2.shape
    row = lambda i: (i, 0)
    const = lambda i: (0, 0)
    return pl.pallas_call(
        _outproj_body,
        grid=(T // tm,),
        in_specs=[
            pl.BlockSpec((tm, GDN_W), row),
            pl.BlockSpec((tm, GDN_W), row),
            pl.BlockSpec((tm, GDN_W), lambda i: (i, COL_Z // GDN_W)),
            pl.BlockSpec((tm, ATT_Q), row),
            pl.BlockSpec((tm, D), row),
            pl.BlockSpec((1, GDN_D), const),
            pl.BlockSpec((GDN_W, D), const),
            pl.BlockSpec((ATT_Q, D), lambda i: (GDN_W // ATT_Q, 0)),
            pl.BlockSpec((1, D), const),
            pl.BlockSpec((D, 2 * LANES), const),
            pl.BlockSpec((tm, tm), const),
        ],
        out_specs=[pl.BlockSpec((tm, D), row), pl.BlockSpec((tm * ROW_TILE, LANES), row),
                   pl.BlockSpec((SUBLANES, tm), lambda i: (0, i)),
                   pl.BlockSpec((ROUTER_ROWS, LANES), const)],
        out_shape=[jax.ShapeDtypeStruct((T, D), F32), jax.ShapeDtypeStruct((T * ROW_TILE, LANES), F32),
                   jax.ShapeDtypeStruct((SUBLANES, T), F32),
                   jax.ShapeDtypeStruct((ROUTER_ROWS, LANES), F32)],
        scratch_shapes=[pltpu.VMEM((ROUTER_ROWS, LANES), F32)],
        compiler_params=_cparams(("arbitrary",)),
        name="outproj_router",
    )(o_f, o_b, proj, att, x2, gnw, w_a, w_b, fnw, w_r, jnp.asarray(np.tri(tm, dtype=np.float32).T, BF16))


def _gather_start(src_hbm, dst_buf, sem, idx_ref, base, slot, n):
    for r in range(n):
        src = pl.multiple_of(idx_ref[base + r] * ROW_TILE, ROW_TILE)
        dst = pl.multiple_of((slot * n + r) * ROW_TILE, ROW_TILE)
        pltpu.make_async_copy(src_hbm.at[pl.ds(src, ROW_TILE)], dst_buf.at[pl.ds(dst, ROW_TILE)],
                              sem.at[slot]).start(priority=r % DMA_QUEUES)


def _gather_wait(src_hbm, dst_buf, sem, slot, n):
    dst = pl.multiple_of(slot * n * ROW_TILE, ROW_TILE)
    pltpu.make_async_copy(src_hbm.at[pl.ds(0, n * ROW_TILE)], dst_buf.at[pl.ds(dst, n * ROW_TILE)],
                          sem.at[slot]).wait()


def _experts_body(be_ref, src_ref, nused_ref,
                  h_hbm, wg_hbm, wu_hbm, wd_hbm, y_ref, xbuf, wg_buf, wu_buf, wd_buf, ws_ref, sem, wsem, *, bm):
    i = pl.program_id(0)
    n_blocks = pl.num_programs(0)
    n_used = nused_ref[0]
    slot = i % ROW_BUFFERS
    expert = be_ref[i]
    first = (i == 0) | (expert != be_ref[jnp.maximum(i - 1, 0)])

    def weight_copies(e, s):
        return [pltpu.make_async_copy(hbm.at[e], buf.at[s], wsem.at[s])
                for hbm, buf in ((wg_hbm, wg_buf), (wu_hbm, wu_buf), (wd_hbm, wd_buf))]

    def compute(prefetch):
        @pl.when(first)
        def _():
            cur = 1 - ws_ref[0]
            ws_ref[0] = cur
            for c in weight_copies(0, cur):
                c.wait()
            nxt = lax.while_loop(lambda j: (j < n_used) & (be_ref[jnp.minimum(j, n_blocks - 1)] == expert),
                                 lambda j: j + 1, i + 1)

            @pl.when(nxt < n_used)
            def _():
                for c in weight_copies(be_ref[jnp.minimum(nxt, n_blocks - 1)], 1 - cur):
                    c.start()

        ws = ws_ref[0]

        _gather_wait(h_hbm, xbuf, sem, slot, bm)
        x = _load_token_tiles(xbuf, slot * (bm * ROW_TILE), bm).astype(BF16)
        if prefetch:
            ahead = i + ROW_BUFFERS - 1
            _gather_start(h_hbm, xbuf, sem, src_ref, ahead * bm, ahead % ROW_BUFFERS, bm)
        gate = jnp.dot(x, wg_buf[ws].astype(BF16), preferred_element_type=F32)
        up = jnp.dot(x, wu_buf[ws].astype(BF16), preferred_element_type=F32)
        hid = (_silu(gate) * up).astype(BF16)
        _store_token_tiles(y_ref, jnp.dot(hid, wd_buf[ws].astype(BF16), preferred_element_type=F32))

    @pl.when((i == 0) & (n_used > 0))
    def _():
        ws_ref[0] = 1
        for c in weight_copies(be_ref[0], 0):
            c.start()
        for blk in range(ROW_BUFFERS - 1):
            @pl.when(blk < n_used)
            def _():
                _gather_start(h_hbm, xbuf, sem, src_ref, blk * bm, blk, bm)

    @pl.when(i + ROW_BUFFERS - 1 < n_used)
    def _():
        compute(True)

    @pl.when((i < n_used) & (i + ROW_BUFFERS - 1 >= n_used))
    def _():
        compute(False)

    @pl.when(i >= n_used)
    def _():
        y_ref[...] = jnp.zeros_like(y_ref)


def _experts(block_expert, src_tok, n_used, h2, w_gate, w_up, w_down, bm):
    D = ROW_TILE * LANES
    P = src_tok.shape[0]
    n_blocks = P // bm
    FF = w_gate.shape[-1]
    kern = functools.partial(_experts_body, bm=bm)
    hbm = pl.BlockSpec(memory_space=pl.ANY)
    return pl.pallas_call(
        kern,
        grid_spec=pltpu.PrefetchScalarGridSpec(
            num_scalar_prefetch=3,
            grid=(n_blocks,),
            in_specs=[hbm, hbm, hbm, hbm],
            out_specs=pl.BlockSpec((bm * ROW_TILE, LANES), lambda i, *_: (i, 0)),
            scratch_shapes=[
                pltpu.VMEM((ROW_BUFFERS * bm * ROW_TILE, LANES), F32),
                pltpu.VMEM((2, D, FF), F32), pltpu.VMEM((2, D, FF), F32), pltpu.VMEM((2, FF, D), F32),
                pltpu.SMEM((1,), jnp.int32),
                pltpu.SemaphoreType.DMA((ROW_BUFFERS,)), pltpu.SemaphoreType.DMA((2,)),
            ],
        ),
        out_shape=jax.ShapeDtypeStruct((P * ROW_TILE, LANES), F32),
        compiler_params=_cparams(("arbitrary",)),
        name="moe_experts",
    )(block_expert, src_tok, n_used, h2, w_gate, w_up, w_down)


def _combine_body(dest_ref, y_hbm, xm_ref, r_ref, fw_ref, o_ref, ybuf, sem, *, tc):
    i = pl.program_id(0)
    n = pl.num_programs(0)
    slot = i % ROW_BUFFERS
    rows = TOP_K * tc
    ahead = ROW_BUFFERS - 1

    @pl.when(i == 0)
    def _():
        for blk in range(ahead):
            @pl.when(blk < n)
            def _():
                _gather_start(y_hbm, ybuf, sem, dest_ref, blk * rows, blk, rows)

    def compute(prefetch):
        _gather_wait(y_hbm, ybuf, sem, slot, rows)
        route = r_ref[...].T
        y0 = _load_token_tiles(ybuf, slot * (rows * ROW_TILE), tc)
        y1 = _load_token_tiles(ybuf, (slot * rows + tc) * ROW_TILE, tc)
        if prefetch:
            _gather_start(y_hbm, ybuf, sem, dest_ref, (i + ahead) * rows, (i + ahead) % ROW_BUFFERS, rows)
        xo = xm_ref[...] + route[:, 2:3] * y0 + route[:, 3:4] * y1
        o_ref[...] = xo * lax.rsqrt(jnp.mean(xo * xo, axis=-1, keepdims=True) + EPS) * fw_ref[...]

    @pl.when(i + ahead < n)
    def _():
        compute(True)

    @pl.when(i + ahead >= n)
    def _():
        compute(False)


def _combine(dest_blocked, y_buf, x_mid, route, final_w, tc):
    T, D = x_mid.shape
    kern = functools.partial(_combine_body, tc=tc)
    return pl.pallas_call(
        kern,
        grid_spec=pltpu.PrefetchScalarGridSpec(
            num_scalar_prefetch=1,
            grid=(T // tc,),
            in_specs=[
                pl.BlockSpec(memory_space=pl.ANY),
                pl.BlockSpec((tc, D), lambda i, d: (i, 0)),
                pl.BlockSpec((SUBLANES, tc), lambda i, d: (0, i)),
                pl.BlockSpec((1, D), lambda i, d: (0, 0)),
            ],
            out_specs=pl.BlockSpec((tc, D), lambda i, d: (i, 0)),
            scratch_shapes=[pltpu.VMEM((ROW_BUFFERS * TOP_K * tc * ROW_TILE, LANES), F32),
                            pltpu.SemaphoreType.DMA((ROW_BUFFERS,))],
        ),
        out_shape=jax.ShapeDtypeStruct((T, D), F32),
        compiler_params=_cparams(("arbitrary",)),
        name="moe_combine",
    )(dest_blocked, y_buf, x_mid, route, final_w.reshape(1, D))


def _layer(x2, B, S, norm_mix_w, w_in, conv_w, a_log, dt_bias, gdn_norm_w, q_norm_w, k_norm_w, w_out,
           norm_ffn_w, w_router_group, w_router_expert, w_gate, w_up, w_down, final_w):
    T, D = x2.shape
    tl = _tiles(B, S)
    o_gate = 4 * GDN_W
    o_qb = o_gate + 2 * GDN_CHAINS
    w_all = jnp.concatenate([w_in[:, :o_gate], w_in[:, o_qb:], w_in[:, o_gate:o_qb],
                             jnp.zeros((D, LANES - 2 * GDN_CHAINS), w_in.dtype)], axis=1).astype(BF16)

    rows = S // GRID_W
    rowp = np.repeat(np.arange(rows), GRID_W).astype(np.float64)
    colp = np.tile(np.arange(GRID_W), rows).astype(np.float64)
    axis_dims = ATT_D // 2
    inv_freq = ROPE_THETA ** (-np.arange(0, axis_dims, 2, dtype=np.float64) / axis_dims)
    ang = np.concatenate([rowp[:, None] * inv_freq, colp[:, None] * inv_freq], axis=-1)
    pair_sign = np.tile(np.array([-1.0, 1.0]), axis_dims)
    cosf = jnp.asarray(np.tile(np.repeat(np.cos(ang), 2, axis=1), (1, LANES // ATT_D)), F32)
    sinf = jnp.asarray(np.tile(np.repeat(np.sin(ang), 2, axis=1) * pair_sign, (1, LANES // ATT_D)), F32)
    q_gain = jnp.max(jnp.abs(q_norm_w)).astype(F32)
    k_gain = jnp.max(jnp.abs(k_norm_w)).astype(F32)
    score_bound = ATT_D ** 0.5 * q_gain * k_gain
    safe = (2.0 * score_bound <= SOFTMAX_SAFE_SPAN).astype(jnp.int32).reshape(1)
    bias = (-LOG2E * score_bound).reshape(1, 1)
    pair = lambda w: jnp.tile(w, LANES // ATT_D).reshape(1, LANES)

    proj, gate_logits, qh, kh, vh = _inproj(x2, norm_mix_w, w_all, pair(q_norm_w), pair(k_norm_w),
                                            cosf, sinf, bias, B, S, tl.proj_rows)

    conv_w8 = jnp.concatenate([conv_w, jnp.zeros((SUBLANES - CONV_W, conv_w.shape[1]), F32)], axis=0)
    gp = jnp.zeros((SUBLANES, LANES), F32)
    gp = gp.at[0, GDN_CHAINS:2 * GDN_CHAINS].set(jnp.exp(a_log.astype(F32)).reshape(-1))
    gp = gp.at[1, GDN_CHAINS:2 * GDN_CHAINS].set(dt_bias.astype(F32).reshape(-1))
    q_a, k_a, v_a, gates, gates_t = _gdn_prep(proj, gate_logits, conv_w8, gp, B, S, tl.prep_rows)
    u, wq, ak, dec = _gdn_chunk(q_a, k_a, v_a, gates, gates_t, tl.gdn_chunks)
    o_f, o_b = _gdn_scan(u, wq, ak, dec, B, S, tl.scan_chunks)

    att = _attention(safe, qh, kh, vh, B, S, tl.att_q, tl.att_k).reshape(T, ATT_Q)

    w_r32 = jnp.concatenate([w_router_group, w_router_expert,
                             jnp.zeros((D, LANES - N_GROUPS - N_EXPERTS), F32)], axis=1).astype(F32)
    w_r_hi = w_r32.astype(BF16)
    w_r = jnp.concatenate([w_r_hi, (w_r32 - w_r_hi.astype(F32)).astype(BF16)], axis=1)
    w_out_bf = w_out.astype(BF16)
    x_mid, h2, route_t, count_rows = _outproj(
        o_f, o_b, proj, att, x2, gdn_norm_w.reshape(1, GDN_D), w_out_bf, w_out_bf,
        norm_ffn_w.reshape(1, D), w_r, tl.proj_rows)

    bm = tl.moe_rows
    n_assign = T * TOP_K
    n_blocks = -(-(n_assign + N_EXPERTS * (bm - 1)) // bm)
    experts = jnp.arange(N_EXPERTS, dtype=jnp.int32)
    counts = count_rows[N_GROUPS:N_GROUPS + N_EXPERTS, 0].astype(jnp.int32)
    padded = (counts + bm - 1) // bm * bm
    pad_end = jnp.cumsum(padded)
    pad_start = pad_end - padded
    block_start = jnp.arange(n_blocks, dtype=jnp.int32) * bm
    done = (pad_end[None, :] <= block_start[:, None]).astype(jnp.int32)
    begun = (pad_start[None, :] <= block_start[:, None]).astype(jnp.int32)
    block_expert = jnp.minimum(jnp.sum(done, axis=1), N_EXPERTS - 1)
    n_used = (pad_end[-1:] // bm).astype(jnp.int32)
    seg_start = jnp.sum(done * padded[None, :], axis=1)
    seg_entry = jnp.sum(done * counts[None, :], axis=1)
    seg_count = jnp.sum(begun * counts[None, :], axis=1) - seg_entry
    e_rows = route_t[0:TOP_K].astype(jnp.int32)
    rank_rows = route_t[4:4 + TOP_K].astype(jnp.int32)
    dest = jnp.sum(jnp.where(e_rows[:, :, None] == experts, pad_start, 0), axis=-1) + rank_rows
    tokens = jnp.tile(jnp.arange(T, dtype=jnp.int32), TOP_K)
    _, compact = lax.sort((dest.reshape(-1), tokens), num_keys=1)
    row = block_start[:, None] + jnp.arange(bm, dtype=jnp.int32)[None, :]
    seg_row = row - seg_start[:, None]
    holds_token = seg_row < seg_count[:, None]
    entry = jnp.clip(seg_entry[:, None] + seg_row, 0, n_assign - 1)
    src_tok = jnp.where(holds_token, compact[entry], lax.rem(row, jnp.full_like(row, T))).reshape(-1)

    y_buf = _experts(block_expert, src_tok, n_used, h2, w_gate, w_up, w_down, bm)

    tc = tl.comb_rows
    dest_blocked = dest.reshape(TOP_K, T // tc, tc).transpose(1, 0, 2).reshape(-1)
    return _combine(dest_blocked, y_buf, x_mid, route_t, final_w, tc)


def kernel(x, norm_mix_w, w_in, conv_w, a_log, dt_bias, gdn_norm_w, q_norm_w, k_norm_w, w_out, norm_ffn_w,
           w_router_group, w_router_expert, w_gate, w_up, w_down, final_norm_w):
    B, S, D = x.shape
    depth = w_in.shape[0]
    assert depth == 1, "the final norm is fused into the last (only) layer's combine step"
    out = _layer(x.reshape(B * S, D), B, S, norm_mix_w[0], w_in[0], conv_w[0], a_log[0], dt_bias[0],
                 gdn_norm_w[0], q_norm_w[0], k_norm_w[0], w_out[0], norm_ffn_w[0], w_router_group[0],
                 w_router_expert[0], w_gate[0], w_up[0], w_down[0], final_norm_w)
    return out.reshape(B, S, D)
```
